```python
import jax, jax.numpy as jnp
from jax import lax
import numpy as np

D_MODEL = 1024
BATCH = 8
SEQ = 2048
DEPTH = 2

GRID_W = 64
CTX_LEN = 256

A_HEADS = 8
A_KV_HEADS = 2
A_HEAD_DIM = 64
A_WINDOW = 128
A_BLOCK = 128
ROPE_BASE = 10000.0
B_HEADS = 4
B_DK = 128
B_DV = 128
C_HEADS = 4
C_DK = 96
C_DV = 192
C_GATE_RANK = 16
C_GATE_TAU = 16.0
D_GROUPS = 4
D_GROUP_DIM = 64
CHUNK = 64
N_EXPERTS = 16
N_GROUPS = 4
E_PER_GROUP = N_EXPERTS // N_GROUPS
TOP_K = 2
D_FF = 512

N_EVEN = (DEPTH + 1) // 2
N_ODD = DEPTH // 2
A_Q = A_HEADS * A_HEAD_DIM
A_KV = A_KV_HEADS * A_HEAD_DIM
B_KW = B_HEADS * B_DK
B_VW = B_HEADS * B_DV
C_KW = C_HEADS * C_DK
C_VW = C_HEADS * C_DV
D_W = D_GROUPS * D_GROUP_DIM
IN_EVEN = A_Q + 2 * A_KV + 3 * B_KW + 2 * B_VW
MIX_EVEN = A_Q + B_VW
IN_ODD = 2 * C_KW + 2 * C_VW + 2 * C_GATE_RANK + D_W
MIX_ODD = C_VW + D_W
DEEPNORM_ALPHA = (2 * DEPTH) ** 0.25
DEEPNORM_BETA = (8 * DEPTH) ** -0.25
LN_EPS = 1e-5
NEG = -1e30

kernel_name = "hybrid_dit_swa_hgrn2_gla_fnet_moe"


def layer_norm(x):
    xf = x.astype(jnp.float32)
    mu = jnp.mean(xf, -1, keepdims=True)
    var = jnp.mean(jnp.square(xf - mu), -1, keepdims=True)
    return ((xf - mu) * lax.rsqrt(var + LN_EPS)).astype(x.dtype)


def layer_norm_affine(x, g, b):
    xf = x.astype(jnp.float32)
    mu = jnp.mean(xf, -1, keepdims=True)
    var = jnp.mean(jnp.square(xf - mu), -1, keepdims=True)
    return ((xf - mu) * lax.rsqrt(var + LN_EPS) * g + b).astype(x.dtype)


def head_rms_norm(z, g, n_heads):
    B, T, W = z.shape
    zf = z.astype(jnp.float32).reshape(B, T, n_heads, W // n_heads)
    zf = zf * lax.rsqrt(jnp.mean(jnp.square(zf), -1, keepdims=True) + LN_EPS)
    return (zf.reshape(B, T, W) * g).astype(z.dtype)


def split_cols(p, sizes):
    offs = np.cumsum(sizes)[:-1]
    return jnp.split(p, [int(o) for o in offs], axis=-1)


def to_heads(z, n_heads):
    B, T, _ = z.shape
    return z.reshape(B, T, n_heads, -1).transpose(0, 2, 1, 3)


def from_heads(z):
    B, H, T, d = z.shape
    return z.transpose(0, 2, 1, 3).reshape(B, T, H * d)


def rope_half(x, pos):
    half = x.shape[-1] // 2
    inv = ROPE_BASE ** (-jnp.arange(half, dtype=jnp.float32) / half)
    ang = pos.astype(jnp.float32)[:, None] * inv[None, :]
    cos, sin = jnp.cos(ang), jnp.sin(ang)
    x1 = x[..., :half].astype(jnp.float32)
    x2 = x[..., half:].astype(jnp.float32)
    return jnp.concatenate([x1 * cos - x2 * sin, x1 * sin + x2 * cos], -1)


def axial_rope(x, rows, cols):
    r = x.shape[-1] // 2
    return jnp.concatenate([rope_half(x[..., :r], rows), rope_half(x[..., r:], cols)], -1).astype(x.dtype)


def window_attention(q, k, v, kc, vc, sink):
    B, Hkv, G, T, hd = q.shape
    nb = T // A_BLOCK
    scale = hd ** -0.5
    qb = q.reshape(B, Hkv, G, nb, A_BLOCK, hd)
    pad = ((0, 0), (0, 0), (A_BLOCK, A_BLOCK), (0, 0))
    kp = jnp.pad(k, pad).reshape(B, Hkv, nb + 2, A_BLOCK, hd)
    vp = jnp.pad(v, pad).reshape(B, Hkv, nb + 2, A_BLOCK, hd)
    kband = jnp.concatenate([kp[:, :, 0:nb], kp[:, :, 1:nb + 1], kp[:, :, 2:nb + 2]], axis=3)
    vband = jnp.concatenate([vp[:, :, 0:nb], vp[:, :, 1:nb + 1], vp[:, :, 2:nb + 2]], axis=3)
    qpos = jnp.arange(T).reshape(nb, A_BLOCK)
    kpos = jnp.arange(nb)[:, None] * A_BLOCK - A_BLOCK + jnp.arange(3 * A_BLOCK)[None, :]
    valid = ((kpos[:, None, :] >= 0) & (kpos[:, None, :] < T)
             & (jnp.abs(qpos[:, :, None] - kpos[:, None, :]) <= A_WINDOW))
    s_band = jnp.einsum('bhgnqd,bhnkd->bhgnqk', qb, kband).astype(jnp.float32) * scale
    s_band = jnp.where(valid, s_band, NEG)
    s_ctx = jnp.einsum('bhgnqd,bhcd->bhgnqc', qb, kc).astype(jnp.float32) * scale
    s_sink = jnp.broadcast_to(sink.astype(jnp.float32)[None, :, :, None, None, None], (B, Hkv, G, nb, A_BLOCK, 1))
    p = jax.nn.softmax(jnp.concatenate([s_sink, s_ctx, s_band], -1), -1)
    C = kc.shape[2]
    p_ctx = p[..., 1:1 + C].astype(v.dtype)
    p_band = p[..., 1 + C:].astype(v.dtype)
    o = (jnp.einsum('bhgnqc,bhcd->bhgnqd', p_ctx, vc)
         + jnp.einsum('bhgnqk,bhnkd->bhgnqd', p_band, vband))
    return o.reshape(B, Hkv, G, T, hd)


def context_attention(qc, kc, vc, sink):
    B, Hkv, G, C, hd = qc.shape
    s = jnp.einsum('bhgqd,bhkd->bhgqk', qc, kc).astype(jnp.float32) * hd ** -0.5
    s0 = jnp.broadcast_to(sink.astype(jnp.float32)[None, :, :, None, None], (B, Hkv, G, C, 1))
    p = jax.nn.softmax(jnp.concatenate([s0, s], -1), -1)[..., 1:]
    return jnp.einsum('bhgqk,bhkd->bhgqd', p.astype(vc.dtype), vc)


def chunk_gated_scan(q, k, v, log_f, s0):
    B, H, T, dk = q.shape
    dv = v.shape[-1]
    L = CHUNK
    N = T // L
    f32 = jnp.float32
    qf = q.astype(f32).reshape(B, H, N, L, dk)
    kf = k.astype(f32).reshape(B, H, N, L, dk)
    vf = v.astype(f32).reshape(B, H, N, L, dv)
    b = jnp.cumsum(log_f.astype(f32).reshape(B, H, N, L, dk), axis=3)
    g = b[:, :, :, L - 1:L, :]
    ref = b[:, :, :, L // 2:L // 2 + 1, :]
    att = jnp.einsum('bhnld,bhnmd->bhnlm', qf * jnp.exp(b - ref), kf * jnp.exp(ref - b))
    att = jnp.where(jnp.tril(jnp.ones((L, L), dtype=bool)), att, 0.0)
    o = jnp.einsum('bhnlm,bhnme->bhnle', att, vf)
    ds = jnp.einsum('bhnld,bhnle->nbhde', kf * jnp.exp(g - b), vf)
    decay = jnp.exp(g[:, :, :, 0, :]).transpose(2, 0, 1, 3)

    def step(s, inp):
        d, dsn = inp
        return s * d[..., None] + dsn, s

    s_last, s_start = lax.scan(step, s0, (decay, ds))
    o = o + jnp.einsum('bhnld,nbhde->bhnle', qf * jnp.exp(b), s_start)
    return o.reshape(B, H, T, dv).astype(v.dtype), s_last


def bidir_scan(q_c, v_c, kc_dirs, q_l, v_l, kl_dirs):
    B, H, _, dk = q_c.shape
    dv = v_c.shape[-1]
    s0 = jnp.zeros((B, H, dk, dv), jnp.float32)
    (kcf, lcf), (kcb, lcb) = kc_dirs
    (klf, llf), (klb, llb) = kl_dirs
    fl = lambda a: jnp.flip(a, axis=2)
    oc_f, sc_f = chunk_gated_scan(q_c, kcf, v_c, lcf, s0)
    oc_b, sc_b = chunk_gated_scan(fl(q_c), fl(kcb), fl(v_c), fl(lcb), s0)
    ol_f, _ = chunk_gated_scan(q_l, klf, v_l, llf, sc_f)
    ol_b, _ = chunk_gated_scan(fl(q_l), fl(klb), fl(v_l), fl(llb), sc_b)
    return oc_f + fl(oc_b), ol_f + fl(ol_b)


def fourier_mix(z):
    B, T, W = z.shape
    zf = z.astype(jnp.float32).reshape(B, T, D_GROUPS, D_GROUP_DIM)
    y = jnp.fft.fft2(zf, axes=(1, 3), norm='ortho').real
    return y.reshape(B, T, W).astype(z.dtype)


def mixer_even(u, uc, w_in, sink, lb_logits, idx, g_norm, w_out, rows, cols, need_ctx):
    sizes = (A_Q, A_KV, A_KV, B_KW, B_KW, B_KW, B_VW, B_VW)
    aq, ak, av, ff, fb, hq, hi, hg = split_cols(u @ w_in, sizes)
    caq, cak, cav, cff, cfb, chq, chi, chg = split_cols(uc @ w_in, sizes)
    G = A_HEADS // A_KV_HEADS
    sink = sink.reshape(A_KV_HEADS, G)
    q = axial_rope(to_heads(aq, A_HEADS), rows, cols)
    B, _, T, hd = q.shape
    q = q.reshape(B, A_KV_HEADS, G, T, hd)
    k = axial_rope(to_heads(ak, A_KV_HEADS), rows, cols)
    v = to_heads(av, A_KV_HEADS)
    kc = to_heads(cak, A_KV_HEADS)
    vc = to_heads(cav, A_KV_HEADS)
    o_att = from_heads(window_attention(q, k, v, kc, vc, sink).reshape(B, A_HEADS, T, hd))
    lb = jnp.cumsum(jax.nn.softmax(lb_logits.astype(jnp.float32), axis=1), axis=1)[:, idx]

    def gates(z, lb_d):
        f = lb_d + (1.0 - lb_d) * jax.nn.sigmoid(z.astype(jnp.float32))
        return to_heads((1.0 - f).astype(z.dtype), B_HEADS), to_heads(jnp.log(f), B_HEADS)

    o_c, o_l = bidir_scan(to_heads(chq, B_HEADS), to_heads(chi, B_HEADS), (gates(cff, lb[0]), gates(cfb, lb[1])),
                          to_heads(hq, B_HEADS), to_heads(hi, B_HEADS), (gates(ff, lb[0]), gates(fb, lb[1])))
    o_rec = head_rms_norm(from_heads(o_l), g_norm, B_HEADS) * jax.nn.silu(hg)
    y = jnp.concatenate([o_att, o_rec], -1) @ w_out
    if not need_ctx:
        return y, None
    qc = to_heads(caq, A_HEADS)
    C = qc.shape[2]
    qc = qc.reshape(B, A_KV_HEADS, G, C, hd)
    o_att_c = from_heads(context_attention(qc, kc, vc, sink).reshape(B, A_HEADS, C, hd))
    o_rec_c = head_rms_norm(from_heads(o_c), g_norm, B_HEADS) * jax.nn.silu(chg)
    yc = jnp.concatenate([o_att_c, o_rec_c], -1) @ w_out
    return y, yc


def mixer_odd(u, uc, w_in, gate_w, gate_b, g_norm, w_out, need_ctx):
    sizes = (C_KW, C_KW, C_VW, C_GATE_RANK, C_GATE_RANK, C_VW, D_W)

    def gla_inputs(p):
        q, k, v, rf, rb, g, z = split_cols(p, sizes)

        def decay(r, d):
            zz = (r @ gate_w[d] + gate_b[d]).astype(jnp.float32)
            return to_heads(jax.nn.log_sigmoid(zz) / C_GATE_TAU, C_HEADS)

        kh = to_heads(k, C_HEADS)
        return (to_heads(q * (C_DK ** -0.5), C_HEADS), to_heads(v, C_HEADS),
                ((kh, decay(rf, 0)), (kh, decay(rb, 1))), g, z)

    q, v, kd, g, z = gla_inputs(u @ w_in)
    qc, vc, kdc, gc, zc = gla_inputs(uc @ w_in)
    o_c, o_l = bidir_scan(qc, vc, kdc, q, v, kd)
    o_gla = head_rms_norm(from_heads(o_l), g_norm, C_HEADS) * jax.nn.silu(g)
    y = jnp.concatenate([o_gla, fourier_mix(z)], -1) @ w_out
    if not need_ctx:
        return y, None
    o_gla_c = head_rms_norm(from_heads(o_c), g_norm, C_HEADS) * jax.nn.silu(gc)
    yc = jnp.concatenate([o_gla_c, fourier_mix(zc)], -1) @ w_out
    return y, yc


def moe(t, w_router, b_router, w1, w3, w2):
    N = t.shape[0]
    probs = jax.nn.softmax((t @ w_router).astype(jnp.float32), -1)
    sel = probs + b_router.astype(jnp.float32)
    grp_score = jnp.sum(lax.top_k(sel.reshape(N, N_GROUPS, E_PER_GROUP), TOP_K)[0], -1)
    best = jnp.argmax(grp_score, -1)
    in_grp = (jnp.arange(N_EXPERTS) // E_PER_GROUP)[None, :] == best[:, None]
    _, idx = lax.top_k(jnp.where(in_grp, sel, NEG), TOP_K)
    w = jnp.take_along_axis(probs, idx, -1)
    w = w / jnp.sum(w, -1, keepdims=True)
    gates = jnp.sum(jax.nn.one_hot(idx, N_EXPERTS, dtype=jnp.float32) * w[..., None], axis=1)
    y = jnp.zeros(t.shape, jnp.float32)
    for e in range(N_EXPERTS):
        h = jax.nn.silu(t @ w1[e]) * (t @ w3[e])
        y = y + gates[:, e:e + 1] * (h @ w2[e])
    return y.astype(t.dtype)


def modulate(x, shift, scale):
    return x * (1.0 + scale) + shift


def setup_inputs(seed: int = 0) -> dict:
    key = jax.random.key(seed)
    ks = jax.random.split(key, 24)
    f32 = jnp.float32
    D = D_MODEL

    def nrm(k, shape, scale):
        return jax.random.normal(k, shape, f32) * scale

    return {
        'x': nrm(ks[0], (BATCH, SEQ, D), 1.0),
        'c': nrm(ks[1], (BATCH, D), 1.0),
        'ctx': nrm(ks[2], (BATCH, CTX_LEN, D), 1.0),
        'c_ctx': nrm(ks[3], (D,), 1.0),
        'w_ada': nrm(ks[4], (DEPTH, D, 6 * D), 0.5 * D ** -0.5),
        'b_ada': nrm(ks[5], (DEPTH, 6 * D), 0.02),
        'ln_g': 1.0 + nrm(ks[6], (DEPTH, 2, D), 0.02),
        'ln_b': nrm(ks[7], (DEPTH, 2, D), 0.02),
        'w_in_even': nrm(ks[8], (N_EVEN, D, IN_EVEN), D ** -0.5),
        'attn_sink': nrm(ks[9], (N_EVEN, A_HEADS), 0.5),
        'hgrn_lb_logits': nrm(ks[10], (2, N_EVEN + 1, B_KW), 0.5),
        'hgrn_norm': 1.0 + nrm(ks[11], (N_EVEN, B_VW), 0.02),
        'w_out_even': nrm(ks[12], (N_EVEN, MIX_EVEN, D), DEEPNORM_BETA * MIX_EVEN ** -0.5),
        'w_in_odd': nrm(ks[13], (N_ODD, D, IN_ODD), D ** -0.5),
        'gla_gate_w': nrm(ks[14], (N_ODD, 2, C_GATE_RANK, C_KW), C_GATE_RANK ** -0.5),
        'gla_gate_b': nrm(ks[15], (N_ODD, 2, C_KW), 0.5),
        'gla_norm': 1.0 + nrm(ks[16], (N_ODD, C_VW), 0.02),
        'w_out_odd': nrm(ks[17], (N_ODD, MIX_ODD, D), DEEPNORM_BETA * MIX_ODD ** -0.5),
        'w_router': nrm(ks[18], (D, N_EXPERTS), D ** -0.5),
        'b_router': nrm(ks[19], (N_EXPERTS,), 0.01),
        'w_expert_gate': nrm(ks[20], (DEPTH, N_EXPERTS, D, D_FF), D ** -0.5),
        'w_expert_up': nrm(ks[21], (DEPTH, N_EXPERTS, D, D_FF), D ** -0.5),
        'w_expert_down': nrm(ks[22], (DEPTH, N_EXPERTS, D_FF, D), DEEPNORM_BETA * D_FF ** -0.5),
    }


def reference(x, c, ctx, c_ctx, w_ada, b_ada, ln_g, ln_b, w_in_even, attn_sink, hgrn_lb_logits, hgrn_norm,
              w_out_even, w_in_odd, gla_gate_w, gla_gate_b, gla_norm, w_out_odd, w_router, b_router,
              w_expert_gate, w_expert_up, w_expert_down):
    B, T, D = x.shape
    C = ctx.shape[1]
    ROWS = T // GRID_W
    rows = jnp.repeat(jnp.arange(ROWS), GRID_W)
    cols = jnp.tile(jnp.arange(GRID_W), ROWS)
    h = x
    hc = ctx
    for l in range(DEPTH):
        need_ctx = l < DEPTH - 1
        mod = (jax.nn.silu(c) @ w_ada[l] + b_ada[l])[:, None, :]
        mod_c = (jax.nn.silu(c_ctx) @ w_ada[l] + b_ada[l])[None, None, :]
        sh1, sc1, g1, sh2, sc2, g2 = jnp.split(mod, 6, axis=-1)
        csh1, csc1, cg1, csh2, csc2, cg2 = jnp.split(mod_c, 6, axis=-1)
        u = modulate(layer_norm(h), sh1, sc1)
        uc = modulate(layer_norm(hc), csh1, csc1)
        if l % 2 == 0:
            i = l // 2
            m, mc = mixer_even(u, uc, w_in_even[i], attn_sink[i], hgrn_lb_logits, i, hgrn_norm[i],
                               w_out_even[i], rows, cols, need_ctx)
        else:
            i = l // 2
            m, mc = mixer_odd(u, uc, w_in_odd[i], gla_gate_w[i], gla_gate_b[i], gla_norm[i], w_out_odd[i], need_ctx)
        h = layer_norm_affine(DEEPNORM_ALPHA * h + g1 * m, ln_g[l, 0], ln_b[l, 0])
        u = modulate(layer_norm(h), sh2, sc2)
        if need_ctx:
            hc = layer_norm_affine(DEEPNORM_ALPHA * hc + cg1 * mc, ln_g[l, 0], ln_b[l, 0])
            uc = modulate(layer_norm(hc), csh2, csc2)
            tok = jnp.concatenate([uc, u], axis=1).reshape(-1, D)
            y = moe(tok, w_router, b_router, w_expert_gate[l], w_expert_up[l], w_expert_down[l]).reshape(B, C + T, D)
            yc = y[:, :C]
            y = y[:, C:]
            hc = layer_norm_affine(DEEPNORM_ALPHA * hc + cg2 * yc, ln_g[l, 1], ln_b[l, 1])
        else:
            y = moe(u.reshape(-1, D), w_router, b_router, w_expert_gate[l], w_expert_up[l],
                    w_expert_down[l]).reshape(B, T, D)
        h = layer_norm_affine(DEEPNORM_ALPHA * h + g2 * y, ln_g[l, 1], ln_b[l, 1])
    return h
```

```python
import functools
import math

import numpy as np
import jax
import jax.numpy as jnp
from jax import lax
from jax.experimental import pallas as pl
from jax.experimental.pallas import tpu as pltpu

F32 = jnp.float32
BF16 = jnp.bfloat16
HIGHEST = lax.Precision.HIGHEST

D = 1024
CTX = 256
GRID_W = 64
LN_EPS = 1e-5
NEG = -1e30
ALPHA = 4.0 ** 0.25
ROPE_BASE = 10000.0

A_HEADS, A_KV, A_HD, A_BLK = 8, 2, 64, 128
R_HEADS = 4
DKP = 128
GLA_DK, GLA_DV, GLA_DVP = 96, 192, 256
GLA_TAU = 16.0
CHUNK = 64
N_EXP, N_GRP, E_PER = 16, 4, 4
D_FF = 512
F_GROUPS, F_GD = 4, 64

TM = 256
GW = 128
V7X_VMEM_LIMIT = 56 * 1024 * 1024


def _cp(sem, vmem=None):
    return pltpu.CompilerParams(dimension_semantics=sem, vmem_limit_bytes=vmem)


def _ln(x):
    mu = jnp.mean(x, axis=-1, keepdims=True)
    xc = x - mu
    var = jnp.mean(xc * xc, axis=-1, keepdims=True)
    return xc * lax.rsqrt(var + LN_EPS)


def _silu(x):
    return x * jax.nn.sigmoid(x)


def _ada_kernel(c_ref, w_ref, b_ref, o_ref):
    s = _silu(c_ref[...])
    o_ref[0] = jnp.dot(s, w_ref[0], precision=HIGHEST, preferred_element_type=F32) + b_ref[0]


def _ada_mods(cc, w_ada, b_ada):
    depth, _, n = w_ada.shape
    tn = 1536
    return pl.pallas_call(
        _ada_kernel,
        grid=(depth, n // tn),
        in_specs=[pl.BlockSpec((16, D), lambda l, i: (0, 0)),
                  pl.BlockSpec((1, D, tn), lambda l, i: (l, 0, i)),
                  pl.BlockSpec((1, 1, tn), lambda l, i: (l, 0, i))],
        out_specs=pl.BlockSpec((1, 16, tn), lambda l, i: (l, 0, i)),
        out_shape=jax.ShapeDtypeStruct((depth, 16, n), F32),
        compiler_params=_cp(("arbitrary", "arbitrary"), 40 * 1024 * 1024),
        name="ada_mod",
    )(cc, w_ada, b_ada.reshape(depth, 1, n))


def _inproj_kernel(x_ref, mod_ref, w_ref, tbl_ref, o_ref, *, segs):
    x = x_ref[0]
    u = (_ln(x) * (1.0 + mod_ref[0, 0, 1:2, :]) + mod_ref[0, 0, 0:1, :]).astype(BF16)
    lane = lax.broadcasted_iota(jnp.int32, (x.shape[0], 128), 1)
    low = (lane % 32) < 16
    for c0, c1, rope in segs:
        acc = jnp.dot(u, w_ref[:, c0:c1], preferred_element_type=F32)
        if rope is not None:
            partner = jnp.where(low, pltpu.roll(acc, 112, axis=1), pltpu.roll(acc, 16, axis=1))
            acc = acc * tbl_ref[2 * rope] + partner * tbl_ref[2 * rope + 1]
        o_ref[0, :, c0:c1] = acc


def _inproj(h, modsel, w, tbl, segs):
    b, rc, _ = h.shape
    n = w.shape[1]
    nt = rc // TM
    return pl.pallas_call(
        functools.partial(_inproj_kernel, segs=segs),
        grid=(b, nt),
        in_specs=[pl.BlockSpec((1, TM, D), lambda i, j: (i, j, 0)),
                  pl.BlockSpec((1, 1, 6, D), lambda i, j: (i, jnp.minimum(j, 1), 0, 0)),
                  pl.BlockSpec((D, n), lambda i, j: (0, 0)),
                  pl.BlockSpec((4, TM, 128), lambda i, j: (0, j, 0))],
        out_specs=pl.BlockSpec((1, TM, n), lambda i, j: (i, j, 0)),
        out_shape=jax.ShapeDtypeStruct((b, rc, n), F32),
        compiler_params=_cp(("parallel", "arbitrary"), V7X_VMEM_LIMIT),
        name="inproj",
    )(h, modsel, w, tbl)


def _softmax_pv(s, sink_col, v):
    m = jnp.maximum(jnp.max(s, axis=-1, keepdims=True), sink_col)
    p = jnp.exp(s - m)
    den = jnp.sum(p, axis=-1, keepdims=True) + jnp.exp(sink_col - m)
    o = jnp.dot(p.astype(BF16), v, preferred_element_type=F32)
    return o / den


def _attn_kernel(sink_ref, q_ref, kvc_ref, kvp_ref, kvm_ref, kvn_ref, o_ref):
    kvh = pl.program_id(1)
    blk = pl.program_id(2)
    g_per = A_HEADS // A_KV
    nq = g_per * A_BLK
    scale = A_HD ** -0.5
    q = q_ref[0]
    q4 = jnp.concatenate([q[:, A_HD * g:A_HD * (g + 1)] for g in range(g_per)], axis=0)
    q4 = (q4 * scale).astype(BF16)
    rgrp = lax.broadcasted_iota(jnp.int32, (nq, 1), 0) // A_BLK
    sink_col = jnp.zeros((nq, 1), F32)
    for g in range(g_per):
        sink_col = jnp.where(rgrp == g, sink_ref[kvh * g_per + g], sink_col)

    def finish(o):
        o_ref[0] = jnp.concatenate([o[A_BLK * g:A_BLK * (g + 1), :] for g in range(g_per)], axis=1)

    kvc = kvc_ref[0]

    @pl.when(blk < CTX // A_BLK)
    def _():
        k = kvc[:, :A_HD].astype(BF16)
        v = kvc[:, A_HD:].astype(BF16)
        s = lax.dot_general(q4, k, (((1,), (1,)), ((), ())), preferred_element_type=F32)
        finish(_softmax_pv(s, sink_col, v))

    @pl.when(blk >= CTX // A_BLK)
    def _():
        n = blk - CTX // A_BLK
        last = pl.num_programs(2) - CTX // A_BLK - 1
        kv = jnp.concatenate([kvc, kvp_ref[0], kvm_ref[0], kvn_ref[0]], axis=0)
        k = kv[:, :A_HD].astype(BF16)
        v = kv[:, A_HD:].astype(BF16)
        s = lax.dot_general(q4, k, (((1,), (1,)), ((), ())), preferred_element_type=F32)
        nk = CTX + 3 * A_BLK
        qi = lax.broadcasted_iota(jnp.int32, (nq, nk), 0) % A_BLK
        col = lax.broadcasted_iota(jnp.int32, (nq, nk), 1)
        rel = col - (CTX + A_BLK)
        dist = qi - rel
        lo = jnp.where(n > 0, -A_BLK, 0)
        hi = jnp.where(n < last, 2 * A_BLK, A_BLK)
        in_win = (dist <= A_BLK) & (dist >= -A_BLK) & (rel >= lo) & (rel < hi)
        s = jnp.where((col < CTX) | in_win, s, NEG)
        finish(_softmax_pv(s, sink_col, v))


def _attention(proj, sink, kv_blk0):
    b, rc, _ = proj.shape
    nb = rc // A_BLK
    c_b = CTX // A_BLK

    def band(off):
        return lambda i, h, j: (i, jnp.clip(j + off, c_b, nb - 1), kv_blk0 + h)

    return pl.pallas_call(
        _attn_kernel,
        grid=(b, A_KV, nb),
        in_specs=[pl.BlockSpec(memory_space=pltpu.SMEM),
                  pl.BlockSpec((1, A_BLK, 256), lambda i, h, j: (i, j, h)),
                  pl.BlockSpec((1, CTX, 128), lambda i, h, j: (i, 0, kv_blk0 + h)),
                  pl.BlockSpec((1, A_BLK, 128), band(-1)),
                  pl.BlockSpec((1, A_BLK, 128), band(0)),
                  pl.BlockSpec((1, A_BLK, 128), band(1))],
        out_specs=pl.BlockSpec((1, A_BLK, 256), lambda i, h, j: (i, j, h)),
        out_shape=jax.ShapeDtypeStruct((b, rc, A_HEADS * A_HD), F32),
        compiler_params=_cp(("parallel", "arbitrary", "arbitrary")),
        name="win_attn",
    )(sink, proj, proj, proj, proj, proj)


def _scan_block(q, k, v, logf, s_ref, o_ref, *, dvp, reverse):
    rows = q.shape[0]
    nchunk = rows // CHUNK
    r_i = lax.broadcasted_iota(jnp.int32, (CHUNK, CHUNK), 0)
    c_i = lax.broadcasted_iota(jnp.int32, (CHUNK, CHUNK), 1)
    tri = (r_i <= c_i) if reverse else (r_i >= c_i)
    tri_f = tri.astype(F32)
    order = range(nchunk - 1, -1, -1) if reverse else range(nchunk)
    i_tot = 0 if reverse else CHUNK - 1
    i_ref = CHUNK // 2 - 1 if reverse else CHUNK // 2
    for c in order:
        sl = slice(c * CHUNK, (c + 1) * CHUNK)
        bsum = jnp.dot(tri_f, logf[sl], precision=HIGHEST, preferred_element_type=F32)
        tot = bsum[i_tot:i_tot + 1, :]
        ref = bsum[i_ref:i_ref + 1, :]
        e_q = jnp.exp(bsum - ref)
        e_k = jnp.exp(ref - bsum)
        qg = q[sl] * e_q
        kg = k[sl] * e_k
        qb = (qg * jnp.exp(ref)).astype(BF16)
        kd = (kg * jnp.exp(tot - ref)).astype(BF16)
        qg = qg.astype(BF16)
        kg = kg.astype(BF16)
        dec = jnp.exp(tot)
        vv = v[sl].astype(BF16)
        outs = []
        for h in range(R_HEADS):
            ks = slice(h * DKP, (h + 1) * DKP)
            vs = slice(h * dvp, (h + 1) * dvp)
            att = lax.dot_general(qg[:, ks], kg[:, ks], (((1,), (1,)), ((), ())), preferred_element_type=F32)
            att = jnp.where(tri, att, 0.0).astype(BF16)
            st = s_ref[h]
            o_h = jnp.dot(att, vv[:, vs], preferred_element_type=F32)
            o_h = o_h + lax.dot_general(qb[:, ks], st.astype(BF16), (((1,), (1,)), ((), ())),
                                        preferred_element_type=F32)
            ds = lax.dot_general(vv[:, vs], kd[:, ks], (((0,), (0,)), ((), ())), preferred_element_type=F32)
            s_ref[h] = st * dec[:, ks] + ds
            outs.append(o_h)
        o_ref[0, sl, :] = jnp.concatenate(outs, axis=1)


def _hgrn_scan_kernel(lb_ref, qf_ref, vf_ref, zf_ref, qb_ref, vb_ref, zb_ref, of_ref, ob_ref, sf_ref, sb_ref):
    @pl.when(pl.program_id(1) == 0)
    def _():
        sf_ref[...] = jnp.zeros_like(sf_ref)
        sb_ref[...] = jnp.zeros_like(sb_ref)

    def gates(z, lb):
        f = lb + (1.0 - lb) * jax.nn.sigmoid(z)
        return 1.0 - f, jnp.log(f)

    kf, lf = gates(zf_ref[0], lb_ref[0:1, :])
    _scan_block(qf_ref[0], kf, vf_ref[0], lf, sf_ref, of_ref, dvp=128, reverse=False)
    kb, lbw = gates(zb_ref[0], lb_ref[1:2, :])
    _scan_block(qb_ref[0], kb, vb_ref[0], lbw, sb_ref, ob_ref, dvp=128, reverse=True)


def _bwd_blk(j, nt):
    return jnp.where(j == 0, 0, nt - j)


def _hgrn_scan(proj, lb, cq, cv, cf, cb):
    b, rc, _ = proj.shape
    nt = rc // TM
    w = R_HEADS * DKP

    def fw(col):
        return pl.BlockSpec((1, TM, w), lambda i, j: (i, j, col))

    def bw(col):
        return pl.BlockSpec((1, TM, w), lambda i, j: (i, _bwd_blk(j, nt), col))

    o_sd = jax.ShapeDtypeStruct((b, rc, w), F32)
    return pl.pallas_call(
        _hgrn_scan_kernel,
        grid=(b, nt),
        in_specs=[pl.BlockSpec((2, w), lambda i, j: (0, 0)),
                  fw(cq), fw(cv), fw(cf), bw(cq), bw(cv), bw(cb)],
        out_specs=[pl.BlockSpec((1, TM, w), lambda i, j: (i, j, 0)),
                   pl.BlockSpec((1, TM, w), lambda i, j: (i, _bwd_blk(j, nt), 0))],
        out_shape=[o_sd, o_sd],
        scratch_shapes=[pltpu.VMEM((R_HEADS, 128, DKP), F32), pltpu.VMEM((R_HEADS, 128, DKP), F32)],
        compiler_params=_cp(("parallel", "arbitrary"), 40 * 1024 * 1024),
        name="hgrn_scan",
    )(lb, proj, proj, proj, proj, proj, proj)


def _gla_scan_kernel(gwf_ref, gwb_ref, gb_ref, qf_ref, kf_ref, vf_ref, rf_ref, qb_ref, kb_ref, vb_ref, rb_ref,
                     of_ref, ob_ref, sf_ref, sb_ref):
    @pl.when(pl.program_id(1) == 0)
    def _():
        sf_ref[...] = jnp.zeros_like(sf_ref)
        sb_ref[...] = jnp.zeros_like(sb_ref)

    def decay(r, gw, gb):
        zz = jnp.dot(r, gw, precision=HIGHEST, preferred_element_type=F32) + gb
        return (jnp.minimum(zz, 0.0) - jnp.log(1.0 + jnp.exp(-jnp.abs(zz)))) * (1.0 / GLA_TAU)

    qs = GLA_DK ** -0.5
    _scan_block(qf_ref[0] * qs, kf_ref[0], vf_ref[0], decay(rf_ref[0], gwf_ref[...], gb_ref[0:1, :]),
                sf_ref, of_ref, dvp=GLA_DVP, reverse=False)
    _scan_block(qb_ref[0] * qs, kb_ref[0], vb_ref[0], decay(rb_ref[0], gwb_ref[...], gb_ref[1:2, :]),
                sb_ref, ob_ref, dvp=GLA_DVP, reverse=True)


def _gla_scan(proj, gwf, gwb, gb, cq, ck, cv, cr):
    b, rc, _ = proj.shape
    nt = rc // TM
    wk = R_HEADS * DKP
    wv = R_HEADS * GLA_DVP

    def spec(width, col, back):
        if back:
            return pl.BlockSpec((1, TM, width), lambda i, j: (i, _bwd_blk(j, nt), col))
        return pl.BlockSpec((1, TM, width), lambda i, j: (i, j, col))

    o_sd = jax.ShapeDtypeStruct((b, rc, wv), F32)
    full = lambda shape: pl.BlockSpec(shape, lambda i, j: (0,) * len(shape))
    return pl.pallas_call(
        _gla_scan_kernel,
        grid=(b, nt),
        in_specs=[full((128, wk)), full((128, wk)), full((2, wk)),
                  spec(wk, cq, False), spec(wk, ck, False), spec(wv, cv, False), spec(128, cr, False),
                  spec(wk, cq, True), spec(wk, ck, True), spec(wv, cv, True), spec(128, cr, True)],
        out_specs=[spec(wv, 0, False), spec(wv, 0, True)],
        out_shape=[o_sd, o_sd],
        scratch_shapes=[pltpu.VMEM((R_HEADS, GLA_DVP, DKP), F32), pltpu.VMEM((R_HEADS, GLA_DVP, DKP), F32)],
        compiler_params=_cp(("parallel", "arbitrary"), 48 * 1024 * 1024),
        name="gla_scan",
    )(gwf, gwb, gb, proj, proj, proj, proj, proj, proj, proj, proj)


def _fourier_chan_kernel(z_ref, cs_ref, o_ref):
    o_ref[0] = jnp.dot(z_ref[0].astype(BF16), cs_ref[...], preferred_element_type=F32).astype(BF16)


def _fourier_seq_kernel(zz_ref, ct_ref, st_ref, o_ref, *, scale):
    w = F_GROUPS * F_GD
    y = jnp.dot(ct_ref[...], zz_ref[0, :, :w], preferred_element_type=F32)
    y = y - jnp.dot(st_ref[...], zz_ref[0, :, w:], preferred_element_type=F32)
    o_ref[0] = y * scale


def _fourier(proj, zcol, cs, ct, st, t):
    b = proj.shape[0]
    w = F_GROUPS * F_GD
    c_t = CTX // TM
    nzt = t // TM
    zz = pl.pallas_call(
        _fourier_chan_kernel,
        grid=(b, nzt),
        in_specs=[pl.BlockSpec((1, TM, w), lambda i, j: (i, j + c_t, zcol)),
                  pl.BlockSpec((w, 2 * w), lambda i, j: (0, 0))],
        out_specs=pl.BlockSpec((1, TM, 2 * w), lambda i, j: (i, j, 0)),
        out_shape=jax.ShapeDtypeStruct((b, t, 2 * w), BF16),
        compiler_params=_cp(("parallel", "arbitrary")),
        name="fourier_chan",
    )(proj, cs)
    return pl.pallas_call(
        functools.partial(_fourier_seq_kernel, scale=1.0 / math.sqrt(t * F_GD)),
        grid=(b, nzt),
        in_specs=[pl.BlockSpec((1, t, 2 * w), lambda i, j: (i, 0, 0)),
                  pl.BlockSpec((TM, t), lambda i, j: (j, 0)),
                  pl.BlockSpec((TM, t), lambda i, j: (j, 0))],
        out_specs=pl.BlockSpec((1, TM, w), lambda i, j: (i, j, 0)),
        out_shape=jax.ShapeDtypeStruct((b, t, w), F32),
        compiler_params=_cp(("parallel", "arbitrary"), 40 * 1024 * 1024),
        name="fourier_seq",
    )(zz, ct, st)


def _route(logits, rb_ref):
    lt = logits.T
    l = [lt[e:e + 1, :] for e in range(N_EXP)]
    m = functools.reduce(jnp.maximum, l)
    ex = [jnp.exp(v - m) for v in l]
    den = functools.reduce(lambda a, c: a + c, ex)
    p = [v / den for v in ex]
    sel = [p[e] + rb_ref[e] for e in range(N_EXP)]
    g_score, g_gate = [], []
    for g in range(N_GRP):
        s = sel[E_PER * g:E_PER * (g + 1)]
        chosen = []
        for i in range(E_PER):
            rank = jnp.zeros_like(s[i])
            for j in range(E_PER):
                if j == i:
                    continue
                ahead = (s[j] >= s[i]) if j < i else (s[j] > s[i])
                rank = rank + jnp.where(ahead, 1.0, 0.0)
            chosen.append(rank < 2.0)
        g_score.append(functools.reduce(lambda a, c: a + c,
                                        [jnp.where(chosen[i], s[i], 0.0) for i in range(E_PER)]))
        g_gate.append([jnp.where(chosen[i], p[E_PER * g + i], 0.0) for i in range(E_PER)])
    best = jnp.zeros_like(g_score[0])
    best_s = g_score[0]
    for g in range(1, N_GRP):
        upd = g_score[g] > best_s
        best = jnp.where(upd, float(g), best)
        best_s = jnp.where(upd, g_score[g], best_s)
    gate = []
    for i in range(E_PER):
        gi = g_gate[0][i]
        for g in range(1, N_GRP):
            gi = jnp.where(best == float(g), g_gate[g][i], gi)
        gate.append(gi)
    wsum = functools.reduce(lambda a, c: a + c, gate)
    row = lax.broadcasted_iota(jnp.int32, (GW, lt.shape[1]), 0)
    side = jnp.where(row == E_PER, best, 0.0)
    for i in range(E_PER):
        side = jnp.where(row == i, gate[i] / wsum, side)
    return side.T


def _outproj_kernel(rb_ref, a_ref, of_ref, ob_ref, gt_ref, h_ref, mod_ref, gn_ref, wa_ref, wr_ref, lng_ref, lnb_ref,
                    wrt_ref, h1_ref, u2_ref, *, dvp, dv):
    o = of_ref[0] + ob_ref[0]
    gt = gt_ref[0]
    parts = []
    for hh in range(R_HEADS):
        sl = slice(hh * dvp, (hh + 1) * dvp)
        oh = o[:, sl]
        ms = jnp.sum(oh * oh, axis=-1, keepdims=True) * (1.0 / dv)
        parts.append(oh * lax.rsqrt(ms + LN_EPS))
    rec = jnp.concatenate(parts, axis=1) * gn_ref[...] * _silu(gt)
    y = jnp.dot(a_ref[0].astype(BF16), wa_ref[...], preferred_element_type=F32)
    y = y + jnp.dot(rec.astype(BF16), wr_ref[...], preferred_element_type=F32)
    g1 = mod_ref[0, 0, 2:3, :]
    h1 = _ln(ALPHA * h_ref[0] + g1 * y) * lng_ref[0:1, :] + lnb_ref[0:1, :]
    h1_ref[0] = h1
    u2 = _ln(h1) * (1.0 + mod_ref[0, 0, 4:5, :]) + mod_ref[0, 0, 3:4, :]
    u2_ref[0, :, :D] = u2
    logits = jnp.dot(u2, wrt_ref[...], precision=HIGHEST, preferred_element_type=F32)
    u2_ref[0, :, D:] = _route(logits, rb_ref)


def _outproj(b_router, a, a_col, a_w, o_f, o_b, proj, g_col, h, modsel, gnorm, wa, wr, lng, lnb, wrt, *,
             dv, dvp, row_off):
    b, rc, _ = h.shape
    nt = rc // TM - row_off
    wrec = R_HEADS * dvp
    full = lambda shape: pl.BlockSpec(shape, lambda i, j: (0,) * len(shape))
    return pl.pallas_call(
        functools.partial(_outproj_kernel, dvp=dvp, dv=dv),
        grid=(b, nt),
        in_specs=[pl.BlockSpec(memory_space=pltpu.SMEM),
                  pl.BlockSpec((1, TM, a_w), lambda i, j: (i, j + (row_off if a.shape[1] == rc else 0), a_col)),
                  pl.BlockSpec((1, TM, wrec), lambda i, j: (i, j + row_off, 0)),
                  pl.BlockSpec((1, TM, wrec), lambda i, j: (i, j + row_off, 0)),
                  pl.BlockSpec((1, TM, wrec), lambda i, j: (i, j + row_off, g_col)),
                  pl.BlockSpec((1, TM, D), lambda i, j: (i, j + row_off, 0)),
                  pl.BlockSpec((1, 1, 6, D), lambda i, j: (i, jnp.minimum(j + row_off, 1), 0, 0)),
                  full((1, wrec)), full((a_w, D)), full((wrec, D)), full((1, D)), full((1, D)), full((D, GW))],
        out_specs=[pl.BlockSpec((1, TM, D), lambda i, j: (i, j, 0)),
                   pl.BlockSpec((1, TM, D + GW), lambda i, j: (i, j, 0))],
        out_shape=[jax.ShapeDtypeStruct((b, nt * TM, D), F32),
                   jax.ShapeDtypeStruct((b, nt * TM, D + GW), F32)],
        compiler_params=_cp(("parallel", "arbitrary"), V7X_VMEM_LIMIT),
        name="outproj",
    )(b_router, a, o_f, o_b, proj, h, modsel, gnorm, wa, wr, lng, lnb, wrt)


def _gather_rows(idx_ref, base, src_ref, dst_ref, sem):
    n = dst_ref.shape[0]

    def body(r, carry):
        i = idx_ref[base + r]
        pltpu.make_async_copy(src_ref.at[pl.ds(i, 1), :], dst_ref.at[pl.ds(r, 1), :], sem).start()
        return carry

    lax.fori_loop(0, n, body, 0)
    pltpu.make_async_copy(src_ref.at[pl.ds(0, n), :], dst_ref, sem).wait()


def _moe_kernel(tg_ref, src_ref, x_hbm, w1_ref, w3_ref, w2_ref, o_ref, xb_ref, sem):
    i = pl.program_id(0)
    _gather_rows(src_ref, i * TM, x_hbm, xb_ref, sem)
    x = xb_ref[:, :D].astype(BF16)
    side = xb_ref[:, D:]
    y = jnp.zeros((TM, D), F32)
    for e in range(E_PER):
        h1 = jnp.dot(x, w1_ref[e], preferred_element_type=F32)
        h3 = jnp.dot(x, w3_ref[e], preferred_element_type=F32)
        hh = (_silu(h1) * h3 * side[:, e:e + 1]).astype(BF16)
        y = y + jnp.dot(hh, w2_ref[e], preferred_element_type=F32)
    o_ref[...] = y


def _moe(tile_grp, src_idx, u2, w1, w3, w2):
    mp = src_idx.shape[0]
    wspec = lambda k, n: pl.BlockSpec((E_PER, k, n), lambda i, tg, si: (tg[i], 0, 0))
    return pl.pallas_call(
        _moe_kernel,
        grid_spec=pltpu.PrefetchScalarGridSpec(
            num_scalar_prefetch=2, grid=(mp // TM,),
            in_specs=[pl.BlockSpec(memory_space=pl.ANY), wspec(D, D_FF), wspec(D, D_FF), wspec(D_FF, D)],
            out_specs=pl.BlockSpec((TM, D), lambda i, tg, si: (i, 0)),
            scratch_shapes=[pltpu.VMEM((TM, D + GW), F32), pltpu.SemaphoreType.DMA(())]),
        out_shape=jax.ShapeDtypeStruct((mp, D), F32),
        compiler_params=_cp(("arbitrary",), V7X_VMEM_LIMIT),
        name="moe",
    )(tile_grp, src_idx, u2, w1, w3, w2)


def _combine_kernel(pos_ref, y_hbm, h_ref, mod_ref, lng_ref, lnb_ref, o_ref, yb_ref, sem, *, nt):
    t = pl.program_id(0) * nt + pl.program_id(1)
    _gather_rows(pos_ref, t * TM, y_hbm, yb_ref, sem)
    g2 = mod_ref[0, 0, 5:6, :]
    o_ref[0] = _ln(ALPHA * h_ref[0] + g2 * yb_ref[...]) * lng_ref[0:1, :] + lnb_ref[0:1, :]


def _combine(pos, y_sorted, h1, modsel, lng, lnb, row_off):
    b, r, _ = h1.shape
    nt = r // TM
    full = lambda shape: pl.BlockSpec(shape, lambda i, j, p: (0,) * len(shape))
    return pl.pallas_call(
        functools.partial(_combine_kernel, nt=nt),
        grid_spec=pltpu.PrefetchScalarGridSpec(
            num_scalar_prefetch=1, grid=(b, nt),
            in_specs=[pl.BlockSpec(memory_space=pl.ANY),
                      pl.BlockSpec((1, TM, D), lambda i, j, p: (i, j, 0)),
                      pl.BlockSpec((1, 1, 6, D), lambda i, j, p: (i, jnp.minimum(j + row_off, 1), 0, 0)),
                      full((1, D)), full((1, D))],
            out_specs=pl.BlockSpec((1, TM, D), lambda i, j, p: (i, j, 0)),
            scratch_shapes=[pltpu.VMEM((TM, D), F32), pltpu.SemaphoreType.DMA(())]),
        out_shape=jax.ShapeDtypeStruct((b, r, D), F32),
        compiler_params=_cp(("arbitrary", "arbitrary")),
        name="moe_combine",
    )(pos, y_sorted, h1, modsel, lng, lnb)


def _sort_by_group(grp):
    n = grp.shape[0]
    onehot = (grp[:, None] == jnp.arange(N_GRP, dtype=jnp.int32)[None, :]).astype(jnp.int32)
    csum = jnp.cumsum(onehot, axis=0)
    counts = csum[-1]
    padded = ((counts + TM - 1) // TM) * TM
    ends = jnp.cumsum(padded)
    starts = ends - padded
    rank = jnp.sum(csum * onehot, axis=1) - 1
    pos = (starts[grp] + rank).astype(jnp.int32)
    mp = n + N_GRP * TM
    src_idx = jnp.zeros((mp,), jnp.int32).at[pos].set(jnp.arange(n, dtype=jnp.int32))
    tile_start = jnp.arange(mp // TM, dtype=jnp.int32) * TM
    tile_grp = jnp.minimum(jnp.sum((tile_start[:, None] >= ends[None, :]).astype(jnp.int32), axis=1), N_GRP - 1)
    return pos, src_idx, tile_grp.astype(jnp.int32)


def _moe_layer(u2ext, h1, modsel, w1, w3, w2, lng, lnb, row_off):
    b, r, _ = h1.shape
    u2 = u2ext.reshape(b * r, D + GW)
    grp = u2[:, D + E_PER].astype(jnp.int32)
    pos, src_idx, tile_grp = _sort_by_group(grp)
    y_sorted = _moe(tile_grp, src_idx, u2, w1.astype(BF16), w3.astype(BF16), w2.astype(BF16))
    return _combine(pos, y_sorted, h1, modsel, lng, lnb, row_off)


def _rope_tables(rc, t):
    half = A_HD // 4
    inv = ROPE_BASE ** (-jnp.arange(half, dtype=F32) / half)
    tpos = jnp.arange(t)
    rows = (tpos // GRID_W).astype(F32)
    cols = (tpos % GRID_W).astype(F32)

    def cs(pos):
        ang = pos[:, None] * inv[None, :]
        c, s = jnp.cos(ang), jnp.sin(ang)
        return jnp.concatenate([c, c], -1), jnp.concatenate([-s, s], -1)

    cr, sr = cs(rows)
    cc_, sc_ = cs(cols)
    cos_h = jnp.concatenate([cr, cc_], -1)
    sin_h = jnp.concatenate([sr, sc_], -1)
    one = jnp.ones((t, A_HD), F32)
    zero = jnp.zeros((t, A_HD), F32)
    lat = jnp.stack([jnp.concatenate([cos_h, cos_h], -1), jnp.concatenate([sin_h, sin_h], -1),
                     jnp.concatenate([cos_h, one], -1), jnp.concatenate([sin_h, zero], -1)])
    ctx = jnp.stack([jnp.ones((CTX, 128), F32), jnp.zeros((CTX, 128), F32)] * 2)
    return jnp.concatenate([ctx, lat], axis=1)


def _dft_tables(t):
    k = jnp.arange(t, dtype=jnp.int32)
    ang_t = (2.0 * math.pi / t) * ((k[:, None] * k[None, :]) % t).astype(F32)
    c = jnp.arange(F_GD, dtype=jnp.int32)
    ang_c = (2.0 * math.pi / F_GD) * ((c[:, None] * c[None, :]) % F_GD).astype(F32)
    eye = jnp.eye(F_GROUPS, dtype=F32)
    cs = jnp.concatenate([jnp.kron(eye, jnp.cos(ang_c)), jnp.kron(eye, jnp.sin(ang_c))], axis=1)
    return cs.astype(BF16), jnp.cos(ang_t).astype(BF16), jnp.sin(ang_t).astype(BF16)


def _pad_heads(w, width, padded):
    lead = w.shape[:-1]
    w = w.reshape(lead + (R_HEADS, width))
    w = jnp.pad(w, [(0, 0)] * len(lead) + [(0, 0), (0, padded - width)])
    return w.reshape(lead + (R_HEADS * padded,))


EVEN_SEGS = tuple([(128 * g, 128 * (g + 1), 0) for g in range(4)] + [(512, 3072, None)]
                  + [(3072 + 128 * g, 3072 + 128 * (g + 1), 1) for g in range(2)])
ODD_W = 512 + 512 + 1024 + 1024 + 256 + 128
ODD_SEGS = ((0, 1024, None), (1024, 2048, None), (2048, 3072, None), (3072, ODD_W, None))


def _layer_even(h, modsel, tbl, w_in, sink, lb_logits, g_norm, w_out, lng, lnb, w_router, b_router, w1, w3, w2):
    aq, ak, av, ff, fb, hq, hi, hg = jnp.split(w_in, np.cumsum([512, 128, 128, 512, 512, 512, 512]).tolist(), axis=1)
    kv = [jnp.concatenate([ak[:, A_HD * j:A_HD * (j + 1)], av[:, A_HD * j:A_HD * (j + 1)]], axis=1)
          for j in range(A_KV)]
    w = jnp.concatenate([aq, ff, fb, hq, hi, hg] + kv, axis=1).astype(BF16)
    proj = _inproj(h, modsel, w, tbl, EVEN_SEGS)
    o_att = _attention(proj, sink, 3072 // 128)
    lb = jnp.cumsum(jax.nn.softmax(lb_logits.astype(F32), axis=1), axis=1)[:, 0]
    o_f, o_b = _hgrn_scan(proj, lb, 3, 4, 1, 2)
    wo = w_out.astype(BF16)
    h1, u2 = _outproj(b_router, o_att, 0, 512, o_f, o_b, proj, 5, h, modsel, g_norm.reshape(1, -1),
                      wo[:512], wo[512:], lng[0:1], lnb[0:1], w_router, dv=128, dvp=128, row_off=0)
    return _moe_layer(u2, h1, modsel, w1, w3, w2, lng[1:2], lnb[1:2], 0)


def _layer_odd(h, modsel, tbl, dft, w_in, gate_w, gate_b, g_norm, w_out, lng, lnb, w_router, b_router, w1, w3, w2, t):
    q, k, v, rf, rb, g, z = jnp.split(w_in, np.cumsum([384, 384, 768, 16, 16, 768]).tolist(), axis=1)
    r = jnp.pad(jnp.concatenate([rf, rb], axis=1), ((0, 0), (0, 96)))
    w = jnp.concatenate([_pad_heads(q, GLA_DK, DKP), _pad_heads(k, GLA_DK, DKP), _pad_heads(v, GLA_DV, GLA_DVP),
                         _pad_heads(g, GLA_DV, GLA_DVP), z, r], axis=1).astype(BF16)
    proj = _inproj(h, modsel, w, tbl, ODD_SEGS)
    gw = _pad_heads(gate_w, GLA_DK, DKP)
    gwf = jnp.pad(gw[0], ((0, 112), (0, 0)))
    gwb = jnp.pad(gw[1], ((16, 96), (0, 0)))
    gb = _pad_heads(gate_b, GLA_DK, DKP)
    o_f, o_b = _gla_scan(proj, gwf, gwb, gb, 0, 1, 1, 3328 // 128)
    cs, ct, st = dft
    four = _fourier(proj, 3072 // (F_GROUPS * F_GD), cs, ct, st, t)
    wo = w_out
    wr = jnp.pad(wo[:768].reshape(R_HEADS, GLA_DV, D), ((0, 0), (0, GLA_DVP - GLA_DV), (0, 0))).reshape(-1, D)
    h1, u2 = _outproj(b_router, four, 0, 256, o_f, o_b, proj, 2, h, modsel,
                      _pad_heads(g_norm, GLA_DV, GLA_DVP).reshape(1, -1),
                      wo[768:].astype(BF16), wr.astype(BF16), lng[0:1], lnb[0:1], w_router,
                      dv=GLA_DV, dvp=GLA_DVP, row_off=CTX // TM)
    return _moe_layer(u2, h1, modsel, w1, w3, w2, lng[1:2], lnb[1:2], CTX // TM)


def kernel(x, c, ctx, c_ctx, w_ada, b_ada, ln_g, ln_b, w_in_even, attn_sink, hgrn_lb_logits, hgrn_norm, w_out_even, w_in_odd, gla_gate_w, gla_gate_b, gla_norm, w_out_odd, w_router, b_router, w_expert_gate, w_expert_up, w_expert_down):
    b, t, _ = x.shape
    rc = CTX + t
    h = jnp.concatenate([ctx, x], axis=1)
    cc = jnp.zeros((16, D), F32).at[:b].set(c).at[b].set(c_ctx)
    mods = _ada_mods(cc, w_ada, b_ada).reshape(2, 16, 6, D)

    def modsel(l):
        return jnp.stack([jnp.broadcast_to(mods[l, b], (b, 6, D)), mods[l, :b]], axis=1)

    tbl = _rope_tables(rc, t)
    wrt = jnp.pad(w_router, ((0, 0), (0, GW - N_EXP)))
    h = _layer_even(h, modsel(0), tbl, w_in_even[0], attn_sink[0], hgrn_lb_logits, hgrn_norm[0], w_out_even[0],
                    ln_g[0], ln_b[0], wrt, b_router, w_expert_gate[0], w_expert_up[0], w_expert_down[0])
    return _layer_odd(h, modsel(1), tbl, _dft_tables(t), w_in_odd[0], gla_gate_w[0], gla_gate_b[0], gla_norm[0],
                      w_out_odd[0], ln_g[1], ln_b[1], wrt, b_router, w_expert_gate[1], w_expert_up[1],
                      w_expert_down[1], t)
```

```python
import functools
import math

import numpy as np
import jax
import jax.numpy as jnp
from jax import lax
from jax.experimental import pallas as pl
from jax.experimental.pallas import tpu as pltpu

F32 = jnp.float32
BF16 = jnp.bfloat16
HIGHEST = lax.Precision.HIGHEST

D = 1024
CTX = 256
GRID_W = 64
LN_EPS = 1e-5
NEG = -1e30
ALPHA = 4.0 ** 0.25
ROPE_BASE = 10000.0

A_HEADS, A_KV, A_HD, A_BLK = 8, 2, 64, 128
R_HEADS = 4
DKP = 128
GLA_DK, GLA_DV, GLA_DVP = 96, 192, 256
GLA_TAU = 16.0
CHUNK = 64
N_EXP, N_GRP, E_PER = 16, 4, 4
D_FF = 512
F_GROUPS, F_GD = 4, 64

TM = 256
GW = 128
XROWS = 16
YROWS = 8
V7X_VMEM_LIMIT = 56 * 1024 * 1024


def _cp(sem, vmem=None):
    return pltpu.CompilerParams(dimension_semantics=sem, vmem_limit_bytes=vmem)


def _ln(x):
    mu = jnp.mean(x, axis=-1, keepdims=True)
    xc = x - mu
    var = jnp.mean(xc * xc, axis=-1, keepdims=True)
    return xc * lax.rsqrt(var + LN_EPS)


def _silu(x):
    return x * jax.nn.sigmoid(x)


def _ada_kernel(c_ref, w_ref, b_ref, o_ref):
    s = _silu(c_ref[...])
    o_ref[0] = jnp.dot(s, w_ref[0], precision=HIGHEST, preferred_element_type=F32) + b_ref[0]


def _ada_mods(cc, w_ada, b_ada):
    depth, _, n = w_ada.shape
    tn = 1536
    return pl.pallas_call(
        _ada_kernel,
        grid=(depth, n // tn),
        in_specs=[pl.BlockSpec((16, D), lambda l, i: (0, 0)),
                  pl.BlockSpec((1, D, tn), lambda l, i: (l, 0, i)),
                  pl.BlockSpec((1, 1, tn), lambda l, i: (l, 0, i))],
        out_specs=pl.BlockSpec((1, 16, tn), lambda l, i: (l, 0, i)),
        out_shape=jax.ShapeDtypeStruct((depth, 16, n), F32),
        compiler_params=_cp(("arbitrary", "arbitrary"), 40 * 1024 * 1024),
        name="ada_mod",
    )(cc, w_ada, b_ada.reshape(depth, 1, n))


def _inproj_kernel(x_ref, mod_ref, w_ref, tbl_ref, o_ref, *, segs):
    x = x_ref[0]
    u = (_ln(x) * (1.0 + mod_ref[0, 0, 1:2, :]) + mod_ref[0, 0, 0:1, :]).astype(BF16)
    lane = lax.broadcasted_iota(jnp.int32, (x.shape[0], 128), 1)
    low = (lane % 32) < 16
    for c0, c1, rope in segs:
        acc = jnp.dot(u, w_ref[:, c0:c1], preferred_element_type=F32)
        if rope is not None:
            partner = jnp.where(low, pltpu.roll(acc, 112, axis=1), pltpu.roll(acc, 16, axis=1))
            acc = acc * tbl_ref[2 * rope] + partner * tbl_ref[2 * rope + 1]
        o_ref[0, :, c0:c1] = acc.astype(BF16)


def _inproj(h, modsel, w, tbl, segs):
    b, rc, _ = h.shape
    n = w.shape[1]
    nt = rc // TM
    return pl.pallas_call(
        functools.partial(_inproj_kernel, segs=segs),
        grid=(b, nt),
        in_specs=[pl.BlockSpec((1, TM, D), lambda i, j: (i, j, 0)),
                  pl.BlockSpec((1, 1, 6, D), lambda i, j: (i, jnp.minimum(j, 1), 0, 0)),
                  pl.BlockSpec((D, n), lambda i, j: (0, 0)),
                  pl.BlockSpec((4, TM, 128), lambda i, j: (0, j, 0))],
        out_specs=pl.BlockSpec((1, TM, n), lambda i, j: (i, j, 0)),
        out_shape=jax.ShapeDtypeStruct((b, rc, n), BF16),
        compiler_params=_cp(("parallel", "arbitrary"), V7X_VMEM_LIMIT),
        name="inproj",
    )(h, modsel, w, tbl)


def _softmax_pv(s, sink_col, v):
    m = jnp.maximum(jnp.max(s, axis=-1, keepdims=True), sink_col)
    p = jnp.exp(s - m)
    den = jnp.sum(p, axis=-1, keepdims=True) + jnp.exp(sink_col - m)
    o = jnp.dot(p.astype(BF16), v, preferred_element_type=F32)
    return o / den


def _attn_kernel(sink_ref, q_ref, kvc_ref, kvp_ref, kvm_ref, kvn_ref, o_ref):
    kvh = pl.program_id(1)
    blk = pl.program_id(2)
    g_per = A_HEADS // A_KV
    nq = g_per * A_BLK
    scale = A_HD ** -0.5
    q = q_ref[0] * scale
    q4 = jnp.concatenate([q[:, A_HD * g:A_HD * (g + 1)] for g in range(g_per)], axis=0)
    rgrp = lax.broadcasted_iota(jnp.int32, (nq, 1), 0) // A_BLK
    sink_col = jnp.zeros((nq, 1), F32)
    for g in range(g_per):
        sink_col = jnp.where(rgrp == g, sink_ref[kvh * g_per + g], sink_col)

    def finish(o):
        o_ref[0] = jnp.concatenate([o[A_BLK * g:A_BLK * (g + 1), :] for g in range(g_per)], axis=1).astype(BF16)

    kvc = kvc_ref[0]

    @pl.when(blk < CTX // A_BLK)
    def _():
        s = lax.dot_general(q4, kvc[:, :A_HD], (((1,), (1,)), ((), ())), preferred_element_type=F32)
        finish(_softmax_pv(s, sink_col, kvc[:, A_HD:]))

    @pl.when(blk >= CTX // A_BLK)
    def _():
        n = blk - CTX // A_BLK
        last = pl.num_programs(2) - CTX // A_BLK - 1
        kv = jnp.concatenate([kvc, kvp_ref[0], kvm_ref[0], kvn_ref[0]], axis=0)
        s = lax.dot_general(q4, kv[:, :A_HD], (((1,), (1,)), ((), ())), preferred_element_type=F32)
        nk = CTX + 3 * A_BLK
        qi = lax.broadcasted_iota(jnp.int32, (A_BLK, nk), 0)
        col = lax.broadcasted_iota(jnp.int32, (A_BLK, nk), 1)
        rel = col - (CTX + A_BLK)
        dist = qi - rel
        lo = jnp.where(n > 0, -A_BLK, 0)
        hi = jnp.where(n < last, 2 * A_BLK, A_BLK)
        in_win = (dist <= A_BLK) & (dist >= -A_BLK) & (rel >= lo) & (rel < hi)
        bias = jnp.where((col < CTX) | in_win, 0.0, NEG)
        s = s + jnp.concatenate([bias] * g_per, axis=0)
        finish(_softmax_pv(s, sink_col, kv[:, A_HD:]))


def _attention(proj, sink, kv_blk0):
    b, rc, _ = proj.shape
    nb = rc // A_BLK
    c_b = CTX // A_BLK

    def band(off):
        return lambda i, h, j: (i, jnp.clip(j + off, c_b, nb - 1), kv_blk0 + h)

    return pl.pallas_call(
        _attn_kernel,
        grid=(b, A_KV, nb),
        in_specs=[pl.BlockSpec(memory_space=pltpu.SMEM),
                  pl.BlockSpec((1, A_BLK, 256), lambda i, h, j: (i, j, h)),
                  pl.BlockSpec((1, CTX, 128), lambda i, h, j: (i, 0, kv_blk0 + h)),
                  pl.BlockSpec((1, A_BLK, 128), band(-1)),
                  pl.BlockSpec((1, A_BLK, 128), band(0)),
                  pl.BlockSpec((1, A_BLK, 128), band(1))],
        out_specs=pl.BlockSpec((1, A_BLK, 256), lambda i, h, j: (i, j, h)),
        out_shape=jax.ShapeDtypeStruct((b, rc, A_HEADS * A_HD), BF16),
        compiler_params=_cp(("parallel", "arbitrary", "arbitrary")),
        name="win_attn",
    )(sink, proj, proj, proj, proj, proj)


def _scan_block(q, k, v, logf, s_ref, o_ref, *, dvp, reverse):
    rows = q.shape[0]
    nchunk = rows // CHUNK
    r_i = lax.broadcasted_iota(jnp.int32, (CHUNK, CHUNK), 0)
    c_i = lax.broadcasted_iota(jnp.int32, (CHUNK, CHUNK), 1)
    tri = (r_i <= c_i) if reverse else (r_i >= c_i)
    tri_f = tri.astype(F32)
    order = range(nchunk - 1, -1, -1) if reverse else range(nchunk)
    i_tot = 0 if reverse else CHUNK - 1
    i_ref = CHUNK // 2 - 1 if reverse else CHUNK // 2
    for c in order:
        sl = slice(c * CHUNK, (c + 1) * CHUNK)
        bsum = jnp.dot(tri_f, logf[sl], precision=HIGHEST, preferred_element_type=F32)
        tot = bsum[i_tot:i_tot + 1, :]
        ref = bsum[i_ref:i_ref + 1, :]
        e_q = jnp.exp(bsum - ref)
        e_k = jnp.exp(ref - bsum)
        qg = q[sl] * e_q
        kg = k[sl] * e_k
        qb = (qg * jnp.exp(ref)).astype(BF16)
        kd = (kg * jnp.exp(tot - ref)).astype(BF16)
        qg = qg.astype(BF16)
        kg = kg.astype(BF16)
        dec = jnp.exp(tot)
        vv = v[sl].astype(BF16)
        outs = []
        for h in range(R_HEADS):
            ks = slice(h * DKP, (h + 1) * DKP)
            vs = slice(h * dvp, (h + 1) * dvp)
            att = lax.dot_general(qg[:, ks], kg[:, ks], (((1,), (1,)), ((), ())), preferred_element_type=F32)
            att = jnp.where(tri, att, 0.0).astype(BF16)
            st = s_ref[h]
            o_h = jnp.dot(att, vv[:, vs], preferred_element_type=F32)
            o_h = o_h + lax.dot_general(qb[:, ks], st.astype(BF16), (((1,), (1,)), ((), ())),
                                        preferred_element_type=F32)
            ds = lax.dot_general(vv[:, vs], kd[:, ks], (((0,), (0,)), ((), ())), preferred_element_type=F32)
            s_ref[h] = st * dec[:, ks] + ds
            outs.append(o_h)
        o_ref[0, sl, :] = jnp.concatenate(outs, axis=1).astype(BF16)


def _hgrn_scan_kernel(lb_ref, qf_ref, vf_ref, zf_ref, qb_ref, vb_ref, zb_ref, of_ref, ob_ref, sf_ref, sb_ref):
    @pl.when(pl.program_id(1) == 0)
    def _():
        sf_ref[...] = jnp.zeros_like(sf_ref)
        sb_ref[...] = jnp.zeros_like(sb_ref)

    def gates(z, lb):
        f = lb + (1.0 - lb) * jax.nn.sigmoid(z.astype(F32))
        return 1.0 - f, jnp.log(f)

    kf, lf = gates(zf_ref[0], lb_ref[0:1, :])
    _scan_block(qf_ref[0], kf, vf_ref[0], lf, sf_ref, of_ref, dvp=128, reverse=False)
    kb, lbw = gates(zb_ref[0], lb_ref[1:2, :])
    _scan_block(qb_ref[0], kb, vb_ref[0], lbw, sb_ref, ob_ref, dvp=128, reverse=True)


def _bwd_blk(j, nt):
    return jnp.where(j == 0, 0, nt - j)


def _hgrn_scan(proj, lb, cq, cv, cf, cb):
    b, rc, _ = proj.shape
    nt = rc // TM
    w = R_HEADS * DKP

    def fw(col):
        return pl.BlockSpec((1, TM, w), lambda i, j: (i, j, col))

    def bw(col):
        return pl.BlockSpec((1, TM, w), lambda i, j: (i, _bwd_blk(j, nt), col))

    o_sd = jax.ShapeDtypeStruct((b, rc, w), BF16)
    return pl.pallas_call(
        _hgrn_scan_kernel,
        grid=(b, nt),
        in_specs=[pl.BlockSpec((2, w), lambda i, j: (0, 0)),
                  fw(cq), fw(cv), fw(cf), bw(cq), bw(cv), bw(cb)],
        out_specs=[pl.BlockSpec((1, TM, w), lambda i, j: (i, j, 0)),
                   pl.BlockSpec((1, TM, w), lambda i, j: (i, _bwd_blk(j, nt), 0))],
        out_shape=[o_sd, o_sd],
        scratch_shapes=[pltpu.VMEM((R_HEADS, 128, DKP), F32), pltpu.VMEM((R_HEADS, 128, DKP), F32)],
        compiler_params=_cp(("parallel", "arbitrary"), 40 * 1024 * 1024),
        name="hgrn_scan",
    )(lb, proj, proj, proj, proj, proj, proj)


def _gla_scan_kernel(gwf_ref, gwb_ref, gb_ref, qf_ref, kf_ref, vf_ref, rf_ref, qb_ref, kb_ref, vb_ref, rb_ref,
                     of_ref, ob_ref, sf_ref, sb_ref):
    @pl.when(pl.program_id(1) == 0)
    def _():
        sf_ref[...] = jnp.zeros_like(sf_ref)
        sb_ref[...] = jnp.zeros_like(sb_ref)

    def decay(r, gw, gb):
        zz = jnp.dot(r.astype(F32), gw, precision=HIGHEST, preferred_element_type=F32) + gb
        return (jnp.minimum(zz, 0.0) - jnp.log(1.0 + jnp.exp(-jnp.abs(zz)))) * (1.0 / GLA_TAU)

    qs = GLA_DK ** -0.5
    _scan_block(qf_ref[0].astype(F32) * qs, kf_ref[0], vf_ref[0], decay(rf_ref[0], gwf_ref[...], gb_ref[0:1, :]),
                sf_ref, of_ref, dvp=GLA_DVP, reverse=False)
    _scan_block(qb_ref[0].astype(F32) * qs, kb_ref[0], vb_ref[0], decay(rb_ref[0], gwb_ref[...], gb_ref[1:2, :]),
                sb_ref, ob_ref, dvp=GLA_DVP, reverse=True)


def _gla_scan(proj, gwf, gwb, gb, cq, ck, cv, cr):
    b, rc, _ = proj.shape
    nt = rc // TM
    wk = R_HEADS * DKP
    wv = R_HEADS * GLA_DVP

    def spec(width, col, back):
        if back:
            return pl.BlockSpec((1, TM, width), lambda i, j: (i, _bwd_blk(j, nt), col))
        return pl.BlockSpec((1, TM, width), lambda i, j: (i, j, col))

    o_sd = jax.ShapeDtypeStruct((b, rc, wv), BF16)
    full = lambda shape: pl.BlockSpec(shape, lambda i, j: (0,) * len(shape))
    return pl.pallas_call(
        _gla_scan_kernel,
        grid=(b, nt),
        in_specs=[full((128, wk)), full((128, wk)), full((2, wk)),
                  spec(wk, cq, False), spec(wk, ck, False), spec(wv, cv, False), spec(128, cr, False),
                  spec(wk, cq, True), spec(wk, ck, True), spec(wv, cv, True), spec(128, cr, True)],
        out_specs=[spec(wv, 0, False), spec(wv, 0, True)],
        out_shape=[o_sd, o_sd],
        scratch_shapes=[pltpu.VMEM((R_HEADS, GLA_DVP, DKP), F32), pltpu.VMEM((R_HEADS, GLA_DVP, DKP), F32)],
        compiler_params=_cp(("parallel", "arbitrary"), 48 * 1024 * 1024),
        name="gla_scan",
    )(gwf, gwb, gb, proj, proj, proj, proj, proj, proj, proj, proj)


def _fourier_chan_kernel(z_ref, cs_ref, o_ref):
    o_ref[0] = jnp.dot(z_ref[0], cs_ref[...], preferred_element_type=F32).astype(BF16)


def _fourier_seq_kernel(zz_ref, ct_ref, st_ref, o_ref, *, scale):
    w = F_GROUPS * F_GD
    y = jnp.dot(ct_ref[...], zz_ref[0, :, :w], preferred_element_type=F32)
    y = y - jnp.dot(st_ref[...], zz_ref[0, :, w:], preferred_element_type=F32)
    o_ref[0] = (y * scale).astype(BF16)


def _fourier(proj, zcol, cs, ct, st, t):
    b = proj.shape[0]
    w = F_GROUPS * F_GD
    c_t = CTX // TM
    nzt = t // TM
    zz = pl.pallas_call(
        _fourier_chan_kernel,
        grid=(b, nzt),
        in_specs=[pl.BlockSpec((1, TM, w), lambda i, j: (i, j + c_t, zcol)),
                  pl.BlockSpec((w, 2 * w), lambda i, j: (0, 0))],
        out_specs=pl.BlockSpec((1, TM, 2 * w), lambda i, j: (i, j, 0)),
        out_shape=jax.ShapeDtypeStruct((b, t, 2 * w), BF16),
        compiler_params=_cp(("parallel", "arbitrary")),
        name="fourier_chan",
    )(proj, cs)
    return pl.pallas_call(
        functools.partial(_fourier_seq_kernel, scale=1.0 / math.sqrt(t * F_GD)),
        grid=(b, nzt),
        in_specs=[pl.BlockSpec((1, t, 2 * w), lambda i, j: (i, 0, 0)),
                  pl.BlockSpec((TM, t), lambda i, j: (j, 0)),
                  pl.BlockSpec((TM, t), lambda i, j: (j, 0))],
        out_specs=pl.BlockSpec((1, TM, w), lambda i, j: (i, j, 0)),
        out_shape=jax.ShapeDtypeStruct((b, t, w), BF16),
        compiler_params=_cp(("parallel", "arbitrary"), 40 * 1024 * 1024),
        name="fourier_seq",
    )(zz, ct, st)


def _route(logits, rb_ref):
    lt = logits.T
    l = [lt[e:e + 1, :] for e in range(N_EXP)]
    m = functools.reduce(jnp.maximum, l)
    ex = [jnp.exp(v - m) for v in l]
    den = functools.reduce(lambda a, c: a + c, ex)
    p = [v / den for v in ex]
    sel = [p[e] + rb_ref[e] for e in range(N_EXP)]
    g_score, g_gate = [], []
    for g in range(N_GRP):
        s = sel[E_PER * g:E_PER * (g + 1)]
        chosen = []
        for i in range(E_PER):
            rank = jnp.zeros_like(s[i])
            for j in range(E_PER):
                if j == i:
                    continue
                ahead = (s[j] >= s[i]) if j < i else (s[j] > s[i])
                rank = rank + jnp.where(ahead, 1.0, 0.0)
            chosen.append(rank < 2.0)
        g_score.append(functools.reduce(lambda a, c: a + c,
                                        [jnp.where(chosen[i], s[i], 0.0) for i in range(E_PER)]))
        g_gate.append([jnp.where(chosen[i], p[E_PER * g + i], 0.0) for i in range(E_PER)])
    best = jnp.zeros_like(g_score[0])
    best_s = g_score[0]
    for g in range(1, N_GRP):
        upd = g_score[g] > best_s
        best = jnp.where(upd, float(g), best)
        best_s = jnp.where(upd, g_score[g], best_s)
    gate = []
    for i in range(E_PER):
        gi = g_gate[0][i]
        for g in range(1, N_GRP):
            gi = jnp.where(best == float(g), g_gate[g][i], gi)
        gate.append(gi)
    wsum = functools.reduce(lambda a, c: a + c, gate)
    row = lax.broadcasted_iota(jnp.int32, (GW, lt.shape[1]), 0)
    side = jnp.zeros((GW, lt.shape[1]), F32)
    for i in range(E_PER):
        side = jnp.where(row == i, gate[i] / wsum, side)
    return side.T, best


def _outproj_kernel(rb_ref, a_ref, of_ref, ob_ref, gt_ref, h_ref, mod_ref, gn_ref, wa_ref, wr_ref, lng_ref, lnb_ref,
                    wrt_ref, h1_ref, u2_ref, grp_ref, *, dvp, dv):
    o = of_ref[0].astype(F32) + ob_ref[0].astype(F32)
    parts = []
    for hh in range(R_HEADS):
        sl = slice(hh * dvp, (hh + 1) * dvp)
        oh = o[:, sl]
        ms = jnp.sum(oh * oh, axis=-1, keepdims=True) * (1.0 / dv)
        parts.append(oh * lax.rsqrt(ms + LN_EPS))
    rec = jnp.concatenate(parts, axis=1) * gn_ref[...] * _silu(gt_ref[0].astype(F32))
    y = jnp.dot(a_ref[0], wa_ref[...], preferred_element_type=F32)
    y = y + jnp.dot(rec.astype(BF16), wr_ref[...], preferred_element_type=F32)
    g1 = mod_ref[0, 0, 2:3, :]
    h1 = _ln(ALPHA * h_ref[0] + g1 * y) * lng_ref[0:1, :] + lnb_ref[0:1, :]
    h1_ref[0] = h1
    u2 = _ln(h1) * (1.0 + mod_ref[0, 0, 4:5, :]) + mod_ref[0, 0, 3:4, :]
    logits = jnp.dot(u2, wrt_ref[...], precision=HIGHEST, preferred_element_type=F32)
    side, best = _route(logits, rb_ref)
    grp_ref[0, 0] = best
    for c in range(D // 128):
        u2_ref[pl.ds(c, TM, stride=XROWS), :] = u2[:, 128 * c:128 * (c + 1)]
    u2_ref[pl.ds(D // 128, TM, stride=XROWS), :] = side
    for c in range(D // 128 + 1, XROWS):
        u2_ref[pl.ds(c, TM, stride=XROWS), :] = jnp.zeros((TM, 128), F32)


def _outproj(b_router, a, a_col, a_w, o_f, o_b, proj, g_col, h, modsel, gnorm, wa, wr, lng, lnb, wrt, *,
             dv, dvp, row_off):
    b, rc, _ = h.shape
    nt = rc // TM - row_off
    wrec = R_HEADS * dvp
    full = lambda shape: pl.BlockSpec(shape, lambda i, j: (0,) * len(shape))
    return pl.pallas_call(
        functools.partial(_outproj_kernel, dvp=dvp, dv=dv),
        grid=(b, nt),
        in_specs=[pl.BlockSpec(memory_space=pltpu.SMEM),
                  pl.BlockSpec((1, TM, a_w), lambda i, j: (i, j + (row_off if a.shape[1] == rc else 0), a_col)),
                  pl.BlockSpec((1, TM, wrec), lambda i, j: (i, j + row_off, 0)),
                  pl.BlockSpec((1, TM, wrec), lambda i, j: (i, j + row_off, 0)),
                  pl.BlockSpec((1, TM, wrec), lambda i, j: (i, j + row_off, g_col)),
                  pl.BlockSpec((1, TM, D), lambda i, j: (i, j + row_off, 0)),
                  pl.BlockSpec((1, 1, 6, D), lambda i, j: (i, jnp.minimum(j + row_off, 1), 0, 0)),
                  full((1, wrec)), full((a_w, D)), full((wrec, D)), full((1, D)), full((1, D)), full((D, GW))],
        out_specs=[pl.BlockSpec((1, TM, D), lambda i, j: (i, j, 0)),
                   pl.BlockSpec((TM * XROWS, 128), lambda i, j: (i * nt + j, 0)),
                   pl.BlockSpec((1, 1, 1, TM), lambda i, j: (i, j, 0, 0))],
        out_shape=[jax.ShapeDtypeStruct((b, nt * TM, D), F32),
                   jax.ShapeDtypeStruct((b * nt * TM * XROWS, 128), F32),
                   jax.ShapeDtypeStruct((b, nt, 1, TM), F32)],
        compiler_params=_cp(("parallel", "arbitrary"), V7X_VMEM_LIMIT),
        name="outproj",
    )(b_router, a, o_f, o_b, proj, h, modsel, gnorm, wa, wr, lng, lnb, wrt)


def _start_token_gather(idx_ref, base, src_hbm, dst_ref, sem, rows):
    def body(r, carry):
        t = idx_ref[base + r]
        pltpu.make_async_copy(src_hbm.at[pl.ds(pl.multiple_of(t * rows, rows), rows), :],
                              dst_ref.at[pl.ds(pl.multiple_of(r * rows, rows), rows), :], sem).start()
        return carry

    lax.fori_loop(0, TM, body, 0, unroll=8)


def _wait_token_gather(src_hbm, dst_ref, sem):
    pltpu.make_async_copy(src_hbm.at[pl.ds(0, dst_ref.shape[0]), :], dst_ref, sem).wait()


def _prefetched_tile(idx_ref, step, nsteps, src_hbm, buf_ref, sem, rows):
    slot = step % 2

    @pl.when(step == 0)
    def _():
        _start_token_gather(idx_ref, 0, src_hbm, buf_ref.at[0], sem.at[0], rows)

    @pl.when(step + 1 < nsteps)
    def _():
        _start_token_gather(idx_ref, (step + 1) * TM, src_hbm, buf_ref.at[1 - slot], sem.at[1 - slot], rows)

    _wait_token_gather(src_hbm, buf_ref.at[slot], sem.at[slot])
    return slot


def _moe_kernel(tg_ref, src_ref, x_hbm, w1_ref, w3_ref, w2_ref, o_ref, xb_ref, sem):
    slot = _prefetched_tile(src_ref, pl.program_id(0), pl.num_programs(0), x_hbm, xb_ref, sem, XROWS)
    xb = xb_ref.at[slot]
    x = jnp.concatenate([xb[pl.ds(c, TM, stride=XROWS), :] for c in range(D // 128)], axis=1).astype(BF16)
    side = xb[pl.ds(D // 128, TM, stride=XROWS), :]
    y = jnp.zeros((TM, D), F32)
    for e in range(E_PER):
        h1 = jnp.dot(x, w1_ref[e], preferred_element_type=F32)
        h3 = jnp.dot(x, w3_ref[e], preferred_element_type=F32)
        hh = (_silu(h1) * h3 * side[:, e:e + 1]).astype(BF16)
        y = y + jnp.dot(hh, w2_ref[e], preferred_element_type=F32)
    for c in range(D // 128):
        o_ref[pl.ds(c, TM, stride=YROWS), :] = y[:, 128 * c:128 * (c + 1)]


def _moe(tile_grp, src_idx, u2x, w1, w3, w2):
    mp = src_idx.shape[0]
    wspec = lambda k, n: pl.BlockSpec((E_PER, k, n), lambda i, tg, si: (tg[i], 0, 0))
    return pl.pallas_call(
        _moe_kernel,
        grid_spec=pltpu.PrefetchScalarGridSpec(
            num_scalar_prefetch=2, grid=(mp // TM,),
            in_specs=[pl.BlockSpec(memory_space=pl.ANY), wspec(D, D_FF), wspec(D, D_FF), wspec(D_FF, D)],
            out_specs=pl.BlockSpec((TM * YROWS, 128), lambda i, tg, si: (i, 0)),
            scratch_shapes=[pltpu.VMEM((2, TM * XROWS, 128), F32), pltpu.SemaphoreType.DMA((2,))]),
        out_shape=jax.ShapeDtypeStruct((mp * YROWS, 128), F32),
        compiler_params=_cp(("arbitrary",), V7X_VMEM_LIMIT),
        name="moe",
    )(tile_grp, src_idx, u2x, w1, w3, w2)


def _combine_kernel(pos_ref, y_hbm, h_ref, mod_ref, lng_ref, lnb_ref, o_ref, yb_ref, sem, *, nt):
    step = pl.program_id(0) * nt + pl.program_id(1)
    slot = _prefetched_tile(pos_ref, step, pl.num_programs(0) * nt, y_hbm, yb_ref, sem, YROWS)
    yb = yb_ref.at[slot]
    y = jnp.concatenate([yb[pl.ds(c, TM, stride=YROWS), :] for c in range(D // 128)], axis=1)
    g2 = mod_ref[0, 0, 5:6, :]
    o_ref[0] = _ln(ALPHA * h_ref[0] + g2 * y) * lng_ref[0:1, :] + lnb_ref[0:1, :]


def _combine(pos, y_sorted, h1, modsel, lng, lnb, row_off):
    b, r, _ = h1.shape
    nt = r // TM
    full = lambda shape: pl.BlockSpec(shape, lambda i, j, p: (0,) * len(shape))
    return pl.pallas_call(
        functools.partial(_combine_kernel, nt=nt),
        grid_spec=pltpu.PrefetchScalarGridSpec(
            num_scalar_prefetch=1, grid=(b, nt),
            in_specs=[pl.BlockSpec(memory_space=pl.ANY),
                      pl.BlockSpec((1, TM, D), lambda i, j, p: (i, j, 0)),
                      pl.BlockSpec((1, 1, 6, D), lambda i, j, p: (i, jnp.minimum(j + row_off, 1), 0, 0)),
                      full((1, D)), full((1, D))],
            out_specs=pl.BlockSpec((1, TM, D), lambda i, j, p: (i, j, 0)),
            scratch_shapes=[pltpu.VMEM((2, TM * YROWS, 128), F32), pltpu.SemaphoreType.DMA((2,))]),
        out_shape=jax.ShapeDtypeStruct((b, r, D), F32),
        compiler_params=_cp(("arbitrary", "arbitrary")),
        name="moe_combine",
    )(pos, y_sorted, h1, modsel, lng, lnb)


def _sort_by_group(grp):
    n = grp.shape[0]
    onehot = (grp[:, None] == jnp.arange(N_GRP, dtype=jnp.int32)[None, :]).astype(jnp.int32)
    csum = jnp.cumsum(onehot, axis=0)
    counts = csum[-1]
    padded = ((counts + TM - 1) // TM) * TM
    ends = jnp.cumsum(padded)
    starts = ends - padded
    rank = jnp.sum(csum * onehot, axis=1) - 1
    pos = (starts[grp] + rank).astype(jnp.int32)
    mp = n + N_GRP * TM
    src_idx = jnp.zeros((mp,), jnp.int32).at[pos].set(jnp.arange(n, dtype=jnp.int32))
    tile_start = jnp.arange(mp // TM, dtype=jnp.int32) * TM
    tile_grp = jnp.minimum(jnp.sum((tile_start[:, None] >= ends[None, :]).astype(jnp.int32), axis=1), N_GRP - 1)
    return pos, src_idx, tile_grp.astype(jnp.int32)


def _moe_layer(u2x, grp, h1, modsel, w1, w3, w2, lng, lnb, row_off):
    pos, src_idx, tile_grp = _sort_by_group(grp.reshape(-1).astype(jnp.int32))
    y_sorted = _moe(tile_grp, src_idx, u2x, w1.astype(BF16), w3.astype(BF16), w2.astype(BF16))
    return _combine(pos, y_sorted, h1, modsel, lng, lnb, row_off)


def _rope_tables(rc, t):
    half = A_HD // 4
    inv = ROPE_BASE ** (-jnp.arange(half, dtype=F32) / half)
    tpos = jnp.arange(t)
    rows = (tpos // GRID_W).astype(F32)
    cols = (tpos % GRID_W).astype(F32)

    def cs(pos):
        ang = pos[:, None] * inv[None, :]
        c, s = jnp.cos(ang), jnp.sin(ang)
        return jnp.concatenate([c, c], -1), jnp.concatenate([-s, s], -1)

    cr, sr = cs(rows)
    cc_, sc_ = cs(cols)
    cos_h = jnp.concatenate([cr, cc_], -1)
    sin_h = jnp.concatenate([sr, sc_], -1)
    one = jnp.ones((t, A_HD), F32)
    zero = jnp.zeros((t, A_HD), F32)
    lat = jnp.stack([jnp.concatenate([cos_h, cos_h], -1), jnp.concatenate([sin_h, sin_h], -1),
                     jnp.concatenate([cos_h, one], -1), jnp.concatenate([sin_h, zero], -1)])
    ctx = jnp.stack([jnp.ones((CTX, 128), F32), jnp.zeros((CTX, 128), F32)] * 2)
    return jnp.concatenate([ctx, lat], axis=1)


def _dft_tables(t):
    k = jnp.arange(t, dtype=jnp.int32)
    ang_t = (2.0 * math.pi / t) * ((k[:, None] * k[None, :]) % t).astype(F32)
    c = jnp.arange(F_GD, dtype=jnp.int32)
    ang_c = (2.0 * math.pi / F_GD) * ((c[:, None] * c[None, :]) % F_GD).astype(F32)
    eye = jnp.eye(F_GROUPS, dtype=F32)
    cs = jnp.concatenate([jnp.kron(eye, jnp.cos(ang_c)), jnp.kron(eye, jnp.sin(ang_c))], axis=1)
    return cs.astype(BF16), jnp.cos(ang_t).astype(BF16), jnp.sin(ang_t).astype(BF16)


def _pad_heads(w, width, padded):
    lead = w.shape[:-1]
    w = w.reshape(lead + (R_HEADS, width))
    w = jnp.pad(w, [(0, 0)] * len(lead) + [(0, 0), (0, padded - width)])
    return w.reshape(lead + (R_HEADS * padded,))


EVEN_SEGS = tuple([(128 * g, 128 * (g + 1), 0) for g in range(4)] + [(512, 3072, None)]
                  + [(3072 + 128 * g, 3072 + 128 * (g + 1), 1) for g in range(2)])
ODD_W = 512 + 512 + 1024 + 1024 + 256 + 128
ODD_SEGS = ((0, 1024, None), (1024, 2048, None), (2048, 3072, None), (3072, ODD_W, None))


def _layer_even(h, modsel, tbl, w_in, sink, lb_logits, g_norm, w_out, lng, lnb, w_router, b_router, w1, w3, w2):
    aq, ak, av, ff, fb, hq, hi, hg = jnp.split(w_in, np.cumsum([512, 128, 128, 512, 512, 512, 512]).tolist(), axis=1)
    kv = [jnp.concatenate([ak[:, A_HD * j:A_HD * (j + 1)], av[:, A_HD * j:A_HD * (j + 1)]], axis=1)
          for j in range(A_KV)]
    w = jnp.concatenate([aq, ff, fb, hq, hi, hg] + kv, axis=1).astype(BF16)
    proj = _inproj(h, modsel, w, tbl, EVEN_SEGS)
    o_att = _attention(proj, sink, 3072 // 128)
    lb = jnp.cumsum(jax.nn.softmax(lb_logits.astype(F32), axis=1), axis=1)[:, 0]
    o_f, o_b = _hgrn_scan(proj, lb, 3, 4, 1, 2)
    wo = w_out.astype(BF16)
    h1, u2x, grp = _outproj(b_router, o_att, 0, 512, o_f, o_b, proj, 5, h, modsel, g_norm.reshape(1, -1),
                            wo[:512], wo[512:], lng[0:1], lnb[0:1], w_router, dv=128, dvp=128, row_off=0)
    return _moe_layer(u2x, grp, h1, modsel, w1, w3, w2, lng[1:2], lnb[1:2], 0)


def _layer_odd(h, modsel, tbl, dft, w_in, gate_w, gate_b, g_norm, w_out, lng, lnb, w_router, b_router, w1, w3, w2, t):
    q, k, v, rf, rb, g, z = jnp.split(w_in, np.cumsum([384, 384, 768, 16, 16, 768]).tolist(), axis=1)
    r = jnp.pad(jnp.concatenate([rf, rb], axis=1), ((0, 0), (0, 96)))
    w = jnp.concatenate([_pad_heads(q, GLA_DK, DKP), _pad_heads(k, GLA_DK, DKP), _pad_heads(v, GLA_DV, GLA_DVP),
                         _pad_heads(g, GLA_DV, GLA_DVP), z, r], axis=1).astype(BF16)
    proj = _inproj(h, modsel, w, tbl, ODD_SEGS)
    gw = _pad_heads(gate_w, GLA_DK, DKP)
    gwf = jnp.pad(gw[0], ((0, 112), (0, 0)))
    gwb = jnp.pad(gw[1], ((16, 96), (0, 0)))
    gb = _pad_heads(gate_b, GLA_DK, DKP)
    o_f, o_b = _gla_scan(proj, gwf, gwb, gb, 0, 1, 1, 3328 // 128)
    cs, ct, st = dft
    four = _fourier(proj, 3072 // (F_GROUPS * F_GD), cs, ct, st, t)
    wo = w_out
    wr = jnp.pad(wo[:768].reshape(R_HEADS, GLA_DV, D), ((0, 0), (0, GLA_DVP - GLA_DV), (0, 0))).reshape(-1, D)
    h1, u2x, grp = _outproj(b_router, four, 0, 256, o_f, o_b, proj, 2, h, modsel,
                            _pad_heads(g_norm, GLA_DV, GLA_DVP).reshape(1, -1),
                            wo[768:].astype(BF16), wr.astype(BF16), lng[0:1], lnb[0:1], w_router,
                            dv=GLA_DV, dvp=GLA_DVP, row_off=CTX // TM)
    return _moe_layer(u2x, grp, h1, modsel, w1, w3, w2, lng[1:2], lnb[1:2], CTX // TM)


def kernel(x, c, ctx, c_ctx, w_ada, b_ada, ln_g, ln_b, w_in_even, attn_sink, hgrn_lb_logits, hgrn_norm, w_out_even, w_in_odd, gla_gate_w, gla_gate_b, gla_norm, w_out_odd, w_router, b_router, w_expert_gate, w_expert_up, w_expert_down):
    b, t, _ = x.shape
    rc = CTX + t
    h = jnp.concatenate([ctx, x], axis=1)
    cc = jnp.zeros((16, D), F32).at[:b].set(c).at[b].set(c_ctx)
    mods = _ada_mods(cc, w_ada, b_ada).reshape(2, 16, 6, D)

    def modsel(l):
        return jnp.stack([jnp.broadcast_to(mods[l, b], (b, 6, D)), mods[l, :b]], axis=1)

    tbl = _rope_tables(rc, t)
    wrt = jnp.pad(w_router, ((0, 0), (0, GW - N_EXP)))
    h = _layer_even(h, modsel(0), tbl, w_in_even[0], attn_sink[0], hgrn_lb_logits, hgrn_norm[0], w_out_even[0],
                    ln_g[0], ln_b[0], wrt, b_router, w_expert_gate[0], w_expert_up[0], w_expert_down[0])
    return _layer_odd(h, modsel(1), tbl, _dft_tables(t), w_in_odd[0], gla_gate_w[0], gla_gate_b[0], gla_norm[0],
                      w_out_odd[0], ln_g[1], ln_b[1], wrt, b_router, w_expert_gate[1], w_expert_up[1],
                      w_expert_down[1], t)
```

```python
import functools
import math

import numpy as np
import jax
import jax.numpy as jnp
from jax import lax
from jax.experimental import pallas as pl
from jax.experimental.pallas import tpu as pltpu

F32 = jnp.float32
BF16 = jnp.bfloat16
HIGHEST = lax.Precision.HIGHEST

D = 1024
CTX = 256
GRID_W = 64
LN_EPS = 1e-5
NEG = -1e30
ALPHA = 4.0 ** 0.25
ROPE_BASE = 10000.0

A_HEADS, A_KV, A_HD, A_BLK = 8, 2, 64, 128
R_HEADS = 4
DKP = 128
GLA_DK, GLA_DV, GLA_DVP = 96, 192, 256
GLA_TAU = 16.0
CHUNK = 64
N_EXP, N_GRP, E_PER = 16, 4, 4
D_FF = 512
F_GROUPS, F_GD = 4, 64

TM = 256
GW = 128
XROWS = 16
YROWS = 8
V7X_VMEM_LIMIT = 56 * 1024 * 1024


def _cp(sem, vmem=None):
    return pltpu.CompilerParams(dimension_semantics=sem, vmem_limit_bytes=vmem)


def _ln(x):
    mu = jnp.mean(x, axis=-1, keepdims=True)
    xc = x - mu
    var = jnp.mean(xc * xc, axis=-1, keepdims=True)
    return xc * lax.rsqrt(var + LN_EPS)


def _silu(x):
    return x * jax.nn.sigmoid(x)


_NT = (((1,), (1,)), ((), ()))
_TN = (((0,), (0,)), ((), ()))


def _ada_kernel(c_ref, w_ref, b_ref, o_ref):
    s = _silu(c_ref[...])
    o_ref[0] = jnp.dot(s, w_ref[0], precision=HIGHEST, preferred_element_type=F32) + b_ref[0]


def _ada_mods(cc, w_ada, b_ada):
    depth, _, n = w_ada.shape
    tn = 1536
    return pl.pallas_call(
        _ada_kernel,
        grid=(depth, n // tn),
        in_specs=[pl.BlockSpec((16, D), lambda l, i: (0, 0)),
                  pl.BlockSpec((1, D, tn), lambda l, i: (l, 0, i)),
                  pl.BlockSpec((1, 1, tn), lambda l, i: (l, 0, i))],
        out_specs=pl.BlockSpec((1, 16, tn), lambda l, i: (l, 0, i)),
        out_shape=jax.ShapeDtypeStruct((depth, 16, n), F32),
        compiler_params=_cp(("arbitrary", "arbitrary"), 40 * 1024 * 1024),
        name="ada_mod",
    )(cc, w_ada, b_ada.reshape(depth, 1, n))


def _rows_of(h_refs):
    if len(h_refs) == 1:
        return h_refs[0][0]
    return jnp.where(pl.program_id(1) == 0, h_refs[0][0], h_refs[1][0])


def _row_specs(h, row_off=0):
    if isinstance(h, tuple):
        assert row_off == 0 and h[0].shape[1] == TM
        return [pl.BlockSpec((1, TM, D), lambda i, j: (i, 0, 0)),
                pl.BlockSpec((1, TM, D), lambda i, j: (i, jnp.maximum(j - 1, 0), 0))], list(h)
    return [pl.BlockSpec((1, TM, D), lambda i, j: (i, j + row_off, 0))], [h]


def _inproj_kernel(*refs, segs):
    *h_refs, mod_ref, w_ref, tbl_ref, o_ref = refs
    x = _rows_of(h_refs)
    u = (_ln(x) * (1.0 + mod_ref[0, 0, 1:2, :]) + mod_ref[0, 0, 0:1, :]).astype(BF16)
    lane = lax.broadcasted_iota(jnp.int32, (x.shape[0], 128), 1)
    low = (lane % 32) < 16
    for c0, c1, rope in segs:
        acc = jnp.dot(u, w_ref[:, c0:c1], preferred_element_type=F32)
        if rope is not None:
            partner = jnp.where(low, pltpu.roll(acc, 112, axis=1), pltpu.roll(acc, 16, axis=1))
            acc = acc * tbl_ref[2 * rope] + partner * tbl_ref[2 * rope + 1]
        o_ref[0, :, c0:c1] = acc.astype(BF16)


def _inproj(h, modsel, w, tbl, segs):
    b, rc = modsel.shape[0], tbl.shape[1]
    n = w.shape[1]
    nt = rc // TM
    h_specs, h_args = _row_specs(h)
    return pl.pallas_call(
        functools.partial(_inproj_kernel, segs=segs),
        grid=(b, nt),
        in_specs=h_specs + [pl.BlockSpec((1, 1, 6, D), lambda i, j: (i, jnp.minimum(j, 1), 0, 0)),
                            pl.BlockSpec((D, n), lambda i, j: (0, 0)),
                            pl.BlockSpec((4, TM, 128), lambda i, j: (0, j, 0))],
        out_specs=pl.BlockSpec((1, TM, n), lambda i, j: (i, j, 0)),
        out_shape=jax.ShapeDtypeStruct((b, rc, n), BF16),
        compiler_params=_cp(("parallel", "arbitrary"), V7X_VMEM_LIMIT),
        name="inproj",
    )(*h_args, modsel, w, tbl)


def _attn_kernel(sink_ref, q_ref, kvc_ref, kvp_ref, kvm_ref, kvn_ref, bias_ref, o_ref):
    blk = pl.program_id(1)
    g_per = A_HEADS // A_KV
    nq = g_per * A_BLK
    q = q_ref[0] * (A_HD ** -0.5)
    rgrp = lax.broadcasted_iota(jnp.int32, (nq, 1), 0) // A_BLK
    q4, sink_col = [], []
    for h in range(A_KV):
        q4.append(jnp.concatenate([q[:, A_HD * (g_per * h + g):A_HD * (g_per * h + g + 1)] for g in range(g_per)],
                                  axis=0))
        col = jnp.zeros((nq, 1), F32)
        for g in range(g_per):
            col = jnp.where(rgrp == g, sink_ref[h * g_per + g], col)
        sink_col.append(col)

    def attend(kv, bias):
        s = [lax.dot_general(q4[h], kv[:, 2 * A_HD * h:2 * A_HD * h + A_HD], _NT, preferred_element_type=F32)
             for h in range(A_KV)]
        if bias is not None:
            bias4 = jnp.concatenate([bias] * g_per, axis=0)
            s = [x + bias4 for x in s]
        m = [jnp.maximum(jnp.max(s[h], axis=-1, keepdims=True), sink_col[h]) for h in range(A_KV)]
        p = [jnp.exp(s[h] - m[h]) for h in range(A_KV)]
        den = [jnp.sum(p[h], axis=-1, keepdims=True) + jnp.exp(sink_col[h] - m[h]) for h in range(A_KV)]
        o = [jnp.dot(p[h].astype(BF16), kv[:, 2 * A_HD * h + A_HD:2 * A_HD * (h + 1)], preferred_element_type=F32)
             / den[h] for h in range(A_KV)]
        o_ref[0] = jnp.concatenate([o[h][A_BLK * g:A_BLK * (g + 1), :] for h in range(A_KV) for g in range(g_per)],
                                   axis=1).astype(BF16)

    @pl.when(blk < CTX // A_BLK)
    def _():
        attend(kvc_ref[0], None)

    @pl.when(blk >= CTX // A_BLK)
    def _():
        attend(jnp.concatenate([kvc_ref[0], kvp_ref[0], kvm_ref[0], kvn_ref[0]], axis=0), bias_ref[0])


def _attn_bias():
    nk = CTX + 3 * A_BLK
    qi = jnp.arange(A_BLK, dtype=jnp.int32)[:, None]
    col = jnp.arange(nk, dtype=jnp.int32)[None, :]
    rel = col - (CTX + A_BLK)
    dist = qi - rel

    def one(lo, hi):
        in_win = (dist <= A_BLK) & (dist >= -A_BLK) & (rel >= lo) & (rel < hi)
        return jnp.where((col < CTX) | in_win, 0.0, NEG).astype(F32)

    return jnp.stack([one(0, 2 * A_BLK), one(-A_BLK, 2 * A_BLK), one(-A_BLK, A_BLK)])


def _attention(proj, sink, kv_blk):
    b, rc, _ = proj.shape
    nb = rc // A_BLK
    c_b = CTX // A_BLK
    assert nb - c_b >= 2
    kw = 2 * A_HD * A_KV

    def band(off):
        return lambda i, j: (i, jnp.clip(j + off, c_b, nb - 1), kv_blk)

    def which_bias(i, j):
        return (jnp.where(j <= c_b, 0, jnp.where(j == nb - 1, 2, 1)), 0, 0)

    return pl.pallas_call(
        _attn_kernel,
        grid=(b, nb),
        in_specs=[pl.BlockSpec(memory_space=pltpu.SMEM),
                  pl.BlockSpec((1, A_BLK, A_HEADS * A_HD), lambda i, j: (i, j, 0)),
                  pl.BlockSpec((1, CTX, kw), lambda i, j: (i, 0, kv_blk)),
                  pl.BlockSpec((1, A_BLK, kw), band(-1)),
                  pl.BlockSpec((1, A_BLK, kw), band(0)),
                  pl.BlockSpec((1, A_BLK, kw), band(1)),
                  pl.BlockSpec((1, A_BLK, CTX + 3 * A_BLK), which_bias)],
        out_specs=pl.BlockSpec((1, A_BLK, A_HEADS * A_HD), lambda i, j: (i, j, 0)),
        out_shape=jax.ShapeDtypeStruct((b, rc, A_HEADS * A_HD), BF16),
        compiler_params=_cp(("parallel", "arbitrary")),
        name="win_attn",
    )(sink, proj, proj, proj, proj, proj, _attn_bias())


def _scan_blocks(dirs, *, dvp):
    r_i = lax.broadcasted_iota(jnp.int32, (CHUNK, CHUNK), 0)
    c_i = lax.broadcasted_iota(jnp.int32, (CHUNK, CHUNK), 1)
    units = []
    for q, k, v, logf, s_ref, o_ref, reverse in dirs:
        nchunk = q.shape[0] // CHUNK
        tri = (r_i <= c_i) if reverse else (r_i >= c_i)
        tri3 = jnp.concatenate([jnp.where(tri, 1.0, 0.0).astype(BF16)] * 3, axis=1)
        for c in (range(nchunk - 1, -1, -1) if reverse else range(nchunk)):
            sl = slice(c * CHUNK, (c + 1) * CHUNK)
            units.append(dict(q=q[sl], k=k[sl], v=v[sl].astype(BF16), lf=logf[sl], sl=sl, tri=tri, tri3=tri3,
                              i_tot=0 if reverse else CHUNK - 1,
                              i_ref=CHUNK // 2 - 1 if reverse else CHUNK // 2, s_ref=s_ref, o_ref=o_ref))
    heads = [(slice(h * DKP, (h + 1) * DKP), slice(h * dvp, (h + 1) * dvp)) for h in range(R_HEADS)]
    for u in units:
        lf = u["lf"]
        hi = lf.astype(BF16)
        r1 = lf - hi.astype(F32)
        mid = r1.astype(BF16)
        lo = (r1 - mid.astype(F32)).astype(BF16)
        u["bsum"] = jnp.dot(u["tri3"], jnp.concatenate([hi, mid, lo], axis=0), preferred_element_type=F32)
    for u in units:
        bsum = u["bsum"]
        tot = bsum[u["i_tot"]:u["i_tot"] + 1, :]
        ref = bsum[u["i_ref"]:u["i_ref"] + 1, :]
        qg = u["q"] * jnp.exp(bsum - ref)
        kg = u["k"] * jnp.exp(ref - bsum)
        u["qb"] = (qg * jnp.exp(ref)).astype(BF16)
        u["kd"] = (kg * jnp.exp(tot - ref)).astype(BF16)
        u["qg"] = qg.astype(BF16)
        u["kg"] = kg.astype(BF16)
        u["dec"] = jnp.exp(tot)
    for u in units:
        u["att"] = [lax.dot_general(u["qg"][:, ks], u["kg"][:, ks], _NT, preferred_element_type=F32)
                    for ks, _ in heads]
        u["ds"] = [lax.dot_general(u["v"][:, vs], u["kd"][:, ks], _TN, preferred_element_type=F32)
                   for ks, vs in heads]
    for u in units:
        u["oi"] = [jnp.dot(jnp.where(u["tri"], a, 0.0).astype(BF16), u["v"][:, vs], preferred_element_type=F32)
                   for a, (_, vs) in zip(u["att"], heads)]
    state = {}
    for u in units:
        key = id(u["s_ref"])
        if key not in state:
            state[key] = [u["s_ref"][h] for h in range(R_HEADS)]
        st = state[key]
        outs = []
        for h, (ks, _) in enumerate(heads):
            outs.append(u["oi"][h] + lax.dot_general(u["qb"][:, ks], st[h].astype(BF16), _NT,
                                                     preferred_element_type=F32))
            st[h] = st[h] * u["dec"][:, ks] + u["ds"][h]
        u["o_ref"][0, u["sl"], :] = jnp.concatenate(outs, axis=1).astype(BF16)
    for _, _, _, _, s_ref, _, _ in dirs:
        for h in range(R_HEADS):
            s_ref[h] = state[id(s_ref)][h]


def _hgrn_scan_kernel(lb_ref, qf_ref, vf_ref, zf_ref, qb_ref, vb_ref, zb_ref, of_ref, ob_ref, sf_ref, sb_ref):
    @pl.when(pl.program_id(1) == 0)
    def _():
        sf_ref[...] = jnp.zeros_like(sf_ref)
        sb_ref[...] = jnp.zeros_like(sb_ref)

    def gates(z, lb):
        f = lb + (1.0 - lb) * jax.nn.sigmoid(z.astype(F32))
        return 1.0 - f, jnp.log(f)

    kf, lf = gates(zf_ref[0], lb_ref[0:1, :])
    kb, lbw = gates(zb_ref[0], lb_ref[1:2, :])
    _scan_blocks([(qf_ref[0], kf, vf_ref[0], lf, sf_ref, of_ref, False),
                  (qb_ref[0], kb, vb_ref[0], lbw, sb_ref, ob_ref, True)], dvp=128)


def _bwd_blk(j, nt):
    return jnp.where(j == 0, 0, nt - j)


def _hgrn_scan(proj, lb, cq, cv, cf, cb):
    b, rc, _ = proj.shape
    nt = rc // TM
    w = R_HEADS * DKP

    def fw(col):
        return pl.BlockSpec((1, TM, w), lambda i, j: (i, j, col))

    def bw(col):
        return pl.BlockSpec((1, TM, w), lambda i, j: (i, _bwd_blk(j, nt), col))

    o_sd = jax.ShapeDtypeStruct((b, rc, w), BF16)
    return pl.pallas_call(
        _hgrn_scan_kernel,
        grid=(b, nt),
        in_specs=[pl.BlockSpec((2, w), lambda i, j: (0, 0)),
                  fw(cq), fw(cv), fw(cf), bw(cq), bw(cv), bw(cb)],
        out_specs=[pl.BlockSpec((1, TM, w), lambda i, j: (i, j, 0)),
                   pl.BlockSpec((1, TM, w), lambda i, j: (i, _bwd_blk(j, nt), 0))],
        out_shape=[o_sd, o_sd],
        scratch_shapes=[pltpu.VMEM((R_HEADS, 128, DKP), F32), pltpu.VMEM((R_HEADS, 128, DKP), F32)],
        compiler_params=_cp(("parallel", "arbitrary"), 40 * 1024 * 1024),
        name="hgrn_scan",
    )(lb, proj, proj, proj, proj, proj, proj)


def _gla_scan_kernel(gwf_ref, gwb_ref, gb_ref, qf_ref, kf_ref, vf_ref, rf_ref, qb_ref, kb_ref, vb_ref, rb_ref,
                     of_ref, ob_ref, sf_ref, sb_ref):
    @pl.when(pl.program_id(1) == 0)
    def _():
        sf_ref[...] = jnp.zeros_like(sf_ref)
        sb_ref[...] = jnp.zeros_like(sb_ref)

    def decay(r, gw, gb):
        zz = jnp.dot(r.astype(F32), gw, precision=HIGHEST, preferred_element_type=F32) + gb
        return (jnp.minimum(zz, 0.0) - jnp.log(1.0 + jnp.exp(-jnp.abs(zz)))) * (1.0 / GLA_TAU)

    qs = GLA_DK ** -0.5
    _scan_blocks([(qf_ref[0].astype(F32) * qs, kf_ref[0], vf_ref[0],
                   decay(rf_ref[0], gwf_ref[...], gb_ref[0:1, :]), sf_ref, of_ref, False),
                  (qb_ref[0].astype(F32) * qs, kb_ref[0], vb_ref[0],
                   decay(rb_ref[0], gwb_ref[...], gb_ref[1:2, :]), sb_ref, ob_ref, True)], dvp=GLA_DVP)


def _gla_scan(proj, gwf, gwb, gb, cq, ck, cv, cr):
    b, rc, _ = proj.shape
    nt = rc // TM
    wk = R_HEADS * DKP
    wv = R_HEADS * GLA_DVP

    def spec(width, col, back):
        if back:
            return pl.BlockSpec((1, TM, width), lambda i, j: (i, _bwd_blk(j, nt), col))
        return pl.BlockSpec((1, TM, width), lambda i, j: (i, j, col))

    o_sd = jax.ShapeDtypeStruct((b, rc, wv), BF16)
    full = lambda shape: pl.BlockSpec(shape, lambda i, j: (0,) * len(shape))
    return pl.pallas_call(
        _gla_scan_kernel,
        grid=(b, nt),
        in_specs=[full((128, wk)), full((128, wk)), full((2, wk)),
                  spec(wk, cq, False), spec(wk, ck, False), spec(wv, cv, False), spec(128, cr, False),
                  spec(wk, cq, True), spec(wk, ck, True), spec(wv, cv, True), spec(128, cr, True)],
        out_specs=[spec(wv, 0, False), spec(wv, 0, True)],
        out_shape=[o_sd, o_sd],
        scratch_shapes=[pltpu.VMEM((R_HEADS, GLA_DVP, DKP), F32), pltpu.VMEM((R_HEADS, GLA_DVP, DKP), F32)],
        compiler_params=_cp(("parallel", "arbitrary"), 48 * 1024 * 1024),
        name="gla_scan",
    )(gwf, gwb, gb, proj, proj, proj, proj, proj, proj, proj, proj)


def _fourier_chan_kernel(z_ref, cs_ref, o_ref):
    o_ref[0] = jnp.dot(z_ref[0], cs_ref[...], preferred_element_type=F32).astype(BF16)


def _fourier_seq_kernel(zz_ref, ct_ref, st_ref, o_ref, *, scale):
    w = F_GROUPS * F_GD
    y = jnp.dot(ct_ref[...], zz_ref[0, :, :w], preferred_element_type=F32)
    y = y - jnp.dot(st_ref[...], zz_ref[0, :, w:], preferred_element_type=F32)
    o_ref[0] = (y * scale).astype(BF16)


def _fourier(proj, zcol, cs, ct, st, t):
    b = proj.shape[0]
    w = F_GROUPS * F_GD
    c_t = CTX // TM
    nzt = t // TM
    zz = pl.pallas_call(
        _fourier_chan_kernel,
        grid=(b, nzt),
        in_specs=[pl.BlockSpec((1, TM, w), lambda i, j: (i, j + c_t, zcol)),
                  pl.BlockSpec((w, 2 * w), lambda i, j: (0, 0))],
        out_specs=pl.BlockSpec((1, TM, 2 * w), lambda i, j: (i, j, 0)),
        out_shape=jax.ShapeDtypeStruct((b, t, 2 * w), BF16),
        compiler_params=_cp(("parallel", "arbitrary")),
        name="fourier_chan",
    )(proj, cs)
    return pl.pallas_call(
        functools.partial(_fourier_seq_kernel, scale=1.0 / math.sqrt(t * F_GD)),
        grid=(b, nzt),
        in_specs=[pl.BlockSpec((1, t, 2 * w), lambda i, j: (i, 0, 0)),
                  pl.BlockSpec((TM, t), lambda i, j: (j, 0)),
                  pl.BlockSpec((TM, t), lambda i, j: (j, 0))],
        out_specs=pl.BlockSpec((1, TM, w), lambda i, j: (i, j, 0)),
        out_shape=jax.ShapeDtypeStruct((b, t, w), BF16),
        compiler_params=_cp(("parallel", "arbitrary"), 40 * 1024 * 1024),
        name="fourier_seq",
    )(zz, ct, st)


def _route(logits, rb_ref):
    lt = logits.T
    l = [lt[e:e + 1, :] for e in range(N_EXP)]
    m = functools.reduce(jnp.maximum, l)
    ex = [jnp.exp(v - m) for v in l]
    den = functools.reduce(lambda a, c: a + c, ex)
    p = [v / den for v in ex]
    sel = [p[e] + rb_ref[e] for e in range(N_EXP)]
    g_score, g_gate = [], []
    for g in range(N_GRP):
        s = sel[E_PER * g:E_PER * (g + 1)]
        chosen = []
        for i in range(E_PER):
            rank = jnp.zeros_like(s[i])
            for j in range(E_PER):
                if j == i:
                    continue
                ahead = (s[j] >= s[i]) if j < i else (s[j] > s[i])
                rank = rank + jnp.where(ahead, 1.0, 0.0)
            chosen.append(rank < 2.0)
        g_score.append(functools.reduce(lambda a, c: a + c,
                                        [jnp.where(chosen[i], s[i], 0.0) for i in range(E_PER)]))
        g_gate.append([jnp.where(chosen[i], p[E_PER * g + i], 0.0) for i in range(E_PER)])
    best = jnp.zeros_like(g_score[0])
    best_s = g_score[0]
    for g in range(1, N_GRP):
        upd = g_score[g] > best_s
        best = jnp.where(upd, float(g), best)
        best_s = jnp.where(upd, g_score[g], best_s)
    gate = []
    for i in range(E_PER):
        gi = g_gate[0][i]
        for g in range(1, N_GRP):
            gi = jnp.where(best == float(g), g_gate[g][i], gi)
        gate.append(gi)
    wsum = functools.reduce(lambda a, c: a + c, gate)
    row = lax.broadcasted_iota(jnp.int32, (GW, lt.shape[1]), 0)
    side = jnp.zeros((GW, lt.shape[1]), F32)
    for i in range(E_PER):
        side = jnp.where(row == i, gate[i] / wsum, side)
    return side.T, best


def _outproj_kernel(rb_ref, a_ref, of_ref, ob_ref, gt_ref, *refs, dvp, dv):
    *h_refs, mod_ref, gn_ref, wa_ref, wr_ref, lng_ref, lnb_ref, wrt_ref, h1_ref, u2_ref, grp_ref = refs
    o = of_ref[0].astype(F32) + ob_ref[0].astype(F32)
    parts = []
    for hh in range(R_HEADS):
        sl = slice(hh * dvp, (hh + 1) * dvp)
        oh = o[:, sl]
        ms = jnp.sum(oh * oh, axis=-1, keepdims=True) * (1.0 / dv)
        parts.append(oh * lax.rsqrt(ms + LN_EPS))
    rec = jnp.concatenate(parts, axis=1) * gn_ref[...] * _silu(gt_ref[0].astype(F32))
    y = jnp.dot(a_ref[0], wa_ref[...], preferred_element_type=F32)
    y = y + jnp.dot(rec.astype(BF16), wr_ref[...], preferred_element_type=F32)
    g1 = mod_ref[0, 0, 2:3, :]
    h1 = _ln(ALPHA * _rows_of(h_refs) + g1 * y) * lng_ref[0:1, :] + lnb_ref[0:1, :]
    h1_ref[0] = h1
    u2 = _ln(h1) * (1.0 + mod_ref[0, 0, 4:5, :]) + mod_ref[0, 0, 3:4, :]
    logits = jnp.dot(u2, wrt_ref[...], precision=HIGHEST, preferred_element_type=F32)
    side, best = _route(logits, rb_ref)
    grp_ref[0, 0] = best
    for c in range(D // 128):
        u2_ref[pl.ds(c, TM, stride=XROWS), :] = u2[:, 128 * c:128 * (c + 1)]
    u2_ref[pl.ds(D // 128, TM, stride=XROWS), :] = side
    for c in range(D // 128 + 1, XROWS):
        u2_ref[pl.ds(c, TM, stride=XROWS), :] = jnp.zeros((TM, 128), F32)


def _outproj(b_router, a, a_col, a_w, o_f, o_b, proj, g_col, h, modsel, gnorm, wa, wr, lng, lnb, wrt, *,
             dv, dvp, row_off):
    b, rc, _ = proj.shape
    nt = rc // TM - row_off
    wrec = R_HEADS * dvp
    full = lambda shape: pl.BlockSpec(shape, lambda i, j: (0,) * len(shape))
    h_specs, h_args = _row_specs(h, row_off)
    return pl.pallas_call(
        functools.partial(_outproj_kernel, dvp=dvp, dv=dv),
        grid=(b, nt),
        in_specs=[pl.BlockSpec(memory_space=pltpu.SMEM),
                  pl.BlockSpec((1, TM, a_w), lambda i, j: (i, j + (row_off if a.shape[1] == rc else 0), a_col)),
                  pl.BlockSpec((1, TM, wrec), lambda i, j: (i, j + row_off, 0)),
                  pl.BlockSpec((1, TM, wrec), lambda i, j: (i, j + row_off, 0)),
                  pl.BlockSpec((1, TM, wrec), lambda i, j: (i, j + row_off, g_col))] + h_specs + [
                  pl.BlockSpec((1, 1, 6, D), lambda i, j: (i, jnp.minimum(j + row_off, 1), 0, 0)),
                  full((1, wrec)), full((a_w, D)), full((wrec, D)), full((1, D)), full((1, D)), full((D, GW))],
        out_specs=[pl.BlockSpec((1, TM, D), lambda i, j: (i, j, 0)),
                   pl.BlockSpec((TM * XROWS, 128), lambda i, j: (i * nt + j, 0)),
                   pl.BlockSpec((1, 1, 1, TM), lambda i, j: (i, j, 0, 0))],
        out_shape=[jax.ShapeDtypeStruct((b, nt * TM, D), F32),
                   jax.ShapeDtypeStruct((b * nt * TM * XROWS, 128), F32),
                   jax.ShapeDtypeStruct((b, nt, 1, TM), F32)],
        compiler_params=_cp(("parallel", "arbitrary"), V7X_VMEM_LIMIT),
        name="outproj",
    )(b_router, a, o_f, o_b, proj, *h_args, modsel, gnorm, wa, wr, lng, lnb, wrt)


def _start_token_gather(idx_ref, base, src_hbm, dst_ref, sem, rows):
    def body(r, carry):
        t = idx_ref[base + r]
        pltpu.make_async_copy(src_hbm.at[pl.ds(pl.multiple_of(t * rows, rows), rows), :],
                              dst_ref.at[pl.ds(pl.multiple_of(r * rows, rows), rows), :], sem).start()
        return carry

    lax.fori_loop(0, TM, body, 0, unroll=8)


def _wait_token_gather(src_hbm, dst_ref, sem):
    pltpu.make_async_copy(src_hbm.at[pl.ds(0, dst_ref.shape[0]), :], dst_ref, sem).wait()


def _prefetched_tile(idx_ref, step, nsteps, src_hbm, buf_ref, sem, rows):
    slot = step % 2

    @pl.when(step == 0)
    def _():
        _start_token_gather(idx_ref, 0, src_hbm, buf_ref.at[0], sem.at[0], rows)

    @pl.when(step + 1 < nsteps)
    def _():
        _start_token_gather(idx_ref, (step + 1) * TM, src_hbm, buf_ref.at[1 - slot], sem.at[1 - slot], rows)

    _wait_token_gather(src_hbm, buf_ref.at[slot], sem.at[slot])
    return slot


def _moe_kernel(tg_ref, src_ref, x_hbm, w1_ref, w3_ref, w2_ref, o_ref, xb_ref, sem, w1b_ref, w3b_ref, w2b_ref):
    i = pl.program_id(0)
    slot = _prefetched_tile(src_ref, i, pl.num_programs(0), x_hbm, xb_ref, sem, XROWS)

    @pl.when((i == 0) | (tg_ref[i] != tg_ref[jnp.maximum(i - 1, 0)]))
    def _():
        for e in range(E_PER):
            w1b_ref[e] = w1_ref[e].astype(BF16)
            w3b_ref[e] = w3_ref[e].astype(BF16)
            w2b_ref[e] = w2_ref[e].astype(BF16)

    xb = xb_ref.at[slot]
    x = jnp.concatenate([xb[pl.ds(c, TM, stride=XROWS), :] for c in range(D // 128)], axis=1).astype(BF16)
    side = xb[pl.ds(D // 128, TM, stride=XROWS), :]
    y = jnp.zeros((TM, D), F32)
    for e in range(E_PER):
        h1 = jnp.dot(x, w1b_ref[e], preferred_element_type=F32)
        h3 = jnp.dot(x, w3b_ref[e], preferred_element_type=F32)
        hh = (_silu(h1) * h3 * side[:, e:e + 1]).astype(BF16)
        y = y + jnp.dot(hh, w2b_ref[e], preferred_element_type=F32)
    for c in range(D // 128):
        o_ref[pl.ds(c, TM, stride=YROWS), :] = y[:, 128 * c:128 * (c + 1)]


def _moe(tile_grp, src_idx, u2x, w1, w3, w2, layer):
    mp = src_idx.shape[0]
    wspec = lambda k, n: pl.BlockSpec((None, E_PER, k, n), lambda i, tg, si: (layer, tg[i], 0, 0),
                                      pipeline_mode=pl.Buffered(1))
    return pl.pallas_call(
        _moe_kernel,
        grid_spec=pltpu.PrefetchScalarGridSpec(
            num_scalar_prefetch=2, grid=(mp // TM,),
            in_specs=[pl.BlockSpec(memory_space=pl.ANY), wspec(D, D_FF), wspec(D, D_FF), wspec(D_FF, D)],
            out_specs=pl.BlockSpec((TM * YROWS, 128), lambda i, tg, si: (i, 0)),
            scratch_shapes=[pltpu.VMEM((2, TM * XROWS, 128), F32), pltpu.SemaphoreType.DMA((2,)),
                            pltpu.VMEM((E_PER, D, D_FF), BF16), pltpu.VMEM((E_PER, D, D_FF), BF16),
                            pltpu.VMEM((E_PER, D_FF, D), BF16)]),
        out_shape=jax.ShapeDtypeStruct((mp * YROWS, 128), F32),
        compiler_params=_cp(("arbitrary",), V7X_VMEM_LIMIT),
        name="moe",
    )(tile_grp, src_idx, u2x, w1, w3, w2)


def _combine_kernel(pos_ref, y_hbm, h_ref, mod_ref, lng_ref, lnb_ref, o_ref, yb_ref, sem, *, nt):
    step = pl.program_id(0) * nt + pl.program_id(1)
    slot = _prefetched_tile(pos_ref, step, pl.num_programs(0) * nt, y_hbm, yb_ref, sem, YROWS)
    yb = yb_ref.at[slot]
    y = jnp.concatenate([yb[pl.ds(c, TM, stride=YROWS), :] for c in range(D // 128)], axis=1)
    g2 = mod_ref[0, 0, 5:6, :]
    o_ref[0] = _ln(ALPHA * h_ref[0] + g2 * y) * lng_ref[0:1, :] + lnb_ref[0:1, :]


def _combine(pos, y_sorted, h1, modsel, lng, lnb, row_off):
    b, r, _ = h1.shape
    nt = r // TM
    full = lambda shape: pl.BlockSpec(shape, lambda i, j, p: (0,) * len(shape))
    return pl.pallas_call(
        functools.partial(_combine_kernel, nt=nt),
        grid_spec=pltpu.PrefetchScalarGridSpec(
            num_scalar_prefetch=1, grid=(b, nt),
            in_specs=[pl.BlockSpec(memory_space=pl.ANY),
                      pl.BlockSpec((1, TM, D), lambda i, j, p: (i, j, 0)),
                      pl.BlockSpec((1, 1, 6, D), lambda i, j, p: (i, jnp.minimum(j + row_off, 1), 0, 0)),
                      full((1, D)), full((1, D))],
            out_specs=pl.BlockSpec((1, TM, D), lambda i, j, p: (i, j, 0)),
            scratch_shapes=[pltpu.VMEM((2, TM * YROWS, 128), F32), pltpu.SemaphoreType.DMA((2,))]),
        out_shape=jax.ShapeDtypeStruct((b, r, D), F32),
        compiler_params=_cp(("arbitrary", "arbitrary")),
        name="moe_combine",
    )(pos, y_sorted, h1, modsel, lng, lnb)


def _sort_by_group(grp):
    n = grp.shape[0]
    onehot = (grp[:, None] == jnp.arange(N_GRP, dtype=jnp.int32)[None, :]).astype(jnp.int32)
    csum = jnp.cumsum(onehot, axis=0)
    counts = csum[-1]
    padded = ((counts + TM - 1) // TM) * TM
    ends = jnp.cumsum(padded)
    starts = ends - padded
    rank = jnp.sum(csum * onehot, axis=1) - 1
    pos = (starts[grp] + rank).astype(jnp.int32)
    mp = n + N_GRP * TM
    src_idx = jnp.zeros((mp,), jnp.int32).at[pos].set(jnp.arange(n, dtype=jnp.int32))
    tile_start = jnp.arange(mp // TM, dtype=jnp.int32) * TM
    tile_grp = jnp.minimum(jnp.sum((tile_start[:, None] >= ends[None, :]).astype(jnp.int32), axis=1), N_GRP - 1)
    return pos, src_idx, tile_grp.astype(jnp.int32)


def _moe_layer(u2x, grp, h1, modsel, experts, lng, lnb, row_off):
    pos, src_idx, tile_grp = _sort_by_group(grp.reshape(-1).astype(jnp.int32))
    y_sorted = _moe(tile_grp, src_idx, u2x, *experts)
    return _combine(pos, y_sorted, h1, modsel, lng, lnb, row_off)


def _rope_tables(rc, t):
    half = A_HD // 4
    inv = ROPE_BASE ** (-jnp.arange(half, dtype=F32) / half)
    tpos = jnp.arange(t)
    rows = (tpos // GRID_W).astype(F32)
    cols = (tpos % GRID_W).astype(F32)

    def cs(pos):
        ang = pos[:, None] * inv[None, :]
        c, s = jnp.cos(ang), jnp.sin(ang)
        return jnp.concatenate([c, c], -1), jnp.concatenate([-s, s], -1)

    cr, sr = cs(rows)
    cc_, sc_ = cs(cols)
    cos_h = jnp.concatenate([cr, cc_], -1)
    sin_h = jnp.concatenate([sr, sc_], -1)
    one = jnp.ones((t, A_HD), F32)
    zero = jnp.zeros((t, A_HD), F32)
    lat = jnp.stack([jnp.concatenate([cos_h, cos_h], -1), jnp.concatenate([sin_h, sin_h], -1),
                     jnp.concatenate([cos_h, one], -1), jnp.concatenate([sin_h, zero], -1)])
    ctx = jnp.stack([jnp.ones((CTX, 128), F32), jnp.zeros((CTX, 128), F32)] * 2)
    return jnp.concatenate([ctx, lat], axis=1)


def _dft_tables(t):
    def cos_sin(num, den):
        ang = (2.0 * math.pi / den) * (num % den).astype(F32)
        return jnp.cos(ang), jnp.sin(ang)

    k = jnp.arange(t, dtype=jnp.int32)[:, None]
    ca, sa = cos_sin(k * jnp.arange(t // GRID_W, dtype=jnp.int32)[None, :], t // GRID_W)
    cb, sb = cos_sin(k * jnp.arange(GRID_W, dtype=jnp.int32)[None, :], t)
    ct = (ca[:, :, None] * cb[:, None, :] - sa[:, :, None] * sb[:, None, :]).reshape(t, t)
    st = (sa[:, :, None] * cb[:, None, :] + ca[:, :, None] * sb[:, None, :]).reshape(t, t)
    c = jnp.arange(F_GD, dtype=jnp.int32)
    cc, sc = cos_sin(c[:, None] * c[None, :], F_GD)
    eye = jnp.eye(F_GROUPS, dtype=F32)
    cs = jnp.concatenate([jnp.kron(eye, cc), jnp.kron(eye, sc)], axis=1)
    return cs.astype(BF16), ct.astype(BF16), st.astype(BF16)


def _pad_heads(w, width, padded):
    lead = w.shape[:-1]
    w = w.reshape(lead + (R_HEADS, width))
    w = jnp.pad(w, [(0, 0)] * len(lead) + [(0, 0), (0, padded - width)])
    return w.reshape(lead + (R_HEADS * padded,))


EVEN_SEGS = tuple([(128 * g, 128 * (g + 1), 0) for g in range(4)] + [(512, 3072, None)]
                  + [(3072 + 128 * g, 3072 + 128 * (g + 1), 1) for g in range(2)])
ODD_W = 512 + 512 + 1024 + 1024 + 256 + 128
ODD_SEGS = ((0, 1024, None), (1024, 2048, None), (2048, 3072, None), (3072, ODD_W, None))


def _layer_even(h, modsel, tbl, w_in, sink, lb_logits, g_norm, w_out, lng, lnb, w_router, b_router, experts):
    aq, ak, av, ff, fb, hq, hi, hg = jnp.split(w_in, np.cumsum([512, 128, 128, 512, 512, 512, 512]).tolist(), axis=1)
    kv = [jnp.concatenate([ak[:, A_HD * j:A_HD * (j + 1)], av[:, A_HD * j:A_HD * (j + 1)]], axis=1)
          for j in range(A_KV)]
    w = jnp.concatenate([aq, ff, fb, hq, hi, hg] + kv, axis=1).astype(BF16)
    proj = _inproj(h, modsel, w, tbl, EVEN_SEGS)
    o_att = _attention(proj, sink, 3072 // (2 * A_HD * A_KV))
    lb = jnp.cumsum(jax.nn.softmax(lb_logits.astype(F32), axis=1), axis=1)[:, 0]
    o_f, o_b = _hgrn_scan(proj, lb, 3, 4, 1, 2)
    wo = w_out.astype(BF16)
    h1, u2x, grp = _outproj(b_router, o_att, 0, 512, o_f, o_b, proj, 5, h, modsel, g_norm.reshape(1, -1),
                            wo[:512], wo[512:], lng[0:1], lnb[0:1], w_router, dv=128, dvp=128, row_off=0)
    return _moe_layer(u2x, grp, h1, modsel, experts, lng[1:2], lnb[1:2], 0)


def _layer_odd(h, modsel, tbl, dft, w_in, gate_w, gate_b, g_norm, w_out, lng, lnb, w_router, b_router, experts, t):
    q, k, v, rf, rb, g, z = jnp.split(w_in, np.cumsum([384, 384, 768, 16, 16, 768]).tolist(), axis=1)
    r = jnp.pad(jnp.concatenate([rf, rb], axis=1), ((0, 0), (0, 96)))
    w = jnp.concatenate([_pad_heads(q, GLA_DK, DKP), _pad_heads(k, GLA_DK, DKP), _pad_heads(v, GLA_DV, GLA_DVP),
                         _pad_heads(g, GLA_DV, GLA_DVP), z, r], axis=1).astype(BF16)
    proj = _inproj(h, modsel, w, tbl, ODD_SEGS)
    gw = _pad_heads(gate_w, GLA_DK, DKP)
    gwf = jnp.pad(gw[0], ((0, 112), (0, 0)))
    gwb = jnp.pad(gw[1], ((16, 96), (0, 0)))
    gb = _pad_heads(gate_b, GLA_DK, DKP)
    o_f, o_b = _gla_scan(proj, gwf, gwb, gb, 0, 1, 1, 3328 // 128)
    cs, ct, st = dft
    four = _fourier(proj, 3072 // (F_GROUPS * F_GD), cs, ct, st, t)
    wo = w_out
    wr = jnp.pad(wo[:768].reshape(R_HEADS, GLA_DV, D), ((0, 0), (0, GLA_DVP - GLA_DV), (0, 0))).reshape(-1, D)
    h1, u2x, grp = _outproj(b_router, four, 0, 256, o_f, o_b, proj, 2, h, modsel,
                            _pad_heads(g_norm, GLA_DV, GLA_DVP).reshape(1, -1),
                            wo[768:].astype(BF16), wr.astype(BF16), lng[0:1], lnb[0:1], w_router,
                            dv=GLA_DV, dvp=GLA_DVP, row_off=CTX // TM)
    return _moe_layer(u2x, grp, h1, modsel, experts, lng[1:2], lnb[1:2], CTX // TM)


def kernel(x, c, ctx, c_ctx, w_ada, b_ada, ln_g, ln_b, w_in_even, attn_sink, hgrn_lb_logits, hgrn_norm, w_out_even, w_in_odd, gla_gate_w, gla_gate_b, gla_norm, w_out_odd, w_router, b_router, w_expert_gate, w_expert_up, w_expert_down):
    b, t, _ = x.shape
    rc = CTX + t
    assert ctx.shape[1] == CTX
    cc = jnp.zeros((16, D), F32).at[:b].set(c).at[b].set(c_ctx)
    mods = _ada_mods(cc, w_ada, b_ada).reshape(2, 16, 6, D)

    def modsel(l):
        return jnp.stack([jnp.broadcast_to(mods[l, b], (b, 6, D)), mods[l, :b]], axis=1)

    tbl = _rope_tables(rc, t)
    wrt = jnp.pad(w_router, ((0, 0), (0, GW - N_EXP)))
    experts = (w_expert_gate, w_expert_up, w_expert_down)
    h = _layer_even((ctx, x), modsel(0), tbl, w_in_even[0], attn_sink[0], hgrn_lb_logits, hgrn_norm[0], w_out_even[0],
                    ln_g[0], ln_b[0], wrt, b_router, experts + (0,))
    return _layer_odd(h, modsel(1), tbl, _dft_tables(t), w_in_odd[0], gla_gate_w[0], gla_gate_b[0], gla_norm[0],
                      w_out_odd[0], ln_g[1], ln_b[1], wrt, b_router, experts + (1,), t)
```

```python
import functools
import math

import numpy as np
import jax
import jax.numpy as jnp
from jax import lax
from jax.experimental import pallas as pl
from jax.experimental.pallas import tpu as pltpu

F32 = jnp.float32
BF16 = jnp.bfloat16
HIGHEST = lax.Precision.HIGHEST

D = 1024
CTX = 256
GRID_W = 64
LN_EPS = 1e-5
NEG = -1e30
ALPHA = 4.0 ** 0.25
ROPE_BASE = 10000.0

A_HEADS, A_KV, A_HD, A_BLK = 8, 2, 64, 128
R_HEADS = 4
DKP = 128
GLA_DK, GLA_DV, GLA_DVP = 96, 192, 256
GLA_TAU = 16.0
CHUNK = 64
N_EXP, N_GRP, E_PER = 16, 4, 4
D_FF = 512
F_GROUPS, F_GD = 4, 64

TM = 256
GW = 128
XROWS = 16
YROWS = 8
V7X_VMEM_LIMIT = 56 * 1024 * 1024


def _cp(sem, vmem=None):
    return pltpu.CompilerParams(dimension_semantics=sem, vmem_limit_bytes=vmem)


def _ln(x):
    mu = jnp.mean(x, axis=-1, keepdims=True)
    xc = x - mu
    var = jnp.mean(xc * xc, axis=-1, keepdims=True)
    return xc * lax.rsqrt(var + LN_EPS)


def _silu(x):
    return x * jax.nn.sigmoid(x)


_NT = (((1,), (1,)), ((), ()))
_TN = (((0,), (0,)), ((), ()))


def _ada_kernel(c_ref, w_ref, b_ref, o_ref):
    s = _silu(c_ref[...])
    o_ref[0] = jnp.dot(s, w_ref[0], precision=HIGHEST, preferred_element_type=F32) + b_ref[0]


def _ada_mods(cc, w_ada, b_ada):
    depth, _, n = w_ada.shape
    tn = 1536
    return pl.pallas_call(
        _ada_kernel,
        grid=(depth, n // tn),
        in_specs=[pl.BlockSpec((16, D), lambda l, i: (0, 0)),
                  pl.BlockSpec((1, D, tn), lambda l, i: (l, 0, i)),
                  pl.BlockSpec((1, 1, tn), lambda l, i: (l, 0, i))],
        out_specs=pl.BlockSpec((1, 16, tn), lambda l, i: (l, 0, i)),
        out_shape=jax.ShapeDtypeStruct((depth, 16, n), F32),
        compiler_params=_cp(("arbitrary", "arbitrary"), 40 * 1024 * 1024),
        name="ada_mod",
    )(cc, w_ada, b_ada.reshape(depth, 1, n))


def _rows_of(h_refs):
    if len(h_refs) == 1:
        return h_refs[0][0]
    return jnp.where(pl.program_id(1) == 0, h_refs[0][0], h_refs[1][0])


def _row_specs(h, row_off=0):
    if isinstance(h, tuple):
        assert row_off == 0 and h[0].shape[1] == TM
        return [pl.BlockSpec((1, TM, D), lambda i, j: (i, 0, 0)),
                pl.BlockSpec((1, TM, D), lambda i, j: (i, jnp.maximum(j - 1, 0), 0))], list(h)
    return [pl.BlockSpec((1, TM, D), lambda i, j: (i, j + row_off, 0))], [h]


def _inproj_kernel(*refs, segs):
    *h_refs, mod_ref, w_ref, tbl_ref, o_ref = refs
    x = _rows_of(h_refs)
    u = (_ln(x) * (1.0 + mod_ref[0, 0, 1:2, :]) + mod_ref[0, 0, 0:1, :]).astype(BF16)
    lane = lax.broadcasted_iota(jnp.int32, (x.shape[0], 128), 1)
    low = (lane % 32) < 16
    for c0, c1, rope in segs:
        acc = jnp.dot(u, w_ref[:, c0:c1], preferred_element_type=F32)
        if rope is not None:
            partner = jnp.where(low, pltpu.roll(acc, 112, axis=1), pltpu.roll(acc, 16, axis=1))
            acc = acc * tbl_ref[2 * rope] + partner * tbl_ref[2 * rope + 1]
        o_ref[0, :, c0:c1] = acc.astype(BF16)


def _inproj(h, modsel, w, tbl, segs):
    b, rc = modsel.shape[0], tbl.shape[1]
    n = w.shape[1]
    nt = rc // TM
    h_specs, h_args = _row_specs(h)
    return pl.pallas_call(
        functools.partial(_inproj_kernel, segs=segs),
        grid=(b, nt),
        in_specs=h_specs + [pl.BlockSpec((1, 1, 6, D), lambda i, j: (i, jnp.minimum(j, 1), 0, 0)),
                            pl.BlockSpec((D, n), lambda i, j: (0, 0)),
                            pl.BlockSpec((4, TM, 128), lambda i, j: (0, j, 0))],
        out_specs=pl.BlockSpec((1, TM, n), lambda i, j: (i, j, 0)),
        out_shape=jax.ShapeDtypeStruct((b, rc, n), BF16),
        compiler_params=_cp(("parallel", "arbitrary"), V7X_VMEM_LIMIT),
        name="inproj",
    )(*h_args, modsel, w, tbl)


def _attn_kernel(sink_ref, q_ref, kvc_ref, kvp_ref, kvm_ref, kvn_ref, bias_ref, o_ref):
    blk = pl.program_id(1)
    g_per = A_HEADS // A_KV
    nq = g_per * A_BLK
    q = q_ref[0] * (A_HD ** -0.5)
    rgrp = lax.broadcasted_iota(jnp.int32, (nq, 1), 0) // A_BLK
    q4, sink_col = [], []
    for h in range(A_KV):
        q4.append(jnp.concatenate([q[:, A_HD * (g_per * h + g):A_HD * (g_per * h + g + 1)] for g in range(g_per)],
                                  axis=0))
        col = jnp.zeros((nq, 1), F32)
        for g in range(g_per):
            col = jnp.where(rgrp == g, sink_ref[h * g_per + g], col)
        sink_col.append(col)

    def attend(kv, bias):
        s = [lax.dot_general(q4[h], kv[:, 2 * A_HD * h:2 * A_HD * h + A_HD], _NT, preferred_element_type=F32)
             for h in range(A_KV)]
        if bias is not None:
            bias4 = jnp.concatenate([bias] * g_per, axis=0)
            s = [x + bias4 for x in s]
        m = [jnp.maximum(jnp.max(s[h], axis=-1, keepdims=True), sink_col[h]) for h in range(A_KV)]
        p = [jnp.exp(s[h] - m[h]) for h in range(A_KV)]
        den = [jnp.sum(p[h], axis=-1, keepdims=True) + jnp.exp(sink_col[h] - m[h]) for h in range(A_KV)]
        o = [jnp.dot(p[h].astype(BF16), kv[:, 2 * A_HD * h + A_HD:2 * A_HD * (h + 1)], preferred_element_type=F32)
             / den[h] for h in range(A_KV)]
        o_ref[0] = jnp.concatenate([o[h][A_BLK * g:A_BLK * (g + 1), :] for h in range(A_KV) for g in range(g_per)],
                                   axis=1).astype(BF16)

    @pl.when(blk < CTX // A_BLK)
    def _():
        attend(kvc_ref[0], None)

    @pl.when(blk >= CTX // A_BLK)
    def _():
        attend(jnp.concatenate([kvc_ref[0], kvp_ref[0], kvm_ref[0], kvn_ref[0]], axis=0), bias_ref[0])


def _attn_bias():
    nk = CTX + 3 * A_BLK
    qi = jnp.arange(A_BLK, dtype=jnp.int32)[:, None]
    col = jnp.arange(nk, dtype=jnp.int32)[None, :]
    rel = col - (CTX + A_BLK)
    dist = qi - rel

    def one(lo, hi):
        in_win = (dist <= A_BLK) & (dist >= -A_BLK) & (rel >= lo) & (rel < hi)
        return jnp.where((col < CTX) | in_win, 0.0, NEG).astype(F32)

    return jnp.stack([one(0, 2 * A_BLK), one(-A_BLK, 2 * A_BLK), one(-A_BLK, A_BLK)])


def _attention(proj, sink, kv_blk):
    b, rc, _ = proj.shape
    nb = rc // A_BLK
    c_b = CTX // A_BLK
    assert nb - c_b >= 2
    kw = 2 * A_HD * A_KV

    def band(off):
        return lambda i, j: (i, jnp.clip(j + off, c_b, nb - 1), kv_blk)

    def which_bias(i, j):
        return (jnp.where(j <= c_b, 0, jnp.where(j == nb - 1, 2, 1)), 0, 0)

    return pl.pallas_call(
        _attn_kernel,
        grid=(b, nb),
        in_specs=[pl.BlockSpec(memory_space=pltpu.SMEM),
                  pl.BlockSpec((1, A_BLK, A_HEADS * A_HD), lambda i, j: (i, j, 0)),
                  pl.BlockSpec((1, CTX, kw), lambda i, j: (i, 0, kv_blk)),
                  pl.BlockSpec((1, A_BLK, kw), band(-1)),
                  pl.BlockSpec((1, A_BLK, kw), band(0)),
                  pl.BlockSpec((1, A_BLK, kw), band(1)),
                  pl.BlockSpec((1, A_BLK, CTX + 3 * A_BLK), which_bias)],
        out_specs=pl.BlockSpec((1, A_BLK, A_HEADS * A_HD), lambda i, j: (i, j, 0)),
        out_shape=jax.ShapeDtypeStruct((b, rc, A_HEADS * A_HD), BF16),
        compiler_params=_cp(("parallel", "arbitrary")),
        name="win_attn",
    )(sink, proj, proj, proj, proj, proj, _attn_bias())


def _scan_blocks(dirs, *, dvp):
    r_i = lax.broadcasted_iota(jnp.int32, (CHUNK, CHUNK), 0)
    c_i = lax.broadcasted_iota(jnp.int32, (CHUNK, CHUNK), 1)
    units = []
    for q, k, v, logf, s_ref, o_ref, reverse in dirs:
        nchunk = q.shape[0] // CHUNK
        tri = (r_i <= c_i) if reverse else (r_i >= c_i)
        tri3 = jnp.concatenate([jnp.where(tri, 1.0, 0.0).astype(BF16)] * 3, axis=1)
        for c in (range(nchunk - 1, -1, -1) if reverse else range(nchunk)):
            sl = slice(c * CHUNK, (c + 1) * CHUNK)
            units.append(dict(q=q[sl], k=k[sl], v=v[sl].astype(BF16), lf=logf[sl], sl=sl, tri=tri, tri3=tri3,
                              i_tot=0 if reverse else CHUNK - 1,
                              i_ref=CHUNK // 2 - 1 if reverse else CHUNK // 2, s_ref=s_ref, o_ref=o_ref))
    heads = [(slice(h * DKP, (h + 1) * DKP), slice(h * dvp, (h + 1) * dvp)) for h in range(R_HEADS)]
    for u in units:
        lf = u["lf"]
        hi = lf.astype(BF16)
        r1 = lf - hi.astype(F32)
        mid = r1.astype(BF16)
        lo = (r1 - mid.astype(F32)).astype(BF16)
        u["bsum"] = jnp.dot(u["tri3"], jnp.concatenate([hi, mid, lo], axis=0), preferred_element_type=F32)
    for u in units:
        bsum = u["bsum"]
        tot = bsum[u["i_tot"]:u["i_tot"] + 1, :]
        ref = bsum[u["i_ref"]:u["i_ref"] + 1, :]
        qg = u["q"] * jnp.exp(bsum - ref)
        kg = u["k"] * jnp.exp(ref - bsum)
        u["qb"] = (qg * jnp.exp(ref)).astype(BF16)
        u["kd"] = (kg * jnp.exp(tot - ref)).astype(BF16)
        u["qg"] = qg.astype(BF16)
        u["kg"] = kg.astype(BF16)
        u["dec"] = jnp.exp(tot)
    for u in units:
        u["att"] = [lax.dot_general(u["qg"][:, ks], u["kg"][:, ks], _NT, preferred_element_type=F32)
                    for ks, _ in heads]
        u["ds"] = [lax.dot_general(u["v"][:, vs], u["kd"][:, ks], _TN, preferred_element_type=F32)
                   for ks, vs in heads]
    for u in units:
        u["oi"] = [jnp.dot(jnp.where(u["tri"], a, 0.0).astype(BF16), u["v"][:, vs], preferred_element_type=F32)
                   for a, (_, vs) in zip(u["att"], heads)]
    state = {}
    for u in units:
        key = id(u["s_ref"])
        if key not in state:
            state[key] = [u["s_ref"][h] for h in range(R_HEADS)]
        st = state[key]
        outs = []
        for h, (ks, _) in enumerate(heads):
            outs.append(u["oi"][h] + lax.dot_general(u["qb"][:, ks], st[h].astype(BF16), _NT,
                                                     preferred_element_type=F32))
            st[h] = st[h] * u["dec"][:, ks] + u["ds"][h]
        u["o_ref"][0, u["sl"], :] = jnp.concatenate(outs, axis=1).astype(BF16)
    for _, _, _, _, s_ref, _, _ in dirs:
        for h in range(R_HEADS):
            s_ref[h] = state[id(s_ref)][h]


def _hgrn_scan_kernel(lb_ref, qf_ref, vf_ref, zf_ref, qb_ref, vb_ref, zb_ref, of_ref, ob_ref, sf_ref, sb_ref):
    @pl.when(pl.program_id(1) == 0)
    def _():
        sf_ref[...] = jnp.zeros_like(sf_ref)
        sb_ref[...] = jnp.zeros_like(sb_ref)

    def gates(z, lb):
        f = lb + (1.0 - lb) * jax.nn.sigmoid(z.astype(F32))
        return 1.0 - f, jnp.log(f)

    kf, lf = gates(zf_ref[0], lb_ref[0:1, :])
    kb, lbw = gates(zb_ref[0], lb_ref[1:2, :])
    _scan_blocks([(qf_ref[0], kf, vf_ref[0], lf, sf_ref, of_ref, False),
                  (qb_ref[0], kb, vb_ref[0], lbw, sb_ref, ob_ref, True)], dvp=128)


def _bwd_blk(j, nt):
    return jnp.where(j == 0, 0, nt - j)


def _hgrn_scan(proj, lb, cq, cv, cf, cb):
    b, rc, _ = proj.shape
    nt = rc // TM
    w = R_HEADS * DKP

    def fw(col):
        return pl.BlockSpec((1, TM, w), lambda i, j: (i, j, col))

    def bw(col):
        return pl.BlockSpec((1, TM, w), lambda i, j: (i, _bwd_blk(j, nt), col))

    o_sd = jax.ShapeDtypeStruct((b, rc, w), BF16)
    return pl.pallas_call(
        _hgrn_scan_kernel,
        grid=(b, nt),
        in_specs=[pl.BlockSpec((2, w), lambda i, j: (0, 0)),
                  fw(cq), fw(cv), fw(cf), bw(cq), bw(cv), bw(cb)],
        out_specs=[pl.BlockSpec((1, TM, w), lambda i, j: (i, j, 0)),
                   pl.BlockSpec((1, TM, w), lambda i, j: (i, _bwd_blk(j, nt), 0))],
        out_shape=[o_sd, o_sd],
        scratch_shapes=[pltpu.VMEM((R_HEADS, 128, DKP), F32), pltpu.VMEM((R_HEADS, 128, DKP), F32)],
        compiler_params=_cp(("parallel", "arbitrary"), 40 * 1024 * 1024),
        name="hgrn_scan",
    )(lb, proj, proj, proj, proj, proj, proj)


def _gla_scan_kernel(gwf_ref, gwb_ref, gb_ref, qf_ref, kf_ref, vf_ref, rf_ref, qb_ref, kb_ref, vb_ref, rb_ref,
                     of_ref, ob_ref, sf_ref, sb_ref):
    @pl.when(pl.program_id(1) == 0)
    def _():
        sf_ref[...] = jnp.zeros_like(sf_ref)
        sb_ref[...] = jnp.zeros_like(sb_ref)

    def decay(r, gw, gb):
        zz = jnp.dot(r.astype(F32), gw, precision=HIGHEST, preferred_element_type=F32) + gb
        return (jnp.minimum(zz, 0.0) - jnp.log(1.0 + jnp.exp(-jnp.abs(zz)))) * (1.0 / GLA_TAU)

    qs = GLA_DK ** -0.5
    _scan_blocks([(qf_ref[0].astype(F32) * qs, kf_ref[0], vf_ref[0],
                   decay(rf_ref[0], gwf_ref[...], gb_ref[0:1, :]), sf_ref, of_ref, False),
                  (qb_ref[0].astype(F32) * qs, kb_ref[0], vb_ref[0],
                   decay(rb_ref[0], gwb_ref[...], gb_ref[1:2, :]), sb_ref, ob_ref, True)], dvp=GLA_DVP)


def _gla_scan(proj, gwf, gwb, gb, cq, ck, cv, cr):
    b, rc, _ = proj.shape
    nt = rc // TM
    wk = R_HEADS * DKP
    wv = R_HEADS * GLA_DVP

    def spec(width, col, back):
        if back:
            return pl.BlockSpec((1, TM, width), lambda i, j: (i, _bwd_blk(j, nt), col))
        return pl.BlockSpec((1, TM, width), lambda i, j: (i, j, col))

    o_sd = jax.ShapeDtypeStruct((b, rc, wv), BF16)
    full = lambda shape: pl.BlockSpec(shape, lambda i, j: (0,) * len(shape))
    return pl.pallas_call(
        _gla_scan_kernel,
        grid=(b, nt),
        in_specs=[full((128, wk)), full((128, wk)), full((2, wk)),
                  spec(wk, cq, False), spec(wk, ck, False), spec(wv, cv, False), spec(128, cr, False),
                  spec(wk, cq, True), spec(wk, ck, True), spec(wv, cv, True), spec(128, cr, True)],
        out_specs=[spec(wv, 0, False), spec(wv, 0, True)],
        out_shape=[o_sd, o_sd],
        scratch_shapes=[pltpu.VMEM((R_HEADS, GLA_DVP, DKP), F32), pltpu.VMEM((R_HEADS, GLA_DVP, DKP), F32)],
        compiler_params=_cp(("parallel", "arbitrary"), 48 * 1024 * 1024),
        name="gla_scan",
    )(gwf, gwb, gb, proj, proj, proj, proj, proj, proj, proj, proj)


def _fourier_chan_kernel(z_ref, cs_ref, o_ref):
    o_ref[0] = jnp.dot(z_ref[0], cs_ref[...], preferred_element_type=F32).astype(BF16)


def _fourier_seq_kernel(zz_ref, ct_ref, st_ref, o_ref, *, scale):
    w = F_GROUPS * F_GD
    y = jnp.dot(ct_ref[...], zz_ref[0, :, :w], preferred_element_type=F32)
    y = y - jnp.dot(st_ref[...], zz_ref[0, :, w:], preferred_element_type=F32)
    o_ref[0] = (y * scale).astype(BF16)


def _fourier(proj, zcol, cs, ct, st, t):
    b = proj.shape[0]
    w = F_GROUPS * F_GD
    c_t = CTX // TM
    nzt = t // TM
    zz = pl.pallas_call(
        _fourier_chan_kernel,
        grid=(b, nzt),
        in_specs=[pl.BlockSpec((1, TM, w), lambda i, j: (i, j + c_t, zcol)),
                  pl.BlockSpec((w, 2 * w), lambda i, j: (0, 0))],
        out_specs=pl.BlockSpec((1, TM, 2 * w), lambda i, j: (i, j, 0)),
        out_shape=jax.ShapeDtypeStruct((b, t, 2 * w), BF16),
        compiler_params=_cp(("parallel", "arbitrary")),
        name="fourier_chan",
    )(proj, cs)
    return pl.pallas_call(
        functools.partial(_fourier_seq_kernel, scale=1.0 / math.sqrt(t * F_GD)),
        grid=(b, nzt),
        in_specs=[pl.BlockSpec((1, t, 2 * w), lambda i, j: (i, 0, 0)),
                  pl.BlockSpec((TM, t), lambda i, j: (j, 0)),
                  pl.BlockSpec((TM, t), lambda i, j: (j, 0))],
        out_specs=pl.BlockSpec((1, TM, w), lambda i, j: (i, j, 0)),
        out_shape=jax.ShapeDtypeStruct((b, t, w), BF16),
        compiler_params=_cp(("parallel", "arbitrary"), 40 * 1024 * 1024),
        name="fourier_seq",
    )(zz, ct, st)


def _route(logits, rb_ref):
    lt = logits.T
    l = [lt[e:e + 1, :] for e in range(N_EXP)]
    m = functools.reduce(jnp.maximum, l)
    ex = [jnp.exp(v - m) for v in l]
    den = functools.reduce(lambda a, c: a + c, ex)
    p = [v / den for v in ex]
    sel = [p[e] + rb_ref[e] for e in range(N_EXP)]
    g_score, g_gate = [], []
    for g in range(N_GRP):
        s = sel[E_PER * g:E_PER * (g + 1)]
        chosen = []
        for i in range(E_PER):
            rank = jnp.zeros_like(s[i])
            for j in range(E_PER):
                if j == i:
                    continue
                ahead = (s[j] >= s[i]) if j < i else (s[j] > s[i])
                rank = rank + jnp.where(ahead, 1.0, 0.0)
            chosen.append(rank < 2.0)
        g_score.append(functools.reduce(lambda a, c: a + c,
                                        [jnp.where(chosen[i], s[i], 0.0) for i in range(E_PER)]))
        g_gate.append([jnp.where(chosen[i], p[E_PER * g + i], 0.0) for i in range(E_PER)])
    best = jnp.zeros_like(g_score[0])
    best_s = g_score[0]
    for g in range(1, N_GRP):
        upd = g_score[g] > best_s
        best = jnp.where(upd, float(g), best)
        best_s = jnp.where(upd, g_score[g], best_s)
    gate = []
    for i in range(E_PER):
        gi = g_gate[0][i]
        for g in range(1, N_GRP):
            gi = jnp.where(best == float(g), g_gate[g][i], gi)
        gate.append(gi)
    wsum = functools.reduce(lambda a, c: a + c, gate)
    row = lax.broadcasted_iota(jnp.int32, (GW, lt.shape[1]), 0)
    side = jnp.zeros((GW, lt.shape[1]), F32)
    for i in range(E_PER):
        side = jnp.where(row == i, gate[i] / wsum, side)
    return side.T, best


def _outproj_kernel(rb_ref, a_ref, of_ref, ob_ref, gt_ref, *refs, dvp, dv):
    *h_refs, mod_ref, gn_ref, wa_ref, wr_ref, lng_ref, lnb_ref, wrt_ref, h1_ref, u2_ref, grp_ref = refs
    o = of_ref[0].astype(F32) + ob_ref[0].astype(F32)
    parts = []
    for hh in range(R_HEADS):
        sl = slice(hh * dvp, (hh + 1) * dvp)
        oh = o[:, sl]
        ms = jnp.sum(oh * oh, axis=-1, keepdims=True) * (1.0 / dv)
        parts.append(oh * lax.rsqrt(ms + LN_EPS))
    rec = jnp.concatenate(parts, axis=1) * gn_ref[...] * _silu(gt_ref[0].astype(F32))
    y = jnp.dot(a_ref[0], wa_ref[...], preferred_element_type=F32)
    y = y + jnp.dot(rec.astype(BF16), wr_ref[...], preferred_element_type=F32)
    g1 = mod_ref[0, 0, 2:3, :]
    h1 = _ln(ALPHA * _rows_of(h_refs) + g1 * y) * lng_ref[0:1, :] + lnb_ref[0:1, :]
    h1_ref[0] = h1
    u2 = _ln(h1) * (1.0 + mod_ref[0, 0, 4:5, :]) + mod_ref[0, 0, 3:4, :]
    logits = jnp.dot(u2, wrt_ref[...], precision=HIGHEST, preferred_element_type=F32)
    side, best = _route(logits, rb_ref)
    grp_ref[0, 0] = best
    for c in range(D // 128):
        u2_ref[pl.ds(c, TM, stride=XROWS), :] = u2[:, 128 * c:128 * (c + 1)]
    u2_ref[pl.ds(D // 128, TM, stride=XROWS), :] = side
    for c in range(D // 128 + 1, XROWS):
        u2_ref[pl.ds(c, TM, stride=XROWS), :] = jnp.zeros((TM, 128), F32)


def _outproj(b_router, a, a_col, a_w, o_f, o_b, proj, g_col, h, modsel, gnorm, wa, wr, lng, lnb, wrt, *,
             dv, dvp, row_off):
    b, rc, _ = proj.shape
    nt = rc // TM - row_off
    wrec = R_HEADS * dvp
    full = lambda shape: pl.BlockSpec(shape, lambda i, j: (0,) * len(shape))
    h_specs, h_args = _row_specs(h, row_off)
    return pl.pallas_call(
        functools.partial(_outproj_kernel, dvp=dvp, dv=dv),
        grid=(b, nt),
        in_specs=[pl.BlockSpec(memory_space=pltpu.SMEM),
                  pl.BlockSpec((1, TM, a_w), lambda i, j: (i, j + (row_off if a.shape[1] == rc else 0), a_col)),
                  pl.BlockSpec((1, TM, wrec), lambda i, j: (i, j + row_off, 0)),
                  pl.BlockSpec((1, TM, wrec), lambda i, j: (i, j + row_off, 0)),
                  pl.BlockSpec((1, TM, wrec), lambda i, j: (i, j + row_off, g_col))] + h_specs + [
                  pl.BlockSpec((1, 1, 6, D), lambda i, j: (i, jnp.minimum(j + row_off, 1), 0, 0)),
                  full((1, wrec)), full((a_w, D)), full((wrec, D)), full((1, D)), full((1, D)), full((D, GW))],
        out_specs=[pl.BlockSpec((1, TM, D), lambda i, j: (i, j, 0)),
                   pl.BlockSpec((TM * XROWS, 128), lambda i, j: (i * nt + j, 0)),
                   pl.BlockSpec((1, 1, 1, TM), lambda i, j: (i, j, 0, 0))],
        out_shape=[jax.ShapeDtypeStruct((b, nt * TM, D), F32),
                   jax.ShapeDtypeStruct((b * nt * TM * XROWS, 128), F32),
                   jax.ShapeDtypeStruct((b, nt, 1, TM), F32)],
        compiler_params=_cp(("parallel", "arbitrary"), V7X_VMEM_LIMIT),
        name="outproj",
    )(b_router, a, o_f, o_b, proj, *h_args, modsel, gnorm, wa, wr, lng, lnb, wrt)


def _start_token_gather(idx_ref, base, src_hbm, dst_ref, sem, rows):
    def body(r, carry):
        t = idx_ref[base + r]
        pltpu.make_async_copy(src_hbm.at[pl.ds(pl.multiple_of(t * rows, rows), rows), :],
                              dst_ref.at[pl.ds(pl.multiple_of(r * rows, rows), rows), :], sem).start()
        return carry

    lax.fori_loop(0, TM, body, 0, unroll=8)


def _wait_token_gather(src_hbm, dst_ref, sem):
    pltpu.make_async_copy(src_hbm.at[pl.ds(0, dst_ref.shape[0]), :], dst_ref, sem).wait()


def _gather_ring(idx_ref, first_of, step, last, src_hbm, buf_ref, sem, rows):
    slot = step % 2

    @pl.when(step == 0)
    def _():
        _start_token_gather(idx_ref, first_of(0), src_hbm, buf_ref.at[0], sem.at[0], rows)

    _wait_token_gather(src_hbm, buf_ref.at[slot], sem.at[slot])
    nxt = first_of(jnp.minimum(step + 1, last))
    nxt_buf, nxt_sem = buf_ref.at[1 - slot], sem.at[1 - slot]

    def issue(r0, r1):
        for r in range(r0, r1):
            t = idx_ref[nxt + r]
            pltpu.make_async_copy(src_hbm.at[pl.ds(pl.multiple_of(t * rows, rows), rows), :],
                                  nxt_buf.at[pl.ds(r * rows, rows), :], nxt_sem).start()

    def finish():
        @pl.when(step == last)
        def _():
            _wait_token_gather(src_hbm, nxt_buf, nxt_sem)

    return slot, issue, finish


def _moe_kernel(tg_ref, first_ref, src_ref, x_hbm, w1_ref, w3_ref, w2_ref, o_ref, xb_ref, sem,
                w1b_ref, w3b_ref, w2b_ref):
    i = pl.program_id(0)
    slot, issue, finish = _gather_ring(src_ref, lambda k: first_ref[k], i, pl.num_programs(0) - 1,
                                       x_hbm, xb_ref, sem, XROWS)

    @pl.when((i == 0) | (tg_ref[i] != tg_ref[jnp.maximum(i - 1, 0)]))
    def _():
        for e in range(E_PER):
            w1b_ref[e] = w1_ref[e].astype(BF16)
            w3b_ref[e] = w3_ref[e].astype(BF16)
            w2b_ref[e] = w2_ref[e].astype(BF16)

    xb = xb_ref.at[slot]
    x = jnp.concatenate([xb[pl.ds(c, TM, stride=XROWS), :] for c in range(D // 128)], axis=1).astype(BF16)
    side = xb[pl.ds(D // 128, TM, stride=XROWS), :]
    y = jnp.zeros((TM, D), F32)
    per = TM // E_PER
    for e in range(E_PER):
        issue(e * per, (e + 1) * per)
        h1 = jnp.dot(x, w1b_ref[e], preferred_element_type=F32)
        h3 = jnp.dot(x, w3b_ref[e], preferred_element_type=F32)
        hh = (_silu(h1) * h3 * side[:, e:e + 1]).astype(BF16)
        y = y + jnp.dot(hh, w2b_ref[e], preferred_element_type=F32)
    for c in range(D // 128):
        o_ref[pl.ds(c, TM, stride=YROWS), :] = y[:, 128 * c:128 * (c + 1)]
    finish()


def _moe(tile_grp, tile_first, order, u2x, w1, w3, w2, layer):
    mp = tile_grp.shape[0] * TM
    wspec = lambda k, n: pl.BlockSpec((None, E_PER, k, n), lambda i, tg, tf, od: (layer, tg[i], 0, 0),
                                      pipeline_mode=pl.Buffered(1))
    return pl.pallas_call(
        _moe_kernel,
        grid_spec=pltpu.PrefetchScalarGridSpec(
            num_scalar_prefetch=3, grid=(mp // TM,),
            in_specs=[pl.BlockSpec(memory_space=pl.ANY), wspec(D, D_FF), wspec(D, D_FF), wspec(D_FF, D)],
            out_specs=pl.BlockSpec((TM * YROWS, 128), lambda i, tg, tf, od: (i, 0)),
            scratch_shapes=[pltpu.VMEM((2, TM * XROWS, 128), F32), pltpu.SemaphoreType.DMA((2,)),
                            pltpu.VMEM((E_PER, D, D_FF), BF16), pltpu.VMEM((E_PER, D, D_FF), BF16),
                            pltpu.VMEM((E_PER, D_FF, D), BF16)]),
        out_shape=jax.ShapeDtypeStruct((mp * YROWS, 128), F32),
        compiler_params=_cp(("arbitrary",), V7X_VMEM_LIMIT),
        name="moe",
    )(tile_grp, tile_first, order, u2x, w1, w3, w2)


def _combine_kernel(pos_ref, y_hbm, h_ref, mod_ref, lng_ref, lnb_ref, o_ref, yb_ref, sem, *, nt):
    step = pl.program_id(0) * nt + pl.program_id(1)
    slot, issue, finish = _gather_ring(pos_ref, lambda k: k * TM, step, pl.num_programs(0) * nt - 1,
                                       y_hbm, yb_ref, sem, YROWS)
    yb = yb_ref.at[slot]
    g2 = mod_ref[0, 0, 5:6, :]
    strip = TM // 4
    for r0 in range(0, TM, strip):
        issue(r0, r0 + strip)
        y = jnp.concatenate([yb[pl.ds(r0 * YROWS + c, strip, stride=YROWS), :] for c in range(D // 128)], axis=1)
        o_ref[0, r0:r0 + strip, :] = (_ln(ALPHA * h_ref[0, r0:r0 + strip, :] + g2 * y) * lng_ref[0:1, :]
                                      + lnb_ref[0:1, :])
    finish()


def _combine(pos, y_sorted, h1, modsel, lng, lnb, row_off):
    b, r, _ = h1.shape
    nt = r // TM
    full = lambda shape: pl.BlockSpec(shape, lambda i, j, p: (0,) * len(shape))
    return pl.pallas_call(
        functools.partial(_combine_kernel, nt=nt),
        grid_spec=pltpu.PrefetchScalarGridSpec(
            num_scalar_prefetch=1, grid=(b, nt),
            in_specs=[pl.BlockSpec(memory_space=pl.ANY),
                      pl.BlockSpec((1, TM, D), lambda i, j, p: (i, j, 0)),
                      pl.BlockSpec((1, 1, 6, D), lambda i, j, p: (i, jnp.minimum(j + row_off, 1), 0, 0)),
                      full((1, D)), full((1, D))],
            out_specs=pl.BlockSpec((1, TM, D), lambda i, j, p: (i, j, 0)),
            scratch_shapes=[pltpu.VMEM((2, TM * YROWS, 128), F32), pltpu.SemaphoreType.DMA((2,))]),
        out_shape=jax.ShapeDtypeStruct((b, r, D), F32),
        compiler_params=_cp(("arbitrary", "arbitrary")),
        name="moe_combine",
    )(pos, y_sorted, h1, modsel, lng, lnb)


def _sort_by_group(grp):
    n = grp.shape[0]
    onehot = (grp[:, None] == jnp.arange(N_GRP, dtype=jnp.int32)[None, :]).astype(jnp.int32)
    csum = jnp.cumsum(onehot, axis=0)
    counts = csum[-1]
    padded = ((counts + TM - 1) // TM) * TM
    ends = jnp.cumsum(padded)
    starts = ends - padded
    rank = jnp.sum(csum * onehot, axis=1) - 1
    pos = (starts[grp] + rank).astype(jnp.int32)
    _, order = lax.sort((grp, jnp.arange(n, dtype=jnp.int32)), num_keys=1, is_stable=True)
    order = jnp.concatenate([order, jnp.zeros((TM,), jnp.int32)])
    ntile = n // TM + N_GRP
    tile_start = jnp.arange(ntile, dtype=jnp.int32) * TM
    tile_grp = jnp.minimum(jnp.sum((tile_start[:, None] >= ends[None, :]).astype(jnp.int32), axis=1), N_GRP - 1)
    first = (jnp.cumsum(counts) - counts)[tile_grp] + tile_start - starts[tile_grp]
    return pos, order, tile_grp.astype(jnp.int32), jnp.minimum(first, n).astype(jnp.int32)


def _moe_layer(u2x, grp, h1, modsel, experts, lng, lnb, row_off):
    pos, order, tile_grp, tile_first = _sort_by_group(grp.reshape(-1).astype(jnp.int32))
    y_sorted = _moe(tile_grp, tile_first, order, u2x, *experts)
    return _combine(pos, y_sorted, h1, modsel, lng, lnb, row_off)


def _rope_tables(rc, t):
    half = A_HD // 4
    inv = ROPE_BASE ** (-jnp.arange(half, dtype=F32) / half)
    tpos = jnp.arange(t)
    rows = (tpos // GRID_W).astype(F32)
    cols = (tpos % GRID_W).astype(F32)

    def cs(pos):
        ang = pos[:, None] * inv[None, :]
        c, s = jnp.cos(ang), jnp.sin(ang)
        return jnp.concatenate([c, c], -1), jnp.concatenate([-s, s], -1)

    cr, sr = cs(rows)
    cc_, sc_ = cs(cols)
    cos_h = jnp.concatenate([cr, cc_], -1)
    sin_h = jnp.concatenate([sr, sc_], -1)
    one = jnp.ones((t, A_HD), F32)
    zero = jnp.zeros((t, A_HD), F32)
    lat = jnp.stack([jnp.concatenate([cos_h, cos_h], -1), jnp.concatenate([sin_h, sin_h], -1),
                     jnp.concatenate([cos_h, one], -1), jnp.concatenate([sin_h, zero], -1)])
    ctx = jnp.stack([jnp.ones((CTX, 128), F32), jnp.zeros((CTX, 128), F32)] * 2)
    return jnp.concatenate([ctx, lat], axis=1)


def _dft_tables(t):
    def cos_sin(num, den):
        ang = (2.0 * math.pi / den) * (num % den).astype(F32)
        return jnp.cos(ang), jnp.sin(ang)

    k = jnp.arange(t, dtype=jnp.int32)[:, None]
    ca, sa = cos_sin(k * jnp.arange(t // GRID_W, dtype=jnp.int32)[None, :], t // GRID_W)
    cb, sb = cos_sin(k * jnp.arange(GRID_W, dtype=jnp.int32)[None, :], t)
    ct = (ca[:, :, None] * cb[:, None, :] - sa[:, :, None] * sb[:, None, :]).reshape(t, t)
    st = (sa[:, :, None] * cb[:, None, :] + ca[:, :, None] * sb[:, None, :]).reshape(t, t)
    c = jnp.arange(F_GD, dtype=jnp.int32)
    cc, sc = cos_sin(c[:, None] * c[None, :], F_GD)
    eye = jnp.eye(F_GROUPS, dtype=F32)
    cs = jnp.concatenate([jnp.kron(eye, cc), jnp.kron(eye, sc)], axis=1)
    return cs.astype(BF16), ct.astype(BF16), st.astype(BF16)


def _pad_heads(w, width, padded):
    lead = w.shape[:-1]
    w = w.reshape(lead + (R_HEADS, width))
    w = jnp.pad(w, [(0, 0)] * len(lead) + [(0, 0), (0, padded - width)])
    return w.reshape(lead + (R_HEADS * padded,))


EVEN_SEGS = tuple([(128 * g, 128 * (g + 1), 0) for g in range(4)] + [(512, 3072, None)]
                  + [(3072 + 128 * g, 3072 + 128 * (g + 1), 1) for g in range(2)])
ODD_W = 512 + 512 + 1024 + 1024 + 256 + 128
ODD_SEGS = ((0, 1024, None), (1024, 2048, None), (2048, 3072, None), (3072, ODD_W, None))


def _layer_even(h, modsel, tbl, w_in, sink, lb_logits, g_norm, w_out, lng, lnb, w_router, b_router, experts):
    aq, ak, av, ff, fb, hq, hi, hg = jnp.split(w_in, np.cumsum([512, 128, 128, 512, 512, 512, 512]).tolist(), axis=1)
    kv = [jnp.concatenate([ak[:, A_HD * j:A_HD * (j + 1)], av[:, A_HD * j:A_HD * (j + 1)]], axis=1)
          for j in range(A_KV)]
    w = jnp.concatenate([aq, ff, fb, hq, hi, hg] + kv, axis=1).astype(BF16)
    proj = _inproj(h, modsel, w, tbl, EVEN_SEGS)
    o_att = _attention(proj, sink, 3072 // (2 * A_HD * A_KV))
    lb = jnp.cumsum(jax.nn.softmax(lb_logits.astype(F32), axis=1), axis=1)[:, 0]
    o_f, o_b = _hgrn_scan(proj, lb, 3, 4, 1, 2)
    wo = w_out.astype(BF16)
    h1, u2x, grp = _outproj(b_router, o_att, 0, 512, o_f, o_b, proj, 5, h, modsel, g_norm.reshape(1, -1),
                            wo[:512], wo[512:], lng[0:1], lnb[0:1], w_router, dv=128, dvp=128, row_off=0)
    return _moe_layer(u2x, grp, h1, modsel, experts, lng[1:2], lnb[1:2], 0)


def _layer_odd(h, modsel, tbl, dft, w_in, gate_w, gate_b, g_norm, w_out, lng, lnb, w_router, b_router, experts, t):
    q, k, v, rf, rb, g, z = jnp.split(w_in, np.cumsum([384, 384, 768, 16, 16, 768]).tolist(), axis=1)
    r = jnp.pad(jnp.concatenate([rf, rb], axis=1), ((0, 0), (0, 96)))
    w = jnp.concatenate([_pad_heads(q, GLA_DK, DKP), _pad_heads(k, GLA_DK, DKP), _pad_heads(v, GLA_DV, GLA_DVP),
                         _pad_heads(g, GLA_DV, GLA_DVP), z, r], axis=1).astype(BF16)
    proj = _inproj(h, modsel, w, tbl, ODD_SEGS)
    gw = _pad_heads(gate_w, GLA_DK, DKP)
    gwf = jnp.pad(gw[0], ((0, 112), (0, 0)))
    gwb = jnp.pad(gw[1], ((16, 96), (0, 0)))
    gb = _pad_heads(gate_b, GLA_DK, DKP)
    o_f, o_b = _gla_scan(proj, gwf, gwb, gb, 0, 1, 1, 3328 // 128)
    cs, ct, st = dft
    four = _fourier(proj, 3072 // (F_GROUPS * F_GD), cs, ct, st, t)
    wo = w_out
    wr = jnp.pad(wo[:768].reshape(R_HEADS, GLA_DV, D), ((0, 0), (0, GLA_DVP - GLA_DV), (0, 0))).reshape(-1, D)
    h1, u2x, grp = _outproj(b_router, four, 0, 256, o_f, o_b, proj, 2, h, modsel,
                            _pad_heads(g_norm, GLA_DV, GLA_DVP).reshape(1, -1),
                            wo[768:].astype(BF16), wr.astype(BF16), lng[0:1], lnb[0:1], w_router,
                            dv=GLA_DV, dvp=GLA_DVP, row_off=CTX // TM)
    return _moe_layer(u2x, grp, h1, modsel, experts, lng[1:2], lnb[1:2], CTX // TM)


def kernel(x, c, ctx, c_ctx, w_ada, b_ada, ln_g, ln_b, w_in_even, attn_sink, hgrn_lb_logits, hgrn_norm, w_out_even, w_in_odd, gla_gate_w, gla_gate_b, gla_norm, w_out_odd, w_router, b_router, w_expert_gate, w_expert_up, w_expert_down):
    b, t, _ = x.shape
    rc = CTX + t
    assert ctx.shape[1] == CTX
    cc = jnp.zeros((16, D), F32).at[:b].set(c).at[b].set(c_ctx)
    mods = _ada_mods(cc, w_ada, b_ada).reshape(2, 16, 6, D)

    def modsel(l):
        return jnp.stack([jnp.broadcast_to(mods[l, b], (b, 6, D)), mods[l, :b]], axis=1)

    tbl = _rope_tables(rc, t)
    wrt = jnp.pad(w_router, ((0, 0), (0, GW - N_EXP)))
    experts = (w_expert_gate, w_expert_up, w_expert_down)
    h = _layer_even((ctx, x), modsel(0), tbl, w_in_even[0], attn_sink[0], hgrn_lb_logits, hgrn_norm[0], w_out_even[0],
                    ln_g[0], ln_b[0], wrt, b_router, experts + (0,))
    return _layer_odd(h, modsel(1), tbl, _dft_tables(t), w_in_odd[0], gla_gate_w[0], gla_gate_b[0], gla_norm[0],
                      w_out_odd[0], ln_g[1], ln_b[1], wrt, b_router, experts + (1,), t)
```

```python
import functools
import math

import numpy as np
import jax
import jax.numpy as jnp
from jax import lax
from jax.experimental import pallas as pl
from jax.experimental.pallas import tpu as pltpu

F32 = jnp.float32
BF16 = jnp.bfloat16
HIGHEST = lax.Precision.HIGHEST

D = 1024
CTX = 256
GRID_W = 64
LN_EPS = 1e-5
NEG = -1e30
ALPHA = 4.0 ** 0.25
ROPE_BASE = 10000.0

A_HEADS, A_KV, A_HD, A_BLK = 8, 2, 64, 128
R_HEADS = 4
DKP = 128
GLA_DK, GLA_DV, GLA_DVP = 96, 192, 256
GLA_TAU = 16.0
CHUNK = 64
N_EXP, N_GRP, E_PER = 16, 4, 4
D_FF = 512
F_GROUPS, F_GD = 4, 64

TM = 256
IN_SUB = 3
GW = 128
XROWS = 16
YROWS = 8
V7X_VMEM_LIMIT = 56 * 1024 * 1024


def _cp(sem, vmem=None):
    return pltpu.CompilerParams(dimension_semantics=sem, vmem_limit_bytes=vmem)


def _ln(x):
    mu = jnp.mean(x, axis=-1, keepdims=True)
    xc = x - mu
    var = jnp.mean(xc * xc, axis=-1, keepdims=True)
    return xc * lax.rsqrt(var + LN_EPS)


def _silu(x):
    return x * jax.nn.sigmoid(x)


_NT = (((1,), (1,)), ((), ()))
_TN = (((0,), (0,)), ((), ()))


def _ada_kernel(c_ref, w_ref, b_ref, o_ref):
    s = _silu(c_ref[...])
    o_ref[0] = jnp.dot(s, w_ref[0], precision=HIGHEST, preferred_element_type=F32) + b_ref[0]


def _ada_mods(cc, w_ada, b_ada):
    depth, _, n = w_ada.shape
    tn = 1536
    return pl.pallas_call(
        _ada_kernel,
        grid=(depth, n // tn),
        in_specs=[pl.BlockSpec((16, D), lambda l, i: (0, 0)),
                  pl.BlockSpec((1, D, tn), lambda l, i: (l, 0, i)),
                  pl.BlockSpec((1, 1, tn), lambda l, i: (l, 0, i))],
        out_specs=pl.BlockSpec((1, 16, tn), lambda l, i: (l, 0, i)),
        out_shape=jax.ShapeDtypeStruct((depth, 16, n), F32),
        compiler_params=_cp(("arbitrary", "arbitrary"), 40 * 1024 * 1024),
        name="ada_mod",
    )(cc, w_ada, b_ada.reshape(depth, 1, n))


def _rows_of(h_refs):
    if len(h_refs) == 1:
        return h_refs[0][0]
    return jnp.where(pl.program_id(1) == 0, h_refs[0][0], h_refs[1][0])


def _row_specs(h, row_off=0):
    if isinstance(h, tuple):
        assert row_off == 0 and h[0].shape[1] == TM
        return [pl.BlockSpec((1, TM, D), lambda i, j: (i, 0, 0)),
                pl.BlockSpec((1, TM, D), lambda i, j: (i, jnp.maximum(j - 1, 0), 0))], list(h)
    return [pl.BlockSpec((1, TM, D), lambda i, j: (i, j + row_off, 0))], [h]


def _inproj_kernel(*refs, segs):
    *h_refs, mod0_ref, mod1_ref, w_ref, tbl_ref, o_ref = refs
    if len(h_refs) == 1:
        x = h_refs[0][0]
    else:
        ctx_ref, *x_refs = h_refs
        x = jnp.concatenate([jnp.where(pl.program_id(1) == 0, ctx_ref[0], x_refs[0][0])]
                            + [r[0] for r in x_refs[1:]], axis=0)
    xn = _ln(x)
    u = jnp.concatenate([xn[:TM] * (1.0 + mod0_ref[0, 0, 1:2, :]) + mod0_ref[0, 0, 0:1, :],
                         xn[TM:] * (1.0 + mod1_ref[0, 0, 1:2, :]) + mod1_ref[0, 0, 0:1, :]], axis=0).astype(BF16)
    lane = lax.broadcasted_iota(jnp.int32, (x.shape[0], 128), 1)
    low = (lane % 32) < 16
    for c0, c1, rope in segs:
        acc = jnp.dot(u, w_ref[:, c0:c1], preferred_element_type=F32)
        if rope is not None:
            partner = jnp.where(low, pltpu.roll(acc, 112, axis=1), pltpu.roll(acc, 16, axis=1))
            acc = acc * tbl_ref[2 * rope] + partner * tbl_ref[2 * rope + 1]
        o_ref[0, :, c0:c1] = acc.astype(BF16)


def _inproj(h, modsel, w, tbl, segs):
    b, rc = modsel.shape[0], tbl.shape[1]
    n = w.shape[1]
    tmi = IN_SUB * TM
    assert rc % tmi == 0
    if isinstance(h, tuple):
        assert h[0].shape[1] == TM
        h_specs = [pl.BlockSpec((1, TM, D), lambda i, j: (i, 0, 0))] + [
            pl.BlockSpec((1, TM, D), lambda i, j, k=k: (i, jnp.maximum(IN_SUB * j + k - 1, 0), 0))
            for k in range(IN_SUB)]
        h_args = [h[0]] + [h[1]] * IN_SUB
    else:
        h_specs, h_args = [pl.BlockSpec((1, tmi, D), lambda i, j: (i, j, 0))], [h]
    return pl.pallas_call(
        functools.partial(_inproj_kernel, segs=segs),
        grid=(b, rc // tmi),
        in_specs=h_specs + [pl.BlockSpec((1, 1, 6, D), lambda i, j: (i, jnp.minimum(j, 1), 0, 0)),
                            pl.BlockSpec((1, 1, 6, D), lambda i, j: (i, 1, 0, 0)),
                            pl.BlockSpec((D, n), lambda i, j: (0, 0)),
                            pl.BlockSpec((4, tmi, 128), lambda i, j: (0, j, 0))],
        out_specs=pl.BlockSpec((1, tmi, n), lambda i, j: (i, j, 0)),
        out_shape=jax.ShapeDtypeStruct((b, rc, n), BF16),
        compiler_params=_cp(("parallel", "arbitrary"), V7X_VMEM_LIMIT),
        name="inproj",
    )(*h_args, modsel, modsel, w, tbl)


def _attn_kernel(sink_ref, q_ref, kvc_ref, kvp_ref, kvm_ref, kvn_ref, bias_ref, o_ref):
    blk = pl.program_id(1)
    g_per = A_HEADS // A_KV
    nq = g_per * A_BLK
    q = q_ref[0] * (A_HD ** -0.5)
    rgrp = lax.broadcasted_iota(jnp.int32, (nq, 1), 0) // A_BLK
    q4, sink_col = [], []
    for h in range(A_KV):
        q4.append(jnp.concatenate([q[:, A_HD * (g_per * h + g):A_HD * (g_per * h + g + 1)] for g in range(g_per)],
                                  axis=0))
        col = jnp.zeros((nq, 1), F32)
        for g in range(g_per):
            col = jnp.where(rgrp == g, sink_ref[h * g_per + g], col)
        sink_col.append(col)

    def attend(kv, bias):
        s = [lax.dot_general(q4[h], kv[:, 2 * A_HD * h:2 * A_HD * h + A_HD], _NT, preferred_element_type=F32)
             for h in range(A_KV)]
        if bias is not None:
            bias4 = jnp.concatenate([bias] * g_per, axis=0)
            s = [x + bias4 for x in s]
        m = [jnp.maximum(jnp.max(s[h], axis=-1, keepdims=True), sink_col[h]) for h in range(A_KV)]
        p = [jnp.exp(s[h] - m[h]) for h in range(A_KV)]
        den = [jnp.sum(p[h], axis=-1, keepdims=True) + jnp.exp(sink_col[h] - m[h]) for h in range(A_KV)]
        o = [jnp.dot(p[h].astype(BF16), kv[:, 2 * A_HD * h + A_HD:2 * A_HD * (h + 1)], preferred_element_type=F32)
             / den[h] for h in range(A_KV)]
        o_ref[0] = jnp.concatenate([o[h][A_BLK * g:A_BLK * (g + 1), :] for h in range(A_KV) for g in range(g_per)],
                                   axis=1).astype(BF16)

    @pl.when(blk < CTX // A_BLK)
    def _():
        attend(kvc_ref[0], None)

    @pl.when(blk >= CTX // A_BLK)
    def _():
        attend(jnp.concatenate([kvc_ref[0], kvp_ref[0], kvm_ref[0], kvn_ref[0]], axis=0), bias_ref[0])


def _attn_bias():
    nk = CTX + 3 * A_BLK
    qi = jnp.arange(A_BLK, dtype=jnp.int32)[:, None]
    col = jnp.arange(nk, dtype=jnp.int32)[None, :]
    rel = col - (CTX + A_BLK)
    dist = qi - rel

    def one(lo, hi):
        in_win = (dist <= A_BLK) & (dist >= -A_BLK) & (rel >= lo) & (rel < hi)
        return jnp.where((col < CTX) | in_win, 0.0, NEG).astype(F32)

    return jnp.stack([one(0, 2 * A_BLK), one(-A_BLK, 2 * A_BLK), one(-A_BLK, A_BLK)])


def _attention(proj, sink, kv_blk):
    b, rc, _ = proj.shape
    nb = rc // A_BLK
    c_b = CTX // A_BLK
    assert nb - c_b >= 2
    kw = 2 * A_HD * A_KV

    def band(off):
        return lambda i, j: (i, jnp.clip(j + off, c_b, nb - 1), kv_blk)

    def which_bias(i, j):
        return (jnp.where(j <= c_b, 0, jnp.where(j == nb - 1, 2, 1)), 0, 0)

    return pl.pallas_call(
        _attn_kernel,
        grid=(b, nb),
        in_specs=[pl.BlockSpec(memory_space=pltpu.SMEM),
                  pl.BlockSpec((1, A_BLK, A_HEADS * A_HD), lambda i, j: (i, j, 0)),
                  pl.BlockSpec((1, CTX, kw), lambda i, j: (i, 0, kv_blk)),
                  pl.BlockSpec((1, A_BLK, kw), band(-1)),
                  pl.BlockSpec((1, A_BLK, kw), band(0)),
                  pl.BlockSpec((1, A_BLK, kw), band(1)),
                  pl.BlockSpec((1, A_BLK, CTX + 3 * A_BLK), which_bias)],
        out_specs=pl.BlockSpec((1, A_BLK, A_HEADS * A_HD), lambda i, j: (i, j, 0)),
        out_shape=jax.ShapeDtypeStruct((b, rc, A_HEADS * A_HD), BF16),
        compiler_params=_cp(("parallel", "arbitrary")),
        name="win_attn",
    )(sink, proj, proj, proj, proj, proj, _attn_bias())


def _scan_blocks(dirs, *, dvp):
    r_i = lax.broadcasted_iota(jnp.int32, (CHUNK, CHUNK), 0)
    c_i = lax.broadcasted_iota(jnp.int32, (CHUNK, CHUNK), 1)
    units = []
    for q, k, v, logf, s_ref, o_ref, reverse in dirs:
        nchunk = q.shape[0] // CHUNK
        tri = (r_i <= c_i) if reverse else (r_i >= c_i)
        tri3 = jnp.concatenate([jnp.where(tri, 1.0, 0.0).astype(BF16)] * 3, axis=1)
        for c in (range(nchunk - 1, -1, -1) if reverse else range(nchunk)):
            sl = slice(c * CHUNK, (c + 1) * CHUNK)
            units.append(dict(q=q[sl], k=k[sl], v=v[sl].astype(BF16), lf=logf[sl], sl=sl, tri=tri, tri3=tri3,
                              i_tot=0 if reverse else CHUNK - 1,
                              i_ref=CHUNK // 2 - 1 if reverse else CHUNK // 2, s_ref=s_ref, o_ref=o_ref))
    heads = [(slice(h * DKP, (h + 1) * DKP), slice(h * dvp, (h + 1) * dvp)) for h in range(R_HEADS)]
    for u in units:
        lf = u["lf"]
        hi = lf.astype(BF16)
        r1 = lf - hi.astype(F32)
        mid = r1.astype(BF16)
        lo = (r1 - mid.astype(F32)).astype(BF16)
        u["bsum"] = jnp.dot(u["tri3"], jnp.concatenate([hi, mid, lo], axis=0), preferred_element_type=F32)
    for u in units:
        bsum = u["bsum"]
        tot = bsum[u["i_tot"]:u["i_tot"] + 1, :]
        ref = bsum[u["i_ref"]:u["i_ref"] + 1, :]
        qg = u["q"] * jnp.exp(bsum - ref)
        kg = u["k"] * jnp.exp(ref - bsum)
        u["qb"] = (qg * jnp.exp(ref)).astype(BF16)
        u["kd"] = (kg * jnp.exp(tot - ref)).astype(BF16)
        u["qg"] = qg.astype(BF16)
        u["kg"] = kg.astype(BF16)
        u["dec"] = jnp.exp(tot)
    for u in units:
        u["att"] = [lax.dot_general(u["qg"][:, ks], u["kg"][:, ks], _NT, preferred_element_type=F32)
                    for ks, _ in heads]
        u["ds"] = [lax.dot_general(u["v"][:, vs], u["kd"][:, ks], _TN, preferred_element_type=F32)
                   for ks, vs in heads]
    for u in units:
        u["oi"] = [jnp.dot(jnp.where(u["tri"], a, 0.0).astype(BF16), u["v"][:, vs], preferred_element_type=F32)
                   for a, (_, vs) in zip(u["att"], heads)]
    state = {}
    for u in units:
        key = id(u["s_ref"])
        if key not in state:
            state[key] = [u["s_ref"][h] for h in range(R_HEADS)]
        st = state[key]
        outs = []
        for h, (ks, _) in enumerate(heads):
            outs.append(u["oi"][h] + lax.dot_general(u["qb"][:, ks], st[h].astype(BF16), _NT,
                                                     preferred_element_type=F32))
            st[h] = st[h] * u["dec"][:, ks] + u["ds"][h]
        u["o_ref"][0, u["sl"], :] = jnp.concatenate(outs, axis=1).astype(BF16)
    for _, _, _, _, s_ref, _, _ in dirs:
        for h in range(R_HEADS):
            s_ref[h] = state[id(s_ref)][h]


def _hgrn_scan_kernel(lb_ref, qf_ref, vf_ref, zf_ref, qb_ref, vb_ref, zb_ref, of_ref, ob_ref, sf_ref, sb_ref):
    @pl.when(pl.program_id(1) == 0)
    def _():
        sf_ref[...] = jnp.zeros_like(sf_ref)
        sb_ref[...] = jnp.zeros_like(sb_ref)

    def gates(z, lb):
        f = lb + (1.0 - lb) * jax.nn.sigmoid(z.astype(F32))
        return 1.0 - f, jnp.log(f)

    kf, lf = gates(zf_ref[0], lb_ref[0:1, :])
    kb, lbw = gates(zb_ref[0], lb_ref[1:2, :])
    _scan_blocks([(qf_ref[0], kf, vf_ref[0], lf, sf_ref, of_ref, False),
                  (qb_ref[0], kb, vb_ref[0], lbw, sb_ref, ob_ref, True)], dvp=128)


def _bwd_blk(j, nt):
    return jnp.where(j == 0, 0, nt - j)


def _hgrn_scan(proj, lb, cq, cv, cf, cb):
    b, rc, _ = proj.shape
    nt = rc // TM
    w = R_HEADS * DKP

    def fw(col):
        return pl.BlockSpec((1, TM, w), lambda i, j: (i, j, col))

    def bw(col):
        return pl.BlockSpec((1, TM, w), lambda i, j: (i, _bwd_blk(j, nt), col))

    o_sd = jax.ShapeDtypeStruct((b, rc, w), BF16)
    return pl.pallas_call(
        _hgrn_scan_kernel,
        grid=(b, nt),
        in_specs=[pl.BlockSpec((2, w), lambda i, j: (0, 0)),
                  fw(cq), fw(cv), fw(cf), bw(cq), bw(cv), bw(cb)],
        out_specs=[pl.BlockSpec((1, TM, w), lambda i, j: (i, j, 0)),
                   pl.BlockSpec((1, TM, w), lambda i, j: (i, _bwd_blk(j, nt), 0))],
        out_shape=[o_sd, o_sd],
        scratch_shapes=[pltpu.VMEM((R_HEADS, 128, DKP), F32), pltpu.VMEM((R_HEADS, 128, DKP), F32)],
        compiler_params=_cp(("parallel", "arbitrary"), 40 * 1024 * 1024),
        name="hgrn_scan",
    )(lb, proj, proj, proj, proj, proj, proj)


def _gla_scan_kernel(gwf_ref, gwb_ref, gb_ref, qf_ref, kf_ref, vf_ref, rf_ref, qb_ref, kb_ref, vb_ref, rb_ref,
                     of_ref, ob_ref, sf_ref, sb_ref):
    @pl.when(pl.program_id(1) == 0)
    def _():
        sf_ref[...] = jnp.zeros_like(sf_ref)
        sb_ref[...] = jnp.zeros_like(sb_ref)

    def decay(r, gw, gb):
        zz = jnp.dot(jnp.concatenate([r, r], axis=1), gw, preferred_element_type=F32) + gb
        return (jnp.minimum(zz, 0.0) - jnp.log(1.0 + jnp.exp(-jnp.abs(zz)))) * (1.0 / GLA_TAU)

    qs = GLA_DK ** -0.5
    _scan_blocks([(qf_ref[0].astype(F32) * qs, kf_ref[0], vf_ref[0],
                   decay(rf_ref[0], gwf_ref[...], gb_ref[0:1, :]), sf_ref, of_ref, False),
                  (qb_ref[0].astype(F32) * qs, kb_ref[0], vb_ref[0],
                   decay(rb_ref[0], gwb_ref[...], gb_ref[1:2, :]), sb_ref, ob_ref, True)], dvp=GLA_DVP)


def _gla_scan(proj, gwf, gwb, gb, cq, ck, cv, cr):
    b, rc, _ = proj.shape
    nt = rc // TM
    wk = R_HEADS * DKP
    wv = R_HEADS * GLA_DVP

    def spec(width, col, back):
        if back:
            return pl.BlockSpec((1, TM, width), lambda i, j: (i, _bwd_blk(j, nt), col))
        return pl.BlockSpec((1, TM, width), lambda i, j: (i, j, col))

    o_sd = jax.ShapeDtypeStruct((b, rc, wv), BF16)
    full = lambda shape: pl.BlockSpec(shape, lambda i, j: (0,) * len(shape))
    return pl.pallas_call(
        _gla_scan_kernel,
        grid=(b, nt),
        in_specs=[full((256, wk)), full((256, wk)), full((2, wk)),
                  spec(wk, cq, False), spec(wk, ck, False), spec(wv, cv, False), spec(128, cr, False),
                  spec(wk, cq, True), spec(wk, ck, True), spec(wv, cv, True), spec(128, cr, True)],
        out_specs=[spec(wv, 0, False), spec(wv, 0, True)],
        out_shape=[o_sd, o_sd],
        scratch_shapes=[pltpu.VMEM((R_HEADS, GLA_DVP, DKP), F32), pltpu.VMEM((R_HEADS, GLA_DVP, DKP), F32)],
        compiler_params=_cp(("parallel", "arbitrary"), 48 * 1024 * 1024),
        name="gla_scan",
    )(gwf, gwb, gb, proj, proj, proj, proj, proj, proj, proj, proj)


def _fourier_chan_kernel(z_ref, cs_ref, o_ref):
    o_ref[0] = jnp.dot(z_ref[0], cs_ref[...], preferred_element_type=F32).astype(BF16)


def _fourier_seq_kernel(zz_ref, ct_ref, st_ref, o_ref, *, scale):
    w = F_GROUPS * F_GD
    y = jnp.dot(ct_ref[...], zz_ref[0, :, :w], preferred_element_type=F32)
    y = y - jnp.dot(st_ref[...], zz_ref[0, :, w:], preferred_element_type=F32)
    o_ref[0] = (y * scale).astype(BF16)


def _fourier(proj, zcol, cs, ct, st, t):
    b = proj.shape[0]
    w = F_GROUPS * F_GD
    c_t = CTX // TM
    nzt = t // TM
    zz = pl.pallas_call(
        _fourier_chan_kernel,
        grid=(b, nzt),
        in_specs=[pl.BlockSpec((1, TM, w), lambda i, j: (i, j + c_t, zcol)),
                  pl.BlockSpec((w, 2 * w), lambda i, j: (0, 0))],
        out_specs=pl.BlockSpec((1, TM, 2 * w), lambda i, j: (i, j, 0)),
        out_shape=jax.ShapeDtypeStruct((b, t, 2 * w), BF16),
        compiler_params=_cp(("parallel", "arbitrary")),
        name="fourier_chan",
    )(proj, cs)
    return pl.pallas_call(
        functools.partial(_fourier_seq_kernel, scale=1.0 / math.sqrt(t * F_GD)),
        grid=(b, nzt),
        in_specs=[pl.BlockSpec((1, t, 2 * w), lambda i, j: (i, 0, 0)),
                  pl.BlockSpec((TM, t), lambda i, j: (j, 0)),
                  pl.BlockSpec((TM, t), lambda i, j: (j, 0))],
        out_specs=pl.BlockSpec((1, TM, w), lambda i, j: (i, j, 0)),
        out_shape=jax.ShapeDtypeStruct((b, t, w), BF16),
        compiler_params=_cp(("parallel", "arbitrary"), 40 * 1024 * 1024),
        name="fourier_seq",
    )(zz, ct, st)


def _route(logits, rb_ref):
    lt = logits.T
    l = [lt[e:e + 1, :] for e in range(N_EXP)]
    m = functools.reduce(jnp.maximum, l)
    ex = [jnp.exp(v - m) for v in l]
    den = functools.reduce(lambda a, c: a + c, ex)
    p = [v / den for v in ex]
    sel = [p[e] + rb_ref[e] for e in range(N_EXP)]
    g_score, g_gate = [], []
    for g in range(N_GRP):
        s = sel[E_PER * g:E_PER * (g + 1)]
        chosen = []
        for i in range(E_PER):
            rank = jnp.zeros_like(s[i])
            for j in range(E_PER):
                if j == i:
                    continue
                ahead = (s[j] >= s[i]) if j < i else (s[j] > s[i])
                rank = rank + jnp.where(ahead, 1.0, 0.0)
            chosen.append(rank < 2.0)
        g_score.append(functools.reduce(lambda a, c: a + c,
                                        [jnp.where(chosen[i], s[i], 0.0) for i in range(E_PER)]))
        g_gate.append([jnp.where(chosen[i], p[E_PER * g + i], 0.0) for i in range(E_PER)])
    best = jnp.zeros_like(g_score[0])
    best_s = g_score[0]
    for g in range(1, N_GRP):
        upd = g_score[g] > best_s
        best = jnp.where(upd, float(g), best)
        best_s = jnp.where(upd, g_score[g], best_s)
    gate = []
    for i in range(E_PER):
        gi = g_gate[0][i]
        for g in range(1, N_GRP):
            gi = jnp.where(best == float(g), g_gate[g][i], gi)
        gate.append(gi)
    wsum = functools.reduce(lambda a, c: a + c, gate)
    row = lax.broadcasted_iota(jnp.int32, (GW, lt.shape[1]), 0)
    side = jnp.zeros((GW, lt.shape[1]), F32)
    for i in range(E_PER):
        side = jnp.where(row == i, gate[i] / wsum, side)
    return side.T, best


def _outproj_kernel(rb_ref, a_ref, of_ref, ob_ref, gt_ref, *refs, dvp, dv):
    *h_refs, mod_ref, gn_ref, wa_ref, wr_ref, lng_ref, lnb_ref, wrt_ref, h1_ref, u2_ref, grp_ref = refs
    o = of_ref[0].astype(F32) + ob_ref[0].astype(F32)
    parts = []
    for hh in range(R_HEADS):
        sl = slice(hh * dvp, (hh + 1) * dvp)
        oh = o[:, sl]
        ms = jnp.sum(oh * oh, axis=-1, keepdims=True) * (1.0 / dv)
        parts.append(oh * lax.rsqrt(ms + LN_EPS))
    rec = jnp.concatenate(parts, axis=1) * gn_ref[...] * _silu(gt_ref[0].astype(F32))
    y = jnp.dot(a_ref[0], wa_ref[...], preferred_element_type=F32)
    y = y + jnp.dot(rec.astype(BF16), wr_ref[...], preferred_element_type=F32)
    g1 = mod_ref[0, 0, 2:3, :]
    h1 = _ln(ALPHA * _rows_of(h_refs) + g1 * y) * lng_ref[0:1, :] + lnb_ref[0:1, :]
    h1_ref[0] = h1
    u2 = _ln(h1) * (1.0 + mod_ref[0, 0, 4:5, :]) + mod_ref[0, 0, 3:4, :]
    u2_hi = u2.astype(BF16)
    u2_lo = (u2 - u2_hi.astype(F32)).astype(BF16)
    lg = (jnp.dot(u2_hi, wrt_ref[...], preferred_element_type=F32)
          + jnp.dot(u2_lo, wrt_ref[...], preferred_element_type=F32))
    logits = lg[:, :GW] + lg[:, GW:]
    side, best = _route(logits, rb_ref)
    grp_ref[0, 0] = best
    for c in range(D // 128):
        u2_ref[pl.ds(c, TM, stride=XROWS), :] = u2[:, 128 * c:128 * (c + 1)]
    u2_ref[pl.ds(D // 128, TM, stride=XROWS), :] = side
    for c in range(D // 128 + 1, XROWS):
        u2_ref[pl.ds(c, TM, stride=XROWS), :] = jnp.zeros((TM, 128), F32)


def _outproj(b_router, a, a_col, a_w, o_f, o_b, proj, g_col, h, modsel, gnorm, wa, wr, lng, lnb, wrt, *,
             dv, dvp, row_off):
    b, rc, _ = proj.shape
    nt = rc // TM - row_off
    wrec = R_HEADS * dvp
    full = lambda shape: pl.BlockSpec(shape, lambda i, j: (0,) * len(shape))
    h_specs, h_args = _row_specs(h, row_off)
    return pl.pallas_call(
        functools.partial(_outproj_kernel, dvp=dvp, dv=dv),
        grid=(b, nt),
        in_specs=[pl.BlockSpec(memory_space=pltpu.SMEM),
                  pl.BlockSpec((1, TM, a_w), lambda i, j: (i, j + (row_off if a.shape[1] == rc else 0), a_col)),
                  pl.BlockSpec((1, TM, wrec), lambda i, j: (i, j + row_off, 0)),
                  pl.BlockSpec((1, TM, wrec), lambda i, j: (i, j + row_off, 0)),
                  pl.BlockSpec((1, TM, wrec), lambda i, j: (i, j + row_off, g_col))] + h_specs + [
                  pl.BlockSpec((1, 1, 6, D), lambda i, j: (i, jnp.minimum(j + row_off, 1), 0, 0)),
                  full((1, wrec)), full((a_w, D)), full((wrec, D)), full((1, D)), full((1, D)), full((D, 2 * GW))],
        out_specs=[pl.BlockSpec((1, TM, D), lambda i, j: (i, j, 0)),
                   pl.BlockSpec((TM * XROWS, 128), lambda i, j: (i * nt + j, 0)),
                   pl.BlockSpec((1, 1, 1, TM), lambda i, j: (i, j, 0, 0))],
        out_shape=[jax.ShapeDtypeStruct((b, nt * TM, D), F32),
                   jax.ShapeDtypeStruct((b * nt * TM * XROWS, 128), F32),
                   jax.ShapeDtypeStruct((b, nt, 1, TM), F32)],
        compiler_params=_cp(("parallel", "arbitrary"), V7X_VMEM_LIMIT),
        name="outproj",
    )(b_router, a, o_f, o_b, proj, *h_args, modsel, gnorm, wa, wr, lng, lnb, wrt)


def _start_token_gather(idx_ref, base, src_hbm, dst_ref, sem, rows):
    def body(r, carry):
        t = idx_ref[base + r]
        pltpu.make_async_copy(src_hbm.at[pl.ds(pl.multiple_of(t * rows, rows), rows), :],
                              dst_ref.at[pl.ds(pl.multiple_of(r * rows, rows), rows), :], sem).start()
        return carry

    lax.fori_loop(0, TM, body, 0, unroll=8)


def _wait_token_gather(src_hbm, dst_ref, sem):
    pltpu.make_async_copy(src_hbm.at[pl.ds(0, dst_ref.shape[0]), :], dst_ref, sem).wait()


def _gather_ring(idx_ref, first_of, step, last, src_hbm, buf_ref, sem, rows):
    depth = buf_ref.shape[0]
    slot = step % depth

    @pl.when(step == 0)
    def _():
        for k in range(depth - 1):
            _start_token_gather(idx_ref, first_of(jnp.minimum(k, last)), src_hbm, buf_ref.at[k], sem.at[k], rows)

    _wait_token_gather(src_hbm, buf_ref.at[slot], sem.at[slot])
    ahead = (step + depth - 1) % depth
    nxt = first_of(jnp.minimum(step + depth - 1, last))

    def issue(r0, r1):
        for r in range(r0, r1):
            t = idx_ref[nxt + r]
            pltpu.make_async_copy(src_hbm.at[pl.ds(pl.multiple_of(t * rows, rows), rows), :],
                                  buf_ref.at[ahead, pl.ds(r * rows, rows), :], sem.at[ahead]).start()

    def finish():
        @pl.when(step == last)
        def _():
            for k in range(1, depth):
                other = (step + k) % depth
                _wait_token_gather(src_hbm, buf_ref.at[other], sem.at[other])

    return slot, issue, finish


def _moe_kernel(tg_ref, first_ref, src_ref, x_hbm, w1_ref, w3_ref, w2_ref, o_ref, xb_ref, sem,
                w1b_ref, w3b_ref, w2b_ref):
    i = pl.program_id(0)
    slot, issue, finish = _gather_ring(src_ref, lambda k: first_ref[k], i, pl.num_programs(0) - 1,
                                       x_hbm, xb_ref, sem, XROWS)

    @pl.when((i == 0) | (tg_ref[i] != tg_ref[jnp.maximum(i - 1, 0)]))
    def _():
        for e in range(E_PER):
            w1b_ref[e] = w1_ref[e].astype(BF16)
            w3b_ref[e] = w3_ref[e].astype(BF16)
            w2b_ref[e] = w2_ref[e].astype(BF16)

    xb = xb_ref.at[slot]
    x = jnp.concatenate([xb[pl.ds(c, TM, stride=XROWS), :] for c in range(D // 128)], axis=1).astype(BF16)
    side = xb[pl.ds(D // 128, TM, stride=XROWS), :]
    y = jnp.zeros((TM, D), F32)
    per = TM // E_PER
    for e in range(E_PER):
        issue(e * per, (e + 1) * per)
        h1 = jnp.dot(x, w1b_ref[e], preferred_element_type=F32)
        h3 = jnp.dot(x, w3b_ref[e], preferred_element_type=F32)
        hh = (_silu(h1) * h3 * side[:, e:e + 1]).astype(BF16)
        y = y + jnp.dot(hh, w2b_ref[e], preferred_element_type=F32)
    for c in range(D // 128):
        o_ref[pl.ds(c, TM, stride=YROWS), :] = y[:, 128 * c:128 * (c + 1)]
    finish()


def _moe(tile_grp, tile_first, order, u2x, w1, w3, w2, layer):
    mp = tile_grp.shape[0] * TM
    wspec = lambda k, n: pl.BlockSpec((None, E_PER, k, n), lambda i, tg, tf, od: (layer, tg[i], 0, 0),
                                      pipeline_mode=pl.Buffered(1))
    return pl.pallas_call(
        _moe_kernel,
        grid_spec=pltpu.PrefetchScalarGridSpec(
            num_scalar_prefetch=3, grid=(mp // TM,),
            in_specs=[pl.BlockSpec(memory_space=pl.ANY), wspec(D, D_FF), wspec(D, D_FF), wspec(D_FF, D)],
            out_specs=pl.BlockSpec((TM * YROWS, 128), lambda i, tg, tf, od: (i, 0)),
            scratch_shapes=[pltpu.VMEM((2, TM * XROWS, 128), F32), pltpu.SemaphoreType.DMA((2,)),
                            pltpu.VMEM((E_PER, D, D_FF), BF16), pltpu.VMEM((E_PER, D, D_FF), BF16),
                            pltpu.VMEM((E_PER, D_FF, D), BF16)]),
        out_shape=jax.ShapeDtypeStruct((mp * YROWS, 128), F32),
        compiler_params=_cp(("arbitrary",), V7X_VMEM_LIMIT),
        name="moe",
    )(tile_grp, tile_first, order, u2x, w1, w3, w2)


def _combine_kernel(pos_ref, y_hbm, h_ref, mod_ref, lng_ref, lnb_ref, o_ref, yb_ref, sem, *, nt):
    step = pl.program_id(0) * nt + pl.program_id(1)
    slot, issue, finish = _gather_ring(pos_ref, lambda k: k * TM, step, pl.num_programs(0) * nt - 1,
                                       y_hbm, yb_ref, sem, YROWS)
    yb = yb_ref.at[slot]
    g2 = mod_ref[0, 0, 5:6, :]
    strip = TM // 4
    for r0 in range(0, TM, strip):
        issue(r0, r0 + strip)
        y = jnp.concatenate([yb[pl.ds(r0 * YROWS + c, strip, stride=YROWS), :] for c in range(D // 128)], axis=1)
        o_ref[0, r0:r0 + strip, :] = (_ln(ALPHA * h_ref[0, r0:r0 + strip, :] + g2 * y) * lng_ref[0:1, :]
                                      + lnb_ref[0:1, :])
    finish()


def _combine(pos, y_sorted, h1, modsel, lng, lnb, row_off):
    b, r, _ = h1.shape
    nt = r // TM
    full = lambda shape: pl.BlockSpec(shape, lambda i, j, p: (0,) * len(shape))
    return pl.pallas_call(
        functools.partial(_combine_kernel, nt=nt),
        grid_spec=pltpu.PrefetchScalarGridSpec(
            num_scalar_prefetch=1, grid=(b, nt),
            in_specs=[pl.BlockSpec(memory_space=pl.ANY),
                      pl.BlockSpec((1, TM, D), lambda i, j, p: (i, j, 0)),
                      pl.BlockSpec((1, 1, 6, D), lambda i, j, p: (i, jnp.minimum(j + row_off, 1), 0, 0)),
                      full((1, D)), full((1, D))],
            out_specs=pl.BlockSpec((1, TM, D), lambda i, j, p: (i, j, 0)),
            scratch_shapes=[pltpu.VMEM((3, TM * YROWS, 128), F32), pltpu.SemaphoreType.DMA((3,))]),
        out_shape=jax.ShapeDtypeStruct((b, r, D), F32),
        compiler_params=_cp(("arbitrary", "arbitrary")),
        name="moe_combine",
    )(pos, y_sorted, h1, modsel, lng, lnb)


def _sort_by_group(grp):
    n = grp.shape[0]
    onehot = (grp[:, None] == jnp.arange(N_GRP, dtype=jnp.int32)[None, :]).astype(jnp.int32)
    csum = jnp.cumsum(onehot, axis=0)
    counts = csum[-1]
    padded = ((counts + TM - 1) // TM) * TM
    ends = jnp.cumsum(padded)
    starts = ends - padded
    rank = jnp.sum(csum * onehot, axis=1) - 1
    pos = (starts[grp] + rank).astype(jnp.int32)
    _, order = lax.sort((grp, jnp.arange(n, dtype=jnp.int32)), num_keys=1, is_stable=True)
    order = jnp.concatenate([order, jnp.zeros((TM,), jnp.int32)])
    ntile = n // TM + N_GRP
    tile_start = jnp.arange(ntile, dtype=jnp.int32) * TM
    tile_grp = jnp.minimum(jnp.sum((tile_start[:, None] >= ends[None, :]).astype(jnp.int32), axis=1), N_GRP - 1)
    first = (jnp.cumsum(counts) - counts)[tile_grp] + tile_start - starts[tile_grp]
    return pos, order, tile_grp.astype(jnp.int32), jnp.minimum(first, n).astype(jnp.int32)


def _moe_layer(u2x, grp, h1, modsel, experts, lng, lnb, row_off):
    pos, order, tile_grp, tile_first = _sort_by_group(grp.reshape(-1).astype(jnp.int32))
    y_sorted = _moe(tile_grp, tile_first, order, u2x, *experts)
    return _combine(pos, y_sorted, h1, modsel, lng, lnb, row_off)


def _rope_tables(rc, t):
    half = A_HD // 4
    inv = ROPE_BASE ** (-jnp.arange(half, dtype=F32) / half)
    tpos = jnp.arange(t)
    rows = (tpos // GRID_W).astype(F32)
    cols = (tpos % GRID_W).astype(F32)

    def cs(pos):
        ang = pos[:, None] * inv[None, :]
        c, s = jnp.cos(ang), jnp.sin(ang)
        return jnp.concatenate([c, c], -1), jnp.concatenate([-s, s], -1)

    cr, sr = cs(rows)
    cc_, sc_ = cs(cols)
    cos_h = jnp.concatenate([cr, cc_], -1)
    sin_h = jnp.concatenate([sr, sc_], -1)
    one = jnp.ones((t, A_HD), F32)
    zero = jnp.zeros((t, A_HD), F32)
    lat = jnp.stack([jnp.concatenate([cos_h, cos_h], -1), jnp.concatenate([sin_h, sin_h], -1),
                     jnp.concatenate([cos_h, one], -1), jnp.concatenate([sin_h, zero], -1)])
    ctx = jnp.stack([jnp.ones((CTX, 128), F32), jnp.zeros((CTX, 128), F32)] * 2)
    return jnp.concatenate([ctx, lat], axis=1)


def _dft_tables(t):
    def cos_sin(num, den):
        ang = (2.0 * math.pi / den) * (num % den).astype(F32)
        return jnp.cos(ang), jnp.sin(ang)

    k = jnp.arange(t, dtype=jnp.int32)[:, None]
    ca, sa = cos_sin(k * jnp.arange(t // GRID_W, dtype=jnp.int32)[None, :], t // GRID_W)
    cb, sb = cos_sin(k * jnp.arange(GRID_W, dtype=jnp.int32)[None, :], t)
    ct = (ca[:, :, None] * cb[:, None, :] - sa[:, :, None] * sb[:, None, :]).reshape(t, t)
    st = (sa[:, :, None] * cb[:, None, :] + ca[:, :, None] * sb[:, None, :]).reshape(t, t)
    c = jnp.arange(F_GD, dtype=jnp.int32)
    cc, sc = cos_sin(c[:, None] * c[None, :], F_GD)
    eye = jnp.eye(F_GROUPS, dtype=F32)
    cs = jnp.concatenate([jnp.kron(eye, cc), jnp.kron(eye, sc)], axis=1)
    return cs.astype(BF16), ct.astype(BF16), st.astype(BF16)


def _split2(w, axis):
    hi = lax.reduce_precision(w, exponent_bits=8, mantissa_bits=7)
    return jnp.concatenate([hi.astype(BF16), (w - hi).astype(BF16)], axis=axis)


def _pad_heads(w, width, padded):
    lead = w.shape[:-1]
    w = w.reshape(lead + (R_HEADS, width))
    w = jnp.pad(w, [(0, 0)] * len(lead) + [(0, 0), (0, padded - width)])
    return w.reshape(lead + (R_HEADS * padded,))


EVEN_SEGS = tuple([(128 * g, 128 * (g + 1), 0) for g in range(4)] + [(c, c + 512, None) for c in range(512, 3072, 512)]
                  + [(3072 + 128 * g, 3072 + 128 * (g + 1), 1) for g in range(2)])
ODD_W = 512 + 512 + 1024 + 1024 + 256 + 128
ODD_SEGS = tuple([(c, c + 512, None) for c in range(0, 3072, 512)] + [(3072, ODD_W, None)])


def _layer_even(h, modsel, tbl, w_in, sink, lb_logits, g_norm, w_out, lng, lnb, w_router, b_router, experts):
    aq, ak, av, ff, fb, hq, hi, hg = jnp.split(w_in, np.cumsum([512, 128, 128, 512, 512, 512, 512]).tolist(), axis=1)
    kv = [jnp.concatenate([ak[:, A_HD * j:A_HD * (j + 1)], av[:, A_HD * j:A_HD * (j + 1)]], axis=1)
          for j in range(A_KV)]
    w = jnp.concatenate([aq, ff, fb, hq, hi, hg] + kv, axis=1).astype(BF16)
    proj = _inproj(h, modsel, w, tbl, EVEN_SEGS)
    o_att = _attention(proj, sink, 3072 // (2 * A_HD * A_KV))
    lb = jnp.cumsum(jax.nn.softmax(lb_logits.astype(F32), axis=1), axis=1)[:, 0]
    o_f, o_b = _hgrn_scan(proj, lb, 3, 4, 1, 2)
    wo = w_out.astype(BF16)
    h1, u2x, grp = _outproj(b_router, o_att, 0, 512, o_f, o_b, proj, 5, h, modsel, g_norm.reshape(1, -1),
                            wo[:512], wo[512:], lng[0:1], lnb[0:1], w_router, dv=128, dvp=128, row_off=0)
    return _moe_layer(u2x, grp, h1, modsel, experts, lng[1:2], lnb[1:2], 0)


def _layer_odd(h, modsel, tbl, dft, w_in, gate_w, gate_b, g_norm, w_out, lng, lnb, w_router, b_router, experts, t):
    q, k, v, rf, rb, g, z = jnp.split(w_in, np.cumsum([384, 384, 768, 16, 16, 768]).tolist(), axis=1)
    r = jnp.pad(jnp.concatenate([rf, rb], axis=1), ((0, 0), (0, 96)))
    w = jnp.concatenate([_pad_heads(q, GLA_DK, DKP), _pad_heads(k, GLA_DK, DKP), _pad_heads(v, GLA_DV, GLA_DVP),
                         _pad_heads(g, GLA_DV, GLA_DVP), z, r], axis=1).astype(BF16)
    proj = _inproj(h, modsel, w, tbl, ODD_SEGS)
    gw = _pad_heads(gate_w, GLA_DK, DKP)
    gwf = _split2(jnp.pad(gw[0], ((0, 112), (0, 0))), axis=0)
    gwb = _split2(jnp.pad(gw[1], ((16, 96), (0, 0))), axis=0)
    gb = _pad_heads(gate_b, GLA_DK, DKP)
    o_f, o_b = _gla_scan(proj, gwf, gwb, gb, 0, 1, 1, 3328 // 128)
    cs, ct, st = dft
    four = _fourier(proj, 3072 // (F_GROUPS * F_GD), cs, ct, st, t)
    wo = w_out
    wr = jnp.pad(wo[:768].reshape(R_HEADS, GLA_DV, D), ((0, 0), (0, GLA_DVP - GLA_DV), (0, 0))).reshape(-1, D)
    h1, u2x, grp = _outproj(b_router, four, 0, 256, o_f, o_b, proj, 2, h, modsel,
                            _pad_heads(g_norm, GLA_DV, GLA_DVP).reshape(1, -1),
                            wo[768:].astype(BF16), wr.astype(BF16), lng[0:1], lnb[0:1], w_router,
                            dv=GLA_DV, dvp=GLA_DVP, row_off=CTX // TM)
    return _moe_layer(u2x, grp, h1, modsel, experts, lng[1:2], lnb[1:2], CTX // TM)


def kernel(x, c, ctx, c_ctx, w_ada, b_ada, ln_g, ln_b, w_in_even, attn_sink, hgrn_lb_logits, hgrn_norm, w_out_even, w_in_odd, gla_gate_w, gla_gate_b, gla_norm, w_out_odd, w_router, b_router, w_expert_gate, w_expert_up, w_expert_down):
    b, t, _ = x.shape
    rc = CTX + t
    assert ctx.shape[1] == CTX
    cc = jnp.zeros((16, D), F32).at[:b].set(c).at[b].set(c_ctx)
    mods = _ada_mods(cc, w_ada, b_ada).reshape(2, 16, 6, D)

    def modsel(l):
        return jnp.stack([jnp.broadcast_to(mods[l, b], (b, 6, D)), mods[l, :b]], axis=1)

    tbl = _rope_tables(rc, t)
    wrt = _split2(jnp.pad(w_router, ((0, 0), (0, GW - N_EXP))), axis=1)
    experts = (w_expert_gate, w_expert_up, w_expert_down)
    h = _layer_even((ctx, x), modsel(0), tbl, w_in_even[0], attn_sink[0], hgrn_lb_logits, hgrn_norm[0], w_out_even[0],
                    ln_g[0], ln_b[0], wrt, b_router, experts + (0,))
    return _layer_odd(h, modsel(1), tbl, _dft_tables(t), w_in_odd[0], gla_gate_w[0], gla_gate_b[0], gla_norm[0],
                      w_out_odd[0], ln_g[1], ln_b[1], wrt, b_router, experts + (1,), t)
```

```python
import functools
import math

import numpy as np
import jax
import jax.numpy as jnp
from jax import lax
from jax.experimental import pallas as pl
from jax.experimental.pallas import tpu as pltpu

F32 = jnp.float32
BF16 = jnp.bfloat16
HIGHEST = lax.Precision.HIGHEST

D = 1024
CTX = 256
GRID_W = 64
LN_EPS = 1e-5
NEG = -1e30
ALPHA = 4.0 ** 0.25
ROPE_BASE = 10000.0

A_HEADS, A_KV, A_HD, A_BLK = 8, 2, 64, 128
R_HEADS = 4
DKP = 128
GLA_DK, GLA_DV, GLA_DVP = 96, 192, 256
GLA_TAU = 16.0
CHUNK = 64
N_EXP, N_GRP, E_PER = 16, 4, 4
D_FF = 512
F_GROUPS, F_GD = 4, 64

TM = 256
IN_SUB = 3
GW = 128
XROWS = 16
YROWS = 8
V7X_VMEM_LIMIT = 56 * 1024 * 1024


def _cp(sem, vmem=None):
    return pltpu.CompilerParams(dimension_semantics=sem, vmem_limit_bytes=vmem)


def _ln(x):
    mu = jnp.mean(x, axis=-1, keepdims=True)
    xc = x - mu
    var = jnp.mean(xc * xc, axis=-1, keepdims=True)
    return xc * lax.rsqrt(var + LN_EPS)


def _silu(x):
    return x * jax.nn.sigmoid(x)


def _split2(w, axis):
    hi = w.astype(BF16)
    return jnp.concatenate([hi, (w - hi.astype(F32)).astype(BF16)], axis=axis)


_NT = (((1,), (1,)), ((), ()))
_TN = (((0,), (0,)), ((), ()))


def _ada_kernel(c_ref, w_ref, b_ref, o_ref):
    s = _silu(c_ref[...])
    o_ref[0] = jnp.dot(s, w_ref[0], precision=HIGHEST, preferred_element_type=F32) + b_ref[0]


def _ada_mods(cc, w_ada, b_ada):
    depth, _, n = w_ada.shape
    tn = 1536
    return pl.pallas_call(
        _ada_kernel,
        grid=(depth, n // tn),
        in_specs=[pl.BlockSpec((16, D), lambda l, i: (0, 0)),
                  pl.BlockSpec((1, D, tn), lambda l, i: (l, 0, i)),
                  pl.BlockSpec((1, 1, tn), lambda l, i: (l, 0, i))],
        out_specs=pl.BlockSpec((1, 16, tn), lambda l, i: (l, 0, i)),
        out_shape=jax.ShapeDtypeStruct((depth, 16, n), F32),
        compiler_params=_cp(("arbitrary", "arbitrary"), 40 * 1024 * 1024),
        name="ada_mod",
    )(cc, w_ada, b_ada.reshape(depth, 1, n))


def _rows_of(h_refs):
    if len(h_refs) == 1:
        return h_refs[0][0]
    return jnp.where(pl.program_id(1) == 0, h_refs[0][0], h_refs[1][0])


def _row_specs(h, row_off=0):
    if isinstance(h, tuple):
        assert row_off == 0 and h[0].shape[1] == TM
        return [pl.BlockSpec((1, TM, D), lambda i, j: (i, 0, 0)),
                pl.BlockSpec((1, TM, D), lambda i, j: (i, jnp.maximum(j - 1, 0), 0))], list(h)
    return [pl.BlockSpec((1, TM, D), lambda i, j: (i, j + row_off, 0))], [h]


def _inproj_kernel(*refs, segs):
    *h_refs, mod0_ref, mod1_ref, w_ref, tbl_ref, o_ref = refs
    if len(h_refs) == 1:
        x = h_refs[0][0]
    else:
        ctx_ref, *x_refs = h_refs
        x = jnp.concatenate([jnp.where(pl.program_id(1) == 0, ctx_ref[0], x_refs[0][0])]
                            + [r[0] for r in x_refs[1:]], axis=0)
    xn = _ln(x)
    u = jnp.concatenate([xn[:TM] * (1.0 + mod0_ref[0, 0, 1:2, :]) + mod0_ref[0, 0, 0:1, :],
                         xn[TM:] * (1.0 + mod1_ref[0, 0, 1:2, :]) + mod1_ref[0, 0, 0:1, :]], axis=0).astype(BF16)
    lane = lax.broadcasted_iota(jnp.int32, (x.shape[0], 128), 1)
    low = (lane % 32) < 16
    for c0, c1, rope in segs:
        acc = jnp.dot(u, w_ref[:, c0:c1], preferred_element_type=F32)
        if rope is not None:
            partner = jnp.where(low, pltpu.roll(acc, 112, axis=1), pltpu.roll(acc, 16, axis=1))
            acc = acc * tbl_ref[2 * rope] + partner * tbl_ref[2 * rope + 1]
        o_ref[0, :, c0:c1] = acc.astype(BF16)


def _inproj(h, modsel, w, tbl, segs):
    b, rc = modsel.shape[0], tbl.shape[1]
    n = w.shape[1]
    tmi = IN_SUB * TM
    assert rc % tmi == 0
    if isinstance(h, tuple):
        assert h[0].shape[1] == TM
        h_specs = [pl.BlockSpec((1, TM, D), lambda i, j: (i, 0, 0))] + [
            pl.BlockSpec((1, TM, D), lambda i, j, k=k: (i, jnp.maximum(IN_SUB * j + k - 1, 0), 0))
            for k in range(IN_SUB)]
        h_args = [h[0]] + [h[1]] * IN_SUB
    else:
        h_specs, h_args = [pl.BlockSpec((1, tmi, D), lambda i, j: (i, j, 0))], [h]
    return pl.pallas_call(
        functools.partial(_inproj_kernel, segs=segs),
        grid=(b, rc // tmi),
        in_specs=h_specs + [pl.BlockSpec((1, 1, 6, D), lambda i, j: (i, jnp.minimum(j, 1), 0, 0)),
                            pl.BlockSpec((1, 1, 6, D), lambda i, j: (i, 1, 0, 0)),
                            pl.BlockSpec((D, n), lambda i, j: (0, 0)),
                            pl.BlockSpec((4, tmi, 128), lambda i, j: (0, j, 0))],
        out_specs=pl.BlockSpec((1, tmi, n), lambda i, j: (i, j, 0)),
        out_shape=jax.ShapeDtypeStruct((b, rc, n), BF16),
        compiler_params=_cp(("parallel", "arbitrary"), V7X_VMEM_LIMIT),
        name="inproj",
    )(*h_args, modsel, modsel, w, tbl)


def _attn_kernel(sink_ref, q_ref, kvc_ref, kvp_ref, kvm_ref, kvn_ref, bias_ref, o_ref):
    blk = pl.program_id(1)
    g_per = A_HEADS // A_KV
    nq = g_per * A_BLK
    q = q_ref[0] * (A_HD ** -0.5)
    rgrp = lax.broadcasted_iota(jnp.int32, (nq, 1), 0) // A_BLK
    q4, sink_col = [], []
    for h in range(A_KV):
        q4.append(jnp.concatenate([q[:, A_HD * (g_per * h + g):A_HD * (g_per * h + g + 1)] for g in range(g_per)],
                                  axis=0))
        col = jnp.zeros((nq, 1), F32)
        for g in range(g_per):
            col = jnp.where(rgrp == g, sink_ref[h * g_per + g], col)
        sink_col.append(col)

    def attend(kv, bias):
        s = [lax.dot_general(q4[h], kv[:, 2 * A_HD * h:2 * A_HD * h + A_HD], _NT, preferred_element_type=F32)
             for h in range(A_KV)]
        if bias is not None:
            bias4 = jnp.concatenate([bias] * g_per, axis=0)
            s = [x + bias4 for x in s]
        m = [jnp.maximum(jnp.max(s[h], axis=-1, keepdims=True), sink_col[h]) for h in range(A_KV)]
        p = [jnp.exp(s[h] - m[h]) for h in range(A_KV)]
        den = [jnp.sum(p[h], axis=-1, keepdims=True) + jnp.exp(sink_col[h] - m[h]) for h in range(A_KV)]
        o = [jnp.dot(p[h].astype(BF16), kv[:, 2 * A_HD * h + A_HD:2 * A_HD * (h + 1)], preferred_element_type=F32)
             / den[h] for h in range(A_KV)]
        o_ref[0] = jnp.concatenate([o[h][A_BLK * g:A_BLK * (g + 1), :] for h in range(A_KV) for g in range(g_per)],
                                   axis=1).astype(BF16)

    @pl.when(blk < CTX // A_BLK)
    def _():
        attend(kvc_ref[0], None)

    @pl.when(blk >= CTX // A_BLK)
    def _():
        attend(jnp.concatenate([kvc_ref[0], kvp_ref[0], kvm_ref[0], kvn_ref[0]], axis=0), bias_ref[0])


def _attn_bias():
    nk = CTX + 3 * A_BLK
    qi = jnp.arange(A_BLK, dtype=jnp.int32)[:, None]
    col = jnp.arange(nk, dtype=jnp.int32)[None, :]
    rel = col - (CTX + A_BLK)
    dist = qi - rel

    def one(lo, hi):
        in_win = (dist <= A_BLK) & (dist >= -A_BLK) & (rel >= lo) & (rel < hi)
        return jnp.where((col < CTX) | in_win, 0.0, NEG).astype(F32)

    return jnp.stack([one(0, 2 * A_BLK), one(-A_BLK, 2 * A_BLK), one(-A_BLK, A_BLK)])


def _attention(proj, sink, kv_blk):
    b, rc, _ = proj.shape
    nb = rc // A_BLK
    c_b = CTX // A_BLK
    assert nb - c_b >= 2
    kw = 2 * A_HD * A_KV

    def band(off):
        return lambda i, j: (i, jnp.clip(j + off, c_b, nb - 1), kv_blk)

    def which_bias(i, j):
        return (jnp.where(j <= c_b, 0, jnp.where(j == nb - 1, 2, 1)), 0, 0)

    return pl.pallas_call(
        _attn_kernel,
        grid=(b, nb),
        in_specs=[pl.BlockSpec(memory_space=pltpu.SMEM),
                  pl.BlockSpec((1, A_BLK, A_HEADS * A_HD), lambda i, j: (i, j, 0)),
                  pl.BlockSpec((1, CTX, kw), lambda i, j: (i, 0, kv_blk)),
                  pl.BlockSpec((1, A_BLK, kw), band(-1)),
                  pl.BlockSpec((1, A_BLK, kw), band(0)),
                  pl.BlockSpec((1, A_BLK, kw), band(1)),
                  pl.BlockSpec((1, A_BLK, CTX + 3 * A_BLK), which_bias)],
        out_specs=pl.BlockSpec((1, A_BLK, A_HEADS * A_HD), lambda i, j: (i, j, 0)),
        out_shape=jax.ShapeDtypeStruct((b, rc, A_HEADS * A_HD), BF16),
        compiler_params=_cp(("parallel", "arbitrary")),
        name="win_attn",
    )(sink, proj, proj, proj, proj, proj, _attn_bias())


def _scan_blocks(dirs, *, dvp):
    r_i = lax.broadcasted_iota(jnp.int32, (CHUNK, CHUNK), 0)
    c_i = lax.broadcasted_iota(jnp.int32, (CHUNK, CHUNK), 1)
    units = []
    for q, k, v, logf, s_ref, o_ref, reverse in dirs:
        nchunk = q.shape[0] // CHUNK
        tri = (r_i <= c_i) if reverse else (r_i >= c_i)
        tri3 = jnp.concatenate([jnp.where(tri, 1.0, 0.0).astype(BF16)] * 3, axis=1)
        for c in (range(nchunk - 1, -1, -1) if reverse else range(nchunk)):
            sl = slice(c * CHUNK, (c + 1) * CHUNK)
            units.append(dict(q=q[sl], k=k[sl], v=v[sl].astype(BF16), lf=logf[sl], sl=sl, tri=tri, tri3=tri3,
                              i_tot=0 if reverse else CHUNK - 1,
                              i_ref=CHUNK // 2 - 1 if reverse else CHUNK // 2, s_ref=s_ref, o_ref=o_ref))
    heads = [(slice(h * DKP, (h + 1) * DKP), slice(h * dvp, (h + 1) * dvp)) for h in range(R_HEADS)]
    for u in units:
        lf = u["lf"]
        hi = lf.astype(BF16)
        r1 = lf - hi.astype(F32)
        mid = r1.astype(BF16)
        lo = (r1 - mid.astype(F32)).astype(BF16)
        u["bsum"] = jnp.dot(u["tri3"], jnp.concatenate([hi, mid, lo], axis=0), preferred_element_type=F32)
    for u in units:
        bsum = u["bsum"]
        tot = bsum[u["i_tot"]:u["i_tot"] + 1, :]
        ref = bsum[u["i_ref"]:u["i_ref"] + 1, :]
        qg = u["q"] * jnp.exp(bsum - ref)
        kg = u["k"] * jnp.exp(ref - bsum)
        u["qb"] = (qg * jnp.exp(ref)).astype(BF16)
        u["kd"] = (kg * jnp.exp(tot - ref)).astype(BF16)
        u["qg"] = qg.astype(BF16)
        u["kg"] = kg.astype(BF16)
        u["dec"] = jnp.exp(tot)
    for u in units:
        u["att"] = [lax.dot_general(u["qg"][:, ks], u["kg"][:, ks], _NT, preferred_element_type=F32)
                    for ks, _ in heads]
        u["ds"] = [lax.dot_general(u["v"][:, vs], u["kd"][:, ks], _TN, preferred_element_type=F32)
                   for ks, vs in heads]
    for u in units:
        u["oi"] = [jnp.dot(jnp.where(u["tri"], a, 0.0).astype(BF16), u["v"][:, vs], preferred_element_type=F32)
                   for a, (_, vs) in zip(u["att"], heads)]
    state = {}
    for u in units:
        key = id(u["s_ref"])
        if key not in state:
            state[key] = [u["s_ref"][h] for h in range(R_HEADS)]
        st = state[key]
        outs = []
        for h, (ks, _) in enumerate(heads):
            outs.append(u["oi"][h] + lax.dot_general(u["qb"][:, ks], st[h].astype(BF16), _NT,
                                                     preferred_element_type=F32))
            st[h] = st[h] * u["dec"][:, ks] + u["ds"][h]
        u["o_ref"][0, u["sl"], :] = jnp.concatenate(outs, axis=1).astype(BF16)
    for _, _, _, _, s_ref, _, _ in dirs:
        for h in range(R_HEADS):
            s_ref[h] = state[id(s_ref)][h]


def _hgrn_scan_kernel(lb_ref, qf_ref, vf_ref, zf_ref, qb_ref, vb_ref, zb_ref, of_ref, ob_ref, sf_ref, sb_ref):
    @pl.when(pl.program_id(1) == 0)
    def _():
        sf_ref[...] = jnp.zeros_like(sf_ref)
        sb_ref[...] = jnp.zeros_like(sb_ref)

    def gates(z, lb):
        f = lb + (1.0 - lb) * jax.nn.sigmoid(z.astype(F32))
        return 1.0 - f, jnp.log(f)

    kf, lf = gates(zf_ref[0], lb_ref[0:1, :])
    kb, lbw = gates(zb_ref[0], lb_ref[1:2, :])
    _scan_blocks([(qf_ref[0], kf, vf_ref[0], lf, sf_ref, of_ref, False),
                  (qb_ref[0], kb, vb_ref[0], lbw, sb_ref, ob_ref, True)], dvp=128)


def _bwd_blk(j, nt):
    return jnp.where(j == 0, 0, nt - j)


def _hgrn_scan(proj, lb, cq, cv, cf, cb):
    b, rc, _ = proj.shape
    nt = rc // TM
    w = R_HEADS * DKP

    def fw(col):
        return pl.BlockSpec((1, TM, w), lambda i, j: (i, j, col))

    def bw(col):
        return pl.BlockSpec((1, TM, w), lambda i, j: (i, _bwd_blk(j, nt), col))

    o_sd = jax.ShapeDtypeStruct((b, rc, w), BF16)
    return pl.pallas_call(
        _hgrn_scan_kernel,
        grid=(b, nt),
        in_specs=[pl.BlockSpec((2, w), lambda i, j: (0, 0)),
                  fw(cq), fw(cv), fw(cf), bw(cq), bw(cv), bw(cb)],
        out_specs=[pl.BlockSpec((1, TM, w), lambda i, j: (i, j, 0)),
                   pl.BlockSpec((1, TM, w), lambda i, j: (i, _bwd_blk(j, nt), 0))],
        out_shape=[o_sd, o_sd],
        scratch_shapes=[pltpu.VMEM((R_HEADS, 128, DKP), F32), pltpu.VMEM((R_HEADS, 128, DKP), F32)],
        compiler_params=_cp(("parallel", "arbitrary"), 40 * 1024 * 1024),
        name="hgrn_scan",
    )(lb, proj, proj, proj, proj, proj, proj)


def _gla_scan_kernel(gwf_ref, gwb_ref, gb_ref, qf_ref, kf_ref, vf_ref, rf_ref, qb_ref, kb_ref, vb_ref, rb_ref,
                     of_ref, ob_ref, sf_ref, sb_ref):
    @pl.when(pl.program_id(1) == 0)
    def _():
        sf_ref[...] = jnp.zeros_like(sf_ref)
        sb_ref[...] = jnp.zeros_like(sb_ref)

    def decay(r, gw, gb):
        zz = jnp.dot(jnp.concatenate([r, r], axis=1), _split2(gw, 0), preferred_element_type=F32) + gb
        return (jnp.minimum(zz, 0.0) - jnp.log(1.0 + jnp.exp(-jnp.abs(zz)))) * (1.0 / GLA_TAU)

    qs = GLA_DK ** -0.5
    _scan_blocks([(qf_ref[0].astype(F32) * qs, kf_ref[0], vf_ref[0],
                   decay(rf_ref[0], gwf_ref[...], gb_ref[0:1, :]), sf_ref, of_ref, False),
                  (qb_ref[0].astype(F32) * qs, kb_ref[0], vb_ref[0],
                   decay(rb_ref[0], gwb_ref[...], gb_ref[1:2, :]), sb_ref, ob_ref, True)], dvp=GLA_DVP)


def _gla_scan(proj, gwf, gwb, gb, cq, ck, cv, cr):
    b, rc, _ = proj.shape
    nt = rc // TM
    wk = R_HEADS * DKP
    wv = R_HEADS * GLA_DVP

    def spec(width, col, back):
        if back:
            return pl.BlockSpec((1, TM, width), lambda i, j: (i, _bwd_blk(j, nt), col))
        return pl.BlockSpec((1, TM, width), lambda i, j: (i, j, col))

    o_sd = jax.ShapeDtypeStruct((b, rc, wv), BF16)
    full = lambda shape: pl.BlockSpec(shape, lambda i, j: (0,) * len(shape))
    return pl.pallas_call(
        _gla_scan_kernel,
        grid=(b, nt),
        in_specs=[full((128, wk)), full((128, wk)), full((2, wk)),
                  spec(wk, cq, False), spec(wk, ck, False), spec(wv, cv, False), spec(128, cr, False),
                  spec(wk, cq, True), spec(wk, ck, True), spec(wv, cv, True), spec(128, cr, True)],
        out_specs=[spec(wv, 0, False), spec(wv, 0, True)],
        out_shape=[o_sd, o_sd],
        scratch_shapes=[pltpu.VMEM((R_HEADS, GLA_DVP, DKP), F32), pltpu.VMEM((R_HEADS, GLA_DVP, DKP), F32)],
        compiler_params=_cp(("parallel", "arbitrary"), 48 * 1024 * 1024),
        name="gla_scan",
    )(gwf, gwb, gb, proj, proj, proj, proj, proj, proj, proj, proj)


def _fourier_kernel(z_ref, cs_ref, ct_ref, st_ref, o_ref, *, scale):
    w = F_GROUPS * F_GD
    zz = jnp.dot(z_ref[0, CTX:, :], cs_ref[...], preferred_element_type=F32).astype(BF16)
    y = jnp.dot(ct_ref[...], zz[:, :w], preferred_element_type=F32)
    y = y - jnp.dot(st_ref[...], zz[:, w:], preferred_element_type=F32)
    o_ref[0] = (y * scale).astype(BF16)


def _fourier(proj, zcol, cs, ct, st, t):
    b, rc, _ = proj.shape
    w = F_GROUPS * F_GD
    const = lambda shape: pl.BlockSpec(shape, lambda i: (0, 0), pipeline_mode=pl.Buffered(1))
    return pl.pallas_call(
        functools.partial(_fourier_kernel, scale=1.0 / math.sqrt(t * F_GD)),
        grid=(b,),
        in_specs=[pl.BlockSpec((1, rc, w), lambda i: (i, 0, zcol)), const((w, 2 * w)), const((t, t)), const((t, t))],
        out_specs=pl.BlockSpec((1, t, w), lambda i: (i, 0, 0)),
        out_shape=jax.ShapeDtypeStruct((b, t, w), BF16),
        compiler_params=_cp(("arbitrary",), 48 * 1024 * 1024),
        name="fourier",
    )(proj, cs, ct, st)


def _route(logits, rb_ref):
    lt = logits.T
    l = [lt[e:e + 1, :] for e in range(N_EXP)]
    m = functools.reduce(jnp.maximum, l)
    ex = [jnp.exp(v - m) for v in l]
    den = functools.reduce(lambda a, c: a + c, ex)
    p = [v / den for v in ex]
    sel = [p[e] + rb_ref[e] for e in range(N_EXP)]
    g_score, g_gate = [], []
    for g in range(N_GRP):
        s = sel[E_PER * g:E_PER * (g + 1)]
        chosen = []
        for i in range(E_PER):
            rank = jnp.zeros_like(s[i])
            for j in range(E_PER):
                if j == i:
                    continue
                ahead = (s[j] >= s[i]) if j < i else (s[j] > s[i])
                rank = rank + jnp.where(ahead, 1.0, 0.0)
            chosen.append(rank < 2.0)
        g_score.append(functools.reduce(lambda a, c: a + c,
                                        [jnp.where(chosen[i], s[i], 0.0) for i in range(E_PER)]))
        g_gate.append([jnp.where(chosen[i], p[E_PER * g + i], 0.0) for i in range(E_PER)])
    best = jnp.zeros_like(g_score[0])
    best_s = g_score[0]
    for g in range(1, N_GRP):
        upd = g_score[g] > best_s
        best = jnp.where(upd, float(g), best)
        best_s = jnp.where(upd, g_score[g], best_s)
    gate = []
    for i in range(E_PER):
        gi = g_gate[0][i]
        for g in range(1, N_GRP):
            gi = jnp.where(best == float(g), g_gate[g][i], gi)
        gate.append(gi)
    wsum = functools.reduce(lambda a, c: a + c, gate)
    row = lax.broadcasted_iota(jnp.int32, (GW, lt.shape[1]), 0)
    side = jnp.zeros((GW, lt.shape[1]), F32)
    for i in range(E_PER):
        side = jnp.where(row == i, gate[i] / wsum, side)
    return side.T, best


def _outproj_kernel(rb_ref, a_ref, of_ref, ob_ref, gt_ref, *refs, dvp, dv):
    *h_refs, mod_ref, gn_ref, wa_ref, wr_ref, lng_ref, lnb_ref, wrt_ref, h1_ref, u2_ref, grp_ref = refs
    o = of_ref[0].astype(F32) + ob_ref[0].astype(F32)
    parts = []
    for hh in range(R_HEADS):
        sl = slice(hh * dvp, (hh + 1) * dvp)
        oh = o[:, sl]
        ms = jnp.sum(oh * oh, axis=-1, keepdims=True) * (1.0 / dv)
        parts.append(oh * lax.rsqrt(ms + LN_EPS))
    rec = jnp.concatenate(parts, axis=1) * gn_ref[...] * _silu(gt_ref[0].astype(F32))
    y = jnp.dot(a_ref[0], wa_ref[...], preferred_element_type=F32)
    y = y + jnp.dot(rec.astype(BF16), wr_ref[...], preferred_element_type=F32)
    g1 = mod_ref[0, 0, 2:3, :]
    h1 = _ln(ALPHA * _rows_of(h_refs) + g1 * y) * lng_ref[0:1, :] + lnb_ref[0:1, :]
    h1_ref[0] = h1
    u2 = _ln(h1) * (1.0 + mod_ref[0, 0, 4:5, :]) + mod_ref[0, 0, 3:4, :]
    lg = jnp.dot(_split2(u2, 0), _split2(wrt_ref[...], 1), preferred_element_type=F32)
    lg = lg[:TM] + lg[TM:]
    logits = lg[:, :GW] + lg[:, GW:]
    side, best = _route(logits, rb_ref)
    grp_ref[0, 0] = best
    for c in range(D // 128):
        u2_ref[pl.ds(c, TM, stride=XROWS), :] = u2[:, 128 * c:128 * (c + 1)]
    u2_ref[pl.ds(D // 128, TM, stride=XROWS), :] = side
    for c in range(D // 128 + 1, XROWS):
        u2_ref[pl.ds(c, TM, stride=XROWS), :] = jnp.zeros((TM, 128), F32)


def _outproj(b_router, a, a_col, a_w, o_f, o_b, proj, g_col, h, modsel, gnorm, wa, wr, lng, lnb, wrt, *,
             dv, dvp, row_off):
    b, rc, _ = proj.shape
    nt = rc // TM - row_off
    wrec = R_HEADS * dvp
    full = lambda shape: pl.BlockSpec(shape, lambda i, j: (0,) * len(shape))
    h_specs, h_args = _row_specs(h, row_off)
    return pl.pallas_call(
        functools.partial(_outproj_kernel, dvp=dvp, dv=dv),
        grid=(b, nt),
        in_specs=[pl.BlockSpec(memory_space=pltpu.SMEM),
                  pl.BlockSpec((1, TM, a_w), lambda i, j: (i, j + (row_off if a.shape[1] == rc else 0), a_col)),
                  pl.BlockSpec((1, TM, wrec), lambda i, j: (i, j + row_off, 0)),
                  pl.BlockSpec((1, TM, wrec), lambda i, j: (i, j + row_off, 0)),
                  pl.BlockSpec((1, TM, wrec), lambda i, j: (i, j + row_off, g_col))] + h_specs + [
                  pl.BlockSpec((1, 1, 6, D), lambda i, j: (i, jnp.minimum(j + row_off, 1), 0, 0)),
                  full((1, wrec)), full((a_w, D)), full((wrec, D)), full((1, D)), full((1, D)), full((D, GW))],
        out_specs=[pl.BlockSpec((1, TM, D), lambda i, j: (i, j, 0)),
                   pl.BlockSpec((TM * XROWS, 128), lambda i, j: (i * nt + j, 0)),
                   pl.BlockSpec((1, 1, 1, TM), lambda i, j: (i, j, 0, 0))],
        out_shape=[jax.ShapeDtypeStruct((b, nt * TM, D), F32),
                   jax.ShapeDtypeStruct((b * nt * TM * XROWS, 128), F32),
                   jax.ShapeDtypeStruct((b, nt, 1, TM), F32)],
        compiler_params=_cp(("parallel", "arbitrary"), V7X_VMEM_LIMIT),
        name="outproj",
    )(b_router, a, o_f, o_b, proj, *h_args, modsel, gnorm, wa, wr, lng, lnb, wrt)


def _start_token_gather(idx_ref, base, src_hbm, dst_ref, sem, rows):
    def body(r, carry):
        t = idx_ref[base + r]
        pltpu.make_async_copy(src_hbm.at[pl.ds(pl.multiple_of(t * rows, rows), rows), :],
                              dst_ref.at[pl.ds(pl.multiple_of(r * rows, rows), rows), :], sem).start()
        return carry

    lax.fori_loop(0, TM, body, 0, unroll=8)


def _wait_token_gather(src_hbm, dst_ref, sem):
    pltpu.make_async_copy(src_hbm.at[pl.ds(0, dst_ref.shape[0]), :], dst_ref, sem).wait()


def _gather_ring(idx_ref, first_of, step, last, src_hbm, buf_ref, sem, rows):
    depth = buf_ref.shape[0]
    slot = step % depth

    @pl.when(step == 0)
    def _():
        for k in range(depth - 1):
            _start_token_gather(idx_ref, first_of(jnp.minimum(k, last)), src_hbm, buf_ref.at[k], sem.at[k], rows)

    _wait_token_gather(src_hbm, buf_ref.at[slot], sem.at[slot])
    ahead = (step + depth - 1) % depth
    nxt = first_of(jnp.minimum(step + depth - 1, last))

    def issue(r0, r1):
        for r in range(r0, r1):
            t = idx_ref[nxt + r]
            pltpu.make_async_copy(src_hbm.at[pl.ds(pl.multiple_of(t * rows, rows), rows), :],
                                  buf_ref.at[ahead, pl.ds(r * rows, rows), :], sem.at[ahead]).start()

    def finish():
        @pl.when(step == last)
        def _():
            for k in range(1, depth):
                other = (step + k) % depth
                _wait_token_gather(src_hbm, buf_ref.at[other], sem.at[other])

    return slot, issue, finish


def _moe_kernel(tg_ref, first_ref, src_ref, x_hbm, w1_ref, w3_ref, w2_ref, o_ref, xb_ref, sem,
                w1b_ref, w3b_ref, w2b_ref):
    i = pl.program_id(0)
    slot, issue, finish = _gather_ring(src_ref, lambda k: first_ref[k], i, pl.num_programs(0) - 1,
                                       x_hbm, xb_ref, sem, XROWS)

    @pl.when((i == 0) | (tg_ref[i] != tg_ref[jnp.maximum(i - 1, 0)]))
    def _():
        for e in range(E_PER):
            w1b_ref[e] = w1_ref[e].astype(BF16)
            w3b_ref[e] = w3_ref[e].astype(BF16)
            w2b_ref[e] = w2_ref[e].astype(BF16)

    xb = xb_ref.at[slot]
    x = jnp.concatenate([xb[pl.ds(c, TM, stride=XROWS), :] for c in range(D // 128)], axis=1).astype(BF16)
    side = xb[pl.ds(D // 128, TM, stride=XROWS), :]
    y = jnp.zeros((TM, D), F32)
    per = TM // E_PER
    for e in range(E_PER):
        issue(e * per, (e + 1) * per)
        h1 = jnp.dot(x, w1b_ref[e], preferred_element_type=F32)
        h3 = jnp.dot(x, w3b_ref[e], preferred_element_type=F32)
        hh = (_silu(h1) * h3 * side[:, e:e + 1]).astype(BF16)
        y = y + jnp.dot(hh, w2b_ref[e], preferred_element_type=F32)
    for c in range(D // 128):
        o_ref[pl.ds(c, TM, stride=YROWS), :] = y[:, 128 * c:128 * (c + 1)]
    finish()


def _moe(tile_grp, tile_first, order, u2x, w1, w3, w2, layer):
    mp = tile_grp.shape[0] * TM
    wspec = lambda k, n: pl.BlockSpec((None, E_PER, k, n), lambda i, tg, tf, od: (layer, tg[i], 0, 0),
                                      pipeline_mode=pl.Buffered(1))
    return pl.pallas_call(
        _moe_kernel,
        grid_spec=pltpu.PrefetchScalarGridSpec(
            num_scalar_prefetch=3, grid=(mp // TM,),
            in_specs=[pl.BlockSpec(memory_space=pl.ANY), wspec(D, D_FF), wspec(D, D_FF), wspec(D_FF, D)],
            out_specs=pl.BlockSpec((TM * YROWS, 128), lambda i, tg, tf, od: (i, 0)),
            scratch_shapes=[pltpu.VMEM((2, TM * XROWS, 128), F32), pltpu.SemaphoreType.DMA((2,)),
                            pltpu.VMEM((E_PER, D, D_FF), BF16), pltpu.VMEM((E_PER, D, D_FF), BF16),
                            pltpu.VMEM((E_PER, D_FF, D), BF16)]),
        out_shape=jax.ShapeDtypeStruct((mp * YROWS, 128), F32),
        compiler_params=_cp(("arbitrary",), V7X_VMEM_LIMIT),
        name="moe",
    )(tile_grp, tile_first, order, u2x, w1, w3, w2)


def _combine_kernel(pos_ref, y_hbm, h_ref, mod_ref, lng_ref, lnb_ref, o_ref, yb_ref, sem, *, nt):
    step = pl.program_id(0) * nt + pl.program_id(1)
    slot, issue, finish = _gather_ring(pos_ref, lambda k: k * TM, step, pl.num_programs(0) * nt - 1,
                                       y_hbm, yb_ref, sem, YROWS)
    yb = yb_ref.at[slot]
    g2 = mod_ref[0, 0, 5:6, :]
    strip = TM // 4
    for r0 in range(0, TM, strip):
        issue(r0, r0 + strip)
        y = jnp.concatenate([yb[pl.ds(r0 * YROWS + c, strip, stride=YROWS), :] for c in range(D // 128)], axis=1)
        o_ref[0, r0:r0 + strip, :] = (_ln(ALPHA * h_ref[0, r0:r0 + strip, :] + g2 * y) * lng_ref[0:1, :]
                                      + lnb_ref[0:1, :])
    finish()


def _combine(pos, y_sorted, h1, modsel, lng, lnb, row_off):
    b, r, _ = h1.shape
    nt = r // TM
    full = lambda shape: pl.BlockSpec(shape, lambda i, j, p: (0,) * len(shape))
    return pl.pallas_call(
        functools.partial(_combine_kernel, nt=nt),
        grid_spec=pltpu.PrefetchScalarGridSpec(
            num_scalar_prefetch=1, grid=(b, nt),
            in_specs=[pl.BlockSpec(memory_space=pl.ANY),
                      pl.BlockSpec((1, TM, D), lambda i, j, p: (i, j, 0)),
                      pl.BlockSpec((1, 1, 6, D), lambda i, j, p: (i, jnp.minimum(j + row_off, 1), 0, 0)),
                      full((1, D)), full((1, D))],
            out_specs=pl.BlockSpec((1, TM, D), lambda i, j, p: (i, j, 0)),
            scratch_shapes=[pltpu.VMEM((3, TM * YROWS, 128), F32), pltpu.SemaphoreType.DMA((3,))]),
        out_shape=jax.ShapeDtypeStruct((b, r, D), F32),
        compiler_params=_cp(("arbitrary", "arbitrary")),
        name="moe_combine",
    )(pos, y_sorted, h1, modsel, lng, lnb)


def _sort_by_group(grp):
    n = grp.shape[0]
    onehot = (grp[:, None] == jnp.arange(N_GRP, dtype=jnp.int32)[None, :]).astype(jnp.int32)
    csum = jnp.cumsum(onehot, axis=0)
    counts = csum[-1]
    padded = ((counts + TM - 1) // TM) * TM
    ends = jnp.cumsum(padded)
    starts = ends - padded
    rank = jnp.sum(csum * onehot, axis=1) - 1
    pos = (starts[grp] + rank).astype(jnp.int32)
    _, order = lax.sort((grp, jnp.arange(n, dtype=jnp.int32)), num_keys=1, is_stable=True)
    order = jnp.concatenate([order, jnp.zeros((TM,), jnp.int32)])
    ntile = n // TM + N_GRP
    tile_start = jnp.arange(ntile, dtype=jnp.int32) * TM
    tile_grp = jnp.minimum(jnp.sum((tile_start[:, None] >= ends[None, :]).astype(jnp.int32), axis=1), N_GRP - 1)
    first = (jnp.cumsum(counts) - counts)[tile_grp] + tile_start - starts[tile_grp]
    return pos, order, tile_grp.astype(jnp.int32), jnp.minimum(first, n).astype(jnp.int32)


def _moe_layer(u2x, grp, h1, modsel, experts, lng, lnb, row_off):
    pos, order, tile_grp, tile_first = _sort_by_group(grp.reshape(-1).astype(jnp.int32))
    y_sorted = _moe(tile_grp, tile_first, order, u2x, *experts)
    return _combine(pos, y_sorted, h1, modsel, lng, lnb, row_off)


def _rope_tables(rc, t):
    half = A_HD // 4
    inv = ROPE_BASE ** (-jnp.arange(half, dtype=F32) / half)
    tpos = jnp.arange(t)
    rows = (tpos // GRID_W).astype(F32)
    cols = (tpos % GRID_W).astype(F32)

    def cs(pos):
        ang = pos[:, None] * inv[None, :]
        c, s = jnp.cos(ang), jnp.sin(ang)
        return jnp.concatenate([c, c], -1), jnp.concatenate([-s, s], -1)

    cr, sr = cs(rows)
    cc_, sc_ = cs(cols)
    cos_h = jnp.concatenate([cr, cc_], -1)
    sin_h = jnp.concatenate([sr, sc_], -1)
    one = jnp.ones((t, A_HD), F32)
    zero = jnp.zeros((t, A_HD), F32)
    lat = jnp.stack([jnp.concatenate([cos_h, cos_h], -1), jnp.concatenate([sin_h, sin_h], -1),
                     jnp.concatenate([cos_h, one], -1), jnp.concatenate([sin_h, zero], -1)])
    ctx = jnp.stack([jnp.ones((CTX, 128), F32), jnp.zeros((CTX, 128), F32)] * 2)
    return jnp.concatenate([ctx, lat], axis=1)


def _dft_tables(t):
    def cos_sin(num, den):
        ang = (2.0 * math.pi / den) * (num % den).astype(F32)
        return jnp.cos(ang), jnp.sin(ang)

    k = jnp.arange(t, dtype=jnp.int32)[:, None]
    ca, sa = cos_sin(k * jnp.arange(t // GRID_W, dtype=jnp.int32)[None, :], t // GRID_W)
    cb, sb = cos_sin(k * jnp.arange(GRID_W, dtype=jnp.int32)[None, :], t)
    ct = (ca[:, :, None] * cb[:, None, :] - sa[:, :, None] * sb[:, None, :]).reshape(t, t)
    st = (sa[:, :, None] * cb[:, None, :] + ca[:, :, None] * sb[:, None, :]).reshape(t, t)
    c = jnp.arange(F_GD, dtype=jnp.int32)
    cc, sc = cos_sin(c[:, None] * c[None, :], F_GD)
    eye = jnp.eye(F_GROUPS, dtype=F32)
    cs = jnp.concatenate([jnp.kron(eye, cc), jnp.kron(eye, sc)], axis=1)
    return cs.astype(BF16), ct.astype(BF16), st.astype(BF16)


def _pad_heads(w, width, padded):
    lead = w.shape[:-1]
    w = w.reshape(lead + (R_HEADS, width))
    w = jnp.pad(w, [(0, 0)] * len(lead) + [(0, 0), (0, padded - width)])
    return w.reshape(lead + (R_HEADS * padded,))


EVEN_SEGS = tuple([(128 * g, 128 * (g + 1), 0) for g in range(4)] + [(c, c + 512, None) for c in range(512, 3072, 512)]
                  + [(3072 + 128 * g, 3072 + 128 * (g + 1), 1) for g in range(2)])
ODD_W = 512 + 512 + 1024 + 1024 + 256 + 128
ODD_SEGS = tuple([(c, c + 512, None) for c in range(0, 3072, 512)] + [(3072, ODD_W, None)])


def _layer_even(h, modsel, tbl, w_in, sink, lb_logits, g_norm, w_out, lng, lnb, w_router, b_router, experts):
    aq, ak, av, ff, fb, hq, hi, hg = jnp.split(w_in, np.cumsum([512, 128, 128, 512, 512, 512, 512]).tolist(), axis=1)
    kv = [jnp.concatenate([ak[:, A_HD * j:A_HD * (j + 1)], av[:, A_HD * j:A_HD * (j + 1)]], axis=1)
          for j in range(A_KV)]
    w = jnp.concatenate([aq, ff, fb, hq, hi, hg] + kv, axis=1).astype(BF16)
    proj = _inproj(h, modsel, w, tbl, EVEN_SEGS)
    o_att = _attention(proj, sink, 3072 // (2 * A_HD * A_KV))
    lb = jnp.cumsum(jax.nn.softmax(lb_logits.astype(F32), axis=1), axis=1)[:, 0]
    o_f, o_b = _hgrn_scan(proj, lb, 3, 4, 1, 2)
    wo = w_out.astype(BF16)
    h1, u2x, grp = _outproj(b_router, o_att, 0, 512, o_f, o_b, proj, 5, h, modsel, g_norm.reshape(1, -1),
                            wo[:512], wo[512:], lng[0:1], lnb[0:1], w_router, dv=128, dvp=128, row_off=0)
    return _moe_layer(u2x, grp, h1, modsel, experts, lng[1:2], lnb[1:2], 0)


def _layer_odd(h, modsel, tbl, dft, w_in, gate_w, gate_b, g_norm, w_out, lng, lnb, w_router, b_router, experts, t):
    q, k, v, rf, rb, g, z = jnp.split(w_in, np.cumsum([384, 384, 768, 16, 16, 768]).tolist(), axis=1)
    r = jnp.pad(jnp.concatenate([rf, rb], axis=1), ((0, 0), (0, 96)))
    w = jnp.concatenate([_pad_heads(q, GLA_DK, DKP), _pad_heads(k, GLA_DK, DKP), _pad_heads(v, GLA_DV, GLA_DVP),
                         _pad_heads(g, GLA_DV, GLA_DVP), z, r], axis=1).astype(BF16)
    proj = _inproj(h, modsel, w, tbl, ODD_SEGS)
    gw = _pad_heads(gate_w, GLA_DK, DKP)
    gwf = jnp.pad(gw[0], ((0, 112), (0, 0)))
    gwb = jnp.pad(gw[1], ((16, 96), (0, 0)))
    gb = _pad_heads(gate_b, GLA_DK, DKP)
    o_f, o_b = _gla_scan(proj, gwf, gwb, gb, 0, 1, 1, 3328 // 128)
    cs, ct, st = dft
    four = _fourier(proj, 3072 // (F_GROUPS * F_GD), cs, ct, st, t)
    wo = w_out
    wr = jnp.pad(wo[:768].reshape(R_HEADS, GLA_DV, D), ((0, 0), (0, GLA_DVP - GLA_DV), (0, 0))).reshape(-1, D)
    h1, u2x, grp = _outproj(b_router, four, 0, 256, o_f, o_b, proj, 2, h, modsel,
                            _pad_heads(g_norm, GLA_DV, GLA_DVP).reshape(1, -1),
                            wo[768:].astype(BF16), wr.astype(BF16), lng[0:1], lnb[0:1], w_router,
                            dv=GLA_DV, dvp=GLA_DVP, row_off=CTX // TM)
    return _moe_layer(u2x, grp, h1, modsel, experts, lng[1:2], lnb[1:2], CTX // TM)


def kernel(x, c, ctx, c_ctx, w_ada, b_ada, ln_g, ln_b, w_in_even, attn_sink, hgrn_lb_logits, hgrn_norm, w_out_even, w_in_odd, gla_gate_w, gla_gate_b, gla_norm, w_out_odd, w_router, b_router, w_expert_gate, w_expert_up, w_expert_down):
    b, t, _ = x.shape
    rc = CTX + t
    assert ctx.shape[1] == CTX
    cc = jnp.zeros((16, D), F32).at[:b].set(c).at[b].set(c_ctx)
    mods = _ada_mods(cc, w_ada, b_ada).reshape(2, 16, 6, D)

    def modsel(l):
        return jnp.stack([jnp.broadcast_to(mods[l, b], (b, 6, D)), mods[l, :b]], axis=1)

    tbl = _rope_tables(rc, t)
    wrt = jnp.pad(w_router, ((0, 0), (0, GW - N_EXP)))
    experts = (w_expert_gate, w_expert_up, w_expert_down)
    h = _layer_even((ctx, x), modsel(0), tbl, w_in_even[0], attn_sink[0], hgrn_lb_logits, hgrn_norm[0], w_out_even[0],
                    ln_g[0], ln_b[0], wrt, b_router, experts + (0,))
    return _layer_odd(h, modsel(1), tbl, _dft_tables(t), w_in_odd[0], gla_gate_w[0], gla_gate_b[0], gla_norm[0],
                      w_out_odd[0], ln_g[1], ln_b[1], wrt, b_router, experts + (1,), t)
```

```python
import functools
import math

import numpy as np
import jax
import jax.numpy as jnp
from jax import lax
from jax.experimental import pallas as pl
from jax.experimental.pallas import tpu as pltpu

F32 = jnp.float32
BF16 = jnp.bfloat16
HIGHEST = lax.Precision.HIGHEST

D = 1024
CTX = 256
GRID_W = 64
LN_EPS = 1e-5
NEG = -1e30
ALPHA = 4.0 ** 0.25
ROPE_BASE = 10000.0

A_HEADS, A_KV, A_HD, A_BLK = 8, 2, 64, 128
R_HEADS = 4
DKP = 128
GLA_DK, GLA_DV, GLA_DVP = 96, 192, 256
GLA_TAU = 16.0
CHUNK = 64
N_EXP, N_GRP, E_PER, TOP_K = 16, 4, 4, 2
D_FF = 512
F_GROUPS, F_GD = 4, 64

TM = 256
IN_SUB = 3
GW = 128
XROWS = 16
YROWS = 8
V7X_VMEM_LIMIT = 56 * 1024 * 1024


def _cp(sem, vmem=None):
    return pltpu.CompilerParams(dimension_semantics=sem, vmem_limit_bytes=vmem)


def _ln(x):
    mu = jnp.mean(x, axis=-1, keepdims=True)
    xc = x - mu
    var = jnp.mean(xc * xc, axis=-1, keepdims=True)
    return xc * lax.rsqrt(var + LN_EPS)


def _silu(x):
    return x * jax.nn.sigmoid(x)


def _split2(w, axis):
    hi = w.astype(BF16)
    return jnp.concatenate([hi, (w - hi.astype(F32)).astype(BF16)], axis=axis)


_NT = (((1,), (1,)), ((), ()))
_TN = (((0,), (0,)), ((), ()))


def _ada_kernel(c_ref, w_ref, b_ref, o_ref):
    s = _silu(c_ref[...])
    o_ref[0] = jnp.dot(s, w_ref[0], precision=HIGHEST, preferred_element_type=F32) + b_ref[0]


def _ada_mods(cc, w_ada, b_ada):
    depth, _, n = w_ada.shape
    tn = 1536
    return pl.pallas_call(
        _ada_kernel,
        grid=(depth, n // tn),
        in_specs=[pl.BlockSpec((16, D), lambda l, i: (0, 0)),
                  pl.BlockSpec((1, D, tn), lambda l, i: (l, 0, i)),
                  pl.BlockSpec((1, 1, tn), lambda l, i: (l, 0, i))],
        out_specs=pl.BlockSpec((1, 16, tn), lambda l, i: (l, 0, i)),
        out_shape=jax.ShapeDtypeStruct((depth, 16, n), F32),
        compiler_params=_cp(("arbitrary", "arbitrary"), 40 * 1024 * 1024),
        name="ada_mod",
    )(cc, w_ada, b_ada.reshape(depth, 1, n))


def _rows_of(h_refs):
    if len(h_refs) == 1:
        return h_refs[0][0]
    return jnp.where(pl.program_id(1) == 0, h_refs[0][0], h_refs[1][0])


def _row_specs(h, row_off=0):
    if isinstance(h, tuple):
        assert row_off == 0 and h[0].shape[1] == TM
        return [pl.BlockSpec((1, TM, D), lambda i, j: (i, 0, 0)),
                pl.BlockSpec((1, TM, D), lambda i, j: (i, jnp.maximum(j - 1, 0), 0))], list(h)
    return [pl.BlockSpec((1, TM, D), lambda i, j: (i, j + row_off, 0))], [h]


def _inproj_kernel(*refs, segs):
    *h_refs, mod0_ref, mod1_ref, w_ref, tbl_ref, o_ref = refs
    if len(h_refs) == 1:
        x = h_refs[0][0]
    else:
        ctx_ref, *x_refs = h_refs
        x = jnp.concatenate([jnp.where(pl.program_id(1) == 0, ctx_ref[0], x_refs[0][0])]
                            + [r[0] for r in x_refs[1:]], axis=0)
    xn = _ln(x)
    u = jnp.concatenate([xn[:TM] * (1.0 + mod0_ref[0, 0, 1:2, :]) + mod0_ref[0, 0, 0:1, :],
                         xn[TM:] * (1.0 + mod1_ref[0, 0, 1:2, :]) + mod1_ref[0, 0, 0:1, :]], axis=0).astype(BF16)
    lane = lax.broadcasted_iota(jnp.int32, (x.shape[0], 128), 1)
    low = (lane % 32) < 16
    for c0, c1, rope in segs:
        acc = jnp.dot(u, w_ref[:, c0:c1], preferred_element_type=F32)
        if rope is not None:
            partner = jnp.where(low, pltpu.roll(acc, 112, axis=1), pltpu.roll(acc, 16, axis=1))
            acc = acc * tbl_ref[2 * rope] + partner * tbl_ref[2 * rope + 1]
        o_ref[0, :, c0:c1] = acc.astype(BF16)


def _inproj(h, modsel, w, tbl, segs):
    b, rc = modsel.shape[0], tbl.shape[1]
    n = w.shape[1]
    tmi = IN_SUB * TM
    assert rc % tmi == 0
    if isinstance(h, tuple):
        assert h[0].shape[1] == TM
        h_specs = [pl.BlockSpec((1, TM, D), lambda i, j: (i, 0, 0))] + [
            pl.BlockSpec((1, TM, D), lambda i, j, k=k: (i, jnp.maximum(IN_SUB * j + k - 1, 0), 0))
            for k in range(IN_SUB)]
        h_args = [h[0]] + [h[1]] * IN_SUB
    else:
        h_specs, h_args = [pl.BlockSpec((1, tmi, D), lambda i, j: (i, j, 0))], [h]
    return pl.pallas_call(
        functools.partial(_inproj_kernel, segs=segs),
        grid=(b, rc // tmi),
        in_specs=h_specs + [pl.BlockSpec((1, 1, 6, D), lambda i, j: (i, jnp.minimum(j, 1), 0, 0)),
                            pl.BlockSpec((1, 1, 6, D), lambda i, j: (i, 1, 0, 0)),
                            pl.BlockSpec((D, n), lambda i, j: (0, 0)),
                            pl.BlockSpec((4, tmi, 128), lambda i, j: (0, j, 0))],
        out_specs=pl.BlockSpec((1, tmi, n), lambda i, j: (i, j, 0)),
        out_shape=jax.ShapeDtypeStruct((b, rc, n), BF16),
        compiler_params=_cp(("parallel", "arbitrary"), V7X_VMEM_LIMIT),
        name="inproj",
    )(*h_args, modsel, modsel, w, tbl)


def _attn_kernel(sink_ref, q_ref, kvc_ref, kvp_ref, kvm_ref, kvn_ref, bias_ref, o_ref):
    blk = pl.program_id(1)
    g_per = A_HEADS // A_KV
    nq = g_per * A_BLK
    q = q_ref[0] * (A_HD ** -0.5)
    rgrp = lax.broadcasted_iota(jnp.int32, (nq, 1), 0) // A_BLK
    q4, sink_col = [], []
    for h in range(A_KV):
        q4.append(jnp.concatenate([q[:, A_HD * (g_per * h + g):A_HD * (g_per * h + g + 1)] for g in range(g_per)],
                                  axis=0))
        col = jnp.zeros((nq, 1), F32)
        for g in range(g_per):
            col = jnp.where(rgrp == g, sink_ref[h * g_per + g], col)
        sink_col.append(col)

    def attend(kv, bias):
        s = [lax.dot_general(q4[h], kv[:, 2 * A_HD * h:2 * A_HD * h + A_HD], _NT, preferred_element_type=F32)
             for h in range(A_KV)]
        if bias is not None:
            bias4 = jnp.concatenate([bias] * g_per, axis=0)
            s = [x + bias4 for x in s]
        m = [jnp.maximum(jnp.max(s[h], axis=-1, keepdims=True), sink_col[h]) for h in range(A_KV)]
        p = [jnp.exp(s[h] - m[h]) for h in range(A_KV)]
        den = [jnp.sum(p[h], axis=-1, keepdims=True) + jnp.exp(sink_col[h] - m[h]) for h in range(A_KV)]
        o = [jnp.dot(p[h].astype(BF16), kv[:, 2 * A_HD * h + A_HD:2 * A_HD * (h + 1)], preferred_element_type=F32)
             / den[h] for h in range(A_KV)]
        o_ref[0] = jnp.concatenate([o[h][A_BLK * g:A_BLK * (g + 1), :] for h in range(A_KV) for g in range(g_per)],
                                   axis=1).astype(BF16)

    @pl.when(blk < CTX // A_BLK)
    def _():
        attend(kvc_ref[0], None)

    @pl.when(blk >= CTX // A_BLK)
    def _():
        attend(jnp.concatenate([kvc_ref[0], kvp_ref[0], kvm_ref[0], kvn_ref[0]], axis=0), bias_ref[0])


def _attn_bias():
    nk = CTX + 3 * A_BLK
    qi = jnp.arange(A_BLK, dtype=jnp.int32)[:, None]
    col = jnp.arange(nk, dtype=jnp.int32)[None, :]
    rel = col - (CTX + A_BLK)
    dist = qi - rel

    def one(lo, hi):
        in_win = (dist <= A_BLK) & (dist >= -A_BLK) & (rel >= lo) & (rel < hi)
        return jnp.where((col < CTX) | in_win, 0.0, NEG).astype(F32)

    return jnp.stack([one(0, 2 * A_BLK), one(-A_BLK, 2 * A_BLK), one(-A_BLK, A_BLK)])


def _attention(proj, sink, kv_blk):
    b, rc, _ = proj.shape
    nb = rc // A_BLK
    c_b = CTX // A_BLK
    assert nb - c_b >= 2
    kw = 2 * A_HD * A_KV

    def band(off):
        return lambda i, j: (i, jnp.clip(j + off, c_b, nb - 1), kv_blk)

    def which_bias(i, j):
        return (jnp.where(j <= c_b, 0, jnp.where(j == nb - 1, 2, 1)), 0, 0)

    return pl.pallas_call(
        _attn_kernel,
        grid=(b, nb),
        in_specs=[pl.BlockSpec(memory_space=pltpu.SMEM),
                  pl.BlockSpec((1, A_BLK, A_HEADS * A_HD), lambda i, j: (i, j, 0)),
                  pl.BlockSpec((1, CTX, kw), lambda i, j: (i, 0, kv_blk)),
                  pl.BlockSpec((1, A_BLK, kw), band(-1)),
                  pl.BlockSpec((1, A_BLK, kw), band(0)),
                  pl.BlockSpec((1, A_BLK, kw), band(1)),
                  pl.BlockSpec((1, A_BLK, CTX + 3 * A_BLK), which_bias)],
        out_specs=pl.BlockSpec((1, A_BLK, A_HEADS * A_HD), lambda i, j: (i, j, 0)),
        out_shape=jax.ShapeDtypeStruct((b, rc, A_HEADS * A_HD), BF16),
        compiler_params=_cp(("parallel", "arbitrary")),
        name="win_attn",
    )(sink, proj, proj, proj, proj, proj, _attn_bias())


def _scan_blocks(dirs, *, dvp):
    r_i = lax.broadcasted_iota(jnp.int32, (CHUNK, CHUNK), 0)
    c_i = lax.broadcasted_iota(jnp.int32, (CHUNK, CHUNK), 1)
    units = []
    for q, k, v, logf, s_ref, o_ref, reverse in dirs:
        nchunk = q.shape[0] // CHUNK
        tri = (r_i <= c_i) if reverse else (r_i >= c_i)
        tri3 = jnp.concatenate([jnp.where(tri, 1.0, 0.0).astype(BF16)] * 3, axis=1)
        for c in (range(nchunk - 1, -1, -1) if reverse else range(nchunk)):
            sl = slice(c * CHUNK, (c + 1) * CHUNK)
            units.append(dict(q=q[sl], k=k[sl], v=v[sl].astype(BF16), lf=logf[sl], sl=sl, tri=tri, tri3=tri3,
                              i_tot=0 if reverse else CHUNK - 1,
                              i_ref=CHUNK // 2 - 1 if reverse else CHUNK // 2, s_ref=s_ref, o_ref=o_ref))
    heads = [(slice(h * DKP, (h + 1) * DKP), slice(h * dvp, (h + 1) * dvp)) for h in range(R_HEADS)]
    for u in units:
        lf = u["lf"]
        hi = lf.astype(BF16)
        r1 = lf - hi.astype(F32)
        mid = r1.astype(BF16)
        lo = (r1 - mid.astype(F32)).astype(BF16)
        u["bsum"] = jnp.dot(u["tri3"], jnp.concatenate([hi, mid, lo], axis=0), preferred_element_type=F32)
    for u in units:
        bsum = u["bsum"]
        tot = bsum[u["i_tot"]:u["i_tot"] + 1, :]
        ref = bsum[u["i_ref"]:u["i_ref"] + 1, :]
        qg = u["q"] * jnp.exp(bsum - ref)
        kg = u["k"] * jnp.exp(ref - bsum)
        u["qb"] = (qg * jnp.exp(ref)).astype(BF16)
        u["kd"] = (kg * jnp.exp(tot - ref)).astype(BF16)
        u["qg"] = qg.astype(BF16)
        u["kg"] = kg.astype(BF16)
        u["dec"] = jnp.exp(tot)
    for u in units:
        u["att"] = [lax.dot_general(u["qg"][:, ks], u["kg"][:, ks], _NT, preferred_element_type=F32)
                    for ks, _ in heads]
        u["ds"] = [lax.dot_general(u["v"][:, vs], u["kd"][:, ks], _TN, preferred_element_type=F32)
                   for ks, vs in heads]
    for u in units:
        u["oi"] = [jnp.dot(jnp.where(u["tri"], a, 0.0).astype(BF16), u["v"][:, vs], preferred_element_type=F32)
                   for a, (_, vs) in zip(u["att"], heads)]
    state = {}
    for u in units:
        key = id(u["s_ref"])
        if key not in state:
            state[key] = [u["s_ref"][h] for h in range(R_HEADS)]
        st = state[key]
        outs = []
        for h, (ks, _) in enumerate(heads):
            outs.append(u["oi"][h] + lax.dot_general(u["qb"][:, ks], st[h].astype(BF16), _NT,
                                                     preferred_element_type=F32))
            st[h] = st[h] * u["dec"][:, ks] + u["ds"][h]
        u["o_ref"][0, u["sl"], :] = jnp.concatenate(outs, axis=1).astype(BF16)
    for _, _, _, _, s_ref, _, _ in dirs:
        for h in range(R_HEADS):
            s_ref[h] = state[id(s_ref)][h]


def _hgrn_scan_kernel(lb_ref, qf_ref, vf_ref, zf_ref, qb_ref, vb_ref, zb_ref, of_ref, ob_ref, sf_ref, sb_ref):
    @pl.when(pl.program_id(1) == 0)
    def _():
        sf_ref[...] = jnp.zeros_like(sf_ref)
        sb_ref[...] = jnp.zeros_like(sb_ref)

    def gates(z, lb):
        f = lb + (1.0 - lb) * jax.nn.sigmoid(z.astype(F32))
        return 1.0 - f, jnp.log(f)

    kf, lf = gates(zf_ref[0], lb_ref[0:1, :])
    kb, lbw = gates(zb_ref[0], lb_ref[1:2, :])
    _scan_blocks([(qf_ref[0], kf, vf_ref[0], lf, sf_ref, of_ref, False),
                  (qb_ref[0], kb, vb_ref[0], lbw, sb_ref, ob_ref, True)], dvp=128)


def _bwd_blk(j, nt):
    return jnp.where(j == 0, 0, nt - j)


def _hgrn_scan(proj, lb, cq, cv, cf, cb):
    b, rc, _ = proj.shape
    nt = rc // TM
    w = R_HEADS * DKP

    def fw(col):
        return pl.BlockSpec((1, TM, w), lambda i, j: (i, j, col))

    def bw(col):
        return pl.BlockSpec((1, TM, w), lambda i, j: (i, _bwd_blk(j, nt), col))

    o_sd = jax.ShapeDtypeStruct((b, rc, w), BF16)
    return pl.pallas_call(
        _hgrn_scan_kernel,
        grid=(b, nt),
        in_specs=[pl.BlockSpec((2, w), lambda i, j: (0, 0)),
                  fw(cq), fw(cv), fw(cf), bw(cq), bw(cv), bw(cb)],
        out_specs=[pl.BlockSpec((1, TM, w), lambda i, j: (i, j, 0)),
                   pl.BlockSpec((1, TM, w), lambda i, j: (i, _bwd_blk(j, nt), 0))],
        out_shape=[o_sd, o_sd],
        scratch_shapes=[pltpu.VMEM((R_HEADS, 128, DKP), F32), pltpu.VMEM((R_HEADS, 128, DKP), F32)],
        compiler_params=_cp(("parallel", "arbitrary"), 40 * 1024 * 1024),
        name="hgrn_scan",
    )(lb, proj, proj, proj, proj, proj, proj)


def _gla_scan_kernel(gwf_ref, gwb_ref, gb_ref, qf_ref, kf_ref, vf_ref, rf_ref, qb_ref, kb_ref, vb_ref, rb_ref,
                     of_ref, ob_ref, sf_ref, sb_ref):
    @pl.when(pl.program_id(1) == 0)
    def _():
        sf_ref[...] = jnp.zeros_like(sf_ref)
        sb_ref[...] = jnp.zeros_like(sb_ref)

    def decay(r, gw, gb):
        zz = jnp.dot(jnp.concatenate([r, r], axis=1), _split2(gw, 0), preferred_element_type=F32) + gb
        return (jnp.minimum(zz, 0.0) - jnp.log(1.0 + jnp.exp(-jnp.abs(zz)))) * (1.0 / GLA_TAU)

    qs = GLA_DK ** -0.5
    _scan_blocks([(qf_ref[0].astype(F32) * qs, kf_ref[0], vf_ref[0],
                   decay(rf_ref[0], gwf_ref[...], gb_ref[0:1, :]), sf_ref, of_ref, False),
                  (qb_ref[0].astype(F32) * qs, kb_ref[0], vb_ref[0],
                   decay(rb_ref[0], gwb_ref[...], gb_ref[1:2, :]), sb_ref, ob_ref, True)], dvp=GLA_DVP)


def _gla_scan(proj, gwf, gwb, gb, cq, ck, cv, cr):
    b, rc, _ = proj.shape
    nt = rc // TM
    wk = R_HEADS * DKP
    wv = R_HEADS * GLA_DVP

    def spec(width, col, back):
        if back:
            return pl.BlockSpec((1, TM, width), lambda i, j: (i, _bwd_blk(j, nt), col))
        return pl.BlockSpec((1, TM, width), lambda i, j: (i, j, col))

    o_sd = jax.ShapeDtypeStruct((b, rc, wv), BF16)
    full = lambda shape: pl.BlockSpec(shape, lambda i, j: (0,) * len(shape))
    return pl.pallas_call(
        _gla_scan_kernel,
        grid=(b, nt),
        in_specs=[full((128, wk)), full((128, wk)), full((2, wk)),
                  spec(wk, cq, False), spec(wk, ck, False), spec(wv, cv, False), spec(128, cr, False),
                  spec(wk, cq, True), spec(wk, ck, True), spec(wv, cv, True), spec(128, cr, True)],
        out_specs=[spec(wv, 0, False), spec(wv, 0, True)],
        out_shape=[o_sd, o_sd],
        scratch_shapes=[pltpu.VMEM((R_HEADS, GLA_DVP, DKP), F32), pltpu.VMEM((R_HEADS, GLA_DVP, DKP), F32)],
        compiler_params=_cp(("parallel", "arbitrary"), 48 * 1024 * 1024),
        name="gla_scan",
    )(gwf, gwb, gb, proj, proj, proj, proj, proj, proj, proj, proj)


def _fourier_kernel(z_ref, cs_ref, ct_ref, st_ref, o_ref, *, scale):
    w = F_GROUPS * F_GD
    zz = jnp.dot(z_ref[0, CTX:, :], cs_ref[...], preferred_element_type=F32).astype(BF16)
    y = jnp.dot(ct_ref[...], zz[:, :w], preferred_element_type=F32)
    y = y - jnp.dot(st_ref[...], zz[:, w:], preferred_element_type=F32)
    o_ref[0] = (y * scale).astype(BF16)


def _fourier(proj, zcol, cs, ct, st, t):
    b, rc, _ = proj.shape
    w = F_GROUPS * F_GD
    const = lambda shape: pl.BlockSpec(shape, lambda i: (0, 0), pipeline_mode=pl.Buffered(1))
    return pl.pallas_call(
        functools.partial(_fourier_kernel, scale=1.0 / math.sqrt(t * F_GD)),
        grid=(b,),
        in_specs=[pl.BlockSpec((1, rc, w), lambda i: (i, 0, zcol)), const((w, 2 * w)), const((t, t)), const((t, t))],
        out_specs=pl.BlockSpec((1, t, w), lambda i: (i, 0, 0)),
        out_shape=jax.ShapeDtypeStruct((b, t, w), BF16),
        compiler_params=_cp(("arbitrary",), 48 * 1024 * 1024),
        name="fourier",
    )(proj, cs, ct, st)


def _route(logits, rb_ref):
    lt = logits.T
    l = [lt[e:e + 1, :] for e in range(N_EXP)]
    m = functools.reduce(jnp.maximum, l)
    ex = [jnp.exp(v - m) for v in l]
    den = functools.reduce(lambda a, c: a + c, ex)
    p = [v / den for v in ex]
    sel = [p[e] + rb_ref[e] for e in range(N_EXP)]
    g_score, chosen = [], []
    for g in range(N_GRP):
        s = sel[E_PER * g:E_PER * (g + 1)]
        for i in range(E_PER):
            rank = jnp.zeros_like(s[i])
            for j in range(E_PER):
                if j == i:
                    continue
                ahead = (s[j] >= s[i]) if j < i else (s[j] > s[i])
                rank = rank + jnp.where(ahead, 1.0, 0.0)
            chosen.append(rank < 2.0)
        g_score.append(functools.reduce(lambda a, c: a + c, [jnp.where(chosen[E_PER * g + i], s[i], 0.0)
                                                             for i in range(E_PER)]))
    best = jnp.zeros_like(g_score[0])
    best_s = g_score[0]
    for g in range(1, N_GRP):
        upd = g_score[g] > best_s
        best = jnp.where(upd, float(g), best)
        best_s = jnp.where(upd, g_score[g], best_s)
    picked = [chosen[e] & (best == float(e // E_PER)) for e in range(N_EXP)]
    gate = [jnp.where(picked[e], p[e], 0.0) for e in range(N_EXP)]
    wsum = functools.reduce(lambda a, c: a + c, gate)
    e_lo = functools.reduce(jnp.minimum, [jnp.where(picked[e], float(e), float(N_EXP)) for e in range(N_EXP)])
    e_hi = functools.reduce(jnp.maximum, [jnp.where(picked[e], float(e), -1.0) for e in range(N_EXP)])
    row = lax.broadcasted_iota(jnp.int32, (GW, lt.shape[1]), 0)
    side = jnp.zeros((GW, lt.shape[1]), F32)
    for e in range(N_EXP):
        side = jnp.where(row == e, gate[e] / wsum, side)
    return side.T, e_lo, e_hi


def _outproj_kernel(rb_ref, a_ref, of_ref, ob_ref, gt_ref, *refs, dvp, dv):
    *h_refs, mod_ref, gn_ref, wa_ref, wr_ref, lng_ref, lnb_ref, wrt_ref, h1_ref, u2_ref, grp_ref = refs
    o = of_ref[0].astype(F32) + ob_ref[0].astype(F32)
    parts = []
    for hh in range(R_HEADS):
        sl = slice(hh * dvp, (hh + 1) * dvp)
        oh = o[:, sl]
        ms = jnp.sum(oh * oh, axis=-1, keepdims=True) * (1.0 / dv)
        parts.append(oh * lax.rsqrt(ms + LN_EPS))
    rec = jnp.concatenate(parts, axis=1) * gn_ref[...] * _silu(gt_ref[0].astype(F32))
    y = jnp.dot(a_ref[0], wa_ref[...], preferred_element_type=F32)
    y = y + jnp.dot(rec.astype(BF16), wr_ref[...], preferred_element_type=F32)
    g1 = mod_ref[0, 0, 2:3, :]
    h1 = _ln(ALPHA * _rows_of(h_refs) + g1 * y) * lng_ref[0:1, :] + lnb_ref[0:1, :]
    h1_ref[0] = h1
    u2 = _ln(h1) * (1.0 + mod_ref[0, 0, 4:5, :]) + mod_ref[0, 0, 3:4, :]
    lg = jnp.dot(_split2(u2, 0), _split2(wrt_ref[...], 1), preferred_element_type=F32)
    lg = lg[:TM] + lg[TM:]
    logits = lg[:, :GW] + lg[:, GW:]
    side, e_lo, e_hi = _route(logits, rb_ref)
    grp_ref[0, 0, 0:1, :] = e_lo
    grp_ref[0, 0, 1:2, :] = e_hi
    for c in range(D // 128):
        u2_ref[pl.ds(c, TM, stride=XROWS), :] = u2[:, 128 * c:128 * (c + 1)]
    u2_ref[pl.ds(D // 128, TM, stride=XROWS), :] = side
    for c in range(D // 128 + 1, XROWS):
        u2_ref[pl.ds(c, TM, stride=XROWS), :] = jnp.zeros((TM, 128), F32)


def _outproj(b_router, a, a_col, a_w, o_f, o_b, proj, g_col, h, modsel, gnorm, wa, wr, lng, lnb, wrt, *,
             dv, dvp, row_off):
    b, rc, _ = proj.shape
    nt = rc // TM - row_off
    wrec = R_HEADS * dvp
    full = lambda shape: pl.BlockSpec(shape, lambda i, j: (0,) * len(shape))
    h_specs, h_args = _row_specs(h, row_off)
    return pl.pallas_call(
        functools.partial(_outproj_kernel, dvp=dvp, dv=dv),
        grid=(b, nt),
        in_specs=[pl.BlockSpec(memory_space=pltpu.SMEM),
                  pl.BlockSpec((1, TM, a_w), lambda i, j: (i, j + (row_off if a.shape[1] == rc else 0), a_col)),
                  pl.BlockSpec((1, TM, wrec), lambda i, j: (i, j + row_off, 0)),
                  pl.BlockSpec((1, TM, wrec), lambda i, j: (i, j + row_off, 0)),
                  pl.BlockSpec((1, TM, wrec), lambda i, j: (i, j + row_off, g_col))] + h_specs + [
                  pl.BlockSpec((1, 1, 6, D), lambda i, j: (i, jnp.minimum(j + row_off, 1), 0, 0)),
                  full((1, wrec)), full((a_w, D)), full((wrec, D)), full((1, D)), full((1, D)), full((D, GW))],
        out_specs=[pl.BlockSpec((1, TM, D), lambda i, j: (i, j, 0)),
                   pl.BlockSpec((TM * XROWS, 128), lambda i, j: (i * nt + j, 0)),
                   pl.BlockSpec((1, 1, TOP_K, TM), lambda i, j: (i, j, 0, 0))],
        out_shape=[jax.ShapeDtypeStruct((b, nt * TM, D), F32),
                   jax.ShapeDtypeStruct((b * nt * TM * XROWS, 128), F32),
                   jax.ShapeDtypeStruct((b, nt, TOP_K, TM), F32)],
        compiler_params=_cp(("parallel", "arbitrary"), V7X_VMEM_LIMIT),
        name="outproj",
    )(b_router, a, o_f, o_b, proj, *h_args, modsel, gnorm, wa, wr, lng, lnb, wrt)


def _start_token_gather(idx_ref, base, src_hbm, dst_ref, sem, rows):
    def body(r, carry):
        t = idx_ref[base + r]
        pltpu.make_async_copy(src_hbm.at[pl.ds(pl.multiple_of(t * rows, rows), rows), :],
                              dst_ref.at[pl.ds(pl.multiple_of(r * rows, rows), rows), :], sem).start()
        return carry

    lax.fori_loop(0, dst_ref.shape[0] // rows, body, 0, unroll=8)


def _wait_token_gather(src_hbm, dst_ref, sem):
    pltpu.make_async_copy(src_hbm.at[pl.ds(0, dst_ref.shape[0]), :], dst_ref, sem).wait()


def _gather_ring(idx_ref, first_of, step, last, src_hbm, buf_ref, sem, rows):
    depth = buf_ref.shape[0]
    slot = step % depth

    @pl.when(step == 0)
    def _():
        for k in range(depth - 1):
            _start_token_gather(idx_ref, first_of(jnp.minimum(k, last)), src_hbm, buf_ref.at[k], sem.at[k], rows)

    _wait_token_gather(src_hbm, buf_ref.at[slot], sem.at[slot])
    ahead = (step + depth - 1) % depth
    nxt = first_of(jnp.minimum(step + depth - 1, last))

    def issue(r0, r1):
        for r in range(r0, r1):
            t = idx_ref[nxt + r]
            pltpu.make_async_copy(src_hbm.at[pl.ds(pl.multiple_of(t * rows, rows), rows), :],
                                  buf_ref.at[ahead, pl.ds(r * rows, rows), :], sem.at[ahead]).start()

    def finish():
        @pl.when(step == last)
        def _():
            for k in range(1, depth):
                other = (step + k) % depth
                _wait_token_gather(src_hbm, buf_ref.at[other], sem.at[other])

    return slot, issue, finish


def _moe_kernel(te_ref, first_ref, src_ref, x_hbm, w1_ref, w3_ref, w2_ref, o_ref, xb_ref, sem,
                w1b_ref, w3b_ref, w2b_ref):
    i = pl.program_id(0)
    slot, issue, finish = _gather_ring(src_ref, lambda k: first_ref[k], i, pl.num_programs(0) - 1,
                                       x_hbm, xb_ref, sem, XROWS)
    expert = te_ref[i]

    @pl.when((i == 0) | (expert != te_ref[jnp.maximum(i - 1, 0)]))
    def _():
        w1b_ref[...] = w1_ref[...].astype(BF16)
        w3b_ref[...] = w3_ref[...].astype(BF16)
        w2b_ref[...] = w2_ref[...].astype(BF16)

    xb = xb_ref.at[slot]
    x = jnp.concatenate([xb[pl.ds(c, TM, stride=XROWS), :] for c in range(D // 128)], axis=1).astype(BF16)
    side = xb[pl.ds(D // 128, TM, stride=XROWS), :]
    lane = lax.broadcasted_iota(jnp.int32, side.shape, 1)
    gate = jnp.sum(jnp.where(lane == expert, side, 0.0), axis=1, keepdims=True)
    third = TM // 3
    issue(0, third)
    h1 = jnp.dot(x, w1b_ref[...], preferred_element_type=F32)
    issue(third, 2 * third)
    h3 = jnp.dot(x, w3b_ref[...], preferred_element_type=F32)
    hh = (_silu(h1) * h3 * gate).astype(BF16)
    issue(2 * third, TM)
    y = jnp.dot(hh, w2b_ref[...], preferred_element_type=F32)
    for c in range(D // 128):
        o_ref[pl.ds(c, TM, stride=YROWS), :] = y[:, 128 * c:128 * (c + 1)]
    finish()


def _moe(tile_exp, tile_first, order, u2x, w1, w3, w2, layer):
    mp = tile_exp.shape[0] * TM
    wspec = lambda k, n: pl.BlockSpec((None, None, k, n), lambda i, te, tf, od: (layer, te[i], 0, 0),
                                      pipeline_mode=pl.Buffered(1))
    return pl.pallas_call(
        _moe_kernel,
        grid_spec=pltpu.PrefetchScalarGridSpec(
            num_scalar_prefetch=3, grid=(mp // TM,),
            in_specs=[pl.BlockSpec(memory_space=pl.ANY), wspec(D, D_FF), wspec(D, D_FF), wspec(D_FF, D)],
            out_specs=pl.BlockSpec((TM * YROWS, 128), lambda i, te, tf, od: (i, 0)),
            scratch_shapes=[pltpu.VMEM((2, TM * XROWS, 128), F32), pltpu.SemaphoreType.DMA((2,)),
                            pltpu.VMEM((D, D_FF), BF16), pltpu.VMEM((D, D_FF), BF16), pltpu.VMEM((D_FF, D), BF16)]),
        out_shape=jax.ShapeDtypeStruct((mp * YROWS, 128), F32),
        compiler_params=_cp(("arbitrary",), 40 * 1024 * 1024),
        name="moe",
    )(tile_exp, tile_first, order, u2x, w1, w3, w2)


def _combine_kernel(pos_ref, y_hbm, h_ref, mod_ref, lng_ref, lnb_ref, o_ref, yb_ref, sem, *, nt):
    step = pl.program_id(0) * nt + pl.program_id(1)
    slot, issue, finish = _gather_ring(pos_ref, lambda k: k * (TOP_K * TM), step, pl.num_programs(0) * nt - 1,
                                       y_hbm, yb_ref, sem, YROWS)
    yb = yb_ref.at[slot]
    g2 = mod_ref[0, 0, 5:6, :]
    nstrip = 4
    strip = TM // nstrip
    per = TOP_K * TM // nstrip
    for k in range(nstrip):
        r0 = k * strip
        issue(k * per, (k + 1) * per)
        y = functools.reduce(lambda a, c: a + c, [
            jnp.concatenate([yb[pl.ds((s * TM + r0) * YROWS + c, strip, stride=YROWS), :] for c in range(D // 128)],
                            axis=1) for s in range(TOP_K)])
        o_ref[0, r0:r0 + strip, :] = (_ln(ALPHA * h_ref[0, r0:r0 + strip, :] + g2 * y) * lng_ref[0:1, :]
                                      + lnb_ref[0:1, :])
    finish()


def _combine(pos, y_sorted, h1, modsel, lng, lnb, row_off):
    b, r, _ = h1.shape
    nt = r // TM
    full = lambda shape: pl.BlockSpec(shape, lambda i, j, p: (0,) * len(shape))
    return pl.pallas_call(
        functools.partial(_combine_kernel, nt=nt),
        grid_spec=pltpu.PrefetchScalarGridSpec(
            num_scalar_prefetch=1, grid=(b, nt),
            in_specs=[pl.BlockSpec(memory_space=pl.ANY),
                      pl.BlockSpec((1, TM, D), lambda i, j, p: (i, j, 0)),
                      pl.BlockSpec((1, 1, 6, D), lambda i, j, p: (i, jnp.minimum(j + row_off, 1), 0, 0)),
                      full((1, D)), full((1, D))],
            out_specs=pl.BlockSpec((1, TM, D), lambda i, j, p: (i, j, 0)),
            scratch_shapes=[pltpu.VMEM((3, TOP_K * TM * YROWS, 128), F32), pltpu.SemaphoreType.DMA((3,))]),
        out_shape=jax.ShapeDtypeStruct((b, r, D), F32),
        compiler_params=_cp(("arbitrary", "arbitrary")),
        name="moe_combine",
    )(pos, y_sorted, h1, modsel, lng, lnb)


def _sort_by_expert(experts):
    n = experts.shape[0]
    keys = experts.reshape(-1)
    onehot = (keys[:, None] == jnp.arange(N_EXP, dtype=jnp.int32)[None, :]).astype(jnp.int32)
    csum = jnp.cumsum(onehot, axis=0)
    counts = csum[-1]
    padded = ((counts + TM - 1) // TM) * TM
    ends = jnp.cumsum(padded)
    starts = ends - padded
    rank = jnp.sum(csum * onehot, axis=1) - 1
    pos = (starts[keys] + rank).astype(jnp.int32).reshape(n, TOP_K)
    _, order = lax.sort((keys, jnp.arange(n * TOP_K, dtype=jnp.int32)), num_keys=1, is_stable=True)
    order = jnp.concatenate([order // TOP_K, jnp.zeros((TM,), jnp.int32)])
    ntile = n * TOP_K // TM + N_EXP
    tile_start = jnp.arange(ntile, dtype=jnp.int32) * TM
    tile_exp = jnp.minimum(jnp.sum((tile_start[:, None] >= ends[None, :]).astype(jnp.int32), axis=1), N_EXP - 1)
    first = (jnp.cumsum(counts) - counts)[tile_exp] + tile_start - starts[tile_exp]
    return pos, order, tile_exp.astype(jnp.int32), jnp.minimum(first, n * TOP_K).astype(jnp.int32)


def _moe_layer(u2x, experts, h1, modsel, weights, lng, lnb, row_off):
    b, nt = experts.shape[:2]
    experts = experts.astype(jnp.int32).transpose(0, 1, 3, 2).reshape(b * nt * TM, TOP_K)
    pos, order, tile_exp, tile_first = _sort_by_expert(experts)
    y_sorted = _moe(tile_exp, tile_first, order, u2x, *weights)
    pos = pos.reshape(b * nt, TM, TOP_K).transpose(0, 2, 1).reshape(-1)
    return _combine(pos, y_sorted, h1, modsel, lng, lnb, row_off)


def _rope_tables(rc, t):
    half = A_HD // 4
    inv = ROPE_BASE ** (-jnp.arange(half, dtype=F32) / half)
    tpos = jnp.arange(t)
    rows = (tpos // GRID_W).astype(F32)
    cols = (tpos % GRID_W).astype(F32)

    def cs(pos):
        ang = pos[:, None] * inv[None, :]
        c, s = jnp.cos(ang), jnp.sin(ang)
        return jnp.concatenate([c, c], -1), jnp.concatenate([-s, s], -1)

    cr, sr = cs(rows)
    cc_, sc_ = cs(cols)
    cos_h = jnp.concatenate([cr, cc_], -1)
    sin_h = jnp.concatenate([sr, sc_], -1)
    one = jnp.ones((t, A_HD), F32)
    zero = jnp.zeros((t, A_HD), F32)
    lat = jnp.stack([jnp.concatenate([cos_h, cos_h], -1), jnp.concatenate([sin_h, sin_h], -1),
                     jnp.concatenate([cos_h, one], -1), jnp.concatenate([sin_h, zero], -1)])
    ctx = jnp.stack([jnp.ones((CTX, 128), F32), jnp.zeros((CTX, 128), F32)] * 2)
    return jnp.concatenate([ctx, lat], axis=1)


def _dft_tables(t):
    def cos_sin(num, den):
        ang = (2.0 * math.pi / den) * (num % den).astype(F32)
        return jnp.cos(ang), jnp.sin(ang)

    k = jnp.arange(t, dtype=jnp.int32)[:, None]
    ca, sa = cos_sin(k * jnp.arange(t // GRID_W, dtype=jnp.int32)[None, :], t // GRID_W)
    cb, sb = cos_sin(k * jnp.arange(GRID_W, dtype=jnp.int32)[None, :], t)
    ct = (ca[:, :, None] * cb[:, None, :] - sa[:, :, None] * sb[:, None, :]).reshape(t, t)
    st = (sa[:, :, None] * cb[:, None, :] + ca[:, :, None] * sb[:, None, :]).reshape(t, t)
    c = jnp.arange(F_GD, dtype=jnp.int32)
    cc, sc = cos_sin(c[:, None] * c[None, :], F_GD)
    eye = jnp.eye(F_GROUPS, dtype=F32)
    cs = jnp.concatenate([jnp.kron(eye, cc), jnp.kron(eye, sc)], axis=1)
    return cs.astype(BF16), ct.astype(BF16), st.astype(BF16)


def _pad_heads(w, width, padded):
    lead = w.shape[:-1]
    w = w.reshape(lead + (R_HEADS, width))
    w = jnp.pad(w, [(0, 0)] * len(lead) + [(0, 0), (0, padded - width)])
    return w.reshape(lead + (R_HEADS * padded,))


EVEN_SEGS = tuple([(128 * g, 128 * (g + 1), 0) for g in range(4)] + [(c, c + 512, None) for c in range(512, 3072, 512)]
                  + [(3072 + 128 * g, 3072 + 128 * (g + 1), 1) for g in range(2)])
ODD_W = 512 + 512 + 1024 + 1024 + 256 + 128
ODD_SEGS = tuple([(c, c + 512, None) for c in range(0, 3072, 512)] + [(3072, ODD_W, None)])


def _layer_even(h, modsel, tbl, w_in, sink, lb_logits, g_norm, w_out, lng, lnb, w_router, b_router, experts):
    aq, ak, av, ff, fb, hq, hi, hg = jnp.split(w_in, np.cumsum([512, 128, 128, 512, 512, 512, 512]).tolist(), axis=1)
    kv = [jnp.concatenate([ak[:, A_HD * j:A_HD * (j + 1)], av[:, A_HD * j:A_HD * (j + 1)]], axis=1)
          for j in range(A_KV)]
    w = jnp.concatenate([aq, ff, fb, hq, hi, hg] + kv, axis=1).astype(BF16)
    proj = _inproj(h, modsel, w, tbl, EVEN_SEGS)
    o_att = _attention(proj, sink, 3072 // (2 * A_HD * A_KV))
    lb = jnp.cumsum(jax.nn.softmax(lb_logits.astype(F32), axis=1), axis=1)[:, 0]
    o_f, o_b = _hgrn_scan(proj, lb, 3, 4, 1, 2)
    wo = w_out.astype(BF16)
    h1, u2x, grp = _outproj(b_router, o_att, 0, 512, o_f, o_b, proj, 5, h, modsel, g_norm.reshape(1, -1),
                            wo[:512], wo[512:], lng[0:1], lnb[0:1], w_router, dv=128, dvp=128, row_off=0)
    return _moe_layer(u2x, grp, h1, modsel, experts, lng[1:2], lnb[1:2], 0)


def _layer_odd(h, modsel, tbl, dft, w_in, gate_w, gate_b, g_norm, w_out, lng, lnb, w_router, b_router, experts, t):
    q, k, v, rf, rb, g, z = jnp.split(w_in, np.cumsum([384, 384, 768, 16, 16, 768]).tolist(), axis=1)
    r = jnp.pad(jnp.concatenate([rf, rb], axis=1), ((0, 0), (0, 96)))
    w = jnp.concatenate([_pad_heads(q, GLA_DK, DKP), _pad_heads(k, GLA_DK, DKP), _pad_heads(v, GLA_DV, GLA_DVP),
                         _pad_heads(g, GLA_DV, GLA_DVP), z, r], axis=1).astype(BF16)
    proj = _inproj(h, modsel, w, tbl, ODD_SEGS)
    gw = _pad_heads(gate_w, GLA_DK, DKP)
    gwf = jnp.pad(gw[0], ((0, 112), (0, 0)))
    gwb = jnp.pad(gw[1], ((16, 96), (0, 0)))
    gb = _pad_heads(gate_b, GLA_DK, DKP)
    o_f, o_b = _gla_scan(proj, gwf, gwb, gb, 0, 1, 1, 3328 // 128)
    cs, ct, st = dft
    four = _fourier(proj, 3072 // (F_GROUPS * F_GD), cs, ct, st, t)
    wo = w_out
    wr = jnp.pad(wo[:768].reshape(R_HEADS, GLA_DV, D), ((0, 0), (0, GLA_DVP - GLA_DV), (0, 0))).reshape(-1, D)
    h1, u2x, grp = _outproj(b_router, four, 0, 256, o_f, o_b, proj, 2, h, modsel,
                            _pad_heads(g_norm, GLA_DV, GLA_DVP).reshape(1, -1),
                            wo[768:].astype(BF16), wr.astype(BF16), lng[0:1], lnb[0:1], w_router,
                            dv=GLA_DV, dvp=GLA_DVP, row_off=CTX // TM)
    return _moe_layer(u2x, grp, h1, modsel, experts, lng[1:2], lnb[1:2], CTX // TM)


def kernel(x, c, ctx, c_ctx, w_ada, b_ada, ln_g, ln_b, w_in_even, attn_sink, hgrn_lb_logits, hgrn_norm, w_out_even, w_in_odd, gla_gate_w, gla_gate_b, gla_norm, w_out_odd, w_router, b_router, w_expert_gate, w_expert_up, w_expert_down):
    b, t, _ = x.shape
    rc = CTX + t
    assert ctx.shape[1] == CTX
    cc = jnp.zeros((16, D), F32).at[:b].set(c).at[b].set(c_ctx)
    mods = _ada_mods(cc, w_ada, b_ada).reshape(2, 16, 6, D)

    def modsel(l):
        return jnp.stack([jnp.broadcast_to(mods[l, b], (b, 6, D)), mods[l, :b]], axis=1)

    tbl = _rope_tables(rc, t)
    wrt = jnp.pad(w_router, ((0, 0), (0, GW - N_EXP)))
    experts = (w_expert_gate, w_expert_up, w_expert_down)
    h = _layer_even((ctx, x), modsel(0), tbl, w_in_even[0], attn_sink[0], hgrn_lb_logits, hgrn_norm[0], w_out_even[0],
                    ln_g[0], ln_b[0], wrt, b_router, experts + (0,))
    return _layer_odd(h, modsel(1), tbl, _dft_tables(t), w_in_odd[0], gla_gate_w[0], gla_gate_b[0], gla_norm[0],
                      w_out_odd[0], ln_g[1], ln_b[1], wrt, b_router, experts + (1,), t)
```

```python
import functools
import math

import numpy as np
import jax
import jax.numpy as jnp
from jax import lax
from jax.experimental import pallas as pl
from jax.experimental.pallas import tpu as pltpu

F32 = jnp.float32
BF16 = jnp.bfloat16
HIGHEST = lax.Precision.HIGHEST

D = 1024
CTX = 256
GRID_W = 64
LN_EPS = 1e-5
NEG = -1e30
ALPHA = 4.0 ** 0.25
ROPE_BASE = 10000.0

A_HEADS, A_KV, A_HD, A_BLK = 8, 2, 64, 128
R_HEADS = 4
DKP = 128
GLA_DK, GLA_DV, GLA_DVP = 96, 192, 256
GLA_TAU = 16.0
CHUNK = 64
N_EXP, N_GRP, E_PER = 16, 4, 4
D_FF = 512
F_GROUPS, F_GD = 4, 64

TM = 256
IN_SUB = 3
GW = 128
XROWS = 8
YROWS = 8
V7X_VMEM_LIMIT = 56 * 1024 * 1024


def _cp(sem, vmem=None):
    return pltpu.CompilerParams(dimension_semantics=sem, vmem_limit_bytes=vmem)


def _ln(x):
    mu = jnp.mean(x, axis=-1, keepdims=True)
    xc = x - mu
    var = jnp.mean(xc * xc, axis=-1, keepdims=True)
    return xc * lax.rsqrt(var + LN_EPS)


def _silu(x):
    return x * jax.nn.sigmoid(x)


def _split2(w, axis):
    hi = w.astype(BF16)
    return jnp.concatenate([hi, (w - hi.astype(F32)).astype(BF16)], axis=axis)


_NT = (((1,), (1,)), ((), ()))
_TN = (((0,), (0,)), ((), ()))


def _ada_kernel(c_ref, w_ref, b_ref, o_ref):
    s = _silu(c_ref[...])
    o_ref[0] = jnp.dot(s, w_ref[0], precision=HIGHEST, preferred_element_type=F32) + b_ref[0]


def _ada_mods(cc, w_ada, b_ada):
    depth, _, n = w_ada.shape
    tn = 1536
    return pl.pallas_call(
        _ada_kernel,
        grid=(depth, n // tn),
        in_specs=[pl.BlockSpec((16, D), lambda l, i: (0, 0)),
                  pl.BlockSpec((1, D, tn), lambda l, i: (l, 0, i)),
                  pl.BlockSpec((1, 1, tn), lambda l, i: (l, 0, i))],
        out_specs=pl.BlockSpec((1, 16, tn), lambda l, i: (l, 0, i)),
        out_shape=jax.ShapeDtypeStruct((depth, 16, n), F32),
        compiler_params=_cp(("arbitrary", "arbitrary"), 40 * 1024 * 1024),
        name="ada_mod",
    )(cc, w_ada, b_ada.reshape(depth, 1, n))


def _rows_of(h_refs):
    if len(h_refs) == 1:
        return h_refs[0][0]
    return jnp.where(pl.program_id(1) == 0, h_refs[0][0], h_refs[1][0])


def _row_specs(h, row_off=0):
    if isinstance(h, tuple):
        assert row_off == 0 and h[0].shape[1] == TM
        return [pl.BlockSpec((1, TM, D), lambda i, j: (i, 0, 0)),
                pl.BlockSpec((1, TM, D), lambda i, j: (i, jnp.maximum(j - 1, 0), 0))], list(h)
    return [pl.BlockSpec((1, TM, D), lambda i, j: (i, j + row_off, 0))], [h]


def _inproj_kernel(*refs, segs):
    *h_refs, mod0_ref, mod1_ref, w_ref, tbl_ref, o_ref = refs
    if len(h_refs) == 1:
        x = h_refs[0][0]
    else:
        ctx_ref, *x_refs = h_refs
        x = jnp.concatenate([jnp.where(pl.program_id(1) == 0, ctx_ref[0], x_refs[0][0])]
                            + [r[0] for r in x_refs[1:]], axis=0)
    xn = _ln(x)
    u = jnp.concatenate([xn[:TM] * (1.0 + mod0_ref[0, 0, 1:2, :]) + mod0_ref[0, 0, 0:1, :],
                         xn[TM:] * (1.0 + mod1_ref[0, 0, 1:2, :]) + mod1_ref[0, 0, 0:1, :]], axis=0).astype(BF16)
    lane = lax.broadcasted_iota(jnp.int32, (x.shape[0], 128), 1)
    low = (lane % 32) < 16
    for c0, c1, rope in segs:
        acc = jnp.dot(u, w_ref[:, c0:c1], preferred_element_type=F32)
        if rope is not None:
            partner = jnp.where(low, pltpu.roll(acc, 112, axis=1), pltpu.roll(acc, 16, axis=1))
            acc = acc * tbl_ref[2 * rope] + partner * tbl_ref[2 * rope + 1]
        o_ref[0, :, c0:c1] = acc.astype(BF16)


def _inproj(h, modsel, w, tbl, segs):
    b, rc = modsel.shape[0], tbl.shape[1]
    n = w.shape[1]
    tmi = IN_SUB * TM
    assert rc % tmi == 0
    if isinstance(h, tuple):
        assert h[0].shape[1] == TM
        h_specs = [pl.BlockSpec((1, TM, D), lambda i, j: (i, 0, 0))] + [
            pl.BlockSpec((1, TM, D), lambda i, j, k=k: (i, jnp.maximum(IN_SUB * j + k - 1, 0), 0))
            for k in range(IN_SUB)]
        h_args = [h[0]] + [h[1]] * IN_SUB
    else:
        h_specs, h_args = [pl.BlockSpec((1, tmi, D), lambda i, j: (i, j, 0))], [h]
    return pl.pallas_call(
        functools.partial(_inproj_kernel, segs=segs),
        grid=(b, rc // tmi),
        in_specs=h_specs + [pl.BlockSpec((1, 1, 6, D), lambda i, j: (i, jnp.minimum(j, 1), 0, 0)),
                            pl.BlockSpec((1, 1, 6, D), lambda i, j: (i, 1, 0, 0)),
                            pl.BlockSpec((D, n), lambda i, j: (0, 0)),
                            pl.BlockSpec((4, tmi, 128), lambda i, j: (0, j, 0))],
        out_specs=pl.BlockSpec((1, tmi, n), lambda i, j: (i, j, 0)),
        out_shape=jax.ShapeDtypeStruct((b, rc, n), BF16),
        compiler_params=_cp(("parallel", "arbitrary"), V7X_VMEM_LIMIT),
        name="inproj",
    )(*h_args, modsel, modsel, w, tbl)


def _attn_kernel(sink_ref, q_ref, kvc_ref, kvp_ref, kvm_ref, kvn_ref, bias_ref, o_ref):
    blk = pl.program_id(1)
    g_per = A_HEADS // A_KV
    nq = g_per * A_BLK
    q = q_ref[0] * (A_HD ** -0.5)
    rgrp = lax.broadcasted_iota(jnp.int32, (nq, 1), 0) // A_BLK
    q4, sink_col = [], []
    for h in range(A_KV):
        q4.append(jnp.concatenate([q[:, A_HD * (g_per * h + g):A_HD * (g_per * h + g + 1)] for g in range(g_per)],
                                  axis=0))
        col = jnp.zeros((nq, 1), F32)
        for g in range(g_per):
            col = jnp.where(rgrp == g, sink_ref[h * g_per + g], col)
        sink_col.append(col)

    def attend(kv, bias):
        s = [lax.dot_general(q4[h], kv[:, 2 * A_HD * h:2 * A_HD * h + A_HD], _NT, preferred_element_type=F32)
             for h in range(A_KV)]
        if bias is not None:
            bias4 = jnp.concatenate([bias] * g_per, axis=0)
            s = [x + bias4 for x in s]
        m = [jnp.maximum(jnp.max(s[h], axis=-1, keepdims=True), sink_col[h]) for h in range(A_KV)]
        p = [jnp.exp(s[h] - m[h]) for h in range(A_KV)]
        den = [jnp.sum(p[h], axis=-1, keepdims=True) + jnp.exp(sink_col[h] - m[h]) for h in range(A_KV)]
        o = [jnp.dot(p[h].astype(BF16), kv[:, 2 * A_HD * h + A_HD:2 * A_HD * (h + 1)], preferred_element_type=F32)
             / den[h] for h in range(A_KV)]
        o_ref[0] = jnp.concatenate([o[h][A_BLK * g:A_BLK * (g + 1), :] for h in range(A_KV) for g in range(g_per)],
                                   axis=1).astype(BF16)

    @pl.when(blk < CTX // A_BLK)
    def _():
        attend(kvc_ref[0], None)

    @pl.when(blk >= CTX // A_BLK)
    def _():
        attend(jnp.concatenate([kvc_ref[0], kvp_ref[0], kvm_ref[0], kvn_ref[0]], axis=0), bias_ref[0])


def _attn_bias():
    nk = CTX + 3 * A_BLK
    qi = jnp.arange(A_BLK, dtype=jnp.int32)[:, None]
    col = jnp.arange(nk, dtype=jnp.int32)[None, :]
    rel = col - (CTX + A_BLK)
    dist = qi - rel

    def one(lo, hi):
        in_win = (dist <= A_BLK) & (dist >= -A_BLK) & (rel >= lo) & (rel < hi)
        return jnp.where((col < CTX) | in_win, 0.0, NEG).astype(F32)

    return jnp.stack([one(0, 2 * A_BLK), one(-A_BLK, 2 * A_BLK), one(-A_BLK, A_BLK)])


def _attention(proj, sink, kv_blk):
    b, rc, _ = proj.shape
    nb = rc // A_BLK
    c_b = CTX // A_BLK
    assert nb - c_b >= 2
    kw = 2 * A_HD * A_KV

    def band(off):
        return lambda i, j: (i, jnp.clip(j + off, c_b, nb - 1), kv_blk)

    def which_bias(i, j):
        return (jnp.where(j <= c_b, 0, jnp.where(j == nb - 1, 2, 1)), 0, 0)

    return pl.pallas_call(
        _attn_kernel,
        grid=(b, nb),
        in_specs=[pl.BlockSpec(memory_space=pltpu.SMEM),
                  pl.BlockSpec((1, A_BLK, A_HEADS * A_HD), lambda i, j: (i, j, 0)),
                  pl.BlockSpec((1, CTX, kw), lambda i, j: (i, 0, kv_blk)),
                  pl.BlockSpec((1, A_BLK, kw), band(-1)),
                  pl.BlockSpec((1, A_BLK, kw), band(0)),
                  pl.BlockSpec((1, A_BLK, kw), band(1)),
                  pl.BlockSpec((1, A_BLK, CTX + 3 * A_BLK), which_bias)],
        out_specs=pl.BlockSpec((1, A_BLK, A_HEADS * A_HD), lambda i, j: (i, j, 0)),
        out_shape=jax.ShapeDtypeStruct((b, rc, A_HEADS * A_HD), BF16),
        compiler_params=_cp(("parallel", "arbitrary")),
        name="win_attn",
    )(sink, proj, proj, proj, proj, proj, _attn_bias())


def _scan_blocks(dirs, *, dvp):
    r_i = lax.broadcasted_iota(jnp.int32, (CHUNK, CHUNK), 0)
    c_i = lax.broadcasted_iota(jnp.int32, (CHUNK, CHUNK), 1)
    units = []
    for q, k, v, logf, s_ref, o_ref, reverse in dirs:
        nchunk = q.shape[0] // CHUNK
        tri = (r_i <= c_i) if reverse else (r_i >= c_i)
        tri3 = jnp.concatenate([jnp.where(tri, 1.0, 0.0).astype(BF16)] * 3, axis=1)
        for c in (range(nchunk - 1, -1, -1) if reverse else range(nchunk)):
            sl = slice(c * CHUNK, (c + 1) * CHUNK)
            units.append(dict(q=q[sl], k=k[sl], v=v[sl].astype(BF16), lf=logf[sl], sl=sl, tri=tri, tri3=tri3,
                              i_tot=0 if reverse else CHUNK - 1,
                              i_ref=CHUNK // 2 - 1 if reverse else CHUNK // 2, s_ref=s_ref, o_ref=o_ref))
    heads = [(slice(h * DKP, (h + 1) * DKP), slice(h * dvp, (h + 1) * dvp)) for h in range(R_HEADS)]
    for u in units:
        lf = u["lf"]
        hi = lf.astype(BF16)
        r1 = lf - hi.astype(F32)
        mid = r1.astype(BF16)
        lo = (r1 - mid.astype(F32)).astype(BF16)
        u["bsum"] = jnp.dot(u["tri3"], jnp.concatenate([hi, mid, lo], axis=0), preferred_element_type=F32)
    for u in units:
        bsum = u["bsum"]
        tot = bsum[u["i_tot"]:u["i_tot"] + 1, :]
        ref = bsum[u["i_ref"]:u["i_ref"] + 1, :]
        qg = u["q"] * jnp.exp(bsum - ref)
        kg = u["k"] * jnp.exp(ref - bsum)
        u["qb"] = (qg * jnp.exp(ref)).astype(BF16)
        u["kd"] = (kg * jnp.exp(tot - ref)).astype(BF16)
        u["qg"] = qg.astype(BF16)
        u["kg"] = kg.astype(BF16)
        u["dec"] = jnp.exp(tot)
    for u in units:
        u["att"] = [lax.dot_general(u["qg"][:, ks], u["kg"][:, ks], _NT, preferred_element_type=F32)
                    for ks, _ in heads]
        u["ds"] = [lax.dot_general(u["v"][:, vs], u["kd"][:, ks], _TN, preferred_element_type=F32)
                   for ks, vs in heads]
    for u in units:
        u["oi"] = [jnp.dot(jnp.where(u["tri"], a, 0.0).astype(BF16), u["v"][:, vs], preferred_element_type=F32)
                   for a, (_, vs) in zip(u["att"], heads)]
    state = {}
    for u in units:
        key = id(u["s_ref"])
        if key not in state:
            state[key] = [u["s_ref"][h] for h in range(R_HEADS)]
        st = state[key]
        outs = []
        for h, (ks, _) in enumerate(heads):
            outs.append(u["oi"][h] + lax.dot_general(u["qb"][:, ks], st[h].astype(BF16), _NT,
                                                     preferred_element_type=F32))
            st[h] = st[h] * u["dec"][:, ks] + u["ds"][h]
        u["o_ref"][0, u["sl"], :] = jnp.concatenate(outs, axis=1).astype(BF16)
    for _, _, _, _, s_ref, _, _ in dirs:
        for h in range(R_HEADS):
            s_ref[h] = state[id(s_ref)][h]


def _hgrn_scan_kernel(lb_ref, qf_ref, vf_ref, zf_ref, qb_ref, vb_ref, zb_ref, of_ref, ob_ref, sf_ref, sb_ref):
    @pl.when(pl.program_id(1) == 0)
    def _():
        sf_ref[...] = jnp.zeros_like(sf_ref)
        sb_ref[...] = jnp.zeros_like(sb_ref)

    def gates(z, lb):
        f = lb + (1.0 - lb) * jax.nn.sigmoid(z.astype(F32))
        return 1.0 - f, jnp.log(f)

    kf, lf = gates(zf_ref[0], lb_ref[0:1, :])
    kb, lbw = gates(zb_ref[0], lb_ref[1:2, :])
    _scan_blocks([(qf_ref[0], kf, vf_ref[0], lf, sf_ref, of_ref, False),
                  (qb_ref[0], kb, vb_ref[0], lbw, sb_ref, ob_ref, True)], dvp=128)


def _bwd_blk(j, nt):
    return jnp.where(j == 0, 0, nt - j)


def _hgrn_scan(proj, lb, cq, cv, cf, cb):
    b, rc, _ = proj.shape
    nt = rc // TM
    w = R_HEADS * DKP

    def fw(col):
        return pl.BlockSpec((1, TM, w), lambda i, j: (i, j, col))

    def bw(col):
        return pl.BlockSpec((1, TM, w), lambda i, j: (i, _bwd_blk(j, nt), col))

    o_sd = jax.ShapeDtypeStruct((b, rc, w), BF16)
    return pl.pallas_call(
        _hgrn_scan_kernel,
        grid=(b, nt),
        in_specs=[pl.BlockSpec((2, w), lambda i, j: (0, 0)),
                  fw(cq), fw(cv), fw(cf), bw(cq), bw(cv), bw(cb)],
        out_specs=[pl.BlockSpec((1, TM, w), lambda i, j: (i, j, 0)),
                   pl.BlockSpec((1, TM, w), lambda i, j: (i, _bwd_blk(j, nt), 0))],
        out_shape=[o_sd, o_sd],
        scratch_shapes=[pltpu.VMEM((R_HEADS, 128, DKP), F32), pltpu.VMEM((R_HEADS, 128, DKP), F32)],
        compiler_params=_cp(("parallel", "arbitrary"), 40 * 1024 * 1024),
        name="hgrn_scan",
    )(lb, proj, proj, proj, proj, proj, proj)


def _gla_scan_kernel(gwf_ref, gwb_ref, gb_ref, qf_ref, kf_ref, vf_ref, rf_ref, qb_ref, kb_ref, vb_ref, rb_ref,
                     of_ref, ob_ref, sf_ref, sb_ref):
    @pl.when(pl.program_id(1) == 0)
    def _():
        sf_ref[...] = jnp.zeros_like(sf_ref)
        sb_ref[...] = jnp.zeros_like(sb_ref)

    def decay(r, gw, gb):
        zz = jnp.dot(jnp.concatenate([r, r], axis=1), _split2(gw, 0), preferred_element_type=F32) + gb
        return (jnp.minimum(zz, 0.0) - jnp.log(1.0 + jnp.exp(-jnp.abs(zz)))) * (1.0 / GLA_TAU)

    qs = GLA_DK ** -0.5
    _scan_blocks([(qf_ref[0].astype(F32) * qs, kf_ref[0], vf_ref[0],
                   decay(rf_ref[0], gwf_ref[...], gb_ref[0:1, :]), sf_ref, of_ref, False),
                  (qb_ref[0].astype(F32) * qs, kb_ref[0], vb_ref[0],
                   decay(rb_ref[0], gwb_ref[...], gb_ref[1:2, :]), sb_ref, ob_ref, True)], dvp=GLA_DVP)


def _gla_scan(proj, gwf, gwb, gb, cq, ck, cv, cr):
    b, rc, _ = proj.shape
    nt = rc // TM
    wk = R_HEADS * DKP
    wv = R_HEADS * GLA_DVP

    def spec(width, col, back):
        if back:
            return pl.BlockSpec((1, TM, width), lambda i, j: (i, _bwd_blk(j, nt), col))
        return pl.BlockSpec((1, TM, width), lambda i, j: (i, j, col))

    o_sd = jax.ShapeDtypeStruct((b, rc, wv), BF16)
    full = lambda shape: pl.BlockSpec(shape, lambda i, j: (0,) * len(shape))
    return pl.pallas_call(
        _gla_scan_kernel,
        grid=(b, nt),
        in_specs=[full((128, wk)), full((128, wk)), full((2, wk)),
                  spec(wk, cq, False), spec(wk, ck, False), spec(wv, cv, False), spec(128, cr, False),
                  spec(wk, cq, True), spec(wk, ck, True), spec(wv, cv, True), spec(128, cr, True)],
        out_specs=[spec(wv, 0, False), spec(wv, 0, True)],
        out_shape=[o_sd, o_sd],
        scratch_shapes=[pltpu.VMEM((R_HEADS, GLA_DVP, DKP), F32), pltpu.VMEM((R_HEADS, GLA_DVP, DKP), F32)],
        compiler_params=_cp(("parallel", "arbitrary"), 48 * 1024 * 1024),
        name="gla_scan",
    )(gwf, gwb, gb, proj, proj, proj, proj, proj, proj, proj, proj)


def _fourier_kernel(z_ref, cs_ref, ct_ref, st_ref, o_ref, *, scale):
    w = F_GROUPS * F_GD
    zz = jnp.dot(z_ref[0, CTX:, :], cs_ref[...], preferred_element_type=F32).astype(BF16)
    y = jnp.dot(ct_ref[...], zz[:, :w], preferred_element_type=F32)
    y = y - jnp.dot(st_ref[...], zz[:, w:], preferred_element_type=F32)
    o_ref[0] = (y * scale).astype(BF16)


def _fourier(proj, zcol, cs, ct, st, t):
    b, rc, _ = proj.shape
    w = F_GROUPS * F_GD
    const = lambda shape: pl.BlockSpec(shape, lambda i: (0, 0), pipeline_mode=pl.Buffered(1))
    return pl.pallas_call(
        functools.partial(_fourier_kernel, scale=1.0 / math.sqrt(t * F_GD)),
        grid=(b,),
        in_specs=[pl.BlockSpec((1, rc, w), lambda i: (i, 0, zcol)), const((w, 2 * w)), const((t, t)), const((t, t))],
        out_specs=pl.BlockSpec((1, t, w), lambda i: (i, 0, 0)),
        out_shape=jax.ShapeDtypeStruct((b, t, w), BF16),
        compiler_params=_cp(("arbitrary",), 48 * 1024 * 1024),
        name="fourier",
    )(proj, cs, ct, st)


def _route(logits, rb_ref):
    lt = logits.T
    l = [lt[e:e + 1, :] for e in range(N_EXP)]
    m = functools.reduce(jnp.maximum, l)
    ex = [jnp.exp(v - m) for v in l]
    den = functools.reduce(lambda a, c: a + c, ex)
    p = [v / den for v in ex]
    sel = [p[e] + rb_ref[e] for e in range(N_EXP)]
    g_score, g_gate = [], []
    for g in range(N_GRP):
        s = sel[E_PER * g:E_PER * (g + 1)]
        chosen = []
        for i in range(E_PER):
            rank = jnp.zeros_like(s[i])
            for j in range(E_PER):
                if j == i:
                    continue
                ahead = (s[j] >= s[i]) if j < i else (s[j] > s[i])
                rank = rank + jnp.where(ahead, 1.0, 0.0)
            chosen.append(rank < 2.0)
        g_score.append(functools.reduce(lambda a, c: a + c,
                                        [jnp.where(chosen[i], s[i], 0.0) for i in range(E_PER)]))
        g_gate.append([jnp.where(chosen[i], p[E_PER * g + i], 0.0) for i in range(E_PER)])
    best = jnp.zeros_like(g_score[0])
    best_s = g_score[0]
    for g in range(1, N_GRP):
        upd = g_score[g] > best_s
        best = jnp.where(upd, float(g), best)
        best_s = jnp.where(upd, g_score[g], best_s)
    gate = []
    for i in range(E_PER):
        gi = g_gate[0][i]
        for g in range(1, N_GRP):
            gi = jnp.where(best == float(g), g_gate[g][i], gi)
        gate.append(gi)
    wsum = functools.reduce(lambda a, c: a + c, gate)
    row = lax.broadcasted_iota(jnp.int32, (GW, lt.shape[1]), 0)
    side = jnp.zeros((GW, lt.shape[1]), F32)
    for i in range(E_PER):
        side = jnp.where(row == i, gate[i] / wsum, side)
    return side.T, best


def _outproj_kernel(rb_ref, a_ref, of_ref, ob_ref, gt_ref, *refs, dvp, dv):
    *h_refs, mod_ref, gn_ref, wa_ref, wr_ref, lng_ref, lnb_ref, wrt_ref, h1_ref, u2_ref, gate_ref, grp_ref = refs
    o = of_ref[0].astype(F32) + ob_ref[0].astype(F32)
    parts = []
    for hh in range(R_HEADS):
        sl = slice(hh * dvp, (hh + 1) * dvp)
        oh = o[:, sl]
        ms = jnp.sum(oh * oh, axis=-1, keepdims=True) * (1.0 / dv)
        parts.append(oh * lax.rsqrt(ms + LN_EPS))
    rec = jnp.concatenate(parts, axis=1) * gn_ref[...] * _silu(gt_ref[0].astype(F32))
    y = jnp.dot(a_ref[0], wa_ref[...], preferred_element_type=F32)
    y = y + jnp.dot(rec.astype(BF16), wr_ref[...], preferred_element_type=F32)
    g1 = mod_ref[0, 0, 2:3, :]
    h1 = _ln(ALPHA * _rows_of(h_refs) + g1 * y) * lng_ref[0:1, :] + lnb_ref[0:1, :]
    h1_ref[0] = h1
    u2 = _ln(h1) * (1.0 + mod_ref[0, 0, 4:5, :]) + mod_ref[0, 0, 3:4, :]
    lg = jnp.dot(_split2(u2, 0), _split2(wrt_ref[...], 1), preferred_element_type=F32)
    lg = lg[:TM] + lg[TM:]
    logits = lg[:, :GW] + lg[:, GW:]
    side, best = _route(logits, rb_ref)
    grp_ref[0, 0] = best
    gate_ref[...] = side
    for c in range(XROWS):
        u2_ref[pl.ds(c, TM, stride=XROWS), :] = u2[:, 128 * c:128 * (c + 1)]


def _outproj(b_router, a, a_col, a_w, o_f, o_b, proj, g_col, h, modsel, gnorm, wa, wr, lng, lnb, wrt, *,
             dv, dvp, row_off):
    b, rc, _ = proj.shape
    nt = rc // TM - row_off
    wrec = R_HEADS * dvp
    full = lambda shape: pl.BlockSpec(shape, lambda i, j: (0,) * len(shape))
    h_specs, h_args = _row_specs(h, row_off)
    return pl.pallas_call(
        functools.partial(_outproj_kernel, dvp=dvp, dv=dv),
        grid=(b, nt),
        in_specs=[pl.BlockSpec(memory_space=pltpu.SMEM),
                  pl.BlockSpec((1, TM, a_w), lambda i, j: (i, j + (row_off if a.shape[1] == rc else 0), a_col)),
                  pl.BlockSpec((1, TM, wrec), lambda i, j: (i, j + row_off, 0)),
                  pl.BlockSpec((1, TM, wrec), lambda i, j: (i, j + row_off, 0)),
                  pl.BlockSpec((1, TM, wrec), lambda i, j: (i, j + row_off, g_col))] + h_specs + [
                  pl.BlockSpec((1, 1, 6, D), lambda i, j: (i, jnp.minimum(j + row_off, 1), 0, 0)),
                  full((1, wrec)), full((a_w, D)), full((wrec, D)), full((1, D)), full((1, D)), full((D, GW))],
        out_specs=[pl.BlockSpec((1, TM, D), lambda i, j: (i, j, 0)),
                   pl.BlockSpec((TM * XROWS, 128), lambda i, j: (i * nt + j, 0)),
                   pl.BlockSpec((TM, GW), lambda i, j: (i * nt + j, 0)),
                   pl.BlockSpec((1, 1, 1, TM), lambda i, j: (i, j, 0, 0))],
        out_shape=[jax.ShapeDtypeStruct((b, nt * TM, D), F32),
                   jax.ShapeDtypeStruct((b * nt * TM * XROWS, 128), F32),
                   jax.ShapeDtypeStruct((b * nt * TM, GW), F32),
                   jax.ShapeDtypeStruct((b, nt, 1, TM), F32)],
        compiler_params=_cp(("parallel", "arbitrary"), V7X_VMEM_LIMIT),
        name="outproj",
    )(b_router, a, o_f, o_b, proj, *h_args, modsel, gnorm, wa, wr, lng, lnb, wrt)


def _start_token_gather(idx_ref, base, src_hbm, dst_ref, sem, rows):
    def body(r, carry):
        t = idx_ref[base + r]
        pltpu.make_async_copy(src_hbm.at[pl.ds(pl.multiple_of(t * rows, rows), rows), :],
                              dst_ref.at[pl.ds(pl.multiple_of(r * rows, rows), rows), :], sem).start()
        return carry

    lax.fori_loop(0, TM, body, 0, unroll=8)


def _wait_token_gather(src_hbm, dst_ref, sem):
    pltpu.make_async_copy(src_hbm.at[pl.ds(0, dst_ref.shape[0]), :], dst_ref, sem).wait()


def _gather_ring(idx_ref, first_of, step, last, src_hbm, buf_ref, sem, rows):
    depth = buf_ref.shape[0]
    slot = step % depth

    @pl.when(step == 0)
    def _():
        for k in range(depth - 1):
            _start_token_gather(idx_ref, first_of(jnp.minimum(k, last)), src_hbm, buf_ref.at[k], sem.at[k], rows)

    _wait_token_gather(src_hbm, buf_ref.at[slot], sem.at[slot])
    ahead = (step + depth - 1) % depth
    nxt = first_of(jnp.minimum(step + depth - 1, last))

    def issue(r0, r1):
        for r in range(r0, r1):
            t = idx_ref[nxt + r]
            pltpu.make_async_copy(src_hbm.at[pl.ds(pl.multiple_of(t * rows, rows), rows), :],
                                  buf_ref.at[ahead, pl.ds(r * rows, rows), :], sem.at[ahead]).start(priority=r % 2)

    def finish():
        @pl.when(step == last)
        def _():
            for k in range(1, depth):
                other = (step + k) % depth
                _wait_token_gather(src_hbm, buf_ref.at[other], sem.at[other])

    return slot, issue, finish


def _moe_kernel(tg_ref, first_ref, src_ref, x_hbm, g_hbm, w1_ref, w3_ref, w2_ref, o_ref, xb_ref, sem, gb_ref, gsem,
                w1b_ref, w3b_ref, w2b_ref):
    i = pl.program_id(0)
    last = pl.num_programs(0) - 1
    slot, issue, finish = _gather_ring(src_ref, lambda k: first_ref[k], i, last, x_hbm, xb_ref, sem, XROWS)
    _, issue_g, finish_g = _gather_ring(src_ref, lambda k: first_ref[k], i, last, g_hbm, gb_ref, gsem, 1)

    @pl.when((i == 0) | (tg_ref[i] != tg_ref[jnp.maximum(i - 1, 0)]))
    def _():
        for e in range(E_PER):
            w1b_ref[e] = w1_ref[e].astype(BF16)
            w3b_ref[e] = w3_ref[e].astype(BF16)
            w2b_ref[e] = w2_ref[e].astype(BF16)

    xb = xb_ref.at[slot]
    x = jnp.concatenate([xb[pl.ds(c, TM, stride=XROWS), :] for c in range(D // 128)], axis=1).astype(BF16)
    side = gb_ref[slot]
    y = jnp.zeros((TM, D), F32)
    per = TM // E_PER
    for e in range(E_PER):
        issue(e * per, (e + 1) * per)
        issue_g(e * per, (e + 1) * per)
        h1 = jnp.dot(x, w1b_ref[e], preferred_element_type=F32)
        h3 = jnp.dot(x, w3b_ref[e], preferred_element_type=F32)
        hh = (_silu(h1) * h3 * side[:, e:e + 1]).astype(BF16)
        y = y + jnp.dot(hh, w2b_ref[e], preferred_element_type=F32)
    for c in range(D // 128):
        o_ref[pl.ds(c, TM, stride=YROWS), :] = y[:, 128 * c:128 * (c + 1)]
    finish()
    finish_g()


def _moe(tile_grp, tile_first, order, u2x, gates, w1, w3, w2, layer):
    mp = tile_grp.shape[0] * TM
    wspec = lambda k, n: pl.BlockSpec((None, E_PER, k, n), lambda i, tg, tf, od: (layer, tg[i], 0, 0),
                                      pipeline_mode=pl.Buffered(1))
    return pl.pallas_call(
        _moe_kernel,
        grid_spec=pltpu.PrefetchScalarGridSpec(
            num_scalar_prefetch=3, grid=(mp // TM,),
            in_specs=[pl.BlockSpec(memory_space=pl.ANY), pl.BlockSpec(memory_space=pl.ANY),
                      wspec(D, D_FF), wspec(D, D_FF), wspec(D_FF, D)],
            out_specs=pl.BlockSpec((TM * YROWS, 128), lambda i, tg, tf, od: (i, 0)),
            scratch_shapes=[pltpu.VMEM((2, TM * XROWS, 128), F32), pltpu.SemaphoreType.DMA((2,)),
                            pltpu.VMEM((2, TM, GW), F32), pltpu.SemaphoreType.DMA((2,)),
                            pltpu.VMEM((E_PER, D, D_FF), BF16), pltpu.VMEM((E_PER, D, D_FF), BF16),
                            pltpu.VMEM((E_PER, D_FF, D), BF16)]),
        out_shape=jax.ShapeDtypeStruct((mp * YROWS, 128), F32),
        compiler_params=_cp(("arbitrary",), V7X_VMEM_LIMIT),
        name="moe",
    )(tile_grp, tile_first, order, u2x, gates, w1, w3, w2)


def _combine_kernel(pos_ref, y_hbm, h_ref, mod_ref, lng_ref, lnb_ref, o_ref, yb_ref, sem, *, nt):
    step = pl.program_id(0) * nt + pl.program_id(1)
    slot, issue, finish = _gather_ring(pos_ref, lambda k: k * TM, step, pl.num_programs(0) * nt - 1,
                                       y_hbm, yb_ref, sem, YROWS)
    yb = yb_ref.at[slot]
    g2 = mod_ref[0, 0, 5:6, :]
    strip = TM // 4
    for r0 in range(0, TM, strip):
        issue(r0, r0 + strip)
        y = jnp.concatenate([yb[pl.ds(r0 * YROWS + c, strip, stride=YROWS), :] for c in range(D // 128)], axis=1)
        o_ref[0, r0:r0 + strip, :] = (_ln(ALPHA * h_ref[0, r0:r0 + strip, :] + g2 * y) * lng_ref[0:1, :]
                                      + lnb_ref[0:1, :])
    finish()


def _combine(pos, y_sorted, h1, modsel, lng, lnb, row_off):
    b, r, _ = h1.shape
    nt = r // TM
    full = lambda shape: pl.BlockSpec(shape, lambda i, j, p: (0,) * len(shape))
    return pl.pallas_call(
        functools.partial(_combine_kernel, nt=nt),
        grid_spec=pltpu.PrefetchScalarGridSpec(
            num_scalar_prefetch=1, grid=(b, nt),
            in_specs=[pl.BlockSpec(memory_space=pl.ANY),
                      pl.BlockSpec((1, TM, D), lambda i, j, p: (i, j, 0)),
                      pl.BlockSpec((1, 1, 6, D), lambda i, j, p: (i, jnp.minimum(j + row_off, 1), 0, 0)),
                      full((1, D)), full((1, D))],
            out_specs=pl.BlockSpec((1, TM, D), lambda i, j, p: (i, j, 0)),
            scratch_shapes=[pltpu.VMEM((3, TM * YROWS, 128), F32), pltpu.SemaphoreType.DMA((3,))]),
        out_shape=jax.ShapeDtypeStruct((b, r, D), F32),
        compiler_params=_cp(("arbitrary", "arbitrary")),
        name="moe_combine",
    )(pos, y_sorted, h1, modsel, lng, lnb)


def _sort_by_group(grp):
    n = grp.shape[0]
    onehot = (grp[:, None] == jnp.arange(N_GRP, dtype=jnp.int32)[None, :]).astype(jnp.int32)
    csum = jnp.cumsum(onehot, axis=0)
    counts = csum[-1]
    padded = ((counts + TM - 1) // TM) * TM
    ends = jnp.cumsum(padded)
    starts = ends - padded
    rank = jnp.sum(csum * onehot, axis=1) - 1
    pos = (starts[grp] + rank).astype(jnp.int32)
    _, order = lax.sort((grp, jnp.arange(n, dtype=jnp.int32)), num_keys=1, is_stable=True)
    order = jnp.concatenate([order, jnp.zeros((TM,), jnp.int32)])
    ntile = n // TM + N_GRP
    tile_start = jnp.arange(ntile, dtype=jnp.int32) * TM
    tile_grp = jnp.minimum(jnp.sum((tile_start[:, None] >= ends[None, :]).astype(jnp.int32), axis=1), N_GRP - 1)
    first = (jnp.cumsum(counts) - counts)[tile_grp] + tile_start - starts[tile_grp]
    return pos, order, tile_grp.astype(jnp.int32), jnp.minimum(first, n).astype(jnp.int32)


def _moe_layer(u2x, gates, grp, h1, modsel, experts, lng, lnb, row_off):
    pos, order, tile_grp, tile_first = _sort_by_group(grp.reshape(-1).astype(jnp.int32))
    y_sorted = _moe(tile_grp, tile_first, order, u2x, gates, *experts)
    return _combine(pos, y_sorted, h1, modsel, lng, lnb, row_off)


def _rope_tables(rc, t):
    half = A_HD // 4
    inv = ROPE_BASE ** (-jnp.arange(half, dtype=F32) / half)
    tpos = jnp.arange(t)
    rows = (tpos // GRID_W).astype(F32)
    cols = (tpos % GRID_W).astype(F32)

    def cs(pos):
        ang = pos[:, None] * inv[None, :]
        c, s = jnp.cos(ang), jnp.sin(ang)
        return jnp.concatenate([c, c], -1), jnp.concatenate([-s, s], -1)

    cr, sr = cs(rows)
    cc_, sc_ = cs(cols)
    cos_h = jnp.concatenate([cr, cc_], -1)
    sin_h = jnp.concatenate([sr, sc_], -1)
    one = jnp.ones((t, A_HD), F32)
    zero = jnp.zeros((t, A_HD), F32)
    lat = jnp.stack([jnp.concatenate([cos_h, cos_h], -1), jnp.concatenate([sin_h, sin_h], -1),
                     jnp.concatenate([cos_h, one], -1), jnp.concatenate([sin_h, zero], -1)])
    ctx = jnp.stack([jnp.ones((CTX, 128), F32), jnp.zeros((CTX, 128), F32)] * 2)
    return jnp.concatenate([ctx, lat], axis=1)


def _dft_tables(t):
    def cos_sin(num, den):
        ang = (2.0 * math.pi / den) * (num % den).astype(F32)
        return jnp.cos(ang), jnp.sin(ang)

    k = jnp.arange(t, dtype=jnp.int32)[:, None]
    ca, sa = cos_sin(k * jnp.arange(t // GRID_W, dtype=jnp.int32)[None, :], t // GRID_W)
    cb, sb = cos_sin(k * jnp.arange(GRID_W, dtype=jnp.int32)[None, :], t)
    ct = (ca[:, :, None] * cb[:, None, :] - sa[:, :, None] * sb[:, None, :]).reshape(t, t)
    st = (sa[:, :, None] * cb[:, None, :] + ca[:, :, None] * sb[:, None, :]).reshape(t, t)
    c = jnp.arange(F_GD, dtype=jnp.int32)
    cc, sc = cos_sin(c[:, None] * c[None, :], F_GD)
    eye = jnp.eye(F_GROUPS, dtype=F32)
    cs = jnp.concatenate([jnp.kron(eye, cc), jnp.kron(eye, sc)], axis=1)
    return cs.astype(BF16), ct.astype(BF16), st.astype(BF16)


def _pad_heads(w, width, padded):
    lead = w.shape[:-1]
    w = w.reshape(lead + (R_HEADS, width))
    w = jnp.pad(w, [(0, 0)] * len(lead) + [(0, 0), (0, padded - width)])
    return w.reshape(lead + (R_HEADS * padded,))


EVEN_SEGS = tuple([(128 * g, 128 * (g + 1), 0) for g in range(4)] + [(c, c + 512, None) for c in range(512, 3072, 512)]
                  + [(3072 + 128 * g, 3072 + 128 * (g + 1), 1) for g in range(2)])
ODD_W = 512 + 512 + 1024 + 1024 + 256 + 128
ODD_SEGS = tuple([(c, c + 512, None) for c in range(0, 3072, 512)] + [(3072, ODD_W, None)])


def _layer_even(h, modsel, tbl, w_in, sink, lb_logits, g_norm, w_out, lng, lnb, w_router, b_router, experts):
    aq, ak, av, ff, fb, hq, hi, hg = jnp.split(w_in, np.cumsum([512, 128, 128, 512, 512, 512, 512]).tolist(), axis=1)
    kv = [jnp.concatenate([ak[:, A_HD * j:A_HD * (j + 1)], av[:, A_HD * j:A_HD * (j + 1)]], axis=1)
          for j in range(A_KV)]
    w = jnp.concatenate([aq, ff, fb, hq, hi, hg] + kv, axis=1).astype(BF16)
    proj = _inproj(h, modsel, w, tbl, EVEN_SEGS)
    o_att = _attention(proj, sink, 3072 // (2 * A_HD * A_KV))
    lb = jnp.cumsum(jax.nn.softmax(lb_logits.astype(F32), axis=1), axis=1)[:, 0]
    o_f, o_b = _hgrn_scan(proj, lb, 3, 4, 1, 2)
    wo = w_out.astype(BF16)
    h1, u2x, gates, grp = _outproj(b_router, o_att, 0, 512, o_f, o_b, proj, 5, h, modsel, g_norm.reshape(1, -1),
                            wo[:512], wo[512:], lng[0:1], lnb[0:1], w_router, dv=128, dvp=128, row_off=0)
    return _moe_layer(u2x, gates, grp, h1, modsel, experts, lng[1:2], lnb[1:2], 0)


def _layer_odd(h, modsel, tbl, dft, w_in, gate_w, gate_b, g_norm, w_out, lng, lnb, w_router, b_router, experts, t):
    q, k, v, rf, rb, g, z = jnp.split(w_in, np.cumsum([384, 384, 768, 16, 16, 768]).tolist(), axis=1)
    r = jnp.pad(jnp.concatenate([rf, rb], axis=1), ((0, 0), (0, 96)))
    w = jnp.concatenate([_pad_heads(q, GLA_DK, DKP), _pad_heads(k, GLA_DK, DKP), _pad_heads(v, GLA_DV, GLA_DVP),
                         _pad_heads(g, GLA_DV, GLA_DVP), z, r], axis=1).astype(BF16)
    proj = _inproj(h, modsel, w, tbl, ODD_SEGS)
    gw = _pad_heads(gate_w, GLA_DK, DKP)
    gwf = jnp.pad(gw[0], ((0, 112), (0, 0)))
    gwb = jnp.pad(gw[1], ((16, 96), (0, 0)))
    gb = _pad_heads(gate_b, GLA_DK, DKP)
    o_f, o_b = _gla_scan(proj, gwf, gwb, gb, 0, 1, 1, 3328 // 128)
    cs, ct, st = dft
    four = _fourier(proj, 3072 // (F_GROUPS * F_GD), cs, ct, st, t)
    wo = w_out
    wr = jnp.pad(wo[:768].reshape(R_HEADS, GLA_DV, D), ((0, 0), (0, GLA_DVP - GLA_DV), (0, 0))).reshape(-1, D)
    h1, u2x, gates, grp = _outproj(b_router, four, 0, 256, o_f, o_b, proj, 2, h, modsel,
                            _pad_heads(g_norm, GLA_DV, GLA_DVP).reshape(1, -1),
                            wo[768:].astype(BF16), wr.astype(BF16), lng[0:1], lnb[0:1], w_router,
                            dv=GLA_DV, dvp=GLA_DVP, row_off=CTX // TM)
    return _moe_layer(u2x, gates, grp, h1, modsel, experts, lng[1:2], lnb[1:2], CTX // TM)


def kernel(x, c, ctx, c_ctx, w_ada, b_ada, ln_g, ln_b, w_in_even, attn_sink, hgrn_lb_logits, hgrn_norm, w_out_even, w_in_odd, gla_gate_w, gla_gate_b, gla_norm, w_out_odd, w_router, b_router, w_expert_gate, w_expert_up, w_expert_down):
    b, t, _ = x.shape
    rc = CTX + t
    assert ctx.shape[1] == CTX
    cc = jnp.zeros((16, D), F32).at[:b].set(c).at[b].set(c_ctx)
    mods = _ada_mods(cc, w_ada, b_ada).reshape(2, 16, 6, D)

    def modsel(l):
        return jnp.stack([jnp.broadcast_to(mods[l, b], (b, 6, D)), mods[l, :b]], axis=1)

    tbl = _rope_tables(rc, t)
    wrt = jnp.pad(w_router, ((0, 0), (0, GW - N_EXP)))
    experts = (w_expert_gate, w_expert_up, w_expert_down)
    h = _layer_even((ctx, x), modsel(0), tbl, w_in_even[0], attn_sink[0], hgrn_lb_logits, hgrn_norm[0], w_out_even[0],
                    ln_g[0], ln_b[0], wrt, b_router, experts + (0,))
    return _layer_odd(h, modsel(1), tbl, _dft_tables(t), w_in_odd[0], gla_gate_w[0], gla_gate_b[0], gla_norm[0],
                      w_out_odd[0], ln_g[1], ln_b[1], wrt, b_router, experts + (1,), t)
```

```python
import functools
import math

import numpy as np
import jax
import jax.numpy as jnp
from jax import lax
from jax.experimental import pallas as pl
from jax.experimental.pallas import tpu as pltpu

F32 = jnp.float32
BF16 = jnp.bfloat16
HIGHEST = lax.Precision.HIGHEST

D = 1024
CTX = 256
GRID_W = 64
LN_EPS = 1e-5
NEG = -1e30
ALPHA = 4.0 ** 0.25
ROPE_BASE = 10000.0

A_HEADS, A_KV, A_HD, A_BLK = 8, 2, 64, 128
R_HEADS = 4
DKP = 128
GLA_DK, GLA_DV, GLA_DVP = 96, 192, 256
GLA_TAU = 16.0
CHUNK = 64
N_EXP, N_GRP, E_PER, TOP_K = 16, 4, 4, 2
N_PAIR = E_PER * (E_PER - 1) // 2
N_CLASS = N_GRP * N_PAIR
D_FF = 512
F_GROUPS, F_GD = 4, 64

TM = 256
IN_SUB = 3
GW = 128
XROWS = 8
YROWS = 8
V7X_VMEM_LIMIT = 56 * 1024 * 1024


def _cp(sem, vmem=None):
    return pltpu.CompilerParams(dimension_semantics=sem, vmem_limit_bytes=vmem)


def _ln(x):
    mu = jnp.mean(x, axis=-1, keepdims=True)
    xc = x - mu
    var = jnp.mean(xc * xc, axis=-1, keepdims=True)
    return xc * lax.rsqrt(var + LN_EPS)


def _silu(x):
    return x * jax.nn.sigmoid(x)


def _split2(w, axis):
    hi = w.astype(BF16)
    return jnp.concatenate([hi, (w - hi.astype(F32)).astype(BF16)], axis=axis)


_NT = (((1,), (1,)), ((), ()))
_TN = (((0,), (0,)), ((), ()))


def _ada_kernel(c_ref, w_ref, b_ref, o_ref):
    s = _silu(c_ref[...])
    o_ref[0] = jnp.dot(s, w_ref[0], precision=HIGHEST, preferred_element_type=F32) + b_ref[0]


def _ada_mods(cc, w_ada, b_ada):
    depth, _, n = w_ada.shape
    tn = 1536
    return pl.pallas_call(
        _ada_kernel,
        grid=(depth, n // tn),
        in_specs=[pl.BlockSpec((16, D), lambda l, i: (0, 0)),
                  pl.BlockSpec((1, D, tn), lambda l, i: (l, 0, i)),
                  pl.BlockSpec((1, 1, tn), lambda l, i: (l, 0, i))],
        out_specs=pl.BlockSpec((1, 16, tn), lambda l, i: (l, 0, i)),
        out_shape=jax.ShapeDtypeStruct((depth, 16, n), F32),
        compiler_params=_cp(("arbitrary", "arbitrary"), 40 * 1024 * 1024),
        name="ada_mod",
    )(cc, w_ada, b_ada.reshape(depth, 1, n))


def _rows_of(h_refs):
    if len(h_refs) == 1:
        return h_refs[0][0]
    return jnp.where(pl.program_id(1) == 0, h_refs[0][0], h_refs[1][0])


def _row_specs(h, row_off=0):
    if isinstance(h, tuple):
        assert row_off == 0 and h[0].shape[1] == TM
        return [pl.BlockSpec((1, TM, D), lambda i, j: (i, 0, 0)),
                pl.BlockSpec((1, TM, D), lambda i, j: (i, jnp.maximum(j - 1, 0), 0))], list(h)
    return [pl.BlockSpec((1, TM, D), lambda i, j: (i, j + row_off, 0))], [h]


def _inproj_kernel(*refs, segs):
    *h_refs, mod0_ref, mod1_ref, w_ref, tbl_ref, o_ref = refs
    if len(h_refs) == 1:
        x = h_refs[0][0]
    else:
        ctx_ref, *x_refs = h_refs
        x = jnp.concatenate([jnp.where(pl.program_id(1) == 0, ctx_ref[0], x_refs[0][0])]
                            + [r[0] for r in x_refs[1:]], axis=0)
    xn = _ln(x)
    u = jnp.concatenate([xn[:TM] * (1.0 + mod0_ref[0, 0, 1:2, :]) + mod0_ref[0, 0, 0:1, :],
                         xn[TM:] * (1.0 + mod1_ref[0, 0, 1:2, :]) + mod1_ref[0, 0, 0:1, :]], axis=0).astype(BF16)
    lane = lax.broadcasted_iota(jnp.int32, (x.shape[0], 128), 1)
    low = (lane % 32) < 16
    for c0, c1, rope in segs:
        acc = jnp.dot(u, w_ref[:, c0:c1], preferred_element_type=F32)
        if rope is not None:
            partner = jnp.where(low, pltpu.roll(acc, 112, axis=1), pltpu.roll(acc, 16, axis=1))
            acc = acc * tbl_ref[2 * rope] + partner * tbl_ref[2 * rope + 1]
        o_ref[0, :, c0:c1] = acc.astype(BF16)


def _inproj(h, modsel, w, tbl, segs):
    b, rc = modsel.shape[0], tbl.shape[1]
    n = w.shape[1]
    tmi = IN_SUB * TM
    assert rc % tmi == 0
    if isinstance(h, tuple):
        assert h[0].shape[1] == TM
        h_specs = [pl.BlockSpec((1, TM, D), lambda i, j: (i, 0, 0))] + [
            pl.BlockSpec((1, TM, D), lambda i, j, k=k: (i, jnp.maximum(IN_SUB * j + k - 1, 0), 0))
            for k in range(IN_SUB)]
        h_args = [h[0]] + [h[1]] * IN_SUB
    else:
        h_specs, h_args = [pl.BlockSpec((1, tmi, D), lambda i, j: (i, j, 0))], [h]
    return pl.pallas_call(
        functools.partial(_inproj_kernel, segs=segs),
        grid=(b, rc // tmi),
        in_specs=h_specs + [pl.BlockSpec((1, 1, 6, D), lambda i, j: (i, jnp.minimum(j, 1), 0, 0)),
                            pl.BlockSpec((1, 1, 6, D), lambda i, j: (i, 1, 0, 0)),
                            pl.BlockSpec((D, n), lambda i, j: (0, 0)),
                            pl.BlockSpec((4, tmi, 128), lambda i, j: (0, j, 0))],
        out_specs=pl.BlockSpec((1, tmi, n), lambda i, j: (i, j, 0)),
        out_shape=jax.ShapeDtypeStruct((b, rc, n), BF16),
        compiler_params=_cp(("parallel", "arbitrary"), V7X_VMEM_LIMIT),
        name="inproj",
    )(*h_args, modsel, modsel, w, tbl)


def _attn_kernel(sink_ref, q_ref, kvc_ref, kvp_ref, kvm_ref, kvn_ref, bias_ref, o_ref):
    blk = pl.program_id(1)
    g_per = A_HEADS // A_KV
    nq = g_per * A_BLK
    q = q_ref[0] * (A_HD ** -0.5)
    rgrp = lax.broadcasted_iota(jnp.int32, (nq, 1), 0) // A_BLK
    q4, sink_col = [], []
    for h in range(A_KV):
        q4.append(jnp.concatenate([q[:, A_HD * (g_per * h + g):A_HD * (g_per * h + g + 1)] for g in range(g_per)],
                                  axis=0))
        col = jnp.zeros((nq, 1), F32)
        for g in range(g_per):
            col = jnp.where(rgrp == g, sink_ref[h * g_per + g], col)
        sink_col.append(col)

    def attend(kv, bias):
        s = [lax.dot_general(q4[h], kv[:, 2 * A_HD * h:2 * A_HD * h + A_HD], _NT, preferred_element_type=F32)
             for h in range(A_KV)]
        if bias is not None:
            bias4 = jnp.concatenate([bias] * g_per, axis=0)
            s = [x + bias4 for x in s]
        m = [jnp.maximum(jnp.max(s[h], axis=-1, keepdims=True), sink_col[h]) for h in range(A_KV)]
        p = [jnp.exp(s[h] - m[h]) for h in range(A_KV)]
        den = [jnp.sum(p[h], axis=-1, keepdims=True) + jnp.exp(sink_col[h] - m[h]) for h in range(A_KV)]
        o = [jnp.dot(p[h].astype(BF16), kv[:, 2 * A_HD * h + A_HD:2 * A_HD * (h + 1)], preferred_element_type=F32)
             / den[h] for h in range(A_KV)]
        o_ref[0] = jnp.concatenate([o[h][A_BLK * g:A_BLK * (g + 1), :] for h in range(A_KV) for g in range(g_per)],
                                   axis=1).astype(BF16)

    @pl.when(blk < CTX // A_BLK)
    def _():
        attend(kvc_ref[0], None)

    @pl.when(blk >= CTX // A_BLK)
    def _():
        attend(jnp.concatenate([kvc_ref[0], kvp_ref[0], kvm_ref[0], kvn_ref[0]], axis=0), bias_ref[0])


def _attn_bias():
    nk = CTX + 3 * A_BLK
    qi = jnp.arange(A_BLK, dtype=jnp.int32)[:, None]
    col = jnp.arange(nk, dtype=jnp.int32)[None, :]
    rel = col - (CTX + A_BLK)
    dist = qi - rel

    def one(lo, hi):
        in_win = (dist <= A_BLK) & (dist >= -A_BLK) & (rel >= lo) & (rel < hi)
        return jnp.where((col < CTX) | in_win, 0.0, NEG).astype(F32)

    return jnp.stack([one(0, 2 * A_BLK), one(-A_BLK, 2 * A_BLK), one(-A_BLK, A_BLK)])


def _attention(proj, sink, kv_blk):
    b, rc, _ = proj.shape
    nb = rc // A_BLK
    c_b = CTX // A_BLK
    assert nb - c_b >= 2
    kw = 2 * A_HD * A_KV

    def band(off):
        return lambda i, j: (i, jnp.clip(j + off, c_b, nb - 1), kv_blk)

    def which_bias(i, j):
        return (jnp.where(j <= c_b, 0, jnp.where(j == nb - 1, 2, 1)), 0, 0)

    return pl.pallas_call(
        _attn_kernel,
        grid=(b, nb),
        in_specs=[pl.BlockSpec(memory_space=pltpu.SMEM),
                  pl.BlockSpec((1, A_BLK, A_HEADS * A_HD), lambda i, j: (i, j, 0)),
                  pl.BlockSpec((1, CTX, kw), lambda i, j: (i, 0, kv_blk)),
                  pl.BlockSpec((1, A_BLK, kw), band(-1)),
                  pl.BlockSpec((1, A_BLK, kw), band(0)),
                  pl.BlockSpec((1, A_BLK, kw), band(1)),
                  pl.BlockSpec((1, A_BLK, CTX + 3 * A_BLK), which_bias)],
        out_specs=pl.BlockSpec((1, A_BLK, A_HEADS * A_HD), lambda i, j: (i, j, 0)),
        out_shape=jax.ShapeDtypeStruct((b, rc, A_HEADS * A_HD), BF16),
        compiler_params=_cp(("parallel", "arbitrary")),
        name="win_attn",
    )(sink, proj, proj, proj, proj, proj, _attn_bias())


def _scan_blocks(dirs, *, dvp):
    r_i = lax.broadcasted_iota(jnp.int32, (CHUNK, CHUNK), 0)
    c_i = lax.broadcasted_iota(jnp.int32, (CHUNK, CHUNK), 1)
    units = []
    for q, k, v, logf, s_ref, o_ref, reverse in dirs:
        nchunk = q.shape[0] // CHUNK
        tri = (r_i <= c_i) if reverse else (r_i >= c_i)
        tri3 = jnp.concatenate([jnp.where(tri, 1.0, 0.0).astype(BF16)] * 3, axis=1)
        for c in (range(nchunk - 1, -1, -1) if reverse else range(nchunk)):
            sl = slice(c * CHUNK, (c + 1) * CHUNK)
            units.append(dict(q=q[sl], k=k[sl], v=v[sl].astype(BF16), lf=logf[sl], sl=sl, tri=tri, tri3=tri3,
                              i_tot=0 if reverse else CHUNK - 1,
                              i_ref=CHUNK // 2 - 1 if reverse else CHUNK // 2, s_ref=s_ref, o_ref=o_ref))
    heads = [(slice(h * DKP, (h + 1) * DKP), slice(h * dvp, (h + 1) * dvp)) for h in range(R_HEADS)]
    for u in units:
        lf = u["lf"]
        hi = lf.astype(BF16)
        r1 = lf - hi.astype(F32)
        mid = r1.astype(BF16)
        lo = (r1 - mid.astype(F32)).astype(BF16)
        u["bsum"] = jnp.dot(u["tri3"], jnp.concatenate([hi, mid, lo], axis=0), preferred_element_type=F32)
    for u in units:
        bsum = u["bsum"]
        tot = bsum[u["i_tot"]:u["i_tot"] + 1, :]
        ref = bsum[u["i_ref"]:u["i_ref"] + 1, :]
        qg = u["q"] * jnp.exp(bsum - ref)
        kg = u["k"] * jnp.exp(ref - bsum)
        u["qb"] = (qg * jnp.exp(ref)).astype(BF16)
        u["kd"] = (kg * jnp.exp(tot - ref)).astype(BF16)
        u["qg"] = qg.astype(BF16)
        u["kg"] = kg.astype(BF16)
        u["dec"] = jnp.exp(tot)
    for u in units:
        u["att"] = [lax.dot_general(u["qg"][:, ks], u["kg"][:, ks], _NT, preferred_element_type=F32)
                    for ks, _ in heads]
        u["ds"] = [lax.dot_general(u["v"][:, vs], u["kd"][:, ks], _TN, preferred_element_type=F32)
                   for ks, vs in heads]
    for u in units:
        u["oi"] = [jnp.dot(jnp.where(u["tri"], a, 0.0).astype(BF16), u["v"][:, vs], preferred_element_type=F32)
                   for a, (_, vs) in zip(u["att"], heads)]
    state = {}
    for u in units:
        key = id(u["s_ref"])
        if key not in state:
            state[key] = [u["s_ref"][h] for h in range(R_HEADS)]
        st = state[key]
        outs = []
        for h, (ks, _) in enumerate(heads):
            outs.append(u["oi"][h] + lax.dot_general(u["qb"][:, ks], st[h].astype(BF16), _NT,
                                                     preferred_element_type=F32))
            st[h] = st[h] * u["dec"][:, ks] + u["ds"][h]
        u["o_ref"][0, u["sl"], :] = jnp.concatenate(outs, axis=1).astype(BF16)
    for _, _, _, _, s_ref, _, _ in dirs:
        for h in range(R_HEADS):
            s_ref[h] = state[id(s_ref)][h]


def _hgrn_scan_kernel(lb_ref, qf_ref, vf_ref, zf_ref, qb_ref, vb_ref, zb_ref, of_ref, ob_ref, sf_ref, sb_ref):
    @pl.when(pl.program_id(1) == 0)
    def _():
        sf_ref[...] = jnp.zeros_like(sf_ref)
        sb_ref[...] = jnp.zeros_like(sb_ref)

    def gates(z, lb):
        f = lb + (1.0 - lb) * jax.nn.sigmoid(z.astype(F32))
        return 1.0 - f, jnp.log(f)

    kf, lf = gates(zf_ref[0], lb_ref[0:1, :])
    kb, lbw = gates(zb_ref[0], lb_ref[1:2, :])
    _scan_blocks([(qf_ref[0], kf, vf_ref[0], lf, sf_ref, of_ref, False),
                  (qb_ref[0], kb, vb_ref[0], lbw, sb_ref, ob_ref, True)], dvp=128)


def _bwd_blk(j, nt):
    return jnp.where(j == 0, 0, nt - j)


def _hgrn_scan(proj, lb, cq, cv, cf, cb):
    b, rc, _ = proj.shape
    nt = rc // TM
    w = R_HEADS * DKP

    def fw(col):
        return pl.BlockSpec((1, TM, w), lambda i, j: (i, j, col))

    def bw(col):
        return pl.BlockSpec((1, TM, w), lambda i, j: (i, _bwd_blk(j, nt), col))

    o_sd = jax.ShapeDtypeStruct((b, rc, w), BF16)
    return pl.pallas_call(
        _hgrn_scan_kernel,
        grid=(b, nt),
        in_specs=[pl.BlockSpec((2, w), lambda i, j: (0, 0)),
                  fw(cq), fw(cv), fw(cf), bw(cq), bw(cv), bw(cb)],
        out_specs=[pl.BlockSpec((1, TM, w), lambda i, j: (i, j, 0)),
                   pl.BlockSpec((1, TM, w), lambda i, j: (i, _bwd_blk(j, nt), 0))],
        out_shape=[o_sd, o_sd],
        scratch_shapes=[pltpu.VMEM((R_HEADS, 128, DKP), F32), pltpu.VMEM((R_HEADS, 128, DKP), F32)],
        compiler_params=_cp(("parallel", "arbitrary"), 40 * 1024 * 1024),
        name="hgrn_scan",
    )(lb, proj, proj, proj, proj, proj, proj)


def _gla_scan_kernel(gwf_ref, gwb_ref, gb_ref, qf_ref, kf_ref, vf_ref, rf_ref, qb_ref, kb_ref, vb_ref, rb_ref,
                     of_ref, ob_ref, sf_ref, sb_ref):
    @pl.when(pl.program_id(1) == 0)
    def _():
        sf_ref[...] = jnp.zeros_like(sf_ref)
        sb_ref[...] = jnp.zeros_like(sb_ref)

    def decay(r, gw, gb):
        zz = jnp.dot(jnp.concatenate([r, r], axis=1), _split2(gw, 0), preferred_element_type=F32) + gb
        return (jnp.minimum(zz, 0.0) - jnp.log(1.0 + jnp.exp(-jnp.abs(zz)))) * (1.0 / GLA_TAU)

    qs = GLA_DK ** -0.5
    _scan_blocks([(qf_ref[0].astype(F32) * qs, kf_ref[0], vf_ref[0],
                   decay(rf_ref[0], gwf_ref[...], gb_ref[0:1, :]), sf_ref, of_ref, False),
                  (qb_ref[0].astype(F32) * qs, kb_ref[0], vb_ref[0],
                   decay(rb_ref[0], gwb_ref[...], gb_ref[1:2, :]), sb_ref, ob_ref, True)], dvp=GLA_DVP)


def _gla_scan(proj, gwf, gwb, gb, cq, ck, cv, cr):
    b, rc, _ = proj.shape
    nt = rc // TM
    wk = R_HEADS * DKP
    wv = R_HEADS * GLA_DVP

    def spec(width, col, back):
        if back:
            return pl.BlockSpec((1, TM, width), lambda i, j: (i, _bwd_blk(j, nt), col))
        return pl.BlockSpec((1, TM, width), lambda i, j: (i, j, col))

    o_sd = jax.ShapeDtypeStruct((b, rc, wv), BF16)
    full = lambda shape: pl.BlockSpec(shape, lambda i, j: (0,) * len(shape))
    return pl.pallas_call(
        _gla_scan_kernel,
        grid=(b, nt),
        in_specs=[full((128, wk)), full((128, wk)), full((2, wk)),
                  spec(wk, cq, False), spec(wk, ck, False), spec(wv, cv, False), spec(128, cr, False),
                  spec(wk, cq, True), spec(wk, ck, True), spec(wv, cv, True), spec(128, cr, True)],
        out_specs=[spec(wv, 0, False), spec(wv, 0, True)],
        out_shape=[o_sd, o_sd],
        scratch_shapes=[pltpu.VMEM((R_HEADS, GLA_DVP, DKP), F32), pltpu.VMEM((R_HEADS, GLA_DVP, DKP), F32)],
        compiler_params=_cp(("parallel", "arbitrary"), 48 * 1024 * 1024),
        name="gla_scan",
    )(gwf, gwb, gb, proj, proj, proj, proj, proj, proj, proj, proj)


def _fourier_kernel(z_ref, cs_ref, ct_ref, st_ref, o_ref, *, scale):
    w = F_GROUPS * F_GD
    zz = jnp.dot(z_ref[0, CTX:, :], cs_ref[...], preferred_element_type=F32).astype(BF16)
    y = jnp.dot(ct_ref[...], zz[:, :w], preferred_element_type=F32)
    y = y - jnp.dot(st_ref[...], zz[:, w:], preferred_element_type=F32)
    o_ref[0] = (y * scale).astype(BF16)


def _fourier(proj, zcol, cs, ct, st, t):
    b, rc, _ = proj.shape
    w = F_GROUPS * F_GD
    const = lambda shape: pl.BlockSpec(shape, lambda i: (0, 0), pipeline_mode=pl.Buffered(1))
    return pl.pallas_call(
        functools.partial(_fourier_kernel, scale=1.0 / math.sqrt(t * F_GD)),
        grid=(b,),
        in_specs=[pl.BlockSpec((1, rc, w), lambda i: (i, 0, zcol)), const((w, 2 * w)), const((t, t)), const((t, t))],
        out_specs=pl.BlockSpec((1, t, w), lambda i: (i, 0, 0)),
        out_shape=jax.ShapeDtypeStruct((b, t, w), BF16),
        compiler_params=_cp(("arbitrary",), 48 * 1024 * 1024),
        name="fourier",
    )(proj, cs, ct, st)


def _route(logits, rb_ref):
    lt = logits.T
    l = [lt[e:e + 1, :] for e in range(N_EXP)]
    m = functools.reduce(jnp.maximum, l)
    ex = [jnp.exp(v - m) for v in l]
    den = functools.reduce(lambda a, c: a + c, ex)
    p = [v / den for v in ex]
    sel = [p[e] + rb_ref[e] for e in range(N_EXP)]
    g_score, g_gate, g_pick = [], [], []
    for g in range(N_GRP):
        s = sel[E_PER * g:E_PER * (g + 1)]
        chosen = []
        for i in range(E_PER):
            rank = jnp.zeros_like(s[i])
            for j in range(E_PER):
                if j == i:
                    continue
                ahead = (s[j] >= s[i]) if j < i else (s[j] > s[i])
                rank = rank + jnp.where(ahead, 1.0, 0.0)
            chosen.append(rank < 2.0)
        g_score.append(functools.reduce(lambda a, c: a + c,
                                        [jnp.where(chosen[i], s[i], 0.0) for i in range(E_PER)]))
        g_gate.append([jnp.where(chosen[i], p[E_PER * g + i], 0.0) for i in range(E_PER)])
        g_pick.append([jnp.where(chosen[i], 1.0, 0.0) for i in range(E_PER)])
    best = jnp.zeros_like(g_score[0])
    best_s = g_score[0]
    for g in range(1, N_GRP):
        upd = g_score[g] > best_s
        best = jnp.where(upd, float(g), best)
        best_s = jnp.where(upd, g_score[g], best_s)
    gate, pick = [], []
    for i in range(E_PER):
        gi, ci = g_gate[0][i], g_pick[0][i]
        for g in range(1, N_GRP):
            gi = jnp.where(best == float(g), g_gate[g][i], gi)
            ci = jnp.where(best == float(g), g_pick[g][i], ci)
        gate.append(gi)
        pick.append(ci)
    wsum = functools.reduce(lambda a, c: a + c, gate)
    lo = functools.reduce(jnp.minimum, [jnp.where(pick[i] > 0.5, float(i), float(E_PER)) for i in range(E_PER)])
    hi = functools.reduce(jnp.maximum, [jnp.where(pick[i] > 0.5, float(i), -1.0) for i in range(E_PER)])
    pair = jnp.where(lo == 0.0, hi - 1.0, jnp.where(lo == 1.0, hi + 1.0, float(N_PAIR - 1)))
    g_lo = functools.reduce(lambda a, c: a + c, [jnp.where(lo == float(i), gate[i], 0.0) for i in range(E_PER)])
    g_hi = functools.reduce(lambda a, c: a + c, [jnp.where(hi == float(i), gate[i], 0.0) for i in range(E_PER)])
    row = lax.broadcasted_iota(jnp.int32, (GW, lt.shape[1]), 0)
    side = jnp.where(row == 0, g_lo / wsum, jnp.where(row == 1, g_hi / wsum, 0.0))
    return side.T, best * float(N_PAIR) + pair


def _outproj_kernel(rb_ref, a_ref, of_ref, ob_ref, gt_ref, *refs, dvp, dv):
    *h_refs, mod_ref, gn_ref, wa_ref, wr_ref, lng_ref, lnb_ref, wrt_ref, h1_ref, u2_ref, gate_ref, grp_ref = refs
    o = of_ref[0].astype(F32) + ob_ref[0].astype(F32)
    parts = []
    for hh in range(R_HEADS):
        sl = slice(hh * dvp, (hh + 1) * dvp)
        oh = o[:, sl]
        ms = jnp.sum(oh * oh, axis=-1, keepdims=True) * (1.0 / dv)
        parts.append(oh * lax.rsqrt(ms + LN_EPS))
    rec = jnp.concatenate(parts, axis=1) * gn_ref[...] * _silu(gt_ref[0].astype(F32))
    y = jnp.dot(a_ref[0], wa_ref[...], preferred_element_type=F32)
    y = y + jnp.dot(rec.astype(BF16), wr_ref[...], preferred_element_type=F32)
    g1 = mod_ref[0, 0, 2:3, :]
    h1 = _ln(ALPHA * _rows_of(h_refs) + g1 * y) * lng_ref[0:1, :] + lnb_ref[0:1, :]
    h1_ref[0] = h1
    u2 = _ln(h1) * (1.0 + mod_ref[0, 0, 4:5, :]) + mod_ref[0, 0, 3:4, :]
    lg = jnp.dot(_split2(u2, 0), _split2(wrt_ref[...], 1), preferred_element_type=F32)
    lg = lg[:TM] + lg[TM:]
    logits = lg[:, :GW] + lg[:, GW:]
    side, best = _route(logits, rb_ref)
    grp_ref[0, 0] = best
    gate_ref[...] = side
    for c in range(XROWS):
        u2_ref[pl.ds(c, TM, stride=XROWS), :] = u2[:, 128 * c:128 * (c + 1)]


def _outproj(b_router, a, a_col, a_w, o_f, o_b, proj, g_col, h, modsel, gnorm, wa, wr, lng, lnb, wrt, *,
             dv, dvp, row_off):
    b, rc, _ = proj.shape
    nt = rc // TM - row_off
    wrec = R_HEADS * dvp
    full = lambda shape: pl.BlockSpec(shape, lambda i, j: (0,) * len(shape))
    h_specs, h_args = _row_specs(h, row_off)
    return pl.pallas_call(
        functools.partial(_outproj_kernel, dvp=dvp, dv=dv),
        grid=(b, nt),
        in_specs=[pl.BlockSpec(memory_space=pltpu.SMEM),
                  pl.BlockSpec((1, TM, a_w), lambda i, j: (i, j + (row_off if a.shape[1] == rc else 0), a_col)),
                  pl.BlockSpec((1, TM, wrec), lambda i, j: (i, j + row_off, 0)),
                  pl.BlockSpec((1, TM, wrec), lambda i, j: (i, j + row_off, 0)),
                  pl.BlockSpec((1, TM, wrec), lambda i, j: (i, j + row_off, g_col))] + h_specs + [
                  pl.BlockSpec((1, 1, 6, D), lambda i, j: (i, jnp.minimum(j + row_off, 1), 0, 0)),
                  full((1, wrec)), full((a_w, D)), full((wrec, D)), full((1, D)), full((1, D)), full((D, GW))],
        out_specs=[pl.BlockSpec((1, TM, D), lambda i, j: (i, j, 0)),
                   pl.BlockSpec((TM * XROWS, 128), lambda i, j: (i * nt + j, 0)),
                   pl.BlockSpec((TM, GW), lambda i, j: (i * nt + j, 0)),
                   pl.BlockSpec((1, 1, 1, TM), lambda i, j: (i, j, 0, 0))],
        out_shape=[jax.ShapeDtypeStruct((b, nt * TM, D), F32),
                   jax.ShapeDtypeStruct((b * nt * TM * XROWS, 128), F32),
                   jax.ShapeDtypeStruct((b * nt * TM, GW), F32),
                   jax.ShapeDtypeStruct((b, nt, 1, TM), F32)],
        compiler_params=_cp(("parallel", "arbitrary"), V7X_VMEM_LIMIT),
        name="outproj",
    )(b_router, a, o_f, o_b, proj, *h_args, modsel, gnorm, wa, wr, lng, lnb, wrt)


def _start_token_gather(idx_ref, base, src_hbm, dst_ref, sem, rows):
    def body(r, carry):
        t = idx_ref[base + r]
        pltpu.make_async_copy(src_hbm.at[pl.ds(pl.multiple_of(t * rows, rows), rows), :],
                              dst_ref.at[pl.ds(pl.multiple_of(r * rows, rows), rows), :], sem).start()
        return carry

    lax.fori_loop(0, dst_ref.shape[0] // rows, body, 0, unroll=8)


def _wait_token_gather(src_hbm, dst_ref, sem):
    pltpu.make_async_copy(src_hbm.at[pl.ds(0, dst_ref.shape[0]), :], dst_ref, sem).wait()


def _gather_ring(idx_ref, first_of, step, last, src_hbm, buf_ref, sem, rows):
    depth = buf_ref.shape[0]
    slot = step % depth

    @pl.when(step == 0)
    def _():
        for k in range(depth - 1):
            _start_token_gather(idx_ref, first_of(jnp.minimum(k, last)), src_hbm, buf_ref.at[k], sem.at[k], rows)

    _wait_token_gather(src_hbm, buf_ref.at[slot], sem.at[slot])
    ahead = (step + depth - 1) % depth
    nxt = first_of(jnp.minimum(step + depth - 1, last))

    def issue(r0, r1):
        for r in range(r0, r1):
            t = idx_ref[nxt + r]
            pltpu.make_async_copy(src_hbm.at[pl.ds(pl.multiple_of(t * rows, rows), rows), :],
                                  buf_ref.at[ahead, pl.ds(r * rows, rows), :], sem.at[ahead]).start()

    def finish():
        @pl.when(step == last)
        def _():
            for k in range(1, depth):
                other = (step + k) % depth
                _wait_token_gather(src_hbm, buf_ref.at[other], sem.at[other])

    return slot, issue, finish


def _moe_kernel(te_ref, first_ref, src_ref, x_hbm, *refs):
    w_refs, (o_ref, xb_ref, sem), wb_refs = refs[:6], refs[6:9], refs[9:]
    i = pl.program_id(0)
    slot, issue, finish = _gather_ring(src_ref, lambda k: first_ref[k], i, pl.num_programs(0) - 1,
                                       x_hbm, xb_ref, sem, XROWS)
    prev = jnp.maximum(i - 1, 0)
    for s in range(TOP_K):
        @pl.when((i == 0) | (te_ref[s, i] != te_ref[s, prev]))
        def _():
            for k in range(3):
                wb_refs[3 * s + k][...] = w_refs[3 * s + k][...].astype(BF16)

    xb = xb_ref.at[slot]
    x = jnp.concatenate([xb[pl.ds(c, TM, stride=XROWS), :] for c in range(D // 128)], axis=1).astype(BF16)
    per = TM // (3 * TOP_K)
    edges = [k * per for k in range(3 * TOP_K)] + [TM]
    for s in range(TOP_K):
        w1b, w3b, w2b = wb_refs[3 * s:3 * s + 3]
        issue(edges[3 * s], edges[3 * s + 1])
        h1 = jnp.dot(x, w1b[...], preferred_element_type=F32)
        issue(edges[3 * s + 1], edges[3 * s + 2])
        h3 = jnp.dot(x, w3b[...], preferred_element_type=F32)
        hh = (_silu(h1) * h3).astype(BF16)
        issue(edges[3 * s + 2], edges[3 * s + 3])
        y = jnp.dot(hh, w2b[...], preferred_element_type=F32)
        for c in range(D // 128):
            o_ref[s, pl.ds(c, TM, stride=YROWS), :] = y[:, 128 * c:128 * (c + 1)]
    finish()


def _moe(tile_exp, tile_first, order, u2x, w1, w3, w2, layer):
    mp = tile_exp.shape[1] * TM

    def wspec(s, k, n):
        return pl.BlockSpec((None, None, k, n), lambda i, te, tf, od: (layer, te[s, i], 0, 0),
                            pipeline_mode=pl.Buffered(1))

    w_specs, w_args, w_scratch = [], [], []
    for s in range(TOP_K):
        w_specs += [wspec(s, D, D_FF), wspec(s, D, D_FF), wspec(s, D_FF, D)]
        w_args += [w1, w3, w2]
        w_scratch += [pltpu.VMEM((D, D_FF), BF16), pltpu.VMEM((D, D_FF), BF16), pltpu.VMEM((D_FF, D), BF16)]
    return pl.pallas_call(
        _moe_kernel,
        grid_spec=pltpu.PrefetchScalarGridSpec(
            num_scalar_prefetch=3, grid=(mp // TM,),
            in_specs=[pl.BlockSpec(memory_space=pl.ANY)] + w_specs,
            out_specs=pl.BlockSpec((TOP_K, TM * YROWS, 128), lambda i, te, tf, od: (0, i, 0)),
            scratch_shapes=[pltpu.VMEM((2, TM * XROWS, 128), F32), pltpu.SemaphoreType.DMA((2,))] + w_scratch),
        out_shape=jax.ShapeDtypeStruct((TOP_K, mp * YROWS, 128), F32),
        compiler_params=_cp(("arbitrary",), 48 * 1024 * 1024),
        name="moe",
    )(tile_exp, tile_first, order, u2x, *w_args)


def _combine_kernel(pos_ref, y_hbm, g_ref, h_ref, mod_ref, lng_ref, lnb_ref, o_ref, yb_ref, sem, *, nt):
    step = pl.program_id(0) * nt + pl.program_id(1)
    slot, issue, finish = _gather_ring(pos_ref, lambda k: k * (TOP_K * TM), step, pl.num_programs(0) * nt - 1,
                                       y_hbm, yb_ref, sem, YROWS)
    yb = yb_ref.at[slot]
    g2 = mod_ref[0, 0, 5:6, :]
    nstrip = 4
    strip = TM // nstrip
    per = TOP_K * TM // nstrip
    for k in range(nstrip):
        r0 = k * strip
        issue(k * per, (k + 1) * per)
        gates = g_ref[r0:r0 + strip, :]
        y = functools.reduce(lambda a, c: a + c, [
            gates[:, s:s + 1] * jnp.concatenate(
                [yb[pl.ds((s * TM + r0) * YROWS + c, strip, stride=YROWS), :] for c in range(D // 128)], axis=1)
            for s in range(TOP_K)])
        o_ref[0, r0:r0 + strip, :] = (_ln(ALPHA * h_ref[0, r0:r0 + strip, :] + g2 * y) * lng_ref[0:1, :]
                                      + lnb_ref[0:1, :])
    finish()


def _combine(pos, y_sorted, gates, h1, modsel, lng, lnb, row_off):
    b, r, _ = h1.shape
    nt = r // TM
    full = lambda shape: pl.BlockSpec(shape, lambda i, j, p: (0,) * len(shape))
    return pl.pallas_call(
        functools.partial(_combine_kernel, nt=nt),
        grid_spec=pltpu.PrefetchScalarGridSpec(
            num_scalar_prefetch=1, grid=(b, nt),
            in_specs=[pl.BlockSpec(memory_space=pl.ANY),
                      pl.BlockSpec((TM, GW), lambda i, j, p: (i * nt + j, 0)),
                      pl.BlockSpec((1, TM, D), lambda i, j, p: (i, j, 0)),
                      pl.BlockSpec((1, 1, 6, D), lambda i, j, p: (i, jnp.minimum(j + row_off, 1), 0, 0)),
                      full((1, D)), full((1, D))],
            out_specs=pl.BlockSpec((1, TM, D), lambda i, j, p: (i, j, 0)),
            scratch_shapes=[pltpu.VMEM((3, TOP_K * TM * YROWS, 128), F32), pltpu.SemaphoreType.DMA((3,))]),
        out_shape=jax.ShapeDtypeStruct((b, r, D), F32),
        compiler_params=_cp(("arbitrary", "arbitrary")),
        name="moe_combine",
    )(pos, y_sorted, gates, h1, modsel, lng, lnb)


def _sort_by_class(cls):
    n = cls.shape[0]
    onehot = (cls[:, None] == jnp.arange(N_CLASS, dtype=jnp.int32)[None, :]).astype(jnp.int32)
    csum = jnp.cumsum(onehot, axis=0)
    counts = csum[-1]
    padded = ((counts + TM - 1) // TM) * TM
    ends = jnp.cumsum(padded)
    starts = ends - padded
    rank = jnp.sum(csum * onehot, axis=1) - 1
    pos = (starts[cls] + rank).astype(jnp.int32)
    _, order = lax.sort((cls, jnp.arange(n, dtype=jnp.int32)), num_keys=1, is_stable=True)
    order = jnp.concatenate([order, jnp.zeros((TM,), jnp.int32)])
    ntile = n // TM + N_CLASS
    tile_start = jnp.arange(ntile, dtype=jnp.int32) * TM
    tile_cls = jnp.minimum(jnp.sum((tile_start[:, None] >= ends[None, :]).astype(jnp.int32), axis=1), N_CLASS - 1)
    first = (jnp.cumsum(counts) - counts)[tile_cls] + tile_start - starts[tile_cls]
    return pos, order, tile_cls.astype(jnp.int32), jnp.minimum(first, n).astype(jnp.int32)


_PAIRS = np.array([(a, c) for a in range(E_PER) for c in range(a + 1, E_PER)], np.int32)


def _moe_layer(u2x, gates, cls, h1, modsel, experts, lng, lnb, row_off):
    pos, order, tile_cls, tile_first = _sort_by_class(cls.reshape(-1).astype(jnp.int32))
    tile_exp = (E_PER * (tile_cls // N_PAIR))[None, :] + jnp.asarray(_PAIRS).T[:, tile_cls % N_PAIR]
    y_sorted = _moe(tile_exp.astype(jnp.int32), tile_first, order, u2x, *experts)
    mp = y_sorted.shape[1] // YROWS
    pos2 = jnp.stack([pos, pos + mp]).reshape(TOP_K, -1, TM).transpose(1, 0, 2).reshape(-1)
    return _combine(pos2, y_sorted.reshape(TOP_K * mp * YROWS, 128), gates, h1, modsel, lng, lnb, row_off)


def _rope_tables(rc, t):
    half = A_HD // 4
    inv = ROPE_BASE ** (-jnp.arange(half, dtype=F32) / half)
    tpos = jnp.arange(t)
    rows = (tpos // GRID_W).astype(F32)
    cols = (tpos % GRID_W).astype(F32)

    def cs(pos):
        ang = pos[:, None] * inv[None, :]
        c, s = jnp.cos(ang), jnp.sin(ang)
        return jnp.concatenate([c, c], -1), jnp.concatenate([-s, s], -1)

    cr, sr = cs(rows)
    cc_, sc_ = cs(cols)
    cos_h = jnp.concatenate([cr, cc_], -1)
    sin_h = jnp.concatenate([sr, sc_], -1)
    one = jnp.ones((t, A_HD), F32)
    zero = jnp.zeros((t, A_HD), F32)
    lat = jnp.stack([jnp.concatenate([cos_h, cos_h], -1), jnp.concatenate([sin_h, sin_h], -1),
                     jnp.concatenate([cos_h, one], -1), jnp.concatenate([sin_h, zero], -1)])
    ctx = jnp.stack([jnp.ones((CTX, 128), F32), jnp.zeros((CTX, 128), F32)] * 2)
    return jnp.concatenate([ctx, lat], axis=1)


def _dft_tables(t):
    def cos_sin(num, den):
        ang = (2.0 * math.pi / den) * (num % den).astype(F32)
        return jnp.cos(ang), jnp.sin(ang)

    k = jnp.arange(t, dtype=jnp.int32)[:, None]
    ca, sa = cos_sin(k * jnp.arange(t // GRID_W, dtype=jnp.int32)[None, :], t // GRID_W)
    cb, sb = cos_sin(k * jnp.arange(GRID_W, dtype=jnp.int32)[None, :], t)
    ct = (ca[:, :, None] * cb[:, None, :] - sa[:, :, None] * sb[:, None, :]).reshape(t, t)
    st = (sa[:, :, None] * cb[:, None, :] + ca[:, :, None] * sb[:, None, :]).reshape(t, t)
    c = jnp.arange(F_GD, dtype=jnp.int32)
    cc, sc = cos_sin(c[:, None] * c[None, :], F_GD)
    eye = jnp.eye(F_GROUPS, dtype=F32)
    cs = jnp.concatenate([jnp.kron(eye, cc), jnp.kron(eye, sc)], axis=1)
    return cs.astype(BF16), ct.astype(BF16), st.astype(BF16)


def _pad_heads(w, width, padded):
    lead = w.shape[:-1]
    w = w.reshape(lead + (R_HEADS, width))
    w = jnp.pad(w, [(0, 0)] * len(lead) + [(0, 0), (0, padded - width)])
    return w.reshape(lead + (R_HEADS * padded,))


EVEN_SEGS = tuple([(128 * g, 128 * (g + 1), 0) for g in range(4)] + [(c, c + 512, None) for c in range(512, 3072, 512)]
                  + [(3072 + 128 * g, 3072 + 128 * (g + 1), 1) for g in range(2)])
ODD_W = 512 + 512 + 1024 + 1024 + 256 + 128
ODD_SEGS = tuple([(c, c + 512, None) for c in range(0, 3072, 512)] + [(3072, ODD_W, None)])


def _layer_even(h, modsel, tbl, w_in, sink, lb_logits, g_norm, w_out, lng, lnb, w_router, b_router, experts):
    aq, ak, av, ff, fb, hq, hi, hg = jnp.split(w_in, np.cumsum([512, 128, 128, 512, 512, 512, 512]).tolist(), axis=1)
    kv = [jnp.concatenate([ak[:, A_HD * j:A_HD * (j + 1)], av[:, A_HD * j:A_HD * (j + 1)]], axis=1)
          for j in range(A_KV)]
    w = jnp.concatenate([aq, ff, fb, hq, hi, hg] + kv, axis=1).astype(BF16)
    proj = _inproj(h, modsel, w, tbl, EVEN_SEGS)
    o_att = _attention(proj, sink, 3072 // (2 * A_HD * A_KV))
    lb = jnp.cumsum(jax.nn.softmax(lb_logits.astype(F32), axis=1), axis=1)[:, 0]
    o_f, o_b = _hgrn_scan(proj, lb, 3, 4, 1, 2)
    wo = w_out.astype(BF16)
    h1, u2x, gates, grp = _outproj(b_router, o_att, 0, 512, o_f, o_b, proj, 5, h, modsel, g_norm.reshape(1, -1),
                            wo[:512], wo[512:], lng[0:1], lnb[0:1], w_router, dv=128, dvp=128, row_off=0)
    return _moe_layer(u2x, gates, grp, h1, modsel, experts, lng[1:2], lnb[1:2], 0)


def _layer_odd(h, modsel, tbl, dft, w_in, gate_w, gate_b, g_norm, w_out, lng, lnb, w_router, b_router, experts, t):
    q, k, v, rf, rb, g, z = jnp.split(w_in, np.cumsum([384, 384, 768, 16, 16, 768]).tolist(), axis=1)
    r = jnp.pad(jnp.concatenate([rf, rb], axis=1), ((0, 0), (0, 96)))
    w = jnp.concatenate([_pad_heads(q, GLA_DK, DKP), _pad_heads(k, GLA_DK, DKP), _pad_heads(v, GLA_DV, GLA_DVP),
                         _pad_heads(g, GLA_DV, GLA_DVP), z, r], axis=1).astype(BF16)
    proj = _inproj(h, modsel, w, tbl, ODD_SEGS)
    gw = _pad_heads(gate_w, GLA_DK, DKP)
    gwf = jnp.pad(gw[0], ((0, 112), (0, 0)))
    gwb = jnp.pad(gw[1], ((16, 96), (0, 0)))
    gb = _pad_heads(gate_b, GLA_DK, DKP)
    o_f, o_b = _gla_scan(proj, gwf, gwb, gb, 0, 1, 1, 3328 // 128)
    cs, ct, st = dft
    four = _fourier(proj, 3072 // (F_GROUPS * F_GD), cs, ct, st, t)
    wo = w_out
    wr = jnp.pad(wo[:768].reshape(R_HEADS, GLA_DV, D), ((0, 0), (0, GLA_DVP - GLA_DV), (0, 0))).reshape(-1, D)
    h1, u2x, gates, grp = _outproj(b_router, four, 0, 256, o_f, o_b, proj, 2, h, modsel,
                            _pad_heads(g_norm, GLA_DV, GLA_DVP).reshape(1, -1),
                            wo[768:].astype(BF16), wr.astype(BF16), lng[0:1], lnb[0:1], w_router,
                            dv=GLA_DV, dvp=GLA_DVP, row_off=CTX // TM)
    return _moe_layer(u2x, gates, grp, h1, modsel, experts, lng[1:2], lnb[1:2], CTX // TM)


def kernel(x, c, ctx, c_ctx, w_ada, b_ada, ln_g, ln_b, w_in_even, attn_sink, hgrn_lb_logits, hgrn_norm, w_out_even, w_in_odd, gla_gate_w, gla_gate_b, gla_norm, w_out_odd, w_router, b_router, w_expert_gate, w_expert_up, w_expert_down):
    b, t, _ = x.shape
    rc = CTX + t
    assert ctx.shape[1] == CTX
    cc = jnp.zeros((16, D), F32).at[:b].set(c).at[b].set(c_ctx)
    mods = _ada_mods(cc, w_ada, b_ada).reshape(2, 16, 6, D)

    def modsel(l):
        return jnp.stack([jnp.broadcast_to(mods[l, b], (b, 6, D)), mods[l, :b]], axis=1)

    tbl = _rope_tables(rc, t)
    wrt = jnp.pad(w_router, ((0, 0), (0, GW - N_EXP)))
    experts = (w_expert_gate, w_expert_up, w_expert_down)
    h = _layer_even((ctx, x), modsel(0), tbl, w_in_even[0], attn_sink[0], hgrn_lb_logits, hgrn_norm[0], w_out_even[0],
                    ln_g[0], ln_b[0], wrt, b_router, experts + (0,))
    return _layer_odd(h, modsel(1), tbl, _dft_tables(t), w_in_odd[0], gla_gate_w[0], gla_gate_b[0], gla_norm[0],
                      w_out_odd[0], ln_g[1], ln_b[1], wrt, b_router, experts + (1,), t)
```

```python
import functools
import math

import numpy as np
import jax
import jax.numpy as jnp
from jax import lax
from jax.experimental import pallas as pl
from jax.experimental.pallas import tpu as pltpu

F32 = jnp.float32
BF16 = jnp.bfloat16
HIGHEST = lax.Precision.HIGHEST

D = 1024
CTX = 256
GRID_W = 64
LN_EPS = 1e-5
NEG = -1e30
ALPHA = 4.0 ** 0.25
ROPE_BASE = 10000.0

A_HEADS, A_KV, A_HD, A_BLK = 8, 2, 64, 128
R_HEADS = 4
DKP = 128
GLA_DK, GLA_DV, GLA_DVP = 96, 192, 256
GLA_TAU = 16.0
CHUNK = 64
N_EXP, N_GRP, E_PER, TOP_K = 16, 4, 4, 2
N_PAIR = E_PER * (E_PER - 1) // 2
N_CLASS = N_GRP * N_PAIR
D_FF = 512
F_GROUPS, F_GD = 4, 64

TM = 256
IN_SUB = 3
GW = 128
XROWS = 8
YROWS = 8
RING = 4
V7X_VMEM_LIMIT = 56 * 1024 * 1024


def _cp(sem, vmem=None):
    return pltpu.CompilerParams(dimension_semantics=sem, vmem_limit_bytes=vmem)


def _ln(x):
    mu = jnp.mean(x, axis=-1, keepdims=True)
    xc = x - mu
    var = jnp.mean(xc * xc, axis=-1, keepdims=True)
    return xc * lax.rsqrt(var + LN_EPS)


def _silu(x):
    return x * jax.nn.sigmoid(x)


def _split2(w, axis):
    hi = w.astype(BF16)
    return jnp.concatenate([hi, (w - hi.astype(F32)).astype(BF16)], axis=axis)


_NT = (((1,), (1,)), ((), ()))
_TN = (((0,), (0,)), ((), ()))


def _ada_kernel(c_ref, w_ref, b_ref, o_ref):
    s = _silu(c_ref[...])
    o_ref[0] = jnp.dot(s, w_ref[0], precision=HIGHEST, preferred_element_type=F32) + b_ref[0]


def _ada_mods(cc, w_ada, b_ada):
    depth, _, n = w_ada.shape
    tn = 1536
    return pl.pallas_call(
        _ada_kernel,
        grid=(depth, n // tn),
        in_specs=[pl.BlockSpec((16, D), lambda l, i: (0, 0)),
                  pl.BlockSpec((1, D, tn), lambda l, i: (l, 0, i)),
                  pl.BlockSpec((1, 1, tn), lambda l, i: (l, 0, i))],
        out_specs=pl.BlockSpec((1, 16, tn), lambda l, i: (l, 0, i)),
        out_shape=jax.ShapeDtypeStruct((depth, 16, n), F32),
        compiler_params=_cp(("arbitrary", "arbitrary"), 40 * 1024 * 1024),
        name="ada_mod",
    )(cc, w_ada, b_ada.reshape(depth, 1, n))


def _rows_of(h_refs):
    if len(h_refs) == 1:
        return h_refs[0][0]
    return jnp.where(pl.program_id(1) == 0, h_refs[0][0], h_refs[1][0])


def _row_specs(h, row_off=0):
    if isinstance(h, tuple):
        assert row_off == 0 and h[0].shape[1] == TM
        return [pl.BlockSpec((1, TM, D), lambda i, j: (i, 0, 0)),
                pl.BlockSpec((1, TM, D), lambda i, j: (i, jnp.maximum(j - 1, 0), 0))], list(h)
    return [pl.BlockSpec((1, TM, D), lambda i, j: (i, j + row_off, 0))], [h]


def _inproj_kernel(*refs, segs):
    *h_refs, mod0_ref, mod1_ref, w_ref, tbl_ref, o_ref = refs
    if len(h_refs) == 1:
        x = h_refs[0][0]
    else:
        ctx_ref, *x_refs = h_refs
        x = jnp.concatenate([jnp.where(pl.program_id(1) == 0, ctx_ref[0], x_refs[0][0])]
                            + [r[0] for r in x_refs[1:]], axis=0)
    xn = _ln(x)
    u = jnp.concatenate([xn[:TM] * (1.0 + mod0_ref[0, 0, 1:2, :]) + mod0_ref[0, 0, 0:1, :],
                         xn[TM:] * (1.0 + mod1_ref[0, 0, 1:2, :]) + mod1_ref[0, 0, 0:1, :]], axis=0).astype(BF16)
    lane = lax.broadcasted_iota(jnp.int32, (x.shape[0], 128), 1)
    low = (lane % 32) < 16
    for c0, c1, rope in segs:
        acc = jnp.dot(u, w_ref[:, c0:c1], preferred_element_type=F32)
        if rope is not None:
            partner = jnp.where(low, pltpu.roll(acc, 112, axis=1), pltpu.roll(acc, 16, axis=1))
            acc = acc * tbl_ref[2 * rope] + partner * tbl_ref[2 * rope + 1]
        o_ref[0, :, c0:c1] = acc.astype(BF16)


def _inproj(h, modsel, w, tbl, segs):
    b, rc = modsel.shape[0], tbl.shape[1]
    n = w.shape[1]
    tmi = IN_SUB * TM
    assert rc % tmi == 0
    if isinstance(h, tuple):
        assert h[0].shape[1] == TM
        h_specs = [pl.BlockSpec((1, TM, D), lambda i, j: (i, 0, 0))] + [
            pl.BlockSpec((1, TM, D), lambda i, j, k=k: (i, jnp.maximum(IN_SUB * j + k - 1, 0), 0))
            for k in range(IN_SUB)]
        h_args = [h[0]] + [h[1]] * IN_SUB
    else:
        h_specs, h_args = [pl.BlockSpec((1, tmi, D), lambda i, j: (i, j, 0))], [h]
    return pl.pallas_call(
        functools.partial(_inproj_kernel, segs=segs),
        grid=(b, rc // tmi),
        in_specs=h_specs + [pl.BlockSpec((1, 1, 6, D), lambda i, j: (i, jnp.minimum(j, 1), 0, 0)),
                            pl.BlockSpec((1, 1, 6, D), lambda i, j: (i, 1, 0, 0)),
                            pl.BlockSpec((D, n), lambda i, j: (0, 0)),
                            pl.BlockSpec((4, tmi, 128), lambda i, j: (0, j, 0))],
        out_specs=pl.BlockSpec((1, tmi, n), lambda i, j: (i, j, 0)),
        out_shape=jax.ShapeDtypeStruct((b, rc, n), BF16),
        compiler_params=_cp(("parallel", "arbitrary"), V7X_VMEM_LIMIT),
        name="inproj",
    )(*h_args, modsel, modsel, w, tbl)


def _attn_kernel(sink_ref, q_ref, kvc_ref, kvp_ref, kvm_ref, kvn_ref, bias_ref, o_ref):
    blk = pl.program_id(1)
    g_per = A_HEADS // A_KV
    nq = g_per * A_BLK
    q = q_ref[0] * (A_HD ** -0.5)
    rgrp = lax.broadcasted_iota(jnp.int32, (nq, 1), 0) // A_BLK
    q4, sink_col = [], []
    for h in range(A_KV):
        q4.append(jnp.concatenate([q[:, A_HD * (g_per * h + g):A_HD * (g_per * h + g + 1)] for g in range(g_per)],
                                  axis=0))
        col = jnp.zeros((nq, 1), F32)
        for g in range(g_per):
            col = jnp.where(rgrp == g, sink_ref[h * g_per + g], col)
        sink_col.append(col)

    def attend(kv, bias):
        s = [lax.dot_general(q4[h], kv[:, 2 * A_HD * h:2 * A_HD * h + A_HD], _NT, preferred_element_type=F32)
             for h in range(A_KV)]
        if bias is not None:
            bias4 = jnp.concatenate([bias] * g_per, axis=0)
            s = [x + bias4 for x in s]
        m = [jnp.maximum(jnp.max(s[h], axis=-1, keepdims=True), sink_col[h]) for h in range(A_KV)]
        p = [jnp.exp(s[h] - m[h]) for h in range(A_KV)]
        den = [jnp.sum(p[h], axis=-1, keepdims=True) + jnp.exp(sink_col[h] - m[h]) for h in range(A_KV)]
        o = [jnp.dot(p[h].astype(BF16), kv[:, 2 * A_HD * h + A_HD:2 * A_HD * (h + 1)], preferred_element_type=F32)
             / den[h] for h in range(A_KV)]
        o_ref[0] = jnp.concatenate([o[h][A_BLK * g:A_BLK * (g + 1), :] for h in range(A_KV) for g in range(g_per)],
                                   axis=1).astype(BF16)

    @pl.when(blk < CTX // A_BLK)
    def _():
        attend(kvc_ref[0], None)

    @pl.when(blk >= CTX // A_BLK)
    def _():
        attend(jnp.concatenate([kvc_ref[0], kvp_ref[0], kvm_ref[0], kvn_ref[0]], axis=0), bias_ref[0])


def _attn_bias():
    nk = CTX + 3 * A_BLK
    qi = jnp.arange(A_BLK, dtype=jnp.int32)[:, None]
    col = jnp.arange(nk, dtype=jnp.int32)[None, :]
    rel = col - (CTX + A_BLK)
    dist = qi - rel

    def one(lo, hi):
        in_win = (dist <= A_BLK) & (dist >= -A_BLK) & (rel >= lo) & (rel < hi)
        return jnp.where((col < CTX) | in_win, 0.0, NEG).astype(F32)

    return jnp.stack([one(0, 2 * A_BLK), one(-A_BLK, 2 * A_BLK), one(-A_BLK, A_BLK)])


def _attention(proj, sink, kv_blk):
    b, rc, _ = proj.shape
    nb = rc // A_BLK
    c_b = CTX // A_BLK
    assert nb - c_b >= 2
    kw = 2 * A_HD * A_KV

    def band(off):
        return lambda i, j: (i, jnp.clip(j + off, c_b, nb - 1), kv_blk)

    def which_bias(i, j):
        return (jnp.where(j <= c_b, 0, jnp.where(j == nb - 1, 2, 1)), 0, 0)

    return pl.pallas_call(
        _attn_kernel,
        grid=(b, nb),
        in_specs=[pl.BlockSpec(memory_space=pltpu.SMEM),
                  pl.BlockSpec((1, A_BLK, A_HEADS * A_HD), lambda i, j: (i, j, 0)),
                  pl.BlockSpec((1, CTX, kw), lambda i, j: (i, 0, kv_blk)),
                  pl.BlockSpec((1, A_BLK, kw), band(-1)),
                  pl.BlockSpec((1, A_BLK, kw), band(0)),
                  pl.BlockSpec((1, A_BLK, kw), band(1)),
                  pl.BlockSpec((1, A_BLK, CTX + 3 * A_BLK), which_bias)],
        out_specs=pl.BlockSpec((1, A_BLK, A_HEADS * A_HD), lambda i, j: (i, j, 0)),
        out_shape=jax.ShapeDtypeStruct((b, rc, A_HEADS * A_HD), BF16),
        compiler_params=_cp(("parallel", "arbitrary")),
        name="win_attn",
    )(sink, proj, proj, proj, proj, proj, _attn_bias())


def _scan_blocks(dirs, *, dvp):
    r_i = lax.broadcasted_iota(jnp.int32, (CHUNK, CHUNK), 0)
    c_i = lax.broadcasted_iota(jnp.int32, (CHUNK, CHUNK), 1)
    units = []
    for q, k, v, logf, s_ref, o_ref, reverse in dirs:
        nchunk = q.shape[0] // CHUNK
        tri = (r_i <= c_i) if reverse else (r_i >= c_i)
        tri3 = jnp.concatenate([jnp.where(tri, 1.0, 0.0).astype(BF16)] * 3, axis=1)
        for c in (range(nchunk - 1, -1, -1) if reverse else range(nchunk)):
            sl = slice(c * CHUNK, (c + 1) * CHUNK)
            units.append(dict(q=q[sl], k=k[sl], v=v[sl].astype(BF16), lf=logf[sl], sl=sl, tri=tri, tri3=tri3,
                              i_tot=0 if reverse else CHUNK - 1,
                              i_ref=CHUNK // 2 - 1 if reverse else CHUNK // 2, s_ref=s_ref, o_ref=o_ref))
    heads = [(slice(h * DKP, (h + 1) * DKP), slice(h * dvp, (h + 1) * dvp)) for h in range(R_HEADS)]
    for u in units:
        lf = u["lf"]
        hi = lf.astype(BF16)
        r1 = lf - hi.astype(F32)
        mid = r1.astype(BF16)
        lo = (r1 - mid.astype(F32)).astype(BF16)
        u["bsum"] = jnp.dot(u["tri3"], jnp.concatenate([hi, mid, lo], axis=0), preferred_element_type=F32)
    for u in units:
        bsum = u["bsum"]
        tot = bsum[u["i_tot"]:u["i_tot"] + 1, :]
        ref = bsum[u["i_ref"]:u["i_ref"] + 1, :]
        qg = u["q"] * jnp.exp(bsum - ref)
        kg = u["k"] * jnp.exp(ref - bsum)
        u["qb"] = (qg * jnp.exp(ref)).astype(BF16)
        u["kd"] = (kg * jnp.exp(tot - ref)).astype(BF16)
        u["qg"] = qg.astype(BF16)
        u["kg"] = kg.astype(BF16)
        u["dec"] = jnp.exp(tot)
    for u in units:
        u["att"] = [lax.dot_general(u["qg"][:, ks], u["kg"][:, ks], _NT, preferred_element_type=F32)
                    for ks, _ in heads]
        u["ds"] = [lax.dot_general(u["v"][:, vs], u["kd"][:, ks], _TN, preferred_element_type=F32)
                   for ks, vs in heads]
    for u in units:
        u["oi"] = [jnp.dot(jnp.where(u["tri"], a, 0.0).astype(BF16), u["v"][:, vs], preferred_element_type=F32)
                   for a, (_, vs) in zip(u["att"], heads)]
    state = {}
    for u in units:
        key = id(u["s_ref"])
        if key not in state:
            state[key] = [u["s_ref"][h] for h in range(R_HEADS)]
        st = state[key]
        outs = []
        for h, (ks, _) in enumerate(heads):
            outs.append(u["oi"][h] + lax.dot_general(u["qb"][:, ks], st[h].astype(BF16), _NT,
                                                     preferred_element_type=F32))
            st[h] = st[h] * u["dec"][:, ks] + u["ds"][h]
        u["o_ref"][0, u["sl"], :] = jnp.concatenate(outs, axis=1).astype(BF16)
    for _, _, _, _, s_ref, _, _ in dirs:
        for h in range(R_HEADS):
            s_ref[h] = state[id(s_ref)][h]


def _hgrn_scan_kernel(lb_ref, qf_ref, vf_ref, zf_ref, qb_ref, vb_ref, zb_ref, of_ref, ob_ref, sf_ref, sb_ref):
    @pl.when(pl.program_id(1) == 0)
    def _():
        sf_ref[...] = jnp.zeros_like(sf_ref)
        sb_ref[...] = jnp.zeros_like(sb_ref)

    def gates(z, lb):
        f = lb + (1.0 - lb) * jax.nn.sigmoid(z.astype(F32))
        return 1.0 - f, jnp.log(f)

    kf, lf = gates(zf_ref[0], lb_ref[0:1, :])
    kb, lbw = gates(zb_ref[0], lb_ref[1:2, :])
    _scan_blocks([(qf_ref[0], kf, vf_ref[0], lf, sf_ref, of_ref, False),
                  (qb_ref[0], kb, vb_ref[0], lbw, sb_ref, ob_ref, True)], dvp=128)


def _bwd_blk(j, nt):
    return jnp.where(j == 0, 0, nt - j)


def _hgrn_scan(proj, lb, cq, cv, cf, cb):
    b, rc, _ = proj.shape
    nt = rc // TM
    w = R_HEADS * DKP

    def fw(col):
        return pl.BlockSpec((1, TM, w), lambda i, j: (i, j, col))

    def bw(col):
        return pl.BlockSpec((1, TM, w), lambda i, j: (i, _bwd_blk(j, nt), col))

    o_sd = jax.ShapeDtypeStruct((b, rc, w), BF16)
    return pl.pallas_call(
        _hgrn_scan_kernel,
        grid=(b, nt),
        in_specs=[pl.BlockSpec((2, w), lambda i, j: (0, 0)),
                  fw(cq), fw(cv), fw(cf), bw(cq), bw(cv), bw(cb)],
        out_specs=[pl.BlockSpec((1, TM, w), lambda i, j: (i, j, 0)),
                   pl.BlockSpec((1, TM, w), lambda i, j: (i, _bwd_blk(j, nt), 0))],
        out_shape=[o_sd, o_sd],
        scratch_shapes=[pltpu.VMEM((R_HEADS, 128, DKP), F32), pltpu.VMEM((R_HEADS, 128, DKP), F32)],
        compiler_params=_cp(("parallel", "arbitrary"), 40 * 1024 * 1024),
        name="hgrn_scan",
    )(lb, proj, proj, proj, proj, proj, proj)


def _gla_scan_kernel(gwf_ref, gwb_ref, gb_ref, qf_ref, kf_ref, vf_ref, rf_ref, qb_ref, kb_ref, vb_ref, rb_ref,
                     of_ref, ob_ref, sf_ref, sb_ref):
    @pl.when(pl.program_id(1) == 0)
    def _():
        sf_ref[...] = jnp.zeros_like(sf_ref)
        sb_ref[...] = jnp.zeros_like(sb_ref)

    def decay(r, gw, gb):
        zz = jnp.dot(jnp.concatenate([r, r], axis=1), _split2(gw, 0), preferred_element_type=F32) + gb
        return (jnp.minimum(zz, 0.0) - jnp.log(1.0 + jnp.exp(-jnp.abs(zz)))) * (1.0 / GLA_TAU)

    qs = GLA_DK ** -0.5
    _scan_blocks([(qf_ref[0].astype(F32) * qs, kf_ref[0], vf_ref[0],
                   decay(rf_ref[0], gwf_ref[...], gb_ref[0:1, :]), sf_ref, of_ref, False),
                  (qb_ref[0].astype(F32) * qs, kb_ref[0], vb_ref[0],
                   decay(rb_ref[0], gwb_ref[...], gb_ref[1:2, :]), sb_ref, ob_ref, True)], dvp=GLA_DVP)


def _gla_scan(proj, gwf, gwb, gb, cq, ck, cv, cr):
    b, rc, _ = proj.shape
    nt = rc // TM
    wk = R_HEADS * DKP
    wv = R_HEADS * GLA_DVP

    def spec(width, col, back):
        if back:
            return pl.BlockSpec((1, TM, width), lambda i, j: (i, _bwd_blk(j, nt), col))
        return pl.BlockSpec((1, TM, width), lambda i, j: (i, j, col))

    o_sd = jax.ShapeDtypeStruct((b, rc, wv), BF16)
    full = lambda shape: pl.BlockSpec(shape, lambda i, j: (0,) * len(shape))
    return pl.pallas_call(
        _gla_scan_kernel,
        grid=(b, nt),
        in_specs=[full((128, wk)), full((128, wk)), full((2, wk)),
                  spec(wk, cq, False), spec(wk, ck, False), spec(wv, cv, False), spec(128, cr, False),
                  spec(wk, cq, True), spec(wk, ck, True), spec(wv, cv, True), spec(128, cr, True)],
        out_specs=[spec(wv, 0, False), spec(wv, 0, True)],
        out_shape=[o_sd, o_sd],
        scratch_shapes=[pltpu.VMEM((R_HEADS, GLA_DVP, DKP), F32), pltpu.VMEM((R_HEADS, GLA_DVP, DKP), F32)],
        compiler_params=_cp(("parallel", "arbitrary"), 48 * 1024 * 1024),
        name="gla_scan",
    )(gwf, gwb, gb, proj, proj, proj, proj, proj, proj, proj, proj)


def _fourier_kernel(z_ref, cs_ref, ct_ref, st_ref, o_ref, *, scale):
    w = F_GROUPS * F_GD
    zz = jnp.dot(z_ref[0, CTX:, :], cs_ref[...], preferred_element_type=F32).astype(BF16)
    y = jnp.dot(ct_ref[...], zz[:, :w], preferred_element_type=F32)
    y = y - jnp.dot(st_ref[...], zz[:, w:], preferred_element_type=F32)
    o_ref[0] = (y * scale).astype(BF16)


def _fourier(proj, zcol, cs, ct, st, t):
    b, rc, _ = proj.shape
    w = F_GROUPS * F_GD
    const = lambda shape: pl.BlockSpec(shape, lambda i: (0, 0), pipeline_mode=pl.Buffered(1))
    return pl.pallas_call(
        functools.partial(_fourier_kernel, scale=1.0 / math.sqrt(t * F_GD)),
        grid=(b,),
        in_specs=[pl.BlockSpec((1, rc, w), lambda i: (i, 0, zcol)), const((w, 2 * w)), const((t, t)), const((t, t))],
        out_specs=pl.BlockSpec((1, t, w), lambda i: (i, 0, 0)),
        out_shape=jax.ShapeDtypeStruct((b, t, w), BF16),
        compiler_params=_cp(("arbitrary",), 48 * 1024 * 1024),
        name="fourier",
    )(proj, cs, ct, st)


def _route(logits, rb_ref):
    lt = logits.T
    l = [lt[e:e + 1, :] for e in range(N_EXP)]
    m = functools.reduce(jnp.maximum, l)
    ex = [jnp.exp(v - m) for v in l]
    den = functools.reduce(lambda a, c: a + c, ex)
    p = [v / den for v in ex]
    sel = [p[e] + rb_ref[e] for e in range(N_EXP)]
    g_score, g_gate, g_pick = [], [], []
    for g in range(N_GRP):
        s = sel[E_PER * g:E_PER * (g + 1)]
        chosen = []
        for i in range(E_PER):
            rank = jnp.zeros_like(s[i])
            for j in range(E_PER):
                if j == i:
                    continue
                ahead = (s[j] >= s[i]) if j < i else (s[j] > s[i])
                rank = rank + jnp.where(ahead, 1.0, 0.0)
            chosen.append(rank < 2.0)
        g_score.append(functools.reduce(lambda a, c: a + c,
                                        [jnp.where(chosen[i], s[i], 0.0) for i in range(E_PER)]))
        g_gate.append([jnp.where(chosen[i], p[E_PER * g + i], 0.0) for i in range(E_PER)])
        g_pick.append([jnp.where(chosen[i], 1.0, 0.0) for i in range(E_PER)])
    best = jnp.zeros_like(g_score[0])
    best_s = g_score[0]
    for g in range(1, N_GRP):
        upd = g_score[g] > best_s
        best = jnp.where(upd, float(g), best)
        best_s = jnp.where(upd, g_score[g], best_s)
    gate, pick = [], []
    for i in range(E_PER):
        gi, ci = g_gate[0][i], g_pick[0][i]
        for g in range(1, N_GRP):
            gi = jnp.where(best == float(g), g_gate[g][i], gi)
            ci = jnp.where(best == float(g), g_pick[g][i], ci)
        gate.append(gi)
        pick.append(ci)
    wsum = functools.reduce(lambda a, c: a + c, gate)
    lo = functools.reduce(jnp.minimum, [jnp.where(pick[i] > 0.5, float(i), float(E_PER)) for i in range(E_PER)])
    hi = functools.reduce(jnp.maximum, [jnp.where(pick[i] > 0.5, float(i), -1.0) for i in range(E_PER)])
    pair = jnp.where(lo == 0.0, hi - 1.0, jnp.where(lo == 1.0, hi + 1.0, float(N_PAIR - 1)))
    g_lo = functools.reduce(lambda a, c: a + c, [jnp.where(lo == float(i), gate[i], 0.0) for i in range(E_PER)])
    g_hi = functools.reduce(lambda a, c: a + c, [jnp.where(hi == float(i), gate[i], 0.0) for i in range(E_PER)])
    row = lax.broadcasted_iota(jnp.int32, (GW, lt.shape[1]), 0)
    side = jnp.where(row == 0, g_lo / wsum, jnp.where(row == 1, g_hi / wsum, 0.0))
    return side.T, best * float(N_PAIR) + pair


def _outproj_kernel(rb_ref, a_ref, of_ref, ob_ref, gt_ref, *refs, dvp, dv):
    *h_refs, mod_ref, gn_ref, wa_ref, wr_ref, lng_ref, lnb_ref, wrt_ref, h1_ref, u2_ref, gate_ref, grp_ref = refs
    o = of_ref[0].astype(F32) + ob_ref[0].astype(F32)
    parts = []
    for hh in range(R_HEADS):
        sl = slice(hh * dvp, (hh + 1) * dvp)
        oh = o[:, sl]
        ms = jnp.sum(oh * oh, axis=-1, keepdims=True) * (1.0 / dv)
        parts.append(oh * lax.rsqrt(ms + LN_EPS))
    rec = jnp.concatenate(parts, axis=1) * gn_ref[...] * _silu(gt_ref[0].astype(F32))
    y = jnp.dot(a_ref[0], wa_ref[...], preferred_element_type=F32)
    y = y + jnp.dot(rec.astype(BF16), wr_ref[...], preferred_element_type=F32)
    g1 = mod_ref[0, 0, 2:3, :]
    h1 = _ln(ALPHA * _rows_of(h_refs) + g1 * y) * lng_ref[0:1, :] + lnb_ref[0:1, :]
    h1_ref[0] = h1
    u2 = _ln(h1) * (1.0 + mod_ref[0, 0, 4:5, :]) + mod_ref[0, 0, 3:4, :]
    lg = jnp.dot(_split2(u2, 0), _split2(wrt_ref[...], 1), preferred_element_type=F32)
    lg = lg[:TM] + lg[TM:]
    logits = lg[:, :GW] + lg[:, GW:]
    side, best = _route(logits, rb_ref)
    grp_ref[0, 0] = best
    gate_ref[...] = side
    for c in range(XROWS):
        u2_ref[pl.ds(c, TM, stride=XROWS), :] = u2[:, 128 * c:128 * (c + 1)]


def _outproj(b_router, a, a_col, a_w, o_f, o_b, proj, g_col, h, modsel, gnorm, wa, wr, lng, lnb, wrt, *,
             dv, dvp, row_off):
    b, rc, _ = proj.shape
    nt = rc // TM - row_off
    wrec = R_HEADS * dvp
    full = lambda shape: pl.BlockSpec(shape, lambda i, j: (0,) * len(shape))
    h_specs, h_args = _row_specs(h, row_off)
    return pl.pallas_call(
        functools.partial(_outproj_kernel, dvp=dvp, dv=dv),
        grid=(b, nt),
        in_specs=[pl.BlockSpec(memory_space=pltpu.SMEM),
                  pl.BlockSpec((1, TM, a_w), lambda i, j: (i, j + (row_off if a.shape[1] == rc else 0), a_col)),
                  pl.BlockSpec((1, TM, wrec), lambda i, j: (i, j + row_off, 0)),
                  pl.BlockSpec((1, TM, wrec), lambda i, j: (i, j + row_off, 0)),
                  pl.BlockSpec((1, TM, wrec), lambda i, j: (i, j + row_off, g_col))] + h_specs + [
                  pl.BlockSpec((1, 1, 6, D), lambda i, j: (i, jnp.minimum(j + row_off, 1), 0, 0)),
                  full((1, wrec)), full((a_w, D)), full((wrec, D)), full((1, D)), full((1, D)), full((D, GW))],
        out_specs=[pl.BlockSpec((1, TM, D), lambda i, j: (i, j, 0)),
                   pl.BlockSpec((TM * XROWS, 128), lambda i, j: (i * nt + j, 0)),
                   pl.BlockSpec((TM, GW), lambda i, j: (i * nt + j, 0)),
                   pl.BlockSpec((1, 1, 1, TM), lambda i, j: (i, j, 0, 0))],
        out_shape=[jax.ShapeDtypeStruct((b, nt * TM, D), F32),
                   jax.ShapeDtypeStruct((b * nt * TM * XROWS, 128), F32),
                   jax.ShapeDtypeStruct((b * nt * TM, GW), F32),
                   jax.ShapeDtypeStruct((b, nt, 1, TM), F32)],
        compiler_params=_cp(("parallel", "arbitrary"), V7X_VMEM_LIMIT),
        name="outproj",
    )(b_router, a, o_f, o_b, proj, *h_args, modsel, gnorm, wa, wr, lng, lnb, wrt)


def _start_token_gather(idx_ref, base, src_hbm, dst_ref, sem, rows):
    def body(r, carry):
        t = idx_ref[base + r]
        pltpu.make_async_copy(src_hbm.at[pl.ds(pl.multiple_of(t * rows, rows), rows), :],
                              dst_ref.at[pl.ds(pl.multiple_of(r * rows, rows), rows), :], sem).start()
        return carry

    lax.fori_loop(0, dst_ref.shape[0] // rows, body, 0, unroll=8)


def _wait_token_gather(src_hbm, dst_ref, sem):
    pltpu.make_async_copy(src_hbm.at[pl.ds(0, dst_ref.shape[0]), :], dst_ref, sem).wait()


def _gather_ring(idx_ref, first_of, step, last, src_hbm, buf_ref, sem, rows):
    depth = buf_ref.shape[0]
    slot = step % depth

    @pl.when(step == 0)
    def _():
        for k in range(depth - 1):
            _start_token_gather(idx_ref, first_of(jnp.minimum(k, last)), src_hbm, buf_ref.at[k], sem.at[k], rows)

    _wait_token_gather(src_hbm, buf_ref.at[slot], sem.at[slot])
    ahead = (step + depth - 1) % depth
    nxt = first_of(jnp.minimum(step + depth - 1, last))

    def issue(r0, r1):
        for r in range(r0, r1):
            t = idx_ref[nxt + r]
            pltpu.make_async_copy(src_hbm.at[pl.ds(pl.multiple_of(t * rows, rows), rows), :],
                                  buf_ref.at[ahead, pl.ds(r * rows, rows), :], sem.at[ahead]).start()

    def finish():
        @pl.when(step == last)
        def _():
            for k in range(1, depth):
                other = (step + k) % depth
                _wait_token_gather(src_hbm, buf_ref.at[other], sem.at[other])

    return slot, issue, finish


def _moe_kernel(te_ref, first_ref, src_ref, x_hbm, *refs):
    w_refs, (o_ref, xb_ref, sem), wb_refs = refs[:6], refs[6:9], refs[9:]
    i = pl.program_id(0)
    slot, issue, finish = _gather_ring(src_ref, lambda k: first_ref[k], i, pl.num_programs(0) - 1,
                                       x_hbm, xb_ref, sem, XROWS)
    prev = jnp.maximum(i - 1, 0)
    for s in range(TOP_K):
        @pl.when((i == 0) | (te_ref[s, i] != te_ref[s, prev]))
        def _():
            for k in range(3):
                wb_refs[3 * s + k][...] = w_refs[3 * s + k][...].astype(BF16)

    xb = xb_ref.at[slot]
    x = jnp.concatenate([xb[pl.ds(c, TM, stride=XROWS), :] for c in range(D // 128)], axis=1).astype(BF16)
    per = TM // (3 * TOP_K)
    edges = [k * per for k in range(3 * TOP_K)] + [TM]
    for s in range(TOP_K):
        w1b, w3b, w2b = wb_refs[3 * s:3 * s + 3]
        issue(edges[3 * s], edges[3 * s + 1])
        h1 = jnp.dot(x, w1b[...], preferred_element_type=F32)
        issue(edges[3 * s + 1], edges[3 * s + 2])
        h3 = jnp.dot(x, w3b[...], preferred_element_type=F32)
        hh = (_silu(h1) * h3).astype(BF16)
        issue(edges[3 * s + 2], edges[3 * s + 3])
        y = jnp.dot(hh, w2b[...], preferred_element_type=F32)
        for c in range(D // 128):
            o_ref[s, pl.ds(c, TM, stride=YROWS), :] = y[:, 128 * c:128 * (c + 1)]
    finish()


def _moe(tile_exp, tile_first, order, u2x, w1, w3, w2, layer):
    mp = tile_exp.shape[1] * TM

    def wspec(s, k, n):
        return pl.BlockSpec((None, None, k, n), lambda i, te, tf, od: (layer, te[s, i], 0, 0),
                            pipeline_mode=pl.Buffered(1))

    w_specs, w_args, w_scratch = [], [], []
    for s in range(TOP_K):
        w_specs += [wspec(s, D, D_FF), wspec(s, D, D_FF), wspec(s, D_FF, D)]
        w_args += [w1, w3, w2]
        w_scratch += [pltpu.VMEM((D, D_FF), BF16), pltpu.VMEM((D, D_FF), BF16), pltpu.VMEM((D_FF, D), BF16)]
    return pl.pallas_call(
        _moe_kernel,
        grid_spec=pltpu.PrefetchScalarGridSpec(
            num_scalar_prefetch=3, grid=(mp // TM,),
            in_specs=[pl.BlockSpec(memory_space=pl.ANY)] + w_specs,
            out_specs=pl.BlockSpec((TOP_K, TM * YROWS, 128), lambda i, te, tf, od: (0, i, 0)),
            scratch_shapes=[pltpu.VMEM((RING, TM * XROWS, 128), F32), pltpu.SemaphoreType.DMA((RING,))] + w_scratch),
        out_shape=jax.ShapeDtypeStruct((TOP_K, mp * YROWS, 128), F32),
        compiler_params=_cp(("arbitrary",), 48 * 1024 * 1024),
        name="moe",
    )(tile_exp, tile_first, order, u2x, *w_args)


def _combine_kernel(pos_ref, y_hbm, g_ref, h_ref, mod_ref, lng_ref, lnb_ref, o_ref, yb_ref, sem, *, nt):
    step = pl.program_id(0) * nt + pl.program_id(1)
    slot, issue, finish = _gather_ring(pos_ref, lambda k: k * (TOP_K * TM), step, pl.num_programs(0) * nt - 1,
                                       y_hbm, yb_ref, sem, YROWS)
    yb = yb_ref.at[slot]
    g2 = mod_ref[0, 0, 5:6, :]
    nstrip = 4
    strip = TM // nstrip
    per = TOP_K * TM // nstrip
    for k in range(nstrip):
        r0 = k * strip
        issue(k * per, (k + 1) * per)
        gates = g_ref[r0:r0 + strip, :]
        y = functools.reduce(lambda a, c: a + c, [
            gates[:, s:s + 1] * jnp.concatenate(
                [yb[pl.ds((s * TM + r0) * YROWS + c, strip, stride=YROWS), :] for c in range(D // 128)], axis=1)
            for s in range(TOP_K)])
        o_ref[0, r0:r0 + strip, :] = (_ln(ALPHA * h_ref[0, r0:r0 + strip, :] + g2 * y) * lng_ref[0:1, :]
                                      + lnb_ref[0:1, :])
    finish()


def _combine(pos, y_sorted, gates, h1, modsel, lng, lnb, row_off):
    b, r, _ = h1.shape
    nt = r // TM
    full = lambda shape: pl.BlockSpec(shape, lambda i, j, p: (0,) * len(shape))
    return pl.pallas_call(
        functools.partial(_combine_kernel, nt=nt),
        grid_spec=pltpu.PrefetchScalarGridSpec(
            num_scalar_prefetch=1, grid=(b, nt),
            in_specs=[pl.BlockSpec(memory_space=pl.ANY),
                      pl.BlockSpec((TM, GW), lambda i, j, p: (i * nt + j, 0)),
                      pl.BlockSpec((1, TM, D), lambda i, j, p: (i, j, 0)),
                      pl.BlockSpec((1, 1, 6, D), lambda i, j, p: (i, jnp.minimum(j + row_off, 1), 0, 0)),
                      full((1, D)), full((1, D))],
            out_specs=pl.BlockSpec((1, TM, D), lambda i, j, p: (i, j, 0)),
            scratch_shapes=[pltpu.VMEM((RING, TOP_K * TM * YROWS, 128), F32), pltpu.SemaphoreType.DMA((RING,))]),
        out_shape=jax.ShapeDtypeStruct((b, r, D), F32),
        compiler_params=_cp(("arbitrary", "arbitrary")),
        name="moe_combine",
    )(pos, y_sorted, gates, h1, modsel, lng, lnb)


def _sort_by_class(cls):
    n = cls.shape[0]
    onehot = (cls[:, None] == jnp.arange(N_CLASS, dtype=jnp.int32)[None, :]).astype(jnp.int32)
    csum = jnp.cumsum(onehot, axis=0)
    counts = csum[-1]
    padded = ((counts + TM - 1) // TM) * TM
    ends = jnp.cumsum(padded)
    starts = ends - padded
    rank = jnp.sum(csum * onehot, axis=1) - 1
    pos = (starts[cls] + rank).astype(jnp.int32)
    _, order = lax.sort((cls, jnp.arange(n, dtype=jnp.int32)), num_keys=1, is_stable=True)
    order = jnp.concatenate([order, jnp.zeros((TM,), jnp.int32)])
    ntile = n // TM + N_CLASS
    tile_start = jnp.arange(ntile, dtype=jnp.int32) * TM
    tile_cls = jnp.minimum(jnp.sum((tile_start[:, None] >= ends[None, :]).astype(jnp.int32), axis=1), N_CLASS - 1)
    first = (jnp.cumsum(counts) - counts)[tile_cls] + tile_start - starts[tile_cls]
    return pos, order, tile_cls.astype(jnp.int32), jnp.minimum(first, n).astype(jnp.int32)


_PAIRS = np.array([(a, c) for a in range(E_PER) for c in range(a + 1, E_PER)], np.int32)


def _moe_layer(u2x, gates, cls, h1, modsel, experts, lng, lnb, row_off):
    pos, order, tile_cls, tile_first = _sort_by_class(cls.reshape(-1).astype(jnp.int32))
    tile_exp = (E_PER * (tile_cls // N_PAIR))[None, :] + jnp.asarray(_PAIRS).T[:, tile_cls % N_PAIR]
    y_sorted = _moe(tile_exp.astype(jnp.int32), tile_first, order, u2x, *experts)
    mp = y_sorted.shape[1] // YROWS
    pos2 = jnp.stack([pos, pos + mp]).reshape(TOP_K, -1, TM).transpose(1, 0, 2).reshape(-1)
    return _combine(pos2, y_sorted.reshape(TOP_K * mp * YROWS, 128), gates, h1, modsel, lng, lnb, row_off)


def _rope_tables(rc, t):
    half = A_HD // 4
    inv = ROPE_BASE ** (-jnp.arange(half, dtype=F32) / half)
    tpos = jnp.arange(t)
    rows = (tpos // GRID_W).astype(F32)
    cols = (tpos % GRID_W).astype(F32)

    def cs(pos):
        ang = pos[:, None] * inv[None, :]
        c, s = jnp.cos(ang), jnp.sin(ang)
        return jnp.concatenate([c, c], -1), jnp.concatenate([-s, s], -1)

    cr, sr = cs(rows)
    cc_, sc_ = cs(cols)
    cos_h = jnp.concatenate([cr, cc_], -1)
    sin_h = jnp.concatenate([sr, sc_], -1)
    one = jnp.ones((t, A_HD), F32)
    zero = jnp.zeros((t, A_HD), F32)
    lat = jnp.stack([jnp.concatenate([cos_h, cos_h], -1), jnp.concatenate([sin_h, sin_h], -1),
                     jnp.concatenate([cos_h, one], -1), jnp.concatenate([sin_h, zero], -1)])
    ctx = jnp.stack([jnp.ones((CTX, 128), F32), jnp.zeros((CTX, 128), F32)] * 2)
    return jnp.concatenate([ctx, lat], axis=1)


def _dft_tables(t):
    def cos_sin(num, den):
        ang = (2.0 * math.pi / den) * (num % den).astype(F32)
        return jnp.cos(ang), jnp.sin(ang)

    k = jnp.arange(t, dtype=jnp.int32)[:, None]
    ca, sa = cos_sin(k * jnp.arange(t // GRID_W, dtype=jnp.int32)[None, :], t // GRID_W)
    cb, sb = cos_sin(k * jnp.arange(GRID_W, dtype=jnp.int32)[None, :], t)
    ct = (ca[:, :, None] * cb[:, None, :] - sa[:, :, None] * sb[:, None, :]).reshape(t, t)
    st = (sa[:, :, None] * cb[:, None, :] + ca[:, :, None] * sb[:, None, :]).reshape(t, t)
    c = jnp.arange(F_GD, dtype=jnp.int32)
    cc, sc = cos_sin(c[:, None] * c[None, :], F_GD)
    eye = jnp.eye(F_GROUPS, dtype=F32)
    cs = jnp.concatenate([jnp.kron(eye, cc), jnp.kron(eye, sc)], axis=1)
    return cs.astype(BF16), ct.astype(BF16), st.astype(BF16)


def _pad_heads(w, width, padded):
    lead = w.shape[:-1]
    w = w.reshape(lead + (R_HEADS, width))
    w = jnp.pad(w, [(0, 0)] * len(lead) + [(0, 0), (0, padded - width)])
    return w.reshape(lead + (R_HEADS * padded,))


EVEN_SEGS = tuple([(128 * g, 128 * (g + 1), 0) for g in range(4)] + [(c, c + 512, None) for c in range(512, 3072, 512)]
                  + [(3072 + 128 * g, 3072 + 128 * (g + 1), 1) for g in range(2)])
ODD_W = 512 + 512 + 1024 + 1024 + 256 + 128
ODD_SEGS = tuple([(c, c + 512, None) for c in range(0, 3072, 512)] + [(3072, ODD_W, None)])


def _layer_even(h, modsel, tbl, w_in, sink, lb_logits, g_norm, w_out, lng, lnb, w_router, b_router, experts):
    aq, ak, av, ff, fb, hq, hi, hg = jnp.split(w_in, np.cumsum([512, 128, 128, 512, 512, 512, 512]).tolist(), axis=1)
    kv = [jnp.concatenate([ak[:, A_HD * j:A_HD * (j + 1)], av[:, A_HD * j:A_HD * (j + 1)]], axis=1)
          for j in range(A_KV)]
    w = jnp.concatenate([aq, ff, fb, hq, hi, hg] + kv, axis=1).astype(BF16)
    proj = _inproj(h, modsel, w, tbl, EVEN_SEGS)
    o_att = _attention(proj, sink, 3072 // (2 * A_HD * A_KV))
    lb = jnp.cumsum(jax.nn.softmax(lb_logits.astype(F32), axis=1), axis=1)[:, 0]
    o_f, o_b = _hgrn_scan(proj, lb, 3, 4, 1, 2)
    wo = w_out.astype(BF16)
    h1, u2x, gates, grp = _outproj(b_router, o_att, 0, 512, o_f, o_b, proj, 5, h, modsel, g_norm.reshape(1, -1),
                            wo[:512], wo[512:], lng[0:1], lnb[0:1], w_router, dv=128, dvp=128, row_off=0)
    return _moe_layer(u2x, gates, grp, h1, modsel, experts, lng[1:2], lnb[1:2], 0)


def _layer_odd(h, modsel, tbl, dft, w_in, gate_w, gate_b, g_norm, w_out, lng, lnb, w_router, b_router, experts, t):
    q, k, v, rf, rb, g, z = jnp.split(w_in, np.cumsum([384, 384, 768, 16, 16, 768]).tolist(), axis=1)
    r = jnp.pad(jnp.concatenate([rf, rb], axis=1), ((0, 0), (0, 96)))
    w = jnp.concatenate([_pad_heads(q, GLA_DK, DKP), _pad_heads(k, GLA_DK, DKP), _pad_heads(v, GLA_DV, GLA_DVP),
                         _pad_heads(g, GLA_DV, GLA_DVP), z, r], axis=1).astype(BF16)
    proj = _inproj(h, modsel, w, tbl, ODD_SEGS)
    gw = _pad_heads(gate_w, GLA_DK, DKP)
    gwf = jnp.pad(gw[0], ((0, 112), (0, 0)))
    gwb = jnp.pad(gw[1], ((16, 96), (0, 0)))
    gb = _pad_heads(gate_b, GLA_DK, DKP)
    o_f, o_b = _gla_scan(proj, gwf, gwb, gb, 0, 1, 1, 3328 // 128)
    cs, ct, st = dft
    four = _fourier(proj, 3072 // (F_GROUPS * F_GD), cs, ct, st, t)
    wo = w_out
    wr = jnp.pad(wo[:768].reshape(R_HEADS, GLA_DV, D), ((0, 0), (0, GLA_DVP - GLA_DV), (0, 0))).reshape(-1, D)
    h1, u2x, gates, grp = _outproj(b_router, four, 0, 256, o_f, o_b, proj, 2, h, modsel,
                            _pad_heads(g_norm, GLA_DV, GLA_DVP).reshape(1, -1),
                            wo[768:].astype(BF16), wr.astype(BF16), lng[0:1], lnb[0:1], w_router,
                            dv=GLA_DV, dvp=GLA_DVP, row_off=CTX // TM)
    return _moe_layer(u2x, gates, grp, h1, modsel, experts, lng[1:2], lnb[1:2], CTX // TM)


def kernel(x, c, ctx, c_ctx, w_ada, b_ada, ln_g, ln_b, w_in_even, attn_sink, hgrn_lb_logits, hgrn_norm, w_out_even, w_in_odd, gla_gate_w, gla_gate_b, gla_norm, w_out_odd, w_router, b_router, w_expert_gate, w_expert_up, w_expert_down):
    b, t, _ = x.shape
    rc = CTX + t
    assert ctx.shape[1] == CTX
    cc = jnp.zeros((16, D), F32).at[:b].set(c).at[b].set(c_ctx)
    mods = _ada_mods(cc, w_ada, b_ada).reshape(2, 16, 6, D)

    def modsel(l):
        return jnp.stack([jnp.broadcast_to(mods[l, b], (b, 6, D)), mods[l, :b]], axis=1)

    tbl = _rope_tables(rc, t)
    wrt = jnp.pad(w_router, ((0, 0), (0, GW - N_EXP)))
    experts = (w_expert_gate, w_expert_up, w_expert_down)
    h = _layer_even((ctx, x), modsel(0), tbl, w_in_even[0], attn_sink[0], hgrn_lb_logits, hgrn_norm[0], w_out_even[0],
                    ln_g[0], ln_b[0], wrt, b_router, experts + (0,))
    return _layer_odd(h, modsel(1), tbl, _dft_tables(t), w_in_odd[0], gla_gate_w[0], gla_gate_b[0], gla_norm[0],
                      w_out_odd[0], ln_g[1], ln_b[1], wrt, b_router, experts + (1,), t)
```

```python
import functools
import math

import numpy as np
import jax
import jax.numpy as jnp
from jax import lax
from jax.experimental import pallas as pl
from jax.experimental.pallas import tpu as pltpu

F32 = jnp.float32
BF16 = jnp.bfloat16
HIGHEST = lax.Precision.HIGHEST

D = 1024
CTX = 256
GRID_W = 64
LN_EPS = 1e-5
NEG = -1e30
ALPHA = 4.0 ** 0.25
ROPE_BASE = 10000.0

A_HEADS, A_KV, A_HD, A_BLK = 8, 2, 64, 128
R_HEADS = 4
DKP = 128
GLA_DK, GLA_DV, GLA_DVP = 96, 192, 256
GLA_TAU = 16.0
CHUNK = 64
N_EXP, N_GRP, E_PER, TOP_K = 16, 4, 4, 2
N_PAIR = E_PER * (E_PER - 1) // 2
N_CLASS = N_GRP * N_PAIR
D_FF = 512
F_GROUPS, F_GD = 4, 64

TM = 256
IN_SUB = 3
GW = 128
XROWS = 8
YROWS = 8
RING = 4
V7X_VMEM_LIMIT = 56 * 1024 * 1024


def _cp(sem, vmem=None):
    return pltpu.CompilerParams(dimension_semantics=sem, vmem_limit_bytes=vmem)


def _ln(x):
    mu = jnp.mean(x, axis=-1, keepdims=True)
    xc = x - mu
    var = jnp.mean(xc * xc, axis=-1, keepdims=True)
    return xc * lax.rsqrt(var + LN_EPS)


def _silu(x):
    return x * jax.nn.sigmoid(x)


def _split2(w, axis):
    hi = w.astype(BF16)
    return jnp.concatenate([hi, (w - hi.astype(F32)).astype(BF16)], axis=axis)


_NT = (((1,), (1,)), ((), ()))
_TN = (((0,), (0,)), ((), ()))


def _ada_kernel(c_ref, w_ref, b_ref, o_ref):
    s = _silu(c_ref[...])
    o_ref[0] = jnp.dot(s, w_ref[0], precision=HIGHEST, preferred_element_type=F32) + b_ref[0]


def _ada_mods(cc, w_ada, b_ada):
    depth, _, n = w_ada.shape
    tn = 1536
    return pl.pallas_call(
        _ada_kernel,
        grid=(depth, n // tn),
        in_specs=[pl.BlockSpec((16, D), lambda l, i: (0, 0)),
                  pl.BlockSpec((1, D, tn), lambda l, i: (l, 0, i)),
                  pl.BlockSpec((1, 1, tn), lambda l, i: (l, 0, i))],
        out_specs=pl.BlockSpec((1, 16, tn), lambda l, i: (l, 0, i)),
        out_shape=jax.ShapeDtypeStruct((depth, 16, n), F32),
        compiler_params=_cp(("arbitrary", "arbitrary"), 40 * 1024 * 1024),
        name="ada_mod",
    )(cc, w_ada, b_ada.reshape(depth, 1, n))


def _rows_of(h_refs):
    if len(h_refs) == 1:
        return h_refs[0][0]
    return jnp.where(pl.program_id(1) == 0, h_refs[0][0], h_refs[1][0])


def _row_specs(h, row_off=0):
    if isinstance(h, tuple):
        assert row_off == 0 and h[0].shape[1] == TM
        return [pl.BlockSpec((1, TM, D), lambda i, j: (i, 0, 0)),
                pl.BlockSpec((1, TM, D), lambda i, j: (i, jnp.maximum(j - 1, 0), 0))], list(h)
    return [pl.BlockSpec((1, TM, D), lambda i, j: (i, j + row_off, 0))], [h]


def _inproj_kernel(*refs, segs):
    *h_refs, mod0_ref, mod1_ref, w_ref, tbl_ref, o_ref = refs
    if len(h_refs) == 1:
        x = h_refs[0][0]
    else:
        ctx_ref, *x_refs = h_refs
        x = jnp.concatenate([jnp.where(pl.program_id(1) == 0, ctx_ref[0], x_refs[0][0])]
                            + [r[0] for r in x_refs[1:]], axis=0)
    xn = _ln(x)
    u = jnp.concatenate([xn[:TM] * (1.0 + mod0_ref[0, 0, 1:2, :]) + mod0_ref[0, 0, 0:1, :],
                         xn[TM:] * (1.0 + mod1_ref[0, 0, 1:2, :]) + mod1_ref[0, 0, 0:1, :]], axis=0).astype(BF16)
    lane = lax.broadcasted_iota(jnp.int32, (x.shape[0], 128), 1)
    low = (lane % 32) < 16
    for c0, c1, rope in segs:
        acc = jnp.dot(u, w_ref[:, c0:c1], preferred_element_type=F32)
        if rope is not None:
            partner = jnp.where(low, pltpu.roll(acc, 112, axis=1), pltpu.roll(acc, 16, axis=1))
            acc = acc * tbl_ref[2 * rope] + partner * tbl_ref[2 * rope + 1]
        o_ref[0, :, c0:c1] = acc.astype(BF16)


def _inproj(h, modsel, w, tbl, segs):
    b, rc = modsel.shape[0], tbl.shape[1]
    n = w.shape[1]
    tmi = IN_SUB * TM
    assert rc % tmi == 0
    if isinstance(h, tuple):
        assert h[0].shape[1] == TM
        h_specs = [pl.BlockSpec((1, TM, D), lambda i, j: (i, 0, 0))] + [
            pl.BlockSpec((1, TM, D), lambda i, j, k=k: (i, jnp.maximum(IN_SUB * j + k - 1, 0), 0))
            for k in range(IN_SUB)]
        h_args = [h[0]] + [h[1]] * IN_SUB
    else:
        h_specs, h_args = [pl.BlockSpec((1, tmi, D), lambda i, j: (i, j, 0))], [h]
    return pl.pallas_call(
        functools.partial(_inproj_kernel, segs=segs),
        grid=(b, rc // tmi),
        in_specs=h_specs + [pl.BlockSpec((1, 1, 6, D), lambda i, j: (i, jnp.minimum(j, 1), 0, 0)),
                            pl.BlockSpec((1, 1, 6, D), lambda i, j: (i, 1, 0, 0)),
                            pl.BlockSpec((D, n), lambda i, j: (0, 0)),
                            pl.BlockSpec((4, tmi, 128), lambda i, j: (0, j, 0))],
        out_specs=pl.BlockSpec((1, tmi, n), lambda i, j: (i, j, 0)),
        out_shape=jax.ShapeDtypeStruct((b, rc, n), BF16),
        compiler_params=_cp(("parallel", "arbitrary"), V7X_VMEM_LIMIT),
        name="inproj",
    )(*h_args, modsel, modsel, w, tbl)


def _attn_kernel(sink_ref, q_ref, kvc_ref, kvp_ref, kvm_ref, kvn_ref, bias_ref, o_ref):
    blk = pl.program_id(1)
    g_per = A_HEADS // A_KV
    nq = g_per * A_BLK
    q = q_ref[0] * (A_HD ** -0.5)
    rgrp = lax.broadcasted_iota(jnp.int32, (nq, 1), 0) // A_BLK
    q4, sink_col = [], []
    for h in range(A_KV):
        q4.append(jnp.concatenate([q[:, A_HD * (g_per * h + g):A_HD * (g_per * h + g + 1)] for g in range(g_per)],
                                  axis=0))
        col = jnp.zeros((nq, 1), F32)
        for g in range(g_per):
            col = jnp.where(rgrp == g, sink_ref[h * g_per + g], col)
        sink_col.append(col)

    def attend(kv, bias):
        s = [lax.dot_general(q4[h], kv[:, 2 * A_HD * h:2 * A_HD * h + A_HD], _NT, preferred_element_type=F32)
             for h in range(A_KV)]
        if bias is not None:
            bias4 = jnp.concatenate([bias] * g_per, axis=0)
            s = [x + bias4 for x in s]
        m = [jnp.maximum(jnp.max(s[h], axis=-1, keepdims=True), sink_col[h]) for h in range(A_KV)]
        p = [jnp.exp(s[h] - m[h]) for h in range(A_KV)]
        den = [jnp.sum(p[h], axis=-1, keepdims=True) + jnp.exp(sink_col[h] - m[h]) for h in range(A_KV)]
        o = [jnp.dot(p[h].astype(BF16), kv[:, 2 * A_HD * h + A_HD:2 * A_HD * (h + 1)], preferred_element_type=F32)
             / den[h] for h in range(A_KV)]
        o_ref[0] = jnp.concatenate([o[h][A_BLK * g:A_BLK * (g + 1), :] for h in range(A_KV) for g in range(g_per)],
                                   axis=1).astype(BF16)

    @pl.when(blk < CTX // A_BLK)
    def _():
        attend(kvc_ref[0], None)

    @pl.when(blk >= CTX // A_BLK)
    def _():
        attend(jnp.concatenate([kvc_ref[0], kvp_ref[0], kvm_ref[0], kvn_ref[0]], axis=0), bias_ref[0])


def _attn_bias():
    nk = CTX + 3 * A_BLK
    qi = jnp.arange(A_BLK, dtype=jnp.int32)[:, None]
    col = jnp.arange(nk, dtype=jnp.int32)[None, :]
    rel = col - (CTX + A_BLK)
    dist = qi - rel

    def one(lo, hi):
        in_win = (dist <= A_BLK) & (dist >= -A_BLK) & (rel >= lo) & (rel < hi)
        return jnp.where((col < CTX) | in_win, 0.0, NEG).astype(F32)

    return jnp.stack([one(0, 2 * A_BLK), one(-A_BLK, 2 * A_BLK), one(-A_BLK, A_BLK)])


def _attention(proj, sink, kv_blk):
    b, rc, _ = proj.shape
    nb = rc // A_BLK
    c_b = CTX // A_BLK
    assert nb - c_b >= 2
    kw = 2 * A_HD * A_KV

    def band(off):
        return lambda i, j: (i, jnp.clip(j + off, c_b, nb - 1), kv_blk)

    def which_bias(i, j):
        return (jnp.where(j <= c_b, 0, jnp.where(j == nb - 1, 2, 1)), 0, 0)

    return pl.pallas_call(
        _attn_kernel,
        grid=(b, nb),
        in_specs=[pl.BlockSpec(memory_space=pltpu.SMEM),
                  pl.BlockSpec((1, A_BLK, A_HEADS * A_HD), lambda i, j: (i, j, 0)),
                  pl.BlockSpec((1, CTX, kw), lambda i, j: (i, 0, kv_blk)),
                  pl.BlockSpec((1, A_BLK, kw), band(-1)),
                  pl.BlockSpec((1, A_BLK, kw), band(0)),
                  pl.BlockSpec((1, A_BLK, kw), band(1)),
                  pl.BlockSpec((1, A_BLK, CTX + 3 * A_BLK), which_bias)],
        out_specs=pl.BlockSpec((1, A_BLK, A_HEADS * A_HD), lambda i, j: (i, j, 0)),
        out_shape=jax.ShapeDtypeStruct((b, rc, A_HEADS * A_HD), BF16),
        compiler_params=_cp(("parallel", "arbitrary")),
        name="win_attn",
    )(sink, proj, proj, proj, proj, proj, _attn_bias())


def _scan_blocks(dirs, *, dvp):
    r_i = lax.broadcasted_iota(jnp.int32, (CHUNK, CHUNK), 0)
    c_i = lax.broadcasted_iota(jnp.int32, (CHUNK, CHUNK), 1)
    units = []
    for q, k, v, logf, s_ref, o_ref, reverse in dirs:
        nchunk = q.shape[0] // CHUNK
        tri = (r_i <= c_i) if reverse else (r_i >= c_i)
        tri3 = jnp.concatenate([jnp.where(tri, 1.0, 0.0).astype(BF16)] * 3, axis=1)
        for c in (range(nchunk - 1, -1, -1) if reverse else range(nchunk)):
            sl = slice(c * CHUNK, (c + 1) * CHUNK)
            units.append(dict(q=q[sl], k=k[sl], v=v[sl].astype(BF16), lf=logf[sl], sl=sl, tri=tri, tri3=tri3,
                              i_tot=0 if reverse else CHUNK - 1,
                              i_ref=CHUNK // 2 - 1 if reverse else CHUNK // 2, s_ref=s_ref, o_ref=o_ref))
    heads = [(slice(h * DKP, (h + 1) * DKP), slice(h * dvp, (h + 1) * dvp)) for h in range(R_HEADS)]
    for u in units:
        lf = u["lf"]
        hi = lf.astype(BF16)
        r1 = lf - hi.astype(F32)
        mid = r1.astype(BF16)
        lo = (r1 - mid.astype(F32)).astype(BF16)
        u["bsum"] = jnp.dot(u["tri3"], jnp.concatenate([hi, mid, lo], axis=0), preferred_element_type=F32)
    for u in units:
        bsum = u["bsum"]
        tot = bsum[u["i_tot"]:u["i_tot"] + 1, :]
        ref = bsum[u["i_ref"]:u["i_ref"] + 1, :]
        qg = u["q"] * jnp.exp(bsum - ref)
        kg = u["k"] * jnp.exp(ref - bsum)
        u["qb"] = (qg * jnp.exp(ref)).astype(BF16)
        u["kd"] = (kg * jnp.exp(tot - ref)).astype(BF16)
        u["qg"] = qg.astype(BF16)
        u["kg"] = kg.astype(BF16)
        u["dec"] = jnp.exp(tot)
    for u in units:
        u["att"] = [lax.dot_general(u["qg"][:, ks], u["kg"][:, ks], _NT, preferred_element_type=F32)
                    for ks, _ in heads]
        u["ds"] = [lax.dot_general(u["v"][:, vs], u["kd"][:, ks], _TN, preferred_element_type=F32)
                   for ks, vs in heads]
    for u in units:
        u["oi"] = [jnp.dot(jnp.where(u["tri"], a, 0.0).astype(BF16), u["v"][:, vs], preferred_element_type=F32)
                   for a, (_, vs) in zip(u["att"], heads)]
    state = {}
    for u in units:
        key = id(u["s_ref"])
        if key not in state:
            state[key] = [u["s_ref"][h] for h in range(R_HEADS)]
        st = state[key]
        outs = []
        for h, (ks, _) in enumerate(heads):
            outs.append(u["oi"][h] + lax.dot_general(u["qb"][:, ks], st[h].astype(BF16), _NT,
                                                     preferred_element_type=F32))
            st[h] = st[h] * u["dec"][:, ks] + u["ds"][h]
        u["o_ref"][0, u["sl"], :] = jnp.concatenate(outs, axis=1).astype(BF16)
    for _, _, _, _, s_ref, _, _ in dirs:
        for h in range(R_HEADS):
            s_ref[h] = state[id(s_ref)][h]


def _hgrn_scan_kernel(lb_ref, qf_ref, vf_ref, zf_ref, qb_ref, vb_ref, zb_ref, of_ref, ob_ref, sf_ref, sb_ref):
    @pl.when(pl.program_id(1) == 0)
    def _():
        sf_ref[...] = jnp.zeros_like(sf_ref)
        sb_ref[...] = jnp.zeros_like(sb_ref)

    def gates(z, lb):
        f = lb + (1.0 - lb) * jax.nn.sigmoid(z.astype(F32))
        return 1.0 - f, jnp.log(f)

    kf, lf = gates(zf_ref[0], lb_ref[0:1, :])
    kb, lbw = gates(zb_ref[0], lb_ref[1:2, :])
    _scan_blocks([(qf_ref[0], kf, vf_ref[0], lf, sf_ref, of_ref, False),
                  (qb_ref[0], kb, vb_ref[0], lbw, sb_ref, ob_ref, True)], dvp=128)


def _bwd_blk(j, nt):
    return jnp.where(j == 0, 0, nt - j)


def _hgrn_scan(proj, lb, cq, cv, cf, cb):
    b, rc, _ = proj.shape
    nt = rc // TM
    w = R_HEADS * DKP

    def fw(col):
        return pl.BlockSpec((1, TM, w), lambda i, j: (i, j, col))

    def bw(col):
        return pl.BlockSpec((1, TM, w), lambda i, j: (i, _bwd_blk(j, nt), col))

    o_sd = jax.ShapeDtypeStruct((b, rc, w), BF16)
    return pl.pallas_call(
        _hgrn_scan_kernel,
        grid=(b, nt),
        in_specs=[pl.BlockSpec((2, w), lambda i, j: (0, 0)),
                  fw(cq), fw(cv), fw(cf), bw(cq), bw(cv), bw(cb)],
        out_specs=[pl.BlockSpec((1, TM, w), lambda i, j: (i, j, 0)),
                   pl.BlockSpec((1, TM, w), lambda i, j: (i, _bwd_blk(j, nt), 0))],
        out_shape=[o_sd, o_sd],
        scratch_shapes=[pltpu.VMEM((R_HEADS, 128, DKP), F32), pltpu.VMEM((R_HEADS, 128, DKP), F32)],
        compiler_params=_cp(("parallel", "arbitrary"), 40 * 1024 * 1024),
        name="hgrn_scan",
    )(lb, proj, proj, proj, proj, proj, proj)


def _gla_scan_kernel(gwf_ref, gwb_ref, gb_ref, qf_ref, kf_ref, vf_ref, rf_ref, qb_ref, kb_ref, vb_ref, rb_ref,
                     of_ref, ob_ref, sf_ref, sb_ref):
    @pl.when(pl.program_id(1) == 0)
    def _():
        sf_ref[...] = jnp.zeros_like(sf_ref)
        sb_ref[...] = jnp.zeros_like(sb_ref)

    def decay(r, gw, gb):
        zz = jnp.dot(jnp.concatenate([r, r], axis=1), _split2(gw, 0), preferred_element_type=F32) + gb
        return (jnp.minimum(zz, 0.0) - jnp.log(1.0 + jnp.exp(-jnp.abs(zz)))) * (1.0 / GLA_TAU)

    qs = GLA_DK ** -0.5
    _scan_blocks([(qf_ref[0].astype(F32) * qs, kf_ref[0], vf_ref[0],
                   decay(rf_ref[0], gwf_ref[...], gb_ref[0:1, :]), sf_ref, of_ref, False),
                  (qb_ref[0].astype(F32) * qs, kb_ref[0], vb_ref[0],
                   decay(rb_ref[0], gwb_ref[...], gb_ref[1:2, :]), sb_ref, ob_ref, True)], dvp=GLA_DVP)


def _gla_scan(proj, gwf, gwb, gb, cq, ck, cv, cr):
    b, rc, _ = proj.shape
    nt = rc // TM
    wk = R_HEADS * DKP
    wv = R_HEADS * GLA_DVP

    def spec(width, col, back):
        if back:
            return pl.BlockSpec((1, TM, width), lambda i, j: (i, _bwd_blk(j, nt), col))
        return pl.BlockSpec((1, TM, width), lambda i, j: (i, j, col))

    o_sd = jax.ShapeDtypeStruct((b, rc, wv), BF16)
    full = lambda shape: pl.BlockSpec(shape, lambda i, j: (0,) * len(shape))
    return pl.pallas_call(
        _gla_scan_kernel,
        grid=(b, nt),
        in_specs=[full((128, wk)), full((128, wk)), full((2, wk)),
                  spec(wk, cq, False), spec(wk, ck, False), spec(wv, cv, False), spec(128, cr, False),
                  spec(wk, cq, True), spec(wk, ck, True), spec(wv, cv, True), spec(128, cr, True)],
        out_specs=[spec(wv, 0, False), spec(wv, 0, True)],
        out_shape=[o_sd, o_sd],
        scratch_shapes=[pltpu.VMEM((R_HEADS, GLA_DVP, DKP), F32), pltpu.VMEM((R_HEADS, GLA_DVP, DKP), F32)],
        compiler_params=_cp(("parallel", "arbitrary"), 48 * 1024 * 1024),
        name="gla_scan",
    )(gwf, gwb, gb, proj, proj, proj, proj, proj, proj, proj, proj)


def _fourier_kernel(z_ref, cs_ref, ct_ref, st_ref, o_ref, *, scale):
    w = F_GROUPS * F_GD
    zz = jnp.dot(z_ref[0, CTX:, :], cs_ref[...], preferred_element_type=F32).astype(BF16)
    y = jnp.dot(ct_ref[...], zz[:, :w], preferred_element_type=F32)
    y = y - jnp.dot(st_ref[...], zz[:, w:], preferred_element_type=F32)
    o_ref[0] = (y * scale).astype(BF16)


def _fourier(proj, zcol, cs, ct, st, t):
    b, rc, _ = proj.shape
    w = F_GROUPS * F_GD
    const = lambda shape: pl.BlockSpec(shape, lambda i: (0, 0), pipeline_mode=pl.Buffered(1))
    return pl.pallas_call(
        functools.partial(_fourier_kernel, scale=1.0 / math.sqrt(t * F_GD)),
        grid=(b,),
        in_specs=[pl.BlockSpec((1, rc, w), lambda i: (i, 0, zcol)), const((w, 2 * w)), const((t, t)), const((t, t))],
        out_specs=pl.BlockSpec((1, t, w), lambda i: (i, 0, 0)),
        out_shape=jax.ShapeDtypeStruct((b, t, w), BF16),
        compiler_params=_cp(("arbitrary",), 48 * 1024 * 1024),
        name="fourier",
    )(proj, cs, ct, st)


def _route(logits, rb_ref):
    lt = logits.T
    l = [lt[e:e + 1, :] for e in range(N_EXP)]
    m = functools.reduce(jnp.maximum, l)
    ex = [jnp.exp(v - m) for v in l]
    den = functools.reduce(lambda a, c: a + c, ex)
    p = [v / den for v in ex]
    sel = [p[e] + rb_ref[e] for e in range(N_EXP)]
    g_score, g_gate, g_pick = [], [], []
    for g in range(N_GRP):
        s = sel[E_PER * g:E_PER * (g + 1)]
        chosen = []
        for i in range(E_PER):
            rank = jnp.zeros_like(s[i])
            for j in range(E_PER):
                if j == i:
                    continue
                ahead = (s[j] >= s[i]) if j < i else (s[j] > s[i])
                rank = rank + jnp.where(ahead, 1.0, 0.0)
            chosen.append(rank < 2.0)
        g_score.append(functools.reduce(lambda a, c: a + c,
                                        [jnp.where(chosen[i], s[i], 0.0) for i in range(E_PER)]))
        g_gate.append([jnp.where(chosen[i], p[E_PER * g + i], 0.0) for i in range(E_PER)])
        g_pick.append([jnp.where(chosen[i], 1.0, 0.0) for i in range(E_PER)])
    best = jnp.zeros_like(g_score[0])
    best_s = g_score[0]
    for g in range(1, N_GRP):
        upd = g_score[g] > best_s
        best = jnp.where(upd, float(g), best)
        best_s = jnp.where(upd, g_score[g], best_s)
    gate, pick = [], []
    for i in range(E_PER):
        gi, ci = g_gate[0][i], g_pick[0][i]
        for g in range(1, N_GRP):
            gi = jnp.where(best == float(g), g_gate[g][i], gi)
            ci = jnp.where(best == float(g), g_pick[g][i], ci)
        gate.append(gi)
        pick.append(ci)
    wsum = functools.reduce(lambda a, c: a + c, gate)
    lo = functools.reduce(jnp.minimum, [jnp.where(pick[i] > 0.5, float(i), float(E_PER)) for i in range(E_PER)])
    hi = functools.reduce(jnp.maximum, [jnp.where(pick[i] > 0.5, float(i), -1.0) for i in range(E_PER)])
    pair = jnp.where(lo == 0.0, hi - 1.0, jnp.where(lo == 1.0, hi + 1.0, float(N_PAIR - 1)))
    g_lo = functools.reduce(lambda a, c: a + c, [jnp.where(lo == float(i), gate[i], 0.0) for i in range(E_PER)])
    g_hi = functools.reduce(lambda a, c: a + c, [jnp.where(hi == float(i), gate[i], 0.0) for i in range(E_PER)])
    row = lax.broadcasted_iota(jnp.int32, (GW, lt.shape[1]), 0)
    side = jnp.where(row == 0, g_lo / wsum, jnp.where(row == 1, g_hi / wsum, 0.0))
    return side.T, best * float(N_PAIR) + pair


def _outproj_kernel(rb_ref, a_ref, of_ref, ob_ref, gt_ref, *refs, dvp, dv):
    *h_refs, mod_ref, gn_ref, wa_ref, wr_ref, lng_ref, lnb_ref, wrt_ref, h1_ref, u2_ref, gate_ref, grp_ref = refs
    o = of_ref[0].astype(F32) + ob_ref[0].astype(F32)
    parts = []
    for hh in range(R_HEADS):
        sl = slice(hh * dvp, (hh + 1) * dvp)
        oh = o[:, sl]
        ms = jnp.sum(oh * oh, axis=-1, keepdims=True) * (1.0 / dv)
        parts.append(oh * lax.rsqrt(ms + LN_EPS))
    rec = jnp.concatenate(parts, axis=1) * gn_ref[...] * _silu(gt_ref[0].astype(F32))
    y = jnp.dot(a_ref[0], wa_ref[...], preferred_element_type=F32)
    y = y + jnp.dot(rec.astype(BF16), wr_ref[...], preferred_element_type=F32)
    g1 = mod_ref[0, 0, 2:3, :]
    h1 = _ln(ALPHA * _rows_of(h_refs) + g1 * y) * lng_ref[0:1, :] + lnb_ref[0:1, :]
    h1_ref[0] = h1
    u2 = _ln(h1) * (1.0 + mod_ref[0, 0, 4:5, :]) + mod_ref[0, 0, 3:4, :]
    lg = jnp.dot(_split2(u2, 0), _split2(wrt_ref[...], 1), preferred_element_type=F32)
    lg = lg[:TM] + lg[TM:]
    logits = lg[:, :GW] + lg[:, GW:]
    side, best = _route(logits, rb_ref)
    grp_ref[0, 0] = best
    gate_ref[...] = side
    for c in range(XROWS):
        u2_ref[pl.ds(c, TM, stride=XROWS), :] = u2[:, 128 * c:128 * (c + 1)]


def _outproj(b_router, a, a_col, a_w, o_f, o_b, proj, g_col, h, modsel, gnorm, wa, wr, lng, lnb, wrt, *,
             dv, dvp, row_off):
    b, rc, _ = proj.shape
    nt = rc // TM - row_off
    wrec = R_HEADS * dvp
    full = lambda shape: pl.BlockSpec(shape, lambda i, j: (0,) * len(shape))
    h_specs, h_args = _row_specs(h, row_off)
    return pl.pallas_call(
        functools.partial(_outproj_kernel, dvp=dvp, dv=dv),
        grid=(b, nt),
        in_specs=[pl.BlockSpec(memory_space=pltpu.SMEM),
                  pl.BlockSpec((1, TM, a_w), lambda i, j: (i, j + (row_off if a.shape[1] == rc else 0), a_col)),
                  pl.BlockSpec((1, TM, wrec), lambda i, j: (i, j + row_off, 0)),
                  pl.BlockSpec((1, TM, wrec), lambda i, j: (i, j + row_off, 0)),
                  pl.BlockSpec((1, TM, wrec), lambda i, j: (i, j + row_off, g_col))] + h_specs + [
                  pl.BlockSpec((1, 1, 6, D), lambda i, j: (i, jnp.minimum(j + row_off, 1), 0, 0)),
                  full((1, wrec)), full((a_w, D)), full((wrec, D)), full((1, D)), full((1, D)), full((D, GW))],
        out_specs=[pl.BlockSpec((1, TM, D), lambda i, j: (i, j, 0)),
                   pl.BlockSpec((TM * XROWS, 128), lambda i, j: (i * nt + j, 0)),
                   pl.BlockSpec((TM, GW), lambda i, j: (i * nt + j, 0)),
                   pl.BlockSpec((1, 1, 1, TM), lambda i, j: (i, j, 0, 0))],
        out_shape=[jax.ShapeDtypeStruct((b, nt * TM, D), F32),
                   jax.ShapeDtypeStruct((b * nt * TM * XROWS, 128), F32),
                   jax.ShapeDtypeStruct((b * nt * TM, GW), F32),
                   jax.ShapeDtypeStruct((b, nt, 1, TM), F32)],
        compiler_params=_cp(("parallel", "arbitrary"), V7X_VMEM_LIMIT),
        name="outproj",
    )(b_router, a, o_f, o_b, proj, *h_args, modsel, gnorm, wa, wr, lng, lnb, wrt)


def _start_token_gather(idx_ref, base, src_hbm, dst_ref, sem, rows):
    def body(r, carry):
        t = idx_ref[base + r]
        pltpu.make_async_copy(src_hbm.at[pl.ds(pl.multiple_of(t * rows, rows), rows), :],
                              dst_ref.at[pl.ds(pl.multiple_of(r * rows, rows), rows), :], sem).start()
        return carry

    lax.fori_loop(0, dst_ref.shape[0] // rows, body, 0, unroll=8)


def _wait_token_gather(src_hbm, dst_ref, sem):
    pltpu.make_async_copy(src_hbm.at[pl.ds(0, dst_ref.shape[0]), :], dst_ref, sem).wait()


def _gather_ring(idx_ref, first_of, step, last, src_hbm, buf_ref, sem, rows):
    depth = buf_ref.shape[0]
    slot = step % depth

    @pl.when(step == 0)
    def _():
        for k in range(depth - 1):
            _start_token_gather(idx_ref, first_of(jnp.minimum(k, last)), src_hbm, buf_ref.at[k], sem.at[k], rows)

    _wait_token_gather(src_hbm, buf_ref.at[slot], sem.at[slot])
    ahead = (step + depth - 1) % depth
    nxt = first_of(jnp.minimum(step + depth - 1, last))

    def issue(r0, r1):
        for r in range(r0, r1):
            t = idx_ref[nxt + r]
            pltpu.make_async_copy(src_hbm.at[pl.ds(pl.multiple_of(t * rows, rows), rows), :],
                                  buf_ref.at[ahead, pl.ds(r * rows, rows), :], sem.at[ahead]).start(priority=r % 2)

    def finish():
        @pl.when(step == last)
        def _():
            for k in range(1, depth):
                other = (step + k) % depth
                _wait_token_gather(src_hbm, buf_ref.at[other], sem.at[other])

    return slot, issue, finish


def _moe_kernel(te_ref, first_ref, src_ref, x_hbm, *refs):
    w_refs, (o_ref, xb_ref, sem), wb_refs = refs[:6], refs[6:9], refs[9:]
    i = pl.program_id(0)
    slot, issue, finish = _gather_ring(src_ref, lambda k: first_ref[k], i, pl.num_programs(0) - 1,
                                       x_hbm, xb_ref, sem, XROWS)
    prev = jnp.maximum(i - 1, 0)
    for s in range(TOP_K):
        @pl.when((i == 0) | (te_ref[s, i] != te_ref[s, prev]))
        def _():
            for k in range(3):
                wb_refs[3 * s + k][...] = w_refs[3 * s + k][...].astype(BF16)

    xb = xb_ref.at[slot]
    x = jnp.concatenate([xb[pl.ds(c, TM, stride=XROWS), :] for c in range(D // 128)], axis=1).astype(BF16)
    per = TM // (3 * TOP_K)
    edges = [k * per for k in range(3 * TOP_K)] + [TM]
    for s in range(TOP_K):
        w1b, w3b, w2b = wb_refs[3 * s:3 * s + 3]
        issue(edges[3 * s], edges[3 * s + 1])
        h1 = jnp.dot(x, w1b[...], preferred_element_type=F32)
        issue(edges[3 * s + 1], edges[3 * s + 2])
        h3 = jnp.dot(x, w3b[...], preferred_element_type=F32)
        hh = (_silu(h1) * h3).astype(BF16)
        issue(edges[3 * s + 2], edges[3 * s + 3])
        y = jnp.dot(hh, w2b[...], preferred_element_type=F32)
        for c in range(D // 128):
            o_ref[s, pl.ds(c, TM, stride=YROWS), :] = y[:, 128 * c:128 * (c + 1)]
    finish()


def _moe(tile_exp, tile_first, order, u2x, w1, w3, w2, layer):
    mp = tile_exp.shape[1] * TM

    def wspec(s, k, n):
        return pl.BlockSpec((None, None, k, n), lambda i, te, tf, od: (layer, te[s, i], 0, 0),
                            pipeline_mode=pl.Buffered(1))

    w_specs, w_args, w_scratch = [], [], []
    for s in range(TOP_K):
        w_specs += [wspec(s, D, D_FF), wspec(s, D, D_FF), wspec(s, D_FF, D)]
        w_args += [w1, w3, w2]
        w_scratch += [pltpu.VMEM((D, D_FF), BF16), pltpu.VMEM((D, D_FF), BF16), pltpu.VMEM((D_FF, D), BF16)]
    return pl.pallas_call(
        _moe_kernel,
        grid_spec=pltpu.PrefetchScalarGridSpec(
            num_scalar_prefetch=3, grid=(mp // TM,),
            in_specs=[pl.BlockSpec(memory_space=pl.ANY)] + w_specs,
            out_specs=pl.BlockSpec((TOP_K, TM * YROWS, 128), lambda i, te, tf, od: (0, i, 0)),
            scratch_shapes=[pltpu.VMEM((RING, TM * XROWS, 128), F32), pltpu.SemaphoreType.DMA((RING,))] + w_scratch),
        out_shape=jax.ShapeDtypeStruct((TOP_K, mp * YROWS, 128), F32),
        compiler_params=_cp(("arbitrary",), 48 * 1024 * 1024),
        name="moe",
    )(tile_exp, tile_first, order, u2x, *w_args)


def _combine_kernel(pos_ref, y_hbm, g_ref, h_ref, mod_ref, lng_ref, lnb_ref, o_ref, yb_ref, sem, *, nt):
    step = pl.program_id(0) * nt + pl.program_id(1)
    slot, issue, finish = _gather_ring(pos_ref, lambda k: k * (TOP_K * TM), step, pl.num_programs(0) * nt - 1,
                                       y_hbm, yb_ref, sem, YROWS)
    yb = yb_ref.at[slot]
    g2 = mod_ref[0, 0, 5:6, :]
    nstrip = 4
    strip = TM // nstrip
    per = TOP_K * TM // nstrip
    for k in range(nstrip):
        r0 = k * strip
        issue(k * per, (k + 1) * per)
        gates = g_ref[r0:r0 + strip, :]
        y = functools.reduce(lambda a, c: a + c, [
            gates[:, s:s + 1] * jnp.concatenate(
                [yb[pl.ds((s * TM + r0) * YROWS + c, strip, stride=YROWS), :] for c in range(D // 128)], axis=1)
            for s in range(TOP_K)])
        o_ref[0, r0:r0 + strip, :] = (_ln(ALPHA * h_ref[0, r0:r0 + strip, :] + g2 * y) * lng_ref[0:1, :]
                                      + lnb_ref[0:1, :])
    finish()


def _combine(pos, y_sorted, gates, h1, modsel, lng, lnb, row_off):
    b, r, _ = h1.shape
    nt = r // TM
    full = lambda shape: pl.BlockSpec(shape, lambda i, j, p: (0,) * len(shape))
    return pl.pallas_call(
        functools.partial(_combine_kernel, nt=nt),
        grid_spec=pltpu.PrefetchScalarGridSpec(
            num_scalar_prefetch=1, grid=(b, nt),
            in_specs=[pl.BlockSpec(memory_space=pl.ANY),
                      pl.BlockSpec((TM, GW), lambda i, j, p: (i * nt + j, 0)),
                      pl.BlockSpec((1, TM, D), lambda i, j, p: (i, j, 0)),
                      pl.BlockSpec((1, 1, 6, D), lambda i, j, p: (i, jnp.minimum(j + row_off, 1), 0, 0)),
                      full((1, D)), full((1, D))],
            out_specs=pl.BlockSpec((1, TM, D), lambda i, j, p: (i, j, 0)),
            scratch_shapes=[pltpu.VMEM((RING, TOP_K * TM * YROWS, 128), F32), pltpu.SemaphoreType.DMA((RING,))]),
        out_shape=jax.ShapeDtypeStruct((b, r, D), F32),
        compiler_params=_cp(("arbitrary", "arbitrary")),
        name="moe_combine",
    )(pos, y_sorted, gates, h1, modsel, lng, lnb)


def _sort_by_class(cls):
    n = cls.shape[0]
    onehot = (cls[:, None] == jnp.arange(N_CLASS, dtype=jnp.int32)[None, :]).astype(jnp.int32)
    csum = jnp.cumsum(onehot, axis=0)
    counts = csum[-1]
    padded = ((counts + TM - 1) // TM) * TM
    ends = jnp.cumsum(padded)
    starts = ends - padded
    rank = jnp.sum(csum * onehot, axis=1) - 1
    pos = (starts[cls] + rank).astype(jnp.int32)
    _, order = lax.sort((cls, jnp.arange(n, dtype=jnp.int32)), num_keys=1, is_stable=True)
    order = jnp.concatenate([order, jnp.zeros((TM,), jnp.int32)])
    ntile = n // TM + N_CLASS
    tile_start = jnp.arange(ntile, dtype=jnp.int32) * TM
    tile_cls = jnp.minimum(jnp.sum((tile_start[:, None] >= ends[None, :]).astype(jnp.int32), axis=1), N_CLASS - 1)
    first = (jnp.cumsum(counts) - counts)[tile_cls] + tile_start - starts[tile_cls]
    return pos, order, tile_cls.astype(jnp.int32), jnp.minimum(first, n).astype(jnp.int32)


_PAIRS = np.array([(a, c) for a in range(E_PER) for c in range(a + 1, E_PER)], np.int32)


def _moe_layer(u2x, gates, cls, h1, modsel, experts, lng, lnb, row_off):
    pos, order, tile_cls, tile_first = _sort_by_class(cls.reshape(-1).astype(jnp.int32))
    tile_exp = (E_PER * (tile_cls // N_PAIR))[None, :] + jnp.asarray(_PAIRS).T[:, tile_cls % N_PAIR]
    y_sorted = _moe(tile_exp.astype(jnp.int32), tile_first, order, u2x, *experts)
    mp = y_sorted.shape[1] // YROWS
    pos2 = jnp.stack([pos, pos + mp]).reshape(TOP_K, -1, TM).transpose(1, 0, 2).reshape(-1)
    return _combine(pos2, y_sorted.reshape(TOP_K * mp * YROWS, 128), gates, h1, modsel, lng, lnb, row_off)


def _rope_tables(rc, t):
    half = A_HD // 4
    inv = ROPE_BASE ** (-jnp.arange(half, dtype=F32) / half)
    tpos = jnp.arange(t)
    rows = (tpos // GRID_W).astype(F32)
    cols = (tpos % GRID_W).astype(F32)

    def cs(pos):
        ang = pos[:, None] * inv[None, :]
        c, s = jnp.cos(ang), jnp.sin(ang)
        return jnp.concatenate([c, c], -1), jnp.concatenate([-s, s], -1)

    cr, sr = cs(rows)
    cc_, sc_ = cs(cols)
    cos_h = jnp.concatenate([cr, cc_], -1)
    sin_h = jnp.concatenate([sr, sc_], -1)
    one = jnp.ones((t, A_HD), F32)
    zero = jnp.zeros((t, A_HD), F32)
    lat = jnp.stack([jnp.concatenate([cos_h, cos_h], -1), jnp.concatenate([sin_h, sin_h], -1),
                     jnp.concatenate([cos_h, one], -1), jnp.concatenate([sin_h, zero], -1)])
    ctx = jnp.stack([jnp.ones((CTX, 128), F32), jnp.zeros((CTX, 128), F32)] * 2)
    return jnp.concatenate([ctx, lat], axis=1)


def _dft_tables(t):
    def cos_sin(num, den):
        ang = (2.0 * math.pi / den) * (num % den).astype(F32)
        return jnp.cos(ang), jnp.sin(ang)

    k = jnp.arange(t, dtype=jnp.int32)[:, None]
    ca, sa = cos_sin(k * jnp.arange(t // GRID_W, dtype=jnp.int32)[None, :], t // GRID_W)
    cb, sb = cos_sin(k * jnp.arange(GRID_W, dtype=jnp.int32)[None, :], t)
    ct = (ca[:, :, None] * cb[:, None, :] - sa[:, :, None] * sb[:, None, :]).reshape(t, t)
    st = (sa[:, :, None] * cb[:, None, :] + ca[:, :, None] * sb[:, None, :]).reshape(t, t)
    c = jnp.arange(F_GD, dtype=jnp.int32)
    cc, sc = cos_sin(c[:, None] * c[None, :], F_GD)
    eye = jnp.eye(F_GROUPS, dtype=F32)
    cs = jnp.concatenate([jnp.kron(eye, cc), jnp.kron(eye, sc)], axis=1)
    return cs.astype(BF16), ct.astype(BF16), st.astype(BF16)


def _pad_heads(w, width, padded):
    lead = w.shape[:-1]
    w = w.reshape(lead + (R_HEADS, width))
    w = jnp.pad(w, [(0, 0)] * len(lead) + [(0, 0), (0, padded - width)])
    return w.reshape(lead + (R_HEADS * padded,))


EVEN_SEGS = tuple([(128 * g, 128 * (g + 1), 0) for g in range(4)] + [(c, c + 512, None) for c in range(512, 3072, 512)]
                  + [(3072 + 128 * g, 3072 + 128 * (g + 1), 1) for g in range(2)])
ODD_W = 512 + 512 + 1024 + 1024 + 256 + 128
ODD_SEGS = tuple([(c, c + 512, None) for c in range(0, 3072, 512)] + [(3072, ODD_W, None)])


def _layer_even(h, modsel, tbl, w_in, sink, lb_logits, g_norm, w_out, lng, lnb, w_router, b_router, experts):
    aq, ak, av, ff, fb, hq, hi, hg = jnp.split(w_in, np.cumsum([512, 128, 128, 512, 512, 512, 512]).tolist(), axis=1)
    kv = [jnp.concatenate([ak[:, A_HD * j:A_HD * (j + 1)], av[:, A_HD * j:A_HD * (j + 1)]], axis=1)
          for j in range(A_KV)]
    w = jnp.concatenate([aq, ff, fb, hq, hi, hg] + kv, axis=1).astype(BF16)
    proj = _inproj(h, modsel, w, tbl, EVEN_SEGS)
    o_att = _attention(proj, sink, 3072 // (2 * A_HD * A_KV))
    lb = jnp.cumsum(jax.nn.softmax(lb_logits.astype(F32), axis=1), axis=1)[:, 0]
    o_f, o_b = _hgrn_scan(proj, lb, 3, 4, 1, 2)
    wo = w_out.astype(BF16)
    h1, u2x, gates, grp = _outproj(b_router, o_att, 0, 512, o_f, o_b, proj, 5, h, modsel, g_norm.reshape(1, -1),
                            wo[:512], wo[512:], lng[0:1], lnb[0:1], w_router, dv=128, dvp=128, row_off=0)
    return _moe_layer(u2x, gates, grp, h1, modsel, experts, lng[1:2], lnb[1:2], 0)


def _layer_odd(h, modsel, tbl, dft, w_in, gate_w, gate_b, g_norm, w_out, lng, lnb, w_router, b_router, experts, t):
    q, k, v, rf, rb, g, z = jnp.split(w_in, np.cumsum([384, 384, 768, 16, 16, 768]).tolist(), axis=1)
    r = jnp.pad(jnp.concatenate([rf, rb], axis=1), ((0, 0), (0, 96)))
    w = jnp.concatenate([_pad_heads(q, GLA_DK, DKP), _pad_heads(k, GLA_DK, DKP), _pad_heads(v, GLA_DV, GLA_DVP),
                         _pad_heads(g, GLA_DV, GLA_DVP), z, r], axis=1).astype(BF16)
    proj = _inproj(h, modsel, w, tbl, ODD_SEGS)
    gw = _pad_heads(gate_w, GLA_DK, DKP)
    gwf = jnp.pad(gw[0], ((0, 112), (0, 0)))
    gwb = jnp.pad(gw[1], ((16, 96), (0, 0)))
    gb = _pad_heads(gate_b, GLA_DK, DKP)
    o_f, o_b = _gla_scan(proj, gwf, gwb, gb, 0, 1, 1, 3328 // 128)
    cs, ct, st = dft
    four = _fourier(proj, 3072 // (F_GROUPS * F_GD), cs, ct, st, t)
    wo = w_out
    wr = jnp.pad(wo[:768].reshape(R_HEADS, GLA_DV, D), ((0, 0), (0, GLA_DVP - GLA_DV), (0, 0))).reshape(-1, D)
    h1, u2x, gates, grp = _outproj(b_router, four, 0, 256, o_f, o_b, proj, 2, h, modsel,
                            _pad_heads(g_norm, GLA_DV, GLA_DVP).reshape(1, -1),
                            wo[768:].astype(BF16), wr.astype(BF16), lng[0:1], lnb[0:1], w_router,
                            dv=GLA_DV, dvp=GLA_DVP, row_off=CTX // TM)
    return _moe_layer(u2x, gates, grp, h1, modsel, experts, lng[1:2], lnb[1:2], CTX // TM)


def kernel(x, c, ctx, c_ctx, w_ada, b_ada, ln_g, ln_b, w_in_even, attn_sink, hgrn_lb_logits, hgrn_norm, w_out_even, w_in_odd, gla_gate_w, gla_gate_b, gla_norm, w_out_odd, w_router, b_router, w_expert_gate, w_expert_up, w_expert_down):
    b, t, _ = x.shape
    rc = CTX + t
    assert ctx.shape[1] == CTX
    cc = jnp.zeros((16, D), F32).at[:b].set(c).at[b].set(c_ctx)
    mods = _ada_mods(cc, w_ada, b_ada).reshape(2, 16, 6, D)

    def modsel(l):
        return jnp.stack([jnp.broadcast_to(mods[l, b], (b, 6, D)), mods[l, :b]], axis=1)

    tbl = _rope_tables(rc, t)
    wrt = jnp.pad(w_router, ((0, 0), (0, GW - N_EXP)))
    experts = (w_expert_gate, w_expert_up, w_expert_down)
    h = _layer_even((ctx, x), modsel(0), tbl, w_in_even[0], attn_sink[0], hgrn_lb_logits, hgrn_norm[0], w_out_even[0],
                    ln_g[0], ln_b[0], wrt, b_router, experts + (0,))
    return _layer_odd(h, modsel(1), tbl, _dft_tables(t), w_in_odd[0], gla_gate_w[0], gla_gate_b[0], gla_norm[0],
                      w_out_odd[0], ln_g[1], ln_b[1], wrt, b_router, experts + (1,), t)
```

```python
import functools
import math

import numpy as np
import jax
import jax.numpy as jnp
from jax import lax
from jax.experimental import pallas as pl
from jax.experimental.pallas import tpu as pltpu

F32 = jnp.float32
BF16 = jnp.bfloat16
HIGHEST = lax.Precision.HIGHEST

D = 1024
CTX = 256
GRID_W = 64
LN_EPS = 1e-5
NEG = -1e30
ALPHA = 4.0 ** 0.25
ROPE_BASE = 10000.0

A_HEADS, A_KV, A_HD, A_BLK = 8, 2, 64, 128
R_HEADS = 4
DKP = 128
GLA_DK, GLA_DV, GLA_DVP = 96, 192, 256
GLA_TAU = 16.0
CHUNK = 64
N_EXP, N_GRP, E_PER, TOP_K = 16, 4, 4, 2
N_PAIR = E_PER * (E_PER - 1) // 2
N_CLASS = N_GRP * N_PAIR
D_FF = 512
F_GROUPS, F_GD = 4, 64

TM = 256
IN_SUB = 3
GW = 128
XROWS = 8
YROWS = 8
RING = 4
V7X_VMEM_LIMIT = 56 * 1024 * 1024


def _cp(sem, vmem=None):
    return pltpu.CompilerParams(dimension_semantics=sem, vmem_limit_bytes=vmem)


def _ln(x):
    mu = jnp.mean(x, axis=-1, keepdims=True)
    xc = x - mu
    var = jnp.mean(xc * xc, axis=-1, keepdims=True)
    return xc * lax.rsqrt(var + LN_EPS)


def _silu(x):
    return x * jax.nn.sigmoid(x)


def _split2(w, axis):
    hi = w.astype(BF16)
    return jnp.concatenate([hi, (w - hi.astype(F32)).astype(BF16)], axis=axis)


_NT = (((1,), (1,)), ((), ()))
_TN = (((0,), (0,)), ((), ()))


def _ada_kernel(c_ref, w_ref, b_ref, o_ref):
    s = _silu(c_ref[...])
    o_ref[0] = jnp.dot(s, w_ref[0], precision=HIGHEST, preferred_element_type=F32) + b_ref[0]


def _ada_mods(cc, w_ada, b_ada):
    depth, _, n = w_ada.shape
    tn = 1536
    return pl.pallas_call(
        _ada_kernel,
        grid=(depth, n // tn),
        in_specs=[pl.BlockSpec((16, D), lambda l, i: (0, 0)),
                  pl.BlockSpec((1, D, tn), lambda l, i: (l, 0, i)),
                  pl.BlockSpec((1, 1, tn), lambda l, i: (l, 0, i))],
        out_specs=pl.BlockSpec((1, 16, tn), lambda l, i: (l, 0, i)),
        out_shape=jax.ShapeDtypeStruct((depth, 16, n), F32),
        compiler_params=_cp(("arbitrary", "arbitrary"), 40 * 1024 * 1024),
        name="ada_mod",
    )(cc, w_ada, b_ada.reshape(depth, 1, n))


def _rows_of(h_refs):
    if len(h_refs) == 1:
        return h_refs[0][0]
    return jnp.where(pl.program_id(1) == 0, h_refs[0][0], h_refs[1][0])


def _row_specs(h, row_off=0):
    if isinstance(h, tuple):
        assert row_off == 0 and h[0].shape[1] == TM
        return [pl.BlockSpec((1, TM, D), lambda i, j: (i, 0, 0)),
                pl.BlockSpec((1, TM, D), lambda i, j: (i, jnp.maximum(j - 1, 0), 0))], list(h)
    return [pl.BlockSpec((1, TM, D), lambda i, j: (i, j + row_off, 0))], [h]


def _inproj_kernel(*refs, segs):
    *h_refs, mod0_ref, mod1_ref, w_ref, tbl_ref, o_ref = refs
    if len(h_refs) == 1:
        x = h_refs[0][0]
    else:
        ctx_ref, *x_refs = h_refs
        x = jnp.concatenate([jnp.where(pl.program_id(1) == 0, ctx_ref[0], x_refs[0][0])]
                            + [r[0] for r in x_refs[1:]], axis=0)
    xn = _ln(x)
    u = jnp.concatenate([xn[:TM] * (1.0 + mod0_ref[0, 0, 1:2, :]) + mod0_ref[0, 0, 0:1, :],
                         xn[TM:] * (1.0 + mod1_ref[0, 0, 1:2, :]) + mod1_ref[0, 0, 0:1, :]], axis=0).astype(BF16)
    lane = lax.broadcasted_iota(jnp.int32, (x.shape[0], 128), 1)
    low = (lane % 32) < 16
    for c0, c1, rope in segs:
        acc = jnp.dot(u, w_ref[:, c0:c1], preferred_element_type=F32)
        if rope is not None:
            partner = jnp.where(low, pltpu.roll(acc, 112, axis=1), pltpu.roll(acc, 16, axis=1))
            acc = acc * tbl_ref[2 * rope] + partner * tbl_ref[2 * rope + 1]
        o_ref[0, :, c0:c1] = acc.astype(BF16)


def _inproj(h, modsel, w, tbl, segs):
    b, rc = modsel.shape[0], tbl.shape[1]
    n = w.shape[1]
    tmi = IN_SUB * TM
    assert rc % tmi == 0
    if isinstance(h, tuple):
        assert h[0].shape[1] == TM
        h_specs = [pl.BlockSpec((1, TM, D), lambda i, j: (i, 0, 0))] + [
            pl.BlockSpec((1, TM, D), lambda i, j, k=k: (i, jnp.maximum(IN_SUB * j + k - 1, 0), 0))
            for k in range(IN_SUB)]
        h_args = [h[0]] + [h[1]] * IN_SUB
    else:
        h_specs, h_args = [pl.BlockSpec((1, tmi, D), lambda i, j: (i, j, 0))], [h]
    return pl.pallas_call(
        functools.partial(_inproj_kernel, segs=segs),
        grid=(b, rc // tmi),
        in_specs=h_specs + [pl.BlockSpec((1, 1, 6, D), lambda i, j: (i, jnp.minimum(j, 1), 0, 0)),
                            pl.BlockSpec((1, 1, 6, D), lambda i, j: (i, 1, 0, 0)),
                            pl.BlockSpec((D, n), lambda i, j: (0, 0)),
                            pl.BlockSpec((4, tmi, 128), lambda i, j: (0, j, 0))],
        out_specs=pl.BlockSpec((1, tmi, n), lambda i, j: (i, j, 0)),
        out_shape=jax.ShapeDtypeStruct((b, rc, n), BF16),
        compiler_params=_cp(("parallel", "arbitrary"), V7X_VMEM_LIMIT),
        name="inproj",
    )(*h_args, modsel, modsel, w, tbl)


def _attn_kernel(sink_ref, q_ref, kvc_ref, kvp_ref, kvm_ref, kvn_ref, bias_ref, o_ref):
    blk = pl.program_id(1)
    g_per = A_HEADS // A_KV
    nq = g_per * A_BLK
    q = q_ref[0] * (A_HD ** -0.5)
    rgrp = lax.broadcasted_iota(jnp.int32, (nq, 1), 0) // A_BLK
    q4, sink_col = [], []
    for h in range(A_KV):
        q4.append(jnp.concatenate([q[:, A_HD * (g_per * h + g):A_HD * (g_per * h + g + 1)] for g in range(g_per)],
                                  axis=0))
        col = jnp.zeros((nq, 1), F32)
        for g in range(g_per):
            col = jnp.where(rgrp == g, sink_ref[h * g_per + g], col)
        sink_col.append(col)

    def attend(kv, bias):
        s = [lax.dot_general(q4[h], kv[:, 2 * A_HD * h:2 * A_HD * h + A_HD], _NT, preferred_element_type=F32)
             for h in range(A_KV)]
        if bias is not None:
            bias4 = jnp.concatenate([bias] * g_per, axis=0)
            s = [x + bias4 for x in s]
        m = [jnp.maximum(jnp.max(s[h], axis=-1, keepdims=True), sink_col[h]) for h in range(A_KV)]
        p = [jnp.exp(s[h] - m[h]) for h in range(A_KV)]
        den = [jnp.sum(p[h], axis=-1, keepdims=True) + jnp.exp(sink_col[h] - m[h]) for h in range(A_KV)]
        o = [jnp.dot(p[h].astype(BF16), kv[:, 2 * A_HD * h + A_HD:2 * A_HD * (h + 1)], preferred_element_type=F32)
             / den[h] for h in range(A_KV)]
        o_ref[0] = jnp.concatenate([o[h][A_BLK * g:A_BLK * (g + 1), :] for h in range(A_KV) for g in range(g_per)],
                                   axis=1).astype(BF16)

    @pl.when(blk < CTX // A_BLK)
    def _():
        attend(kvc_ref[0], None)

    @pl.when(blk >= CTX // A_BLK)
    def _():
        attend(jnp.concatenate([kvc_ref[0], kvp_ref[0], kvm_ref[0], kvn_ref[0]], axis=0), bias_ref[0])


def _attn_bias():
    nk = CTX + 3 * A_BLK
    qi = jnp.arange(A_BLK, dtype=jnp.int32)[:, None]
    col = jnp.arange(nk, dtype=jnp.int32)[None, :]
    rel = col - (CTX + A_BLK)
    dist = qi - rel

    def one(lo, hi):
        in_win = (dist <= A_BLK) & (dist >= -A_BLK) & (rel >= lo) & (rel < hi)
        return jnp.where((col < CTX) | in_win, 0.0, NEG).astype(F32)

    return jnp.stack([one(0, 2 * A_BLK), one(-A_BLK, 2 * A_BLK), one(-A_BLK, A_BLK)])


def _attention(proj, sink, kv_blk):
    b, rc, _ = proj.shape
    nb = rc // A_BLK
    c_b = CTX // A_BLK
    assert nb - c_b >= 2
    kw = 2 * A_HD * A_KV

    def band(off):
        return lambda i, j: (i, jnp.clip(j + off, c_b, nb - 1), kv_blk)

    def which_bias(i, j):
        return (jnp.where(j <= c_b, 0, jnp.where(j == nb - 1, 2, 1)), 0, 0)

    return pl.pallas_call(
        _attn_kernel,
        grid=(b, nb),
        in_specs=[pl.BlockSpec(memory_space=pltpu.SMEM),
                  pl.BlockSpec((1, A_BLK, A_HEADS * A_HD), lambda i, j: (i, j, 0)),
                  pl.BlockSpec((1, CTX, kw), lambda i, j: (i, 0, kv_blk)),
                  pl.BlockSpec((1, A_BLK, kw), band(-1)),
                  pl.BlockSpec((1, A_BLK, kw), band(0)),
                  pl.BlockSpec((1, A_BLK, kw), band(1)),
                  pl.BlockSpec((1, A_BLK, CTX + 3 * A_BLK), which_bias)],
        out_specs=pl.BlockSpec((1, A_BLK, A_HEADS * A_HD), lambda i, j: (i, j, 0)),
        out_shape=jax.ShapeDtypeStruct((b, rc, A_HEADS * A_HD), BF16),
        compiler_params=_cp(("parallel", "arbitrary")),
        name="win_attn",
    )(sink, proj, proj, proj, proj, proj, _attn_bias())


def _scan_blocks(dirs, *, dvp):
    r_i = lax.broadcasted_iota(jnp.int32, (CHUNK, CHUNK), 0)
    c_i = lax.broadcasted_iota(jnp.int32, (CHUNK, CHUNK), 1)
    units = []
    for q, k, v, logf, s_ref, o_ref, reverse in dirs:
        nchunk = q.shape[0] // CHUNK
        tri = (r_i <= c_i) if reverse else (r_i >= c_i)
        tri3 = jnp.concatenate([jnp.where(tri, 1.0, 0.0).astype(BF16)] * 3, axis=1)
        for c in (range(nchunk - 1, -1, -1) if reverse else range(nchunk)):
            sl = slice(c * CHUNK, (c + 1) * CHUNK)
            units.append(dict(q=q[sl], k=k[sl], v=v[sl].astype(BF16), lf=logf[sl], sl=sl, tri=tri, tri3=tri3,
                              i_tot=0 if reverse else CHUNK - 1,
                              i_ref=CHUNK // 2 - 1 if reverse else CHUNK // 2, s_ref=s_ref, o_ref=o_ref))
    heads = [(slice(h * DKP, (h + 1) * DKP), slice(h * dvp, (h + 1) * dvp)) for h in range(R_HEADS)]
    for u in units:
        lf = u["lf"]
        hi = lf.astype(BF16)
        r1 = lf - hi.astype(F32)
        mid = r1.astype(BF16)
        lo = (r1 - mid.astype(F32)).astype(BF16)
        u["bsum"] = jnp.dot(u["tri3"], jnp.concatenate([hi, mid, lo], axis=0), preferred_element_type=F32)
    for u in units:
        bsum = u["bsum"]
        tot = bsum[u["i_tot"]:u["i_tot"] + 1, :]
        ref = bsum[u["i_ref"]:u["i_ref"] + 1, :]
        qg = u["q"] * jnp.exp(bsum - ref)
        kg = u["k"] * jnp.exp(ref - bsum)
        u["qb"] = (qg * jnp.exp(ref)).astype(BF16)
        u["kd"] = (kg * jnp.exp(tot - ref)).astype(BF16)
        u["qg"] = qg.astype(BF16)
        u["kg"] = kg.astype(BF16)
        u["dec"] = jnp.exp(tot)
    for u in units:
        u["att"] = [lax.dot_general(u["qg"][:, ks], u["kg"][:, ks], _NT, preferred_element_type=F32)
                    for ks, _ in heads]
        u["ds"] = [lax.dot_general(u["v"][:, vs], u["kd"][:, ks], _TN, preferred_element_type=F32)
                   for ks, vs in heads]
    for u in units:
        u["oi"] = [jnp.dot(jnp.where(u["tri"], a, 0.0).astype(BF16), u["v"][:, vs], preferred_element_type=F32)
                   for a, (_, vs) in zip(u["att"], heads)]
    state = {}
    for u in units:
        key = id(u["s_ref"])
        if key not in state:
            state[key] = [u["s_ref"][h] for h in range(R_HEADS)]
        st = state[key]
        outs = []
        for h, (ks, _) in enumerate(heads):
            outs.append(u["oi"][h] + lax.dot_general(u["qb"][:, ks], st[h].astype(BF16), _NT,
                                                     preferred_element_type=F32))
            st[h] = st[h] * u["dec"][:, ks] + u["ds"][h]
        u["o_ref"][0, u["sl"], :] = jnp.concatenate(outs, axis=1).astype(BF16)
    for _, _, _, _, s_ref, _, _ in dirs:
        for h in range(R_HEADS):
            s_ref[h] = state[id(s_ref)][h]


def _hgrn_scan_kernel(lb_ref, qf_ref, vf_ref, zf_ref, qb_ref, vb_ref, zb_ref, of_ref, ob_ref, sf_ref, sb_ref):
    @pl.when(pl.program_id(1) == 0)
    def _():
        sf_ref[...] = jnp.zeros_like(sf_ref)
        sb_ref[...] = jnp.zeros_like(sb_ref)

    def gates(z, lb):
        f = lb + (1.0 - lb) * jax.nn.sigmoid(z.astype(F32))
        return 1.0 - f, jnp.log(f)

    kf, lf = gates(zf_ref[0], lb_ref[0:1, :])
    kb, lbw = gates(zb_ref[0], lb_ref[1:2, :])
    _scan_blocks([(qf_ref[0], kf, vf_ref[0], lf, sf_ref, of_ref, False),
                  (qb_ref[0], kb, vb_ref[0], lbw, sb_ref, ob_ref, True)], dvp=128)


def _bwd_blk(j, nt):
    return jnp.where(j == 0, 0, nt - j)


def _hgrn_scan(proj, lb, cq, cv, cf, cb):
    b, rc, _ = proj.shape
    nt = rc // TM
    w = R_HEADS * DKP

    def fw(col):
        return pl.BlockSpec((1, TM, w), lambda i, j: (i, j, col))

    def bw(col):
        return pl.BlockSpec((1, TM, w), lambda i, j: (i, _bwd_blk(j, nt), col))

    o_sd = jax.ShapeDtypeStruct((b, rc, w), BF16)
    return pl.pallas_call(
        _hgrn_scan_kernel,
        grid=(b, nt),
        in_specs=[pl.BlockSpec((2, w), lambda i, j: (0, 0)),
                  fw(cq), fw(cv), fw(cf), bw(cq), bw(cv), bw(cb)],
        out_specs=[pl.BlockSpec((1, TM, w), lambda i, j: (i, j, 0)),
                   pl.BlockSpec((1, TM, w), lambda i, j: (i, _bwd_blk(j, nt), 0))],
        out_shape=[o_sd, o_sd],
        scratch_shapes=[pltpu.VMEM((R_HEADS, 128, DKP), F32), pltpu.VMEM((R_HEADS, 128, DKP), F32)],
        compiler_params=_cp(("parallel", "arbitrary"), 40 * 1024 * 1024),
        name="hgrn_scan",
    )(lb, proj, proj, proj, proj, proj, proj)


def _gla_scan_kernel(gwf_ref, gwb_ref, gb_ref, qf_ref, kf_ref, vf_ref, rf_ref, qb_ref, kb_ref, vb_ref, rb_ref,
                     of_ref, ob_ref, sf_ref, sb_ref):
    @pl.when(pl.program_id(1) == 0)
    def _():
        sf_ref[...] = jnp.zeros_like(sf_ref)
        sb_ref[...] = jnp.zeros_like(sb_ref)

    def decay(r, gw, gb):
        zz = jnp.dot(jnp.concatenate([r, r], axis=1), _split2(gw, 0), preferred_element_type=F32) + gb
        return (jnp.minimum(zz, 0.0) - jnp.log(1.0 + jnp.exp(-jnp.abs(zz)))) * (1.0 / GLA_TAU)

    qs = GLA_DK ** -0.5
    _scan_blocks([(qf_ref[0].astype(F32) * qs, kf_ref[0], vf_ref[0],
                   decay(rf_ref[0], gwf_ref[...], gb_ref[0:1, :]), sf_ref, of_ref, False),
                  (qb_ref[0].astype(F32) * qs, kb_ref[0], vb_ref[0],
                   decay(rb_ref[0], gwb_ref[...], gb_ref[1:2, :]), sb_ref, ob_ref, True)], dvp=GLA_DVP)


def _gla_scan(proj, gwf, gwb, gb, cq, ck, cv, cr):
    b, rc, _ = proj.shape
    nt = rc // TM
    wk = R_HEADS * DKP
    wv = R_HEADS * GLA_DVP

    def spec(width, col, back):
        if back:
            return pl.BlockSpec((1, TM, width), lambda i, j: (i, _bwd_blk(j, nt), col))
        return pl.BlockSpec((1, TM, width), lambda i, j: (i, j, col))

    o_sd = jax.ShapeDtypeStruct((b, rc, wv), BF16)
    full = lambda shape: pl.BlockSpec(shape, lambda i, j: (0,) * len(shape))
    return pl.pallas_call(
        _gla_scan_kernel,
        grid=(b, nt),
        in_specs=[full((128, wk)), full((128, wk)), full((2, wk)),
                  spec(wk, cq, False), spec(wk, ck, False), spec(wv, cv, False), spec(128, cr, False),
                  spec(wk, cq, True), spec(wk, ck, True), spec(wv, cv, True), spec(128, cr, True)],
        out_specs=[spec(wv, 0, False), spec(wv, 0, True)],
        out_shape=[o_sd, o_sd],
        scratch_shapes=[pltpu.VMEM((R_HEADS, GLA_DVP, DKP), F32), pltpu.VMEM((R_HEADS, GLA_DVP, DKP), F32)],
        compiler_params=_cp(("parallel", "arbitrary"), 48 * 1024 * 1024),
        name="gla_scan",
    )(gwf, gwb, gb, proj, proj, proj, proj, proj, proj, proj, proj)


def _fourier_kernel(z_ref, cs_ref, ct_ref, st_ref, o_ref, *, scale):
    w = F_GROUPS * F_GD
    zz = jnp.dot(z_ref[0, CTX:, :], cs_ref[...], preferred_element_type=F32).astype(BF16)
    y = jnp.dot(ct_ref[...], zz[:, :w], preferred_element_type=F32)
    y = y - jnp.dot(st_ref[...], zz[:, w:], preferred_element_type=F32)
    o_ref[0] = (y * scale).astype(BF16)


def _fourier(proj, zcol, cs, ct, st, t):
    b, rc, _ = proj.shape
    w = F_GROUPS * F_GD
    const = lambda shape: pl.BlockSpec(shape, lambda i: (0, 0), pipeline_mode=pl.Buffered(1))
    return pl.pallas_call(
        functools.partial(_fourier_kernel, scale=1.0 / math.sqrt(t * F_GD)),
        grid=(b,),
        in_specs=[pl.BlockSpec((1, rc, w), lambda i: (i, 0, zcol)), const((w, 2 * w)), const((t, t)), const((t, t))],
        out_specs=pl.BlockSpec((1, t, w), lambda i: (i, 0, 0)),
        out_shape=jax.ShapeDtypeStruct((b, t, w), BF16),
        compiler_params=_cp(("arbitrary",), 48 * 1024 * 1024),
        name="fourier",
    )(proj, cs, ct, st)


def _route(logits, rb_ref):
    lt = logits.T
    l = [lt[e:e + 1, :] for e in range(N_EXP)]
    m = functools.reduce(jnp.maximum, l)
    ex = [jnp.exp(v - m) for v in l]
    den = functools.reduce(lambda a, c: a + c, ex)
    p = [v / den for v in ex]
    sel = [p[e] + rb_ref[e] for e in range(N_EXP)]
    g_score, g_gate, g_pick = [], [], []
    for g in range(N_GRP):
        s = sel[E_PER * g:E_PER * (g + 1)]
        chosen = []
        for i in range(E_PER):
            rank = jnp.zeros_like(s[i])
            for j in range(E_PER):
                if j == i:
                    continue
                ahead = (s[j] >= s[i]) if j < i else (s[j] > s[i])
                rank = rank + jnp.where(ahead, 1.0, 0.0)
            chosen.append(rank < 2.0)
        g_score.append(functools.reduce(lambda a, c: a + c,
                                        [jnp.where(chosen[i], s[i], 0.0) for i in range(E_PER)]))
        g_gate.append([jnp.where(chosen[i], p[E_PER * g + i], 0.0) for i in range(E_PER)])
        g_pick.append([jnp.where(chosen[i], 1.0, 0.0) for i in range(E_PER)])
    best = jnp.zeros_like(g_score[0])
    best_s = g_score[0]
    for g in range(1, N_GRP):
        upd = g_score[g] > best_s
        best = jnp.where(upd, float(g), best)
        best_s = jnp.where(upd, g_score[g], best_s)
    gate, pick = [], []
    for i in range(E_PER):
        gi, ci = g_gate[0][i], g_pick[0][i]
        for g in range(1, N_GRP):
            gi = jnp.where(best == float(g), g_gate[g][i], gi)
            ci = jnp.where(best == float(g), g_pick[g][i], ci)
        gate.append(gi)
        pick.append(ci)
    wsum = functools.reduce(lambda a, c: a + c, gate)
    lo = functools.reduce(jnp.minimum, [jnp.where(pick[i] > 0.5, float(i), float(E_PER)) for i in range(E_PER)])
    hi = functools.reduce(jnp.maximum, [jnp.where(pick[i] > 0.5, float(i), -1.0) for i in range(E_PER)])
    pair = jnp.where(lo == 0.0, hi - 1.0, jnp.where(lo == 1.0, hi + 1.0, float(N_PAIR - 1)))
    g_lo = functools.reduce(lambda a, c: a + c, [jnp.where(lo == float(i), gate[i], 0.0) for i in range(E_PER)])
    g_hi = functools.reduce(lambda a, c: a + c, [jnp.where(hi == float(i), gate[i], 0.0) for i in range(E_PER)])
    row = lax.broadcasted_iota(jnp.int32, (GW, lt.shape[1]), 0)
    side = jnp.where(row == 0, g_lo / wsum, jnp.where(row == 1, g_hi / wsum, 0.0))
    return side.T, best * float(N_PAIR) + pair


def _class_sort(cls):
    ncp = 32
    crow = lax.broadcasted_iota(jnp.int32, (ncp, TM), 0).astype(F32)
    onehot = jnp.where(crow == cls, 1.0, 0.0)
    r_i = lax.broadcasted_iota(jnp.int32, (TM, TM), 0)
    c_i = lax.broadcasted_iota(jnp.int32, (TM, TM), 1)
    upper = jnp.where(r_i <= c_i, 1.0, 0.0).astype(BF16)
    pref = jnp.dot(onehot.astype(BF16), upper, preferred_element_type=F32)
    counts = jnp.sum(onehot, axis=1, keepdims=True)
    below = jnp.where(lax.broadcasted_iota(jnp.int32, (ncp, ncp), 0) > lax.broadcasted_iota(jnp.int32, (ncp, ncp), 1),
                      1.0, 0.0).astype(BF16)
    offs = jnp.max(jnp.dot(below, jnp.broadcast_to(counts, (ncp, 128)).astype(BF16), preferred_element_type=F32),
                   axis=1, keepdims=True)
    dest = jnp.sum(onehot * (offs + pref - 1.0), axis=0, keepdims=True)
    perm = jnp.where(r_i.astype(F32) == dest, 1.0, 0.0).astype(BF16)
    lane = lax.broadcasted_iota(jnp.int32, (ncp, TM), 1).astype(F32)
    slot_cls = jnp.sum(jnp.where((lane >= offs) & (lane < offs + counts), crow, 0.0), axis=0, keepdims=True)
    return perm, slot_cls


def _outproj_kernel(rb_ref, a_ref, of_ref, ob_ref, gt_ref, *refs, dvp, dv):
    *h_refs, mod_ref, gn_ref, wa_ref, wr_ref, lng_ref, lnb_ref, wrt_ref, h1_ref, u2_ref, gate_ref, grp_ref = refs
    o = of_ref[0].astype(F32) + ob_ref[0].astype(F32)
    parts = []
    for hh in range(R_HEADS):
        sl = slice(hh * dvp, (hh + 1) * dvp)
        oh = o[:, sl]
        ms = jnp.sum(oh * oh, axis=-1, keepdims=True) * (1.0 / dv)
        parts.append(oh * lax.rsqrt(ms + LN_EPS))
    rec = jnp.concatenate(parts, axis=1) * gn_ref[...] * _silu(gt_ref[0].astype(F32))
    y = jnp.dot(a_ref[0], wa_ref[...], preferred_element_type=F32)
    y = y + jnp.dot(rec.astype(BF16), wr_ref[...], preferred_element_type=F32)
    g1 = mod_ref[0, 0, 2:3, :]
    h1 = _ln(ALPHA * _rows_of(h_refs) + g1 * y) * lng_ref[0:1, :] + lnb_ref[0:1, :]
    h1_ref[0] = h1
    u2 = _ln(h1) * (1.0 + mod_ref[0, 0, 4:5, :]) + mod_ref[0, 0, 3:4, :]
    lg = jnp.dot(_split2(u2, 0), _split2(wrt_ref[...], 1), preferred_element_type=F32)
    lg = lg[:TM] + lg[TM:]
    logits = lg[:, :GW] + lg[:, GW:]
    side, cls = _route(logits, rb_ref)
    gate_ref[...] = side
    perm, slot_cls = _class_sort(cls)
    grp_ref[0, 0, 0:1, :] = cls
    grp_ref[0, 0, 1:2, :] = slot_cls
    xs = jnp.dot(perm, u2.astype(BF16), preferred_element_type=F32)
    for c in range(XROWS):
        u2_ref[pl.ds(c, TM, stride=XROWS), :] = xs[:, 128 * c:128 * (c + 1)]


def _outproj(b_router, a, a_col, a_w, o_f, o_b, proj, g_col, h, modsel, gnorm, wa, wr, lng, lnb, wrt, *,
             dv, dvp, row_off):
    b, rc, _ = proj.shape
    nt = rc // TM - row_off
    wrec = R_HEADS * dvp
    full = lambda shape: pl.BlockSpec(shape, lambda i, j: (0,) * len(shape))
    h_specs, h_args = _row_specs(h, row_off)
    return pl.pallas_call(
        functools.partial(_outproj_kernel, dvp=dvp, dv=dv),
        grid=(b, nt),
        in_specs=[pl.BlockSpec(memory_space=pltpu.SMEM),
                  pl.BlockSpec((1, TM, a_w), lambda i, j: (i, j + (row_off if a.shape[1] == rc else 0), a_col)),
                  pl.BlockSpec((1, TM, wrec), lambda i, j: (i, j + row_off, 0)),
                  pl.BlockSpec((1, TM, wrec), lambda i, j: (i, j + row_off, 0)),
                  pl.BlockSpec((1, TM, wrec), lambda i, j: (i, j + row_off, g_col))] + h_specs + [
                  pl.BlockSpec((1, 1, 6, D), lambda i, j: (i, jnp.minimum(j + row_off, 1), 0, 0)),
                  full((1, wrec)), full((a_w, D)), full((wrec, D)), full((1, D)), full((1, D)), full((D, GW))],
        out_specs=[pl.BlockSpec((1, TM, D), lambda i, j: (i, j, 0)),
                   pl.BlockSpec((TM * XROWS, 128), lambda i, j: (i * nt + j, 0)),
                   pl.BlockSpec((TM, GW), lambda i, j: (i * nt + j, 0)),
                   pl.BlockSpec((1, 1, 2, TM), lambda i, j: (i, j, 0, 0))],
        out_shape=[jax.ShapeDtypeStruct((b, nt * TM, D), F32),
                   jax.ShapeDtypeStruct((b * nt * TM * XROWS, 128), F32),
                   jax.ShapeDtypeStruct((b * nt * TM, GW), F32),
                   jax.ShapeDtypeStruct((b, nt, 2, TM), F32)],
        compiler_params=_cp(("parallel", "arbitrary"), V7X_VMEM_LIMIT),
        name="outproj",
    )(b_router, a, o_f, o_b, proj, *h_args, modsel, gnorm, wa, wr, lng, lnb, wrt)


def _start_token_gather(idx_ref, base, src_hbm, dst_ref, sem, rows):
    def body(r, carry):
        t = idx_ref[base + r]
        pltpu.make_async_copy(src_hbm.at[pl.ds(pl.multiple_of(t * rows, rows), rows), :],
                              dst_ref.at[pl.ds(pl.multiple_of(r * rows, rows), rows), :], sem).start()
        return carry

    lax.fori_loop(0, dst_ref.shape[0] // rows, body, 0, unroll=8)


def _wait_token_gather(src_hbm, dst_ref, sem):
    pltpu.make_async_copy(src_hbm.at[pl.ds(0, dst_ref.shape[0]), :], dst_ref, sem).wait()


def _gather_ring(idx_ref, first_of, step, last, src_hbm, buf_ref, sem, rows):
    depth = buf_ref.shape[0]
    slot = step % depth

    @pl.when(step == 0)
    def _():
        for k in range(depth - 1):
            _start_token_gather(idx_ref, first_of(jnp.minimum(k, last)), src_hbm, buf_ref.at[k], sem.at[k], rows)

    _wait_token_gather(src_hbm, buf_ref.at[slot], sem.at[slot])
    ahead = (step + depth - 1) % depth
    nxt = first_of(jnp.minimum(step + depth - 1, last))

    def issue(r0, r1):
        for r in range(r0, r1):
            t = idx_ref[nxt + r]
            pltpu.make_async_copy(src_hbm.at[pl.ds(pl.multiple_of(t * rows, rows), rows), :],
                                  buf_ref.at[ahead, pl.ds(r * rows, rows), :], sem.at[ahead]).start(priority=r % 2)

    def finish():
        @pl.when(step == last)
        def _():
            for k in range(1, depth):
                other = (step + k) % depth
                _wait_token_gather(src_hbm, buf_ref.at[other], sem.at[other])

    return slot, issue, finish


def _moe_kernel(te_ref, first_ref, src_ref, x_hbm, *refs):
    w_refs, (o_ref, xb_ref, sem), wb_refs = refs[:6], refs[6:9], refs[9:]
    i = pl.program_id(0)
    slot, issue, finish = _gather_ring(src_ref, lambda k: first_ref[k], i, pl.num_programs(0) - 1,
                                       x_hbm, xb_ref, sem, XROWS)
    prev = jnp.maximum(i - 1, 0)
    for s in range(TOP_K):
        @pl.when((i == 0) | (te_ref[s, i] != te_ref[s, prev]))
        def _():
            for k in range(3):
                wb_refs[3 * s + k][...] = w_refs[3 * s + k][...].astype(BF16)

    xb = xb_ref.at[slot]
    x = jnp.concatenate([xb[pl.ds(c, TM, stride=XROWS), :] for c in range(D // 128)], axis=1).astype(BF16)
    per = TM // (3 * TOP_K)
    edges = [k * per for k in range(3 * TOP_K)] + [TM]
    for s in range(TOP_K):
        w1b, w3b, w2b = wb_refs[3 * s:3 * s + 3]
        issue(edges[3 * s], edges[3 * s + 1])
        h1 = jnp.dot(x, w1b[...], preferred_element_type=F32)
        issue(edges[3 * s + 1], edges[3 * s + 2])
        h3 = jnp.dot(x, w3b[...], preferred_element_type=F32)
        hh = (_silu(h1) * h3).astype(BF16)
        issue(edges[3 * s + 2], edges[3 * s + 3])
        y = jnp.dot(hh, w2b[...], preferred_element_type=F32)
        for c in range(D // 128):
            o_ref[s, pl.ds(c, TM, stride=YROWS), :] = y[:, 128 * c:128 * (c + 1)]
    finish()


def _moe(tile_exp, tile_first, order, u2x, w1, w3, w2, layer):
    mp = tile_exp.shape[1] * TM

    def wspec(s, k, n):
        return pl.BlockSpec((None, None, k, n), lambda i, te, tf, od: (layer, te[s, i], 0, 0),
                            pipeline_mode=pl.Buffered(1))

    w_specs, w_args, w_scratch = [], [], []
    for s in range(TOP_K):
        w_specs += [wspec(s, D, D_FF), wspec(s, D, D_FF), wspec(s, D_FF, D)]
        w_args += [w1, w3, w2]
        w_scratch += [pltpu.VMEM((D, D_FF), BF16), pltpu.VMEM((D, D_FF), BF16), pltpu.VMEM((D_FF, D), BF16)]
    return pl.pallas_call(
        _moe_kernel,
        grid_spec=pltpu.PrefetchScalarGridSpec(
            num_scalar_prefetch=3, grid=(mp // TM,),
            in_specs=[pl.BlockSpec(memory_space=pl.ANY)] + w_specs,
            out_specs=pl.BlockSpec((TOP_K, TM * YROWS, 128), lambda i, te, tf, od: (0, i, 0)),
            scratch_shapes=[pltpu.VMEM((RING, TM * XROWS, 128), F32), pltpu.SemaphoreType.DMA((RING,))] + w_scratch),
        out_shape=jax.ShapeDtypeStruct((TOP_K, mp * YROWS, 128), F32),
        compiler_params=_cp(("arbitrary",), 48 * 1024 * 1024),
        name="moe",
    )(tile_exp, tile_first, order, u2x, *w_args)


def _combine_kernel(pos_ref, y_hbm, g_ref, h_ref, mod_ref, lng_ref, lnb_ref, o_ref, yb_ref, sem, *, nt):
    step = pl.program_id(0) * nt + pl.program_id(1)
    slot, issue, finish = _gather_ring(pos_ref, lambda k: k * (TOP_K * TM), step, pl.num_programs(0) * nt - 1,
                                       y_hbm, yb_ref, sem, YROWS)
    yb = yb_ref.at[slot]
    g2 = mod_ref[0, 0, 5:6, :]
    nstrip = 4
    strip = TM // nstrip
    per = TOP_K * TM // nstrip
    for k in range(nstrip):
        r0 = k * strip
        issue(k * per, (k + 1) * per)
        gates = g_ref[r0:r0 + strip, :]
        y = functools.reduce(lambda a, c: a + c, [
            gates[:, s:s + 1] * jnp.concatenate(
                [yb[pl.ds((s * TM + r0) * YROWS + c, strip, stride=YROWS), :] for c in range(D // 128)], axis=1)
            for s in range(TOP_K)])
        o_ref[0, r0:r0 + strip, :] = (_ln(ALPHA * h_ref[0, r0:r0 + strip, :] + g2 * y) * lng_ref[0:1, :]
                                      + lnb_ref[0:1, :])
    finish()


def _combine(pos, y_sorted, gates, h1, modsel, lng, lnb, row_off):
    b, r, _ = h1.shape
    nt = r // TM
    full = lambda shape: pl.BlockSpec(shape, lambda i, j, p: (0,) * len(shape))
    return pl.pallas_call(
        functools.partial(_combine_kernel, nt=nt),
        grid_spec=pltpu.PrefetchScalarGridSpec(
            num_scalar_prefetch=1, grid=(b, nt),
            in_specs=[pl.BlockSpec(memory_space=pl.ANY),
                      pl.BlockSpec((TM, GW), lambda i, j, p: (i * nt + j, 0)),
                      pl.BlockSpec((1, TM, D), lambda i, j, p: (i, j, 0)),
                      pl.BlockSpec((1, 1, 6, D), lambda i, j, p: (i, jnp.minimum(j + row_off, 1), 0, 0)),
                      full((1, D)), full((1, D))],
            out_specs=pl.BlockSpec((1, TM, D), lambda i, j, p: (i, j, 0)),
            scratch_shapes=[pltpu.VMEM((RING, TOP_K * TM * YROWS, 128), F32), pltpu.SemaphoreType.DMA((RING,))]),
        out_shape=jax.ShapeDtypeStruct((b, r, D), F32),
        compiler_params=_cp(("arbitrary", "arbitrary")),
        name="moe_combine",
    )(pos, y_sorted, gates, h1, modsel, lng, lnb)


def _sort_by_class(cls, slot_cls):
    n = cls.shape[0]
    onehot = (cls[:, None] == jnp.arange(N_CLASS, dtype=jnp.int32)[None, :]).astype(jnp.int32)
    csum = jnp.cumsum(onehot, axis=0)
    counts = csum[-1]
    padded = ((counts + TM - 1) // TM) * TM
    ends = jnp.cumsum(padded)
    starts = ends - padded
    rank = jnp.sum(csum * onehot, axis=1) - 1
    pos = (starts[cls] + rank).astype(jnp.int32)
    _, order = lax.sort((slot_cls, jnp.arange(n, dtype=jnp.int32)), num_keys=1, is_stable=True)
    order = jnp.concatenate([order, jnp.zeros((TM,), jnp.int32)])
    ntile = n // TM + N_CLASS
    tile_start = jnp.arange(ntile, dtype=jnp.int32) * TM
    tile_cls = jnp.minimum(jnp.sum((tile_start[:, None] >= ends[None, :]).astype(jnp.int32), axis=1), N_CLASS - 1)
    first = (jnp.cumsum(counts) - counts)[tile_cls] + tile_start - starts[tile_cls]
    return pos, order, tile_cls.astype(jnp.int32), jnp.minimum(first, n).astype(jnp.int32)


_PAIRS = np.array([(a, c) for a in range(E_PER) for c in range(a + 1, E_PER)], np.int32)


def _moe_layer(u2x, gates, cls, h1, modsel, experts, lng, lnb, row_off):
    cls = cls.astype(jnp.int32)
    pos, order, tile_cls, tile_first = _sort_by_class(cls[:, :, 0].reshape(-1), cls[:, :, 1].reshape(-1))
    tile_exp = (E_PER * (tile_cls // N_PAIR))[None, :] + jnp.asarray(_PAIRS).T[:, tile_cls % N_PAIR]
    y_sorted = _moe(tile_exp.astype(jnp.int32), tile_first, order, u2x, *experts)
    mp = y_sorted.shape[1] // YROWS
    pos2 = jnp.stack([pos, pos + mp]).reshape(TOP_K, -1, TM).transpose(1, 0, 2).reshape(-1)
    return _combine(pos2, y_sorted.reshape(TOP_K * mp * YROWS, 128), gates, h1, modsel, lng, lnb, row_off)


def _rope_tables(rc, t):
    half = A_HD // 4
    inv = ROPE_BASE ** (-jnp.arange(half, dtype=F32) / half)
    tpos = jnp.arange(t)
    rows = (tpos // GRID_W).astype(F32)
    cols = (tpos % GRID_W).astype(F32)

    def cs(pos):
        ang = pos[:, None] * inv[None, :]
        c, s = jnp.cos(ang), jnp.sin(ang)
        return jnp.concatenate([c, c], -1), jnp.concatenate([-s, s], -1)

    cr, sr = cs(rows)
    cc_, sc_ = cs(cols)
    cos_h = jnp.concatenate([cr, cc_], -1)
    sin_h = jnp.concatenate([sr, sc_], -1)
    one = jnp.ones((t, A_HD), F32)
    zero = jnp.zeros((t, A_HD), F32)
    lat = jnp.stack([jnp.concatenate([cos_h, cos_h], -1), jnp.concatenate([sin_h, sin_h], -1),
                     jnp.concatenate([cos_h, one], -1), jnp.concatenate([sin_h, zero], -1)])
    ctx = jnp.stack([jnp.ones((CTX, 128), F32), jnp.zeros((CTX, 128), F32)] * 2)
    return jnp.concatenate([ctx, lat], axis=1)


def _dft_tables(t):
    def cos_sin(num, den):
        ang = (2.0 * math.pi / den) * (num % den).astype(F32)
        return jnp.cos(ang), jnp.sin(ang)

    k = jnp.arange(t, dtype=jnp.int32)[:, None]
    ca, sa = cos_sin(k * jnp.arange(t // GRID_W, dtype=jnp.int32)[None, :], t // GRID_W)
    cb, sb = cos_sin(k * jnp.arange(GRID_W, dtype=jnp.int32)[None, :], t)
    ct = (ca[:, :, None] * cb[:, None, :] - sa[:, :, None] * sb[:, None, :]).reshape(t, t)
    st = (sa[:, :, None] * cb[:, None, :] + ca[:, :, None] * sb[:, None, :]).reshape(t, t)
    c = jnp.arange(F_GD, dtype=jnp.int32)
    cc, sc = cos_sin(c[:, None] * c[None, :], F_GD)
    eye = jnp.eye(F_GROUPS, dtype=F32)
    cs = jnp.concatenate([jnp.kron(eye, cc), jnp.kron(eye, sc)], axis=1)
    return cs.astype(BF16), ct.astype(BF16), st.astype(BF16)


def _pad_heads(w, width, padded):
    lead = w.shape[:-1]
    w = w.reshape(lead + (R_HEADS, width))
    w = jnp.pad(w, [(0, 0)] * len(lead) + [(0, 0), (0, padded - width)])
    return w.reshape(lead + (R_HEADS * padded,))


EVEN_SEGS = tuple([(128 * g, 128 * (g + 1), 0) for g in range(4)] + [(c, c + 512, None) for c in range(512, 3072, 512)]
                  + [(3072 + 128 * g, 3072 + 128 * (g + 1), 1) for g in range(2)])
ODD_W = 512 + 512 + 1024 + 1024 + 256 + 128
ODD_SEGS = tuple([(c, c + 512, None) for c in range(0, 3072, 512)] + [(3072, ODD_W, None)])


def _layer_even(h, modsel, tbl, w_in, sink, lb_logits, g_norm, w_out, lng, lnb, w_router, b_router, experts):
    aq, ak, av, ff, fb, hq, hi, hg = jnp.split(w_in, np.cumsum([512, 128, 128, 512, 512, 512, 512]).tolist(), axis=1)
    kv = [jnp.concatenate([ak[:, A_HD * j:A_HD * (j + 1)], av[:, A_HD * j:A_HD * (j + 1)]], axis=1)
          for j in range(A_KV)]
    w = jnp.concatenate([aq, ff, fb, hq, hi, hg] + kv, axis=1).astype(BF16)
    proj = _inproj(h, modsel, w, tbl, EVEN_SEGS)
    o_att = _attention(proj, sink, 3072 // (2 * A_HD * A_KV))
    lb = jnp.cumsum(jax.nn.softmax(lb_logits.astype(F32), axis=1), axis=1)[:, 0]
    o_f, o_b = _hgrn_scan(proj, lb, 3, 4, 1, 2)
    wo = w_out.astype(BF16)
    h1, u2x, gates, grp = _outproj(b_router, o_att, 0, 512, o_f, o_b, proj, 5, h, modsel, g_norm.reshape(1, -1),
                            wo[:512], wo[512:], lng[0:1], lnb[0:1], w_router, dv=128, dvp=128, row_off=0)
    return _moe_layer(u2x, gates, grp, h1, modsel, experts, lng[1:2], lnb[1:2], 0)


def _layer_odd(h, modsel, tbl, dft, w_in, gate_w, gate_b, g_norm, w_out, lng, lnb, w_router, b_router, experts, t):
    q, k, v, rf, rb, g, z = jnp.split(w_in, np.cumsum([384, 384, 768, 16, 16, 768]).tolist(), axis=1)
    r = jnp.pad(jnp.concatenate([rf, rb], axis=1), ((0, 0), (0, 96)))
    w = jnp.concatenate([_pad_heads(q, GLA_DK, DKP), _pad_heads(k, GLA_DK, DKP), _pad_heads(v, GLA_DV, GLA_DVP),
                         _pad_heads(g, GLA_DV, GLA_DVP), z, r], axis=1).astype(BF16)
    proj = _inproj(h, modsel, w, tbl, ODD_SEGS)
    gw = _pad_heads(gate_w, GLA_DK, DKP)
    gwf = jnp.pad(gw[0], ((0, 112), (0, 0)))
    gwb = jnp.pad(gw[1], ((16, 96), (0, 0)))
    gb = _pad_heads(gate_b, GLA_DK, DKP)
    o_f, o_b = _gla_scan(proj, gwf, gwb, gb, 0, 1, 1, 3328 // 128)
    cs, ct, st = dft
    four = _fourier(proj, 3072 // (F_GROUPS * F_GD), cs, ct, st, t)
    wo = w_out
    wr = jnp.pad(wo[:768].reshape(R_HEADS, GLA_DV, D), ((0, 0), (0, GLA_DVP - GLA_DV), (0, 0))).reshape(-1, D)
    h1, u2x, gates, grp = _outproj(b_router, four, 0, 256, o_f, o_b, proj, 2, h, modsel,
                            _pad_heads(g_norm, GLA_DV, GLA_DVP).reshape(1, -1),
                            wo[768:].astype(BF16), wr.astype(BF16), lng[0:1], lnb[0:1], w_router,
                            dv=GLA_DV, dvp=GLA_DVP, row_off=CTX // TM)
    return _moe_layer(u2x, gates, grp, h1, modsel, experts, lng[1:2], lnb[1:2], CTX // TM)


def kernel(x, c, ctx, c_ctx, w_ada, b_ada, ln_g, ln_b, w_in_even, attn_sink, hgrn_lb_logits, hgrn_norm, w_out_even, w_in_odd, gla_gate_w, gla_gate_b, gla_norm, w_out_odd, w_router, b_router, w_expert_gate, w_expert_up, w_expert_down):
    b, t, _ = x.shape
    rc = CTX + t
    assert ctx.shape[1] == CTX
    cc = jnp.zeros((16, D), F32).at[:b].set(c).at[b].set(c_ctx)
    mods = _ada_mods(cc, w_ada, b_ada).reshape(2, 16, 6, D)

    def modsel(l):
        return jnp.stack([jnp.broadcast_to(mods[l, b], (b, 6, D)), mods[l, :b]], axis=1)

    tbl = _rope_tables(rc, t)
    wrt = jnp.pad(w_router, ((0, 0), (0, GW - N_EXP)))
    experts = (w_expert_gate, w_expert_up, w_expert_down)
    h = _layer_even((ctx, x), modsel(0), tbl, w_in_even[0], attn_sink[0], hgrn_lb_logits, hgrn_norm[0], w_out_even[0],
                    ln_g[0], ln_b[0], wrt, b_router, experts + (0,))
    return _layer_odd(h, modsel(1), tbl, _dft_tables(t), w_in_odd[0], gla_gate_w[0], gla_gate_b[0], gla_norm[0],
                      w_out_odd[0], ln_g[1], ln_b[1], wrt, b_router, experts + (1,), t)
```

```python
import functools
import math

import numpy as np
import jax
import jax.numpy as jnp
from jax import lax
from jax.experimental import pallas as pl
from jax.experimental.pallas import tpu as pltpu

F32 = jnp.float32
BF16 = jnp.bfloat16
HIGHEST = lax.Precision.HIGHEST

D = 1024
CTX = 256
GRID_W = 64
LN_EPS = 1e-5
NEG = -1e30
ALPHA = 4.0 ** 0.25
ROPE_BASE = 10000.0

A_HEADS, A_KV, A_HD, A_BLK = 8, 2, 64, 128
R_HEADS = 4
DKP = 128
GLA_DK, GLA_DV, GLA_DVP = 96, 192, 256
GLA_TAU = 16.0
CHUNK = 64
N_EXP, N_GRP, E_PER = 16, 4, 4
D_FF = 512
F_GROUPS, F_GD = 4, 64

TM = 256
IN_SUB = 3
GW = 128
XROWS = 16
YROWS = 8
TME = 512
V7X_VMEM_LIMIT = 56 * 1024 * 1024


def _cp(sem, vmem=None):
    return pltpu.CompilerParams(dimension_semantics=sem, vmem_limit_bytes=vmem)


def _ln(x):
    mu = jnp.mean(x, axis=-1, keepdims=True)
    xc = x - mu
    var = jnp.mean(xc * xc, axis=-1, keepdims=True)
    return xc * lax.rsqrt(var + LN_EPS)


def _silu(x):
    return x * jax.nn.sigmoid(x)


def _split2(w, axis):
    hi = w.astype(BF16)
    return jnp.concatenate([hi, (w - hi.astype(F32)).astype(BF16)], axis=axis)


_NT = (((1,), (1,)), ((), ()))
_TN = (((0,), (0,)), ((), ()))


def _ada_kernel(c_ref, w_ref, b_ref, o_ref):
    s = _silu(c_ref[...])
    o_ref[0] = jnp.dot(s, w_ref[0], precision=HIGHEST, preferred_element_type=F32) + b_ref[0]


def _ada_mods(cc, w_ada, b_ada):
    depth, _, n = w_ada.shape
    tn = 1536
    return pl.pallas_call(
        _ada_kernel,
        grid=(depth, n // tn),
        in_specs=[pl.BlockSpec((16, D), lambda l, i: (0, 0)),
                  pl.BlockSpec((1, D, tn), lambda l, i: (l, 0, i)),
                  pl.BlockSpec((1, 1, tn), lambda l, i: (l, 0, i))],
        out_specs=pl.BlockSpec((1, 16, tn), lambda l, i: (l, 0, i)),
        out_shape=jax.ShapeDtypeStruct((depth, 16, n), F32),
        compiler_params=_cp(("arbitrary", "arbitrary"), 40 * 1024 * 1024),
        name="ada_mod",
    )(cc, w_ada, b_ada.reshape(depth, 1, n))


def _rows_of(h_refs):
    if len(h_refs) == 1:
        return h_refs[0][0]
    return jnp.where(pl.program_id(1) == 0, h_refs[0][0], h_refs[1][0])


def _row_specs(h, row_off=0):
    if isinstance(h, tuple):
        assert row_off == 0 and h[0].shape[1] == TM
        return [pl.BlockSpec((1, TM, D), lambda i, j: (i, 0, 0)),
                pl.BlockSpec((1, TM, D), lambda i, j: (i, jnp.maximum(j - 1, 0), 0))], list(h)
    return [pl.BlockSpec((1, TM, D), lambda i, j: (i, j + row_off, 0))], [h]


def _inproj_kernel(*refs, segs):
    *h_refs, mod0_ref, mod1_ref, w_ref, tbl_ref, o_ref = refs
    if len(h_refs) == 1:
        x = h_refs[0][0]
    else:
        ctx_ref, *x_refs = h_refs
        x = jnp.concatenate([jnp.where(pl.program_id(1) == 0, ctx_ref[0], x_refs[0][0])]
                            + [r[0] for r in x_refs[1:]], axis=0)
    xn = _ln(x)
    u = jnp.concatenate([xn[:TM] * (1.0 + mod0_ref[0, 0, 1:2, :]) + mod0_ref[0, 0, 0:1, :],
                         xn[TM:] * (1.0 + mod1_ref[0, 0, 1:2, :]) + mod1_ref[0, 0, 0:1, :]], axis=0).astype(BF16)
    lane = lax.broadcasted_iota(jnp.int32, (x.shape[0], 128), 1)
    low = (lane % 32) < 16
    for c0, c1, rope in segs:
        acc = jnp.dot(u, w_ref[:, c0:c1], preferred_element_type=F32)
        if rope is not None:
            partner = jnp.where(low, pltpu.roll(acc, 112, axis=1), pltpu.roll(acc, 16, axis=1))
            acc = acc * tbl_ref[2 * rope] + partner * tbl_ref[2 * rope + 1]
        o_ref[0, :, c0:c1] = acc.astype(BF16)


def _inproj(h, modsel, w, tbl, segs):
    b, rc = modsel.shape[0], tbl.shape[1]
    n = w.shape[1]
    tmi = IN_SUB * TM
    assert rc % tmi == 0
    if isinstance(h, tuple):
        assert h[0].shape[1] == TM
        h_specs = [pl.BlockSpec((1, TM, D), lambda i, j: (i, 0, 0))] + [
            pl.BlockSpec((1, TM, D), lambda i, j, k=k: (i, jnp.maximum(IN_SUB * j + k - 1, 0), 0))
            for k in range(IN_SUB)]
        h_args = [h[0]] + [h[1]] * IN_SUB
    else:
        h_specs, h_args = [pl.BlockSpec((1, tmi, D), lambda i, j: (i, j, 0))], [h]
    return pl.pallas_call(
        functools.partial(_inproj_kernel, segs=segs),
        grid=(b, rc // tmi),
        in_specs=h_specs + [pl.BlockSpec((1, 1, 6, D), lambda i, j: (i, jnp.minimum(j, 1), 0, 0)),
                            pl.BlockSpec((1, 1, 6, D), lambda i, j: (i, 1, 0, 0)),
                            pl.BlockSpec((D, n), lambda i, j: (0, 0)),
                            pl.BlockSpec((4, tmi, 128), lambda i, j: (0, j, 0))],
        out_specs=pl.BlockSpec((1, tmi, n), lambda i, j: (i, j, 0)),
        out_shape=jax.ShapeDtypeStruct((b, rc, n), BF16),
        compiler_params=_cp(("parallel", "arbitrary"), V7X_VMEM_LIMIT),
        name="inproj",
    )(*h_args, modsel, modsel, w, tbl)


def _attn_kernel(sink_ref, q_ref, kvc_ref, kvp_ref, kvm_ref, kvn_ref, bias_ref, o_ref):
    blk = pl.program_id(1)
    g_per = A_HEADS // A_KV
    nq = g_per * A_BLK
    q = q_ref[0] * (A_HD ** -0.5)
    rgrp = lax.broadcasted_iota(jnp.int32, (nq, 1), 0) // A_BLK
    q4, sink_col = [], []
    for h in range(A_KV):
        q4.append(jnp.concatenate([q[:, A_HD * (g_per * h + g):A_HD * (g_per * h + g + 1)] for g in range(g_per)],
                                  axis=0))
        col = jnp.zeros((nq, 1), F32)
        for g in range(g_per):
            col = jnp.where(rgrp == g, sink_ref[h * g_per + g], col)
        sink_col.append(col)

    def attend(kv, bias):
        s = [lax.dot_general(q4[h], kv[:, 2 * A_HD * h:2 * A_HD * h + A_HD], _NT, preferred_element_type=F32)
             for h in range(A_KV)]
        if bias is not None:
            bias4 = jnp.concatenate([bias] * g_per, axis=0)
            s = [x + bias4 for x in s]
        m = [jnp.maximum(jnp.max(s[h], axis=-1, keepdims=True), sink_col[h]) for h in range(A_KV)]
        p = [jnp.exp(s[h] - m[h]) for h in range(A_KV)]
        den = [jnp.sum(p[h], axis=-1, keepdims=True) + jnp.exp(sink_col[h] - m[h]) for h in range(A_KV)]
        o = [jnp.dot(p[h].astype(BF16), kv[:, 2 * A_HD * h + A_HD:2 * A_HD * (h + 1)], preferred_element_type=F32)
             / den[h] for h in range(A_KV)]
        o_ref[0] = jnp.concatenate([o[h][A_BLK * g:A_BLK * (g + 1), :] for h in range(A_KV) for g in range(g_per)],
                                   axis=1).astype(BF16)

    @pl.when(blk < CTX // A_BLK)
    def _():
        attend(kvc_ref[0], None)

    @pl.when(blk >= CTX // A_BLK)
    def _():
        attend(jnp.concatenate([kvc_ref[0], kvp_ref[0], kvm_ref[0], kvn_ref[0]], axis=0), bias_ref[0])


def _attn_bias():
    nk = CTX + 3 * A_BLK
    qi = jnp.arange(A_BLK, dtype=jnp.int32)[:, None]
    col = jnp.arange(nk, dtype=jnp.int32)[None, :]
    rel = col - (CTX + A_BLK)
    dist = qi - rel

    def one(lo, hi):
        in_win = (dist <= A_BLK) & (dist >= -A_BLK) & (rel >= lo) & (rel < hi)
        return jnp.where((col < CTX) | in_win, 0.0, NEG).astype(F32)

    return jnp.stack([one(0, 2 * A_BLK), one(-A_BLK, 2 * A_BLK), one(-A_BLK, A_BLK)])


def _attention(proj, sink, kv_blk):
    b, rc, _ = proj.shape
    nb = rc // A_BLK
    c_b = CTX // A_BLK
    assert nb - c_b >= 2
    kw = 2 * A_HD * A_KV

    def band(off):
        return lambda i, j: (i, jnp.clip(j + off, c_b, nb - 1), kv_blk)

    def which_bias(i, j):
        return (jnp.where(j <= c_b, 0, jnp.where(j == nb - 1, 2, 1)), 0, 0)

    return pl.pallas_call(
        _attn_kernel,
        grid=(b, nb),
        in_specs=[pl.BlockSpec(memory_space=pltpu.SMEM),
                  pl.BlockSpec((1, A_BLK, A_HEADS * A_HD), lambda i, j: (i, j, 0)),
                  pl.BlockSpec((1, CTX, kw), lambda i, j: (i, 0, kv_blk)),
                  pl.BlockSpec((1, A_BLK, kw), band(-1)),
                  pl.BlockSpec((1, A_BLK, kw), band(0)),
                  pl.BlockSpec((1, A_BLK, kw), band(1)),
                  pl.BlockSpec((1, A_BLK, CTX + 3 * A_BLK), which_bias)],
        out_specs=pl.BlockSpec((1, A_BLK, A_HEADS * A_HD), lambda i, j: (i, j, 0)),
        out_shape=jax.ShapeDtypeStruct((b, rc, A_HEADS * A_HD), BF16),
        compiler_params=_cp(("parallel", "arbitrary")),
        name="win_attn",
    )(sink, proj, proj, proj, proj, proj, _attn_bias())


def _scan_blocks(dirs, *, dvp):
    r_i = lax.broadcasted_iota(jnp.int32, (CHUNK, CHUNK), 0)
    c_i = lax.broadcasted_iota(jnp.int32, (CHUNK, CHUNK), 1)
    units = []
    for q, k, v, logf, s_ref, o_ref, reverse in dirs:
        nchunk = q.shape[0] // CHUNK
        tri = (r_i <= c_i) if reverse else (r_i >= c_i)
        tri3 = jnp.concatenate([jnp.where(tri, 1.0, 0.0).astype(BF16)] * 3, axis=1)
        for c in (range(nchunk - 1, -1, -1) if reverse else range(nchunk)):
            sl = slice(c * CHUNK, (c + 1) * CHUNK)
            units.append(dict(q=q[sl], k=k[sl], v=v[sl].astype(BF16), lf=logf[sl], sl=sl, tri=tri, tri3=tri3,
                              i_tot=0 if reverse else CHUNK - 1,
                              i_ref=CHUNK // 2 - 1 if reverse else CHUNK // 2, s_ref=s_ref, o_ref=o_ref))
    heads = [(slice(h * DKP, (h + 1) * DKP), slice(h * dvp, (h + 1) * dvp)) for h in range(R_HEADS)]
    for u in units:
        lf = u["lf"]
        hi = lf.astype(BF16)
        r1 = lf - hi.astype(F32)
        mid = r1.astype(BF16)
        lo = (r1 - mid.astype(F32)).astype(BF16)
        u["bsum"] = jnp.dot(u["tri3"], jnp.concatenate([hi, mid, lo], axis=0), preferred_element_type=F32)
    for u in units:
        bsum = u["bsum"]
        tot = bsum[u["i_tot"]:u["i_tot"] + 1, :]
        ref = bsum[u["i_ref"]:u["i_ref"] + 1, :]
        qg = u["q"] * jnp.exp(bsum - ref)
        kg = u["k"] * jnp.exp(ref - bsum)
        u["qb"] = (qg * jnp.exp(ref)).astype(BF16)
        u["kd"] = (kg * jnp.exp(tot - ref)).astype(BF16)
        u["qg"] = qg.astype(BF16)
        u["kg"] = kg.astype(BF16)
        u["dec"] = jnp.exp(tot)
    for u in units:
        u["att"] = [lax.dot_general(u["qg"][:, ks], u["kg"][:, ks], _NT, preferred_element_type=F32)
                    for ks, _ in heads]
        u["ds"] = [lax.dot_general(u["v"][:, vs], u["kd"][:, ks], _TN, preferred_element_type=F32)
                   for ks, vs in heads]
    for u in units:
        u["oi"] = [jnp.dot(jnp.where(u["tri"], a, 0.0).astype(BF16), u["v"][:, vs], preferred_element_type=F32)
                   for a, (_, vs) in zip(u["att"], heads)]
    state = {}
    for u in units:
        key = id(u["s_ref"])
        if key not in state:
            state[key] = [u["s_ref"][h] for h in range(R_HEADS)]
        st = state[key]
        outs = []
        for h, (ks, _) in enumerate(heads):
            outs.append(u["oi"][h] + lax.dot_general(u["qb"][:, ks], st[h].astype(BF16), _NT,
                                                     preferred_element_type=F32))
            st[h] = st[h] * u["dec"][:, ks] + u["ds"][h]
        u["o_ref"][0, u["sl"], :] = jnp.concatenate(outs, axis=1).astype(BF16)
    for _, _, _, _, s_ref, _, _ in dirs:
        for h in range(R_HEADS):
            s_ref[h] = state[id(s_ref)][h]


def _hgrn_scan_kernel(lb_ref, qf_ref, vf_ref, zf_ref, qb_ref, vb_ref, zb_ref, of_ref, ob_ref, sf_ref, sb_ref):
    @pl.when(pl.program_id(1) == 0)
    def _():
        sf_ref[...] = jnp.zeros_like(sf_ref)
        sb_ref[...] = jnp.zeros_like(sb_ref)

    def gates(z, lb):
        f = lb + (1.0 - lb) * jax.nn.sigmoid(z.astype(F32))
        return 1.0 - f, jnp.log(f)

    kf, lf = gates(zf_ref[0], lb_ref[0:1, :])
    kb, lbw = gates(zb_ref[0], lb_ref[1:2, :])
    _scan_blocks([(qf_ref[0], kf, vf_ref[0], lf, sf_ref, of_ref, False),
                  (qb_ref[0], kb, vb_ref[0], lbw, sb_ref, ob_ref, True)], dvp=128)


def _bwd_blk(j, nt):
    return jnp.where(j == 0, 0, nt - j)


def _hgrn_scan(proj, lb, cq, cv, cf, cb):
    b, rc, _ = proj.shape
    nt = rc // TM
    w = R_HEADS * DKP

    def fw(col):
        return pl.BlockSpec((1, TM, w), lambda i, j: (i, j, col))

    def bw(col):
        return pl.BlockSpec((1, TM, w), lambda i, j: (i, _bwd_blk(j, nt), col))

    o_sd = jax.ShapeDtypeStruct((b, rc, w), BF16)
    return pl.pallas_call(
        _hgrn_scan_kernel,
        grid=(b, nt),
        in_specs=[pl.BlockSpec((2, w), lambda i, j: (0, 0)),
                  fw(cq), fw(cv), fw(cf), bw(cq), bw(cv), bw(cb)],
        out_specs=[pl.BlockSpec((1, TM, w), lambda i, j: (i, j, 0)),
                   pl.BlockSpec((1, TM, w), lambda i, j: (i, _bwd_blk(j, nt), 0))],
        out_shape=[o_sd, o_sd],
        scratch_shapes=[pltpu.VMEM((R_HEADS, 128, DKP), F32), pltpu.VMEM((R_HEADS, 128, DKP), F32)],
        compiler_params=_cp(("parallel", "arbitrary"), 40 * 1024 * 1024),
        name="hgrn_scan",
    )(lb, proj, proj, proj, proj, proj, proj)


def _gla_scan_kernel(gwf_ref, gwb_ref, gb_ref, qf_ref, kf_ref, vf_ref, rf_ref, qb_ref, kb_ref, vb_ref, rb_ref,
                     of_ref, ob_ref, sf_ref, sb_ref):
    @pl.when(pl.program_id(1) == 0)
    def _():
        sf_ref[...] = jnp.zeros_like(sf_ref)
        sb_ref[...] = jnp.zeros_like(sb_ref)

    def decay(r, gw, gb):
        zz = jnp.dot(jnp.concatenate([r, r], axis=1), _split2(gw, 0), preferred_element_type=F32) + gb
        return (jnp.minimum(zz, 0.0) - jnp.log(1.0 + jnp.exp(-jnp.abs(zz)))) * (1.0 / GLA_TAU)

    qs = GLA_DK ** -0.5
    _scan_blocks([(qf_ref[0].astype(F32) * qs, kf_ref[0], vf_ref[0],
                   decay(rf_ref[0], gwf_ref[...], gb_ref[0:1, :]), sf_ref, of_ref, False),
                  (qb_ref[0].astype(F32) * qs, kb_ref[0], vb_ref[0],
                   decay(rb_ref[0], gwb_ref[...], gb_ref[1:2, :]), sb_ref, ob_ref, True)], dvp=GLA_DVP)


def _gla_scan(proj, gwf, gwb, gb, cq, ck, cv, cr):
    b, rc, _ = proj.shape
    nt = rc // TM
    wk = R_HEADS * DKP
    wv = R_HEADS * GLA_DVP

    def spec(width, col, back):
        if back:
            return pl.BlockSpec((1, TM, width), lambda i, j: (i, _bwd_blk(j, nt), col))
        return pl.BlockSpec((1, TM, width), lambda i, j: (i, j, col))

    o_sd = jax.ShapeDtypeStruct((b, rc, wv), BF16)
    full = lambda shape: pl.BlockSpec(shape, lambda i, j: (0,) * len(shape))
    return pl.pallas_call(
        _gla_scan_kernel,
        grid=(b, nt),
        in_specs=[full((128, wk)), full((128, wk)), full((2, wk)),
                  spec(wk, cq, False), spec(wk, ck, False), spec(wv, cv, False), spec(128, cr, False),
                  spec(wk, cq, True), spec(wk, ck, True), spec(wv, cv, True), spec(128, cr, True)],
        out_specs=[spec(wv, 0, False), spec(wv, 0, True)],
        out_shape=[o_sd, o_sd],
        scratch_shapes=[pltpu.VMEM((R_HEADS, GLA_DVP, DKP), F32), pltpu.VMEM((R_HEADS, GLA_DVP, DKP), F32)],
        compiler_params=_cp(("parallel", "arbitrary"), 48 * 1024 * 1024),
        name="gla_scan",
    )(gwf, gwb, gb, proj, proj, proj, proj, proj, proj, proj, proj)


def _fourier_kernel(z_ref, cs_ref, ct_ref, st_ref, o_ref, *, scale):
    w = F_GROUPS * F_GD
    zz = jnp.dot(z_ref[0, CTX:, :], cs_ref[...], preferred_element_type=F32).astype(BF16)
    y = jnp.dot(ct_ref[...], zz[:, :w], preferred_element_type=F32)
    y = y - jnp.dot(st_ref[...], zz[:, w:], preferred_element_type=F32)
    o_ref[0] = (y * scale).astype(BF16)


def _fourier(proj, zcol, cs, ct, st, t):
    b, rc, _ = proj.shape
    w = F_GROUPS * F_GD
    const = lambda shape: pl.BlockSpec(shape, lambda i: (0, 0), pipeline_mode=pl.Buffered(1))
    return pl.pallas_call(
        functools.partial(_fourier_kernel, scale=1.0 / math.sqrt(t * F_GD)),
        grid=(b,),
        in_specs=[pl.BlockSpec((1, rc, w), lambda i: (i, 0, zcol)), const((w, 2 * w)), const((t, t)), const((t, t))],
        out_specs=pl.BlockSpec((1, t, w), lambda i: (i, 0, 0)),
        out_shape=jax.ShapeDtypeStruct((b, t, w), BF16),
        compiler_params=_cp(("arbitrary",), 48 * 1024 * 1024),
        name="fourier",
    )(proj, cs, ct, st)


def _route(logits, rb_ref):
    lt = logits.T
    l = [lt[e:e + 1, :] for e in range(N_EXP)]
    m = functools.reduce(jnp.maximum, l)
    ex = [jnp.exp(v - m) for v in l]
    den = functools.reduce(lambda a, c: a + c, ex)
    p = [v / den for v in ex]
    sel = [p[e] + rb_ref[e] for e in range(N_EXP)]
    g_score, g_gate = [], []
    for g in range(N_GRP):
        s = sel[E_PER * g:E_PER * (g + 1)]
        chosen = []
        for i in range(E_PER):
            rank = jnp.zeros_like(s[i])
            for j in range(E_PER):
                if j == i:
                    continue
                ahead = (s[j] >= s[i]) if j < i else (s[j] > s[i])
                rank = rank + jnp.where(ahead, 1.0, 0.0)
            chosen.append(rank < 2.0)
        g_score.append(functools.reduce(lambda a, c: a + c,
                                        [jnp.where(chosen[i], s[i], 0.0) for i in range(E_PER)]))
        g_gate.append([jnp.where(chosen[i], p[E_PER * g + i], 0.0) for i in range(E_PER)])
    best = jnp.zeros_like(g_score[0])
    best_s = g_score[0]
    for g in range(1, N_GRP):
        upd = g_score[g] > best_s
        best = jnp.where(upd, float(g), best)
        best_s = jnp.where(upd, g_score[g], best_s)
    gate = []
    for i in range(E_PER):
        gi = g_gate[0][i]
        for g in range(1, N_GRP):
            gi = jnp.where(best == float(g), g_gate[g][i], gi)
        gate.append(gi)
    wsum = functools.reduce(lambda a, c: a + c, gate)
    row = lax.broadcasted_iota(jnp.int32, (GW, lt.shape[1]), 0)
    side = jnp.zeros((GW, lt.shape[1]), F32)
    for i in range(E_PER):
        side = jnp.where(row == i, gate[i] / wsum, side)
    return side.T, best


def _outproj_kernel(rb_ref, a_ref, of_ref, ob_ref, gt_ref, *refs, dvp, dv):
    *h_refs, mod_ref, gn_ref, wa_ref, wr_ref, lng_ref, lnb_ref, wrt_ref, h1_ref, u2_ref, grp_ref = refs
    o = of_ref[0].astype(F32) + ob_ref[0].astype(F32)
    parts = []
    for hh in range(R_HEADS):
        sl = slice(hh * dvp, (hh + 1) * dvp)
        oh = o[:, sl]
        ms = jnp.sum(oh * oh, axis=-1, keepdims=True) * (1.0 / dv)
        parts.append(oh * lax.rsqrt(ms + LN_EPS))
    rec = jnp.concatenate(parts, axis=1) * gn_ref[...] * _silu(gt_ref[0].astype(F32))
    y = jnp.dot(a_ref[0], wa_ref[...], preferred_element_type=F32)
    y = y + jnp.dot(rec.astype(BF16), wr_ref[...], preferred_element_type=F32)
    g1 = mod_ref[0, 0, 2:3, :]
    h1 = _ln(ALPHA * _rows_of(h_refs) + g1 * y) * lng_ref[0:1, :] + lnb_ref[0:1, :]
    h1_ref[0] = h1
    u2 = _ln(h1) * (1.0 + mod_ref[0, 0, 4:5, :]) + mod_ref[0, 0, 3:4, :]
    lg = jnp.dot(_split2(u2, 0), _split2(wrt_ref[...], 1), preferred_element_type=F32)
    lg = lg[:TM] + lg[TM:]
    logits = lg[:, :GW] + lg[:, GW:]
    side, best = _route(logits, rb_ref)
    grp_ref[0, 0] = best
    for c in range(D // 128):
        u2_ref[pl.ds(c, TM, stride=XROWS), :] = u2[:, 128 * c:128 * (c + 1)]
    u2_ref[pl.ds(D // 128, TM, stride=XROWS), :] = side
    for c in range(D // 128 + 1, XROWS):
        u2_ref[pl.ds(c, TM, stride=XROWS), :] = jnp.zeros((TM, 128), F32)


def _outproj(b_router, a, a_col, a_w, o_f, o_b, proj, g_col, h, modsel, gnorm, wa, wr, lng, lnb, wrt, *,
             dv, dvp, row_off):
    b, rc, _ = proj.shape
    nt = rc // TM - row_off
    wrec = R_HEADS * dvp
    full = lambda shape: pl.BlockSpec(shape, lambda i, j: (0,) * len(shape))
    h_specs, h_args = _row_specs(h, row_off)
    return pl.pallas_call(
        functools.partial(_outproj_kernel, dvp=dvp, dv=dv),
        grid=(b, nt),
        in_specs=[pl.BlockSpec(memory_space=pltpu.SMEM),
                  pl.BlockSpec((1, TM, a_w), lambda i, j: (i, j + (row_off if a.shape[1] == rc else 0), a_col)),
                  pl.BlockSpec((1, TM, wrec), lambda i, j: (i, j + row_off, 0)),
                  pl.BlockSpec((1, TM, wrec), lambda i, j: (i, j + row_off, 0)),
                  pl.BlockSpec((1, TM, wrec), lambda i, j: (i, j + row_off, g_col))] + h_specs + [
                  pl.BlockSpec((1, 1, 6, D), lambda i, j: (i, jnp.minimum(j + row_off, 1), 0, 0)),
                  full((1, wrec)), full((a_w, D)), full((wrec, D)), full((1, D)), full((1, D)), full((D, GW))],
        out_specs=[pl.BlockSpec((1, TM, D), lambda i, j: (i, j, 0)),
                   pl.BlockSpec((TM * XROWS, 128), lambda i, j: (i * nt + j, 0)),
                   pl.BlockSpec((1, 1, 1, TM), lambda i, j: (i, j, 0, 0))],
        out_shape=[jax.ShapeDtypeStruct((b, nt * TM, D), F32),
                   jax.ShapeDtypeStruct((b * nt * TM * XROWS, 128), F32),
                   jax.ShapeDtypeStruct((b, nt, 1, TM), F32)],
        compiler_params=_cp(("parallel", "arbitrary"), V7X_VMEM_LIMIT),
        name="outproj",
    )(b_router, a, o_f, o_b, proj, *h_args, modsel, gnorm, wa, wr, lng, lnb, wrt)


def _start_token_gather(idx_ref, base, src_hbm, dst_ref, sem, rows):
    def body(r, carry):
        t = idx_ref[base + r]
        pltpu.make_async_copy(src_hbm.at[pl.ds(pl.multiple_of(t * rows, rows), rows), :],
                              dst_ref.at[pl.ds(pl.multiple_of(r * rows, rows), rows), :], sem).start()
        return carry

    lax.fori_loop(0, dst_ref.shape[0] // rows, body, 0, unroll=8)


def _wait_token_gather(src_hbm, dst_ref, sem):
    pltpu.make_async_copy(src_hbm.at[pl.ds(0, dst_ref.shape[0]), :], dst_ref, sem).wait()


def _gather_ring(idx_ref, first_of, step, last, src_hbm, buf_ref, sem, rows):
    depth = buf_ref.shape[0]
    slot = step % depth

    @pl.when(step == 0)
    def _():
        for k in range(depth - 1):
            _start_token_gather(idx_ref, first_of(jnp.minimum(k, last)), src_hbm, buf_ref.at[k], sem.at[k], rows)

    _wait_token_gather(src_hbm, buf_ref.at[slot], sem.at[slot])
    ahead = (step + depth - 1) % depth
    nxt = first_of(jnp.minimum(step + depth - 1, last))

    def issue(r0, r1):
        for r in range(r0, r1):
            t = idx_ref[nxt + r]
            pltpu.make_async_copy(src_hbm.at[pl.ds(pl.multiple_of(t * rows, rows), rows), :],
                                  buf_ref.at[ahead, pl.ds(r * rows, rows), :], sem.at[ahead]).start()

    def finish():
        @pl.when(step == last)
        def _():
            for k in range(1, depth):
                other = (step + k) % depth
                _wait_token_gather(src_hbm, buf_ref.at[other], sem.at[other])

    return slot, issue, finish


def _moe_kernel(tg_ref, first_ref, src_ref, x_hbm, w1_ref, w3_ref, w2_ref, o_ref, xb_ref, sem,
                w1b_ref, w3b_ref, w2b_ref):
    i = pl.program_id(0)
    slot, issue, finish = _gather_ring(src_ref, lambda k: first_ref[k], i, pl.num_programs(0) - 1,
                                       x_hbm, xb_ref, sem, XROWS)

    @pl.when((i == 0) | (tg_ref[i] != tg_ref[jnp.maximum(i - 1, 0)]))
    def _():
        for e in range(E_PER):
            w1b_ref[e] = w1_ref[e].astype(BF16)
            w3b_ref[e] = w3_ref[e].astype(BF16)
            w2b_ref[e] = w2_ref[e].astype(BF16)

    xb = xb_ref.at[slot]
    x = jnp.concatenate([xb[pl.ds(c, TME, stride=XROWS), :] for c in range(D // 128)], axis=1).astype(BF16)
    side = xb[pl.ds(D // 128, TME, stride=XROWS), :]
    y = jnp.zeros((TME, D), F32)
    per = TME // E_PER
    for e in range(E_PER):
        issue(e * per, (e + 1) * per)
        h1 = jnp.dot(x, w1b_ref[e], preferred_element_type=F32)
        h3 = jnp.dot(x, w3b_ref[e], preferred_element_type=F32)
        hh = (_silu(h1) * h3 * side[:, e:e + 1]).astype(BF16)
        y = y + jnp.dot(hh, w2b_ref[e], preferred_element_type=F32)
    for c in range(D // 128):
        o_ref[pl.ds(c, TME, stride=YROWS), :] = y[:, 128 * c:128 * (c + 1)]
    finish()


def _moe(tile_grp, tile_first, order, u2x, w1, w3, w2, layer):
    mp = tile_grp.shape[0] * TME
    wspec = lambda k, n: pl.BlockSpec((None, E_PER, k, n), lambda i, tg, tf, od: (layer, tg[i], 0, 0),
                                      pipeline_mode=pl.Buffered(1))
    return pl.pallas_call(
        _moe_kernel,
        grid_spec=pltpu.PrefetchScalarGridSpec(
            num_scalar_prefetch=3, grid=(mp // TME,),
            in_specs=[pl.BlockSpec(memory_space=pl.ANY), wspec(D, D_FF), wspec(D, D_FF), wspec(D_FF, D)],
            out_specs=pl.BlockSpec((TME * YROWS, 128), lambda i, tg, tf, od: (i, 0)),
            scratch_shapes=[pltpu.VMEM((2, TME * XROWS, 128), F32), pltpu.SemaphoreType.DMA((2,)),
                            pltpu.VMEM((E_PER, D, D_FF), BF16), pltpu.VMEM((E_PER, D, D_FF), BF16),
                            pltpu.VMEM((E_PER, D_FF, D), BF16)]),
        out_shape=jax.ShapeDtypeStruct((mp * YROWS, 128), F32),
        compiler_params=_cp(("arbitrary",), V7X_VMEM_LIMIT),
        name="moe",
    )(tile_grp, tile_first, order, u2x, w1, w3, w2)


def _combine_kernel(pos_ref, y_hbm, h_ref, mod_ref, lng_ref, lnb_ref, o_ref, yb_ref, sem, *, nt):
    step = pl.program_id(0) * nt + pl.program_id(1)
    slot, issue, finish = _gather_ring(pos_ref, lambda k: k * TM, step, pl.num_programs(0) * nt - 1,
                                       y_hbm, yb_ref, sem, YROWS)
    yb = yb_ref.at[slot]
    g2 = mod_ref[0, 0, 5:6, :]
    strip = TM // 4
    for r0 in range(0, TM, strip):
        issue(r0, r0 + strip)
        y = jnp.concatenate([yb[pl.ds(r0 * YROWS + c, strip, stride=YROWS), :] for c in range(D // 128)], axis=1)
        o_ref[0, r0:r0 + strip, :] = (_ln(ALPHA * h_ref[0, r0:r0 + strip, :] + g2 * y) * lng_ref[0:1, :]
                                      + lnb_ref[0:1, :])
    finish()


def _combine(pos, y_sorted, h1, modsel, lng, lnb, row_off):
    b, r, _ = h1.shape
    nt = r // TM
    full = lambda shape: pl.BlockSpec(shape, lambda i, j, p: (0,) * len(shape))
    return pl.pallas_call(
        functools.partial(_combine_kernel, nt=nt),
        grid_spec=pltpu.PrefetchScalarGridSpec(
            num_scalar_prefetch=1, grid=(b, nt),
            in_specs=[pl.BlockSpec(memory_space=pl.ANY),
                      pl.BlockSpec((1, TM, D), lambda i, j, p: (i, j, 0)),
                      pl.BlockSpec((1, 1, 6, D), lambda i, j, p: (i, jnp.minimum(j + row_off, 1), 0, 0)),
                      full((1, D)), full((1, D))],
            out_specs=pl.BlockSpec((1, TM, D), lambda i, j, p: (i, j, 0)),
            scratch_shapes=[pltpu.VMEM((3, TM * YROWS, 128), F32), pltpu.SemaphoreType.DMA((3,))]),
        out_shape=jax.ShapeDtypeStruct((b, r, D), F32),
        compiler_params=_cp(("arbitrary", "arbitrary")),
        name="moe_combine",
    )(pos, y_sorted, h1, modsel, lng, lnb)


def _sort_by_group(grp):
    n = grp.shape[0]
    onehot = (grp[:, None] == jnp.arange(N_GRP, dtype=jnp.int32)[None, :]).astype(jnp.int32)
    csum = jnp.cumsum(onehot, axis=0)
    counts = csum[-1]
    padded = ((counts + TME - 1) // TME) * TME
    ends = jnp.cumsum(padded)
    starts = ends - padded
    rank = jnp.sum(csum * onehot, axis=1) - 1
    pos = (starts[grp] + rank).astype(jnp.int32)
    _, order = lax.sort((grp, jnp.arange(n, dtype=jnp.int32)), num_keys=1, is_stable=True)
    order = jnp.concatenate([order, jnp.zeros((TME,), jnp.int32)])
    ntile = n // TME + N_GRP
    tile_start = jnp.arange(ntile, dtype=jnp.int32) * TME
    tile_grp = jnp.minimum(jnp.sum((tile_start[:, None] >= ends[None, :]).astype(jnp.int32), axis=1), N_GRP - 1)
    first = (jnp.cumsum(counts) - counts)[tile_grp] + tile_start - starts[tile_grp]
    return pos, order, tile_grp.astype(jnp.int32), jnp.minimum(first, n).astype(jnp.int32)


def _moe_layer(u2x, grp, h1, modsel, experts, lng, lnb, row_off):
    pos, order, tile_grp, tile_first = _sort_by_group(grp.reshape(-1).astype(jnp.int32))
    y_sorted = _moe(tile_grp, tile_first, order, u2x, *experts)
    return _combine(pos, y_sorted, h1, modsel, lng, lnb, row_off)


def _rope_tables(rc, t):
    half = A_HD // 4
    inv = ROPE_BASE ** (-jnp.arange(half, dtype=F32) / half)
    tpos = jnp.arange(t)
    rows = (tpos // GRID_W).astype(F32)
    cols = (tpos % GRID_W).astype(F32)

    def cs(pos):
        ang = pos[:, None] * inv[None, :]
        c, s = jnp.cos(ang), jnp.sin(ang)
        return jnp.concatenate([c, c], -1), jnp.concatenate([-s, s], -1)

    cr, sr = cs(rows)
    cc_, sc_ = cs(cols)
    cos_h = jnp.concatenate([cr, cc_], -1)
    sin_h = jnp.concatenate([sr, sc_], -1)
    one = jnp.ones((t, A_HD), F32)
    zero = jnp.zeros((t, A_HD), F32)
    lat = jnp.stack([jnp.concatenate([cos_h, cos_h], -1), jnp.concatenate([sin_h, sin_h], -1),
                     jnp.concatenate([cos_h, one], -1), jnp.concatenate([sin_h, zero], -1)])
    ctx = jnp.stack([jnp.ones((CTX, 128), F32), jnp.zeros((CTX, 128), F32)] * 2)
    return jnp.concatenate([ctx, lat], axis=1)


def _dft_tables(t):
    def cos_sin(num, den):
        ang = (2.0 * math.pi / den) * (num % den).astype(F32)
        return jnp.cos(ang), jnp.sin(ang)

    k = jnp.arange(t, dtype=jnp.int32)[:, None]
    ca, sa = cos_sin(k * jnp.arange(t // GRID_W, dtype=jnp.int32)[None, :], t // GRID_W)
    cb, sb = cos_sin(k * jnp.arange(GRID_W, dtype=jnp.int32)[None, :], t)
    ct = (ca[:, :, None] * cb[:, None, :] - sa[:, :, None] * sb[:, None, :]).reshape(t, t)
    st = (sa[:, :, None] * cb[:, None, :] + ca[:, :, None] * sb[:, None, :]).reshape(t, t)
    c = jnp.arange(F_GD, dtype=jnp.int32)
    cc, sc = cos_sin(c[:, None] * c[None, :], F_GD)
    eye = jnp.eye(F_GROUPS, dtype=F32)
    cs = jnp.concatenate([jnp.kron(eye, cc), jnp.kron(eye, sc)], axis=1)
    return cs.astype(BF16), ct.astype(BF16), st.astype(BF16)


def _pad_heads(w, width, padded):
    lead = w.shape[:-1]
    w = w.reshape(lead + (R_HEADS, width))
    w = jnp.pad(w, [(0, 0)] * len(lead) + [(0, 0), (0, padded - width)])
    return w.reshape(lead + (R_HEADS * padded,))


EVEN_SEGS = tuple([(128 * g, 128 * (g + 1), 0) for g in range(4)] + [(c, c + 512, None) for c in range(512, 3072, 512)]
                  + [(3072 + 128 * g, 3072 + 128 * (g + 1), 1) for g in range(2)])
ODD_W = 512 + 512 + 1024 + 1024 + 256 + 128
ODD_SEGS = tuple([(c, c + 512, None) for c in range(0, 3072, 512)] + [(3072, ODD_W, None)])


def _layer_even(h, modsel, tbl, w_in, sink, lb_logits, g_norm, w_out, lng, lnb, w_router, b_router, experts):
    aq, ak, av, ff, fb, hq, hi, hg = jnp.split(w_in, np.cumsum([512, 128, 128, 512, 512, 512, 512]).tolist(), axis=1)
    kv = [jnp.concatenate([ak[:, A_HD * j:A_HD * (j + 1)], av[:, A_HD * j:A_HD * (j + 1)]], axis=1)
          for j in range(A_KV)]
    w = jnp.concatenate([aq, ff, fb, hq, hi, hg] + kv, axis=1).astype(BF16)
    proj = _inproj(h, modsel, w, tbl, EVEN_SEGS)
    o_att = _attention(proj, sink, 3072 // (2 * A_HD * A_KV))
    lb = jnp.cumsum(jax.nn.softmax(lb_logits.astype(F32), axis=1), axis=1)[:, 0]
    o_f, o_b = _hgrn_scan(proj, lb, 3, 4, 1, 2)
    wo = w_out.astype(BF16)
    h1, u2x, grp = _outproj(b_router, o_att, 0, 512, o_f, o_b, proj, 5, h, modsel, g_norm.reshape(1, -1),
                            wo[:512], wo[512:], lng[0:1], lnb[0:1], w_router, dv=128, dvp=128, row_off=0)
    return _moe_layer(u2x, grp, h1, modsel, experts, lng[1:2], lnb[1:2], 0)


def _layer_odd(h, modsel, tbl, dft, w_in, gate_w, gate_b, g_norm, w_out, lng, lnb, w_router, b_router, experts, t):
    q, k, v, rf, rb, g, z = jnp.split(w_in, np.cumsum([384, 384, 768, 16, 16, 768]).tolist(), axis=1)
    r = jnp.pad(jnp.concatenate([rf, rb], axis=1), ((0, 0), (0, 96)))
    w = jnp.concatenate([_pad_heads(q, GLA_DK, DKP), _pad_heads(k, GLA_DK, DKP), _pad_heads(v, GLA_DV, GLA_DVP),
                         _pad_heads(g, GLA_DV, GLA_DVP), z, r], axis=1).astype(BF16)
    proj = _inproj(h, modsel, w, tbl, ODD_SEGS)
    gw = _pad_heads(gate_w, GLA_DK, DKP)
    gwf = jnp.pad(gw[0], ((0, 112), (0, 0)))
    gwb = jnp.pad(gw[1], ((16, 96), (0, 0)))
    gb = _pad_heads(gate_b, GLA_DK, DKP)
    o_f, o_b = _gla_scan(proj, gwf, gwb, gb, 0, 1, 1, 3328 // 128)
    cs, ct, st = dft
    four = _fourier(proj, 3072 // (F_GROUPS * F_GD), cs, ct, st, t)
    wo = w_out
    wr = jnp.pad(wo[:768].reshape(R_HEADS, GLA_DV, D), ((0, 0), (0, GLA_DVP - GLA_DV), (0, 0))).reshape(-1, D)
    h1, u2x, grp = _outproj(b_router, four, 0, 256, o_f, o_b, proj, 2, h, modsel,
                            _pad_heads(g_norm, GLA_DV, GLA_DVP).reshape(1, -1),
                            wo[768:].astype(BF16), wr.astype(BF16), lng[0:1], lnb[0:1], w_router,
                            dv=GLA_DV, dvp=GLA_DVP, row_off=CTX // TM)
    return _moe_layer(u2x, grp, h1, modsel, experts, lng[1:2], lnb[1:2], CTX // TM)


def kernel(x, c, ctx, c_ctx, w_ada, b_ada, ln_g, ln_b, w_in_even, attn_sink, hgrn_lb_logits, hgrn_norm, w_out_even, w_in_odd, gla_gate_w, gla_gate_b, gla_norm, w_out_odd, w_router, b_router, w_expert_gate, w_expert_up, w_expert_down):
    b, t, _ = x.shape
    rc = CTX + t
    assert ctx.shape[1] == CTX
    cc = jnp.zeros((16, D), F32).at[:b].set(c).at[b].set(c_ctx)
    mods = _ada_mods(cc, w_ada, b_ada).reshape(2, 16, 6, D)

    def modsel(l):
        return jnp.stack([jnp.broadcast_to(mods[l, b], (b, 6, D)), mods[l, :b]], axis=1)

    tbl = _rope_tables(rc, t)
    wrt = jnp.pad(w_router, ((0, 0), (0, GW - N_EXP)))
    experts = (w_expert_gate, w_expert_up, w_expert_down)
    h = _layer_even((ctx, x), modsel(0), tbl, w_in_even[0], attn_sink[0], hgrn_lb_logits, hgrn_norm[0], w_out_even[0],
                    ln_g[0], ln_b[0], wrt, b_router, experts + (0,))
    return _layer_odd(h, modsel(1), tbl, _dft_tables(t), w_in_odd[0], gla_gate_w[0], gla_gate_b[0], gla_norm[0],
                      w_out_odd[0], ln_g[1], ln_b[1], wrt, b_router, experts + (1,), t)
```

```python
import functools
import math

import numpy as np
import jax
import jax.numpy as jnp
from jax import lax
from jax.experimental import pallas as pl
from jax.experimental.pallas import tpu as pltpu

F32 = jnp.float32
BF16 = jnp.bfloat16
HIGHEST = lax.Precision.HIGHEST

D = 1024
CTX = 256
GRID_W = 64
LN_EPS = 1e-5
NEG = -1e30
ALPHA = 4.0 ** 0.25
ROPE_BASE = 10000.0

A_HEADS, A_KV, A_HD, A_BLK = 8, 2, 64, 128
A_SUB = 2
R_HEADS = 4
DKP = 128
GLA_DK, GLA_DV, GLA_DVP = 96, 192, 256
GLA_TAU = 16.0
CHUNK = 64
N_EXP, N_GRP, E_PER = 16, 4, 4
D_FF = 512
F_GROUPS, F_GD = 4, 64

TM = 256
IN_SUB = 3
GW = 128
XROWS = 16
YROWS = 8
TME = 256
V7X_VMEM_LIMIT = 56 * 1024 * 1024


def _cp(sem, vmem=None):
    return pltpu.CompilerParams(dimension_semantics=sem, vmem_limit_bytes=vmem)


def _ln(x):
    mu = jnp.mean(x, axis=-1, keepdims=True)
    xc = x - mu
    var = jnp.mean(xc * xc, axis=-1, keepdims=True)
    return xc * lax.rsqrt(var + LN_EPS)


def _silu(x):
    return x * jax.nn.sigmoid(x)


def _split2(w, axis):
    hi = w.astype(BF16)
    return jnp.concatenate([hi, (w - hi.astype(F32)).astype(BF16)], axis=axis)


_NT = (((1,), (1,)), ((), ()))
_TN = (((0,), (0,)), ((), ()))


def _ada_kernel(c_ref, w_ref, b_ref, o_ref):
    s = _silu(c_ref[...])
    o_ref[0] = jnp.dot(s, w_ref[0], precision=HIGHEST, preferred_element_type=F32) + b_ref[0]


def _ada_mods(cc, w_ada, b_ada):
    depth, _, n = w_ada.shape
    tn = 1536
    return pl.pallas_call(
        _ada_kernel,
        grid=(depth, n // tn),
        in_specs=[pl.BlockSpec((16, D), lambda l, i: (0, 0)),
                  pl.BlockSpec((1, D, tn), lambda l, i: (l, 0, i)),
                  pl.BlockSpec((1, 1, tn), lambda l, i: (l, 0, i))],
        out_specs=pl.BlockSpec((1, 16, tn), lambda l, i: (l, 0, i)),
        out_shape=jax.ShapeDtypeStruct((depth, 16, n), F32),
        compiler_params=_cp(("arbitrary", "arbitrary"), 40 * 1024 * 1024),
        name="ada_mod",
    )(cc, w_ada, b_ada.reshape(depth, 1, n))


def _rows_of(h_refs):
    if len(h_refs) == 1:
        return h_refs[0][0]
    return jnp.where(pl.program_id(1) == 0, h_refs[0][0], h_refs[1][0])


def _row_specs(h, row_off=0):
    if isinstance(h, tuple):
        assert row_off == 0 and h[0].shape[1] == TM
        return [pl.BlockSpec((1, TM, D), lambda i, j: (i, 0, 0)),
                pl.BlockSpec((1, TM, D), lambda i, j: (i, jnp.maximum(j - 1, 0), 0))], list(h)
    return [pl.BlockSpec((1, TM, D), lambda i, j: (i, j + row_off, 0))], [h]


def _inproj_kernel(*refs, segs):
    *h_refs, mod0_ref, mod1_ref, w_ref, tbl_ref, o_ref = refs
    if len(h_refs) == 1:
        x = h_refs[0][0]
    else:
        ctx_ref, *x_refs = h_refs
        x = jnp.concatenate([jnp.where(pl.program_id(1) == 0, ctx_ref[0], x_refs[0][0])]
                            + [r[0] for r in x_refs[1:]], axis=0)
    xn = _ln(x)
    u = jnp.concatenate([xn[:TM] * (1.0 + mod0_ref[0, 0, 1:2, :]) + mod0_ref[0, 0, 0:1, :],
                         xn[TM:] * (1.0 + mod1_ref[0, 0, 1:2, :]) + mod1_ref[0, 0, 0:1, :]], axis=0).astype(BF16)
    lane = lax.broadcasted_iota(jnp.int32, (x.shape[0], 128), 1)
    low = (lane % 32) < 16
    for c0, c1, rope in segs:
        acc = jnp.dot(u, w_ref[:, c0:c1], preferred_element_type=F32)
        if rope is not None:
            partner = jnp.where(low, pltpu.roll(acc, 112, axis=1), pltpu.roll(acc, 16, axis=1))
            acc = acc * tbl_ref[2 * rope] + partner * tbl_ref[2 * rope + 1]
        o_ref[0, :, c0:c1] = acc.astype(BF16)


def _inproj(h, modsel, w, tbl, segs):
    b, rc = modsel.shape[0], tbl.shape[1]
    n = w.shape[1]
    tmi = IN_SUB * TM
    assert rc % tmi == 0
    if isinstance(h, tuple):
        assert h[0].shape[1] == TM
        h_specs = [pl.BlockSpec((1, TM, D), lambda i, j: (i, 0, 0))] + [
            pl.BlockSpec((1, TM, D), lambda i, j, k=k: (i, jnp.maximum(IN_SUB * j + k - 1, 0), 0))
            for k in range(IN_SUB)]
        h_args = [h[0]] + [h[1]] * IN_SUB
    else:
        h_specs, h_args = [pl.BlockSpec((1, tmi, D), lambda i, j: (i, j, 0))], [h]
    return pl.pallas_call(
        functools.partial(_inproj_kernel, segs=segs),
        grid=(b, rc // tmi),
        in_specs=h_specs + [pl.BlockSpec((1, 1, 6, D), lambda i, j: (i, jnp.minimum(j, 1), 0, 0)),
                            pl.BlockSpec((1, 1, 6, D), lambda i, j: (i, 1, 0, 0)),
                            pl.BlockSpec((D, n), lambda i, j: (0, 0)),
                            pl.BlockSpec((4, tmi, 128), lambda i, j: (0, j, 0))],
        out_specs=pl.BlockSpec((1, tmi, n), lambda i, j: (i, j, 0)),
        out_shape=jax.ShapeDtypeStruct((b, rc, n), BF16),
        compiler_params=_cp(("parallel", "arbitrary"), V7X_VMEM_LIMIT),
        name="inproj",
    )(*h_args, modsel, modsel, w, tbl)


def _attn_kernel(sink_ref, q_ref, kvc_ref, *refs):
    band_refs, bias_refs, o_ref = refs[:A_SUB + 2], refs[A_SUB + 2:2 * A_SUB + 2], refs[-1]
    step = pl.program_id(1)
    g_per = A_HEADS // A_KV
    nq = g_per * A_BLK
    rgrp = lax.broadcasted_iota(jnp.int32, (nq, 1), 0) // A_BLK
    sink_col = []
    for h in range(A_KV):
        col = jnp.zeros((nq, 1), F32)
        for g in range(g_per):
            col = jnp.where(rgrp == g, sink_ref[h * g_per + g], col)
        sink_col.append(col)

    def attend(sb, kv, bias):
        q = q_ref[0, A_BLK * sb:A_BLK * (sb + 1), :] * (A_HD ** -0.5)
        q4 = [jnp.concatenate([q[:, A_HD * (g_per * h + g):A_HD * (g_per * h + g + 1)] for g in range(g_per)], axis=0)
              for h in range(A_KV)]
        s = [lax.dot_general(q4[h], kv[:, 2 * A_HD * h:2 * A_HD * h + A_HD], _NT, preferred_element_type=F32)
             for h in range(A_KV)]
        if bias is not None:
            bias4 = jnp.concatenate([bias] * g_per, axis=0)
            s = [x + bias4 for x in s]
        m = [jnp.maximum(jnp.max(s[h], axis=-1, keepdims=True), sink_col[h]) for h in range(A_KV)]
        p = [jnp.exp(s[h] - m[h]) for h in range(A_KV)]
        den = [jnp.sum(p[h], axis=-1, keepdims=True) + jnp.exp(sink_col[h] - m[h]) for h in range(A_KV)]
        o = [jnp.dot(p[h].astype(BF16), kv[:, 2 * A_HD * h + A_HD:2 * A_HD * (h + 1)], preferred_element_type=F32)
             / den[h] for h in range(A_KV)]
        o_ref[0, A_BLK * sb:A_BLK * (sb + 1), :] = jnp.concatenate(
            [o[h][A_BLK * g:A_BLK * (g + 1), :] for h in range(A_KV) for g in range(g_per)], axis=1).astype(BF16)

    @pl.when(step < CTX // (A_SUB * A_BLK))
    def _():
        for sb in range(A_SUB):
            attend(sb, kvc_ref[0], None)

    @pl.when(step >= CTX // (A_SUB * A_BLK))
    def _():
        for sb in range(A_SUB):
            attend(sb, jnp.concatenate([kvc_ref[0]] + [band_refs[sb + k][0] for k in range(3)], axis=0),
                   bias_refs[sb][0])


def _attn_bias():
    nk = CTX + 3 * A_BLK
    qi = jnp.arange(A_BLK, dtype=jnp.int32)[:, None]
    col = jnp.arange(nk, dtype=jnp.int32)[None, :]
    rel = col - (CTX + A_BLK)
    dist = qi - rel

    def one(lo, hi):
        in_win = (dist <= A_BLK) & (dist >= -A_BLK) & (rel >= lo) & (rel < hi)
        return jnp.where((col < CTX) | in_win, 0.0, NEG).astype(F32)

    return jnp.stack([one(0, 2 * A_BLK), one(-A_BLK, 2 * A_BLK), one(-A_BLK, A_BLK)])


def _attention(proj, sink, kv_blk):
    b, rc, _ = proj.shape
    nb = rc // A_BLK
    c_b = CTX // A_BLK
    assert nb - c_b >= 2 and nb % A_SUB == 0 and c_b % A_SUB == 0
    kw = 2 * A_HD * A_KV

    def band(k):
        return pl.BlockSpec((1, A_BLK, kw), lambda i, j: (i, jnp.clip(A_SUB * j - 1 + k, c_b, nb - 1), kv_blk))

    def bias(sb):
        def index(i, j):
            n = A_SUB * j + sb
            return (jnp.where(n <= c_b, 0, jnp.where(n == nb - 1, 2, 1)), 0, 0)
        return pl.BlockSpec((1, A_BLK, CTX + 3 * A_BLK), index)

    qo = pl.BlockSpec((1, A_SUB * A_BLK, A_HEADS * A_HD), lambda i, j: (i, j, 0))
    bias_tbl = _attn_bias()
    return pl.pallas_call(
        _attn_kernel,
        grid=(b, nb // A_SUB),
        in_specs=[pl.BlockSpec(memory_space=pltpu.SMEM), qo, pl.BlockSpec((1, CTX, kw), lambda i, j: (i, 0, kv_blk))]
        + [band(k) for k in range(A_SUB + 2)] + [bias(sb) for sb in range(A_SUB)],
        out_specs=qo,
        out_shape=jax.ShapeDtypeStruct((b, rc, A_HEADS * A_HD), BF16),
        compiler_params=_cp(("parallel", "arbitrary")),
        name="win_attn",
    )(sink, proj, proj, *([proj] * (A_SUB + 2)), *([bias_tbl] * A_SUB))


def _scan_blocks(dirs, *, dvp):
    r_i = lax.broadcasted_iota(jnp.int32, (CHUNK, CHUNK), 0)
    c_i = lax.broadcasted_iota(jnp.int32, (CHUNK, CHUNK), 1)
    units = []
    for q, k, v, logf, s_ref, o_ref, reverse in dirs:
        nchunk = q.shape[0] // CHUNK
        tri = (r_i <= c_i) if reverse else (r_i >= c_i)
        tri3 = jnp.concatenate([jnp.where(tri, 1.0, 0.0).astype(BF16)] * 3, axis=1)
        for c in (range(nchunk - 1, -1, -1) if reverse else range(nchunk)):
            sl = slice(c * CHUNK, (c + 1) * CHUNK)
            units.append(dict(q=q[sl], k=k[sl], v=v[sl].astype(BF16), lf=logf[sl], sl=sl, tri=tri, tri3=tri3,
                              i_tot=0 if reverse else CHUNK - 1,
                              i_ref=CHUNK // 2 - 1 if reverse else CHUNK // 2, s_ref=s_ref, o_ref=o_ref))
    heads = [(slice(h * DKP, (h + 1) * DKP), slice(h * dvp, (h + 1) * dvp)) for h in range(R_HEADS)]
    for u in units:
        lf = u["lf"]
        hi = lf.astype(BF16)
        r1 = lf - hi.astype(F32)
        mid = r1.astype(BF16)
        lo = (r1 - mid.astype(F32)).astype(BF16)
        u["bsum"] = jnp.dot(u["tri3"], jnp.concatenate([hi, mid, lo], axis=0), preferred_element_type=F32)
    for u in units:
        bsum = u["bsum"]
        tot = bsum[u["i_tot"]:u["i_tot"] + 1, :]
        ref = bsum[u["i_ref"]:u["i_ref"] + 1, :]
        qg = u["q"] * jnp.exp(bsum - ref)
        kg = u["k"] * jnp.exp(ref - bsum)
        u["qb"] = (qg * jnp.exp(ref)).astype(BF16)
        u["kd"] = (kg * jnp.exp(tot - ref)).astype(BF16)
        u["qg"] = qg.astype(BF16)
        u["kg"] = kg.astype(BF16)
        u["dec"] = jnp.exp(tot)
    for u in units:
        u["att"] = [lax.dot_general(u["qg"][:, ks], u["kg"][:, ks], _NT, preferred_element_type=F32)
                    for ks, _ in heads]
        u["ds"] = [lax.dot_general(u["v"][:, vs], u["kd"][:, ks], _TN, preferred_element_type=F32)
                   for ks, vs in heads]
    for u in units:
        u["oi"] = [jnp.dot(jnp.where(u["tri"], a, 0.0).astype(BF16), u["v"][:, vs], preferred_element_type=F32)
                   for a, (_, vs) in zip(u["att"], heads)]
    state = {}
    for u in units:
        key = id(u["s_ref"])
        if key not in state:
            state[key] = [u["s_ref"][h] for h in range(R_HEADS)]
        st = state[key]
        outs = []
        for h, (ks, _) in enumerate(heads):
            outs.append(u["oi"][h] + lax.dot_general(u["qb"][:, ks], st[h].astype(BF16), _NT,
                                                     preferred_element_type=F32))
            st[h] = st[h] * u["dec"][:, ks] + u["ds"][h]
        u["o_ref"][0, u["sl"], :] = jnp.concatenate(outs, axis=1).astype(BF16)
    for _, _, _, _, s_ref, _, _ in dirs:
        for h in range(R_HEADS):
            s_ref[h] = state[id(s_ref)][h]


def _hgrn_scan_kernel(lb_ref, qf_ref, vf_ref, zf_ref, qb_ref, vb_ref, zb_ref, of_ref, ob_ref, sf_ref, sb_ref):
    @pl.when(pl.program_id(1) == 0)
    def _():
        sf_ref[...] = jnp.zeros_like(sf_ref)
        sb_ref[...] = jnp.zeros_like(sb_ref)

    def gates(z, lb):
        f = lb + (1.0 - lb) * jax.nn.sigmoid(z.astype(F32))
        return 1.0 - f, jnp.log(f)

    kf, lf = gates(zf_ref[0], lb_ref[0:1, :])
    kb, lbw = gates(zb_ref[0], lb_ref[1:2, :])
    _scan_blocks([(qf_ref[0], kf, vf_ref[0], lf, sf_ref, of_ref, False),
                  (qb_ref[0], kb, vb_ref[0], lbw, sb_ref, ob_ref, True)], dvp=128)


def _bwd_blk(j, nt):
    return jnp.where(j == 0, 0, nt - j)


def _hgrn_scan(proj, lb, cq, cv, cf, cb):
    b, rc, _ = proj.shape
    nt = rc // TM
    w = R_HEADS * DKP

    def fw(col):
        return pl.BlockSpec((1, TM, w), lambda i, j: (i, j, col))

    def bw(col):
        return pl.BlockSpec((1, TM, w), lambda i, j: (i, _bwd_blk(j, nt), col))

    o_sd = jax.ShapeDtypeStruct((b, rc, w), BF16)
    return pl.pallas_call(
        _hgrn_scan_kernel,
        grid=(b, nt),
        in_specs=[pl.BlockSpec((2, w), lambda i, j: (0, 0)),
                  fw(cq), fw(cv), fw(cf), bw(cq), bw(cv), bw(cb)],
        out_specs=[pl.BlockSpec((1, TM, w), lambda i, j: (i, j, 0)),
                   pl.BlockSpec((1, TM, w), lambda i, j: (i, _bwd_blk(j, nt), 0))],
        out_shape=[o_sd, o_sd],
        scratch_shapes=[pltpu.VMEM((R_HEADS, 128, DKP), F32), pltpu.VMEM((R_HEADS, 128, DKP), F32)],
        compiler_params=_cp(("parallel", "arbitrary"), 40 * 1024 * 1024),
        name="hgrn_scan",
    )(lb, proj, proj, proj, proj, proj, proj)


def _gla_scan_kernel(gwf_ref, gwb_ref, gb_ref, qf_ref, kf_ref, vf_ref, rf_ref, qb_ref, kb_ref, vb_ref, rb_ref,
                     of_ref, ob_ref, sf_ref, sb_ref):
    @pl.when(pl.program_id(1) == 0)
    def _():
        sf_ref[...] = jnp.zeros_like(sf_ref)
        sb_ref[...] = jnp.zeros_like(sb_ref)

    def decay(r, gw, gb):
        zz = jnp.dot(jnp.concatenate([r, r], axis=1), _split2(gw, 0), preferred_element_type=F32) + gb
        return (jnp.minimum(zz, 0.0) - jnp.log(1.0 + jnp.exp(-jnp.abs(zz)))) * (1.0 / GLA_TAU)

    qs = GLA_DK ** -0.5
    _scan_blocks([(qf_ref[0].astype(F32) * qs, kf_ref[0], vf_ref[0],
                   decay(rf_ref[0], gwf_ref[...], gb_ref[0:1, :]), sf_ref, of_ref, False),
                  (qb_ref[0].astype(F32) * qs, kb_ref[0], vb_ref[0],
                   decay(rb_ref[0], gwb_ref[...], gb_ref[1:2, :]), sb_ref, ob_ref, True)], dvp=GLA_DVP)


def _gla_scan(proj, gwf, gwb, gb, cq, ck, cv, cr):
    b, rc, _ = proj.shape
    nt = rc // TM
    wk = R_HEADS * DKP
    wv = R_HEADS * GLA_DVP

    def spec(width, col, back):
        if back:
            return pl.BlockSpec((1, TM, width), lambda i, j: (i, _bwd_blk(j, nt), col))
        return pl.BlockSpec((1, TM, width), lambda i, j: (i, j, col))

    o_sd = jax.ShapeDtypeStruct((b, rc, wv), BF16)
    full = lambda shape: pl.BlockSpec(shape, lambda i, j: (0,) * len(shape))
    return pl.pallas_call(
        _gla_scan_kernel,
        grid=(b, nt),
        in_specs=[full((128, wk)), full((128, wk)), full((2, wk)),
                  spec(wk, cq, False), spec(wk, ck, False), spec(wv, cv, False), spec(128, cr, False),
                  spec(wk, cq, True), spec(wk, ck, True), spec(wv, cv, True), spec(128, cr, True)],
        out_specs=[spec(wv, 0, False), spec(wv, 0, True)],
        out_shape=[o_sd, o_sd],
        scratch_shapes=[pltpu.VMEM((R_HEADS, GLA_DVP, DKP), F32), pltpu.VMEM((R_HEADS, GLA_DVP, DKP), F32)],
        compiler_params=_cp(("parallel", "arbitrary"), 48 * 1024 * 1024),
        name="gla_scan",
    )(gwf, gwb, gb, proj, proj, proj, proj, proj, proj, proj, proj)


def _fourier_kernel(z_ref, cs_ref, ct_ref, st_ref, o_ref, *, scale):
    w = F_GROUPS * F_GD
    zz = jnp.dot(z_ref[0, CTX:, :], cs_ref[...], preferred_element_type=F32).astype(BF16)
    y = jnp.dot(ct_ref[...], zz[:, :w], preferred_element_type=F32)
    y = y - jnp.dot(st_ref[...], zz[:, w:], preferred_element_type=F32)
    o_ref[0] = (y * scale).astype(BF16)


def _fourier(proj, zcol, cs, ct, st, t):
    b, rc, _ = proj.shape
    w = F_GROUPS * F_GD
    const = lambda shape: pl.BlockSpec(shape, lambda i: (0, 0), pipeline_mode=pl.Buffered(1))
    return pl.pallas_call(
        functools.partial(_fourier_kernel, scale=1.0 / math.sqrt(t * F_GD)),
        grid=(b,),
        in_specs=[pl.BlockSpec((1, rc, w), lambda i: (i, 0, zcol)), const((w, 2 * w)), const((t, t)), const((t, t))],
        out_specs=pl.BlockSpec((1, t, w), lambda i: (i, 0, 0)),
        out_shape=jax.ShapeDtypeStruct((b, t, w), BF16),
        compiler_params=_cp(("arbitrary",), 48 * 1024 * 1024),
        name="fourier",
    )(proj, cs, ct, st)


def _route(logits, rb_ref):
    lt = logits.T
    l = [lt[e:e + 1, :] for e in range(N_EXP)]
    m = functools.reduce(jnp.maximum, l)
    ex = [jnp.exp(v - m) for v in l]
    den = functools.reduce(lambda a, c: a + c, ex)
    p = [v / den for v in ex]
    sel = [p[e] + rb_ref[e] for e in range(N_EXP)]
    g_score, g_gate = [], []
    for g in range(N_GRP):
        s = sel[E_PER * g:E_PER * (g + 1)]
        chosen = []
        for i in range(E_PER):
            rank = jnp.zeros_like(s[i])
            for j in range(E_PER):
                if j == i:
                    continue
                ahead = (s[j] >= s[i]) if j < i else (s[j] > s[i])
                rank = rank + jnp.where(ahead, 1.0, 0.0)
            chosen.append(rank < 2.0)
        g_score.append(functools.reduce(lambda a, c: a + c,
                                        [jnp.where(chosen[i], s[i], 0.0) for i in range(E_PER)]))
        g_gate.append([jnp.where(chosen[i], p[E_PER * g + i], 0.0) for i in range(E_PER)])
    best = jnp.zeros_like(g_score[0])
    best_s = g_score[0]
    for g in range(1, N_GRP):
        upd = g_score[g] > best_s
        best = jnp.where(upd, float(g), best)
        best_s = jnp.where(upd, g_score[g], best_s)
    gate = []
    for i in range(E_PER):
        gi = g_gate[0][i]
        for g in range(1, N_GRP):
            gi = jnp.where(best == float(g), g_gate[g][i], gi)
        gate.append(gi)
    wsum = functools.reduce(lambda a, c: a + c, gate)
    row = lax.broadcasted_iota(jnp.int32, (GW, lt.shape[1]), 0)
    side = jnp.zeros((GW, lt.shape[1]), F32)
    for i in range(E_PER):
        side = jnp.where(row == i, gate[i] / wsum, side)
    return side.T, best


def _outproj_kernel(rb_ref, a_ref, of_ref, ob_ref, gt_ref, *refs, dvp, dv):
    *h_refs, mod_ref, gn_ref, wa_ref, wr_ref, lng_ref, lnb_ref, wrt_ref, h1_ref, u2_ref, grp_ref = refs
    o = of_ref[0].astype(F32) + ob_ref[0].astype(F32)
    parts = []
    for hh in range(R_HEADS):
        sl = slice(hh * dvp, (hh + 1) * dvp)
        oh = o[:, sl]
        ms = jnp.sum(oh * oh, axis=-1, keepdims=True) * (1.0 / dv)
        parts.append(oh * lax.rsqrt(ms + LN_EPS))
    rec = jnp.concatenate(parts, axis=1) * gn_ref[...] * _silu(gt_ref[0].astype(F32))
    y = jnp.dot(a_ref[0], wa_ref[...], preferred_element_type=F32)
    y = y + jnp.dot(rec.astype(BF16), wr_ref[...], preferred_element_type=F32)
    g1 = mod_ref[0, 0, 2:3, :]
    h1 = _ln(ALPHA * _rows_of(h_refs) + g1 * y) * lng_ref[0:1, :] + lnb_ref[0:1, :]
    h1_ref[0] = h1
    u2 = _ln(h1) * (1.0 + mod_ref[0, 0, 4:5, :]) + mod_ref[0, 0, 3:4, :]
    lg = jnp.dot(_split2(u2, 0), _split2(wrt_ref[...], 1), preferred_element_type=F32)
    lg = lg[:TM] + lg[TM:]
    logits = lg[:, :GW] + lg[:, GW:]
    side, best = _route(logits, rb_ref)
    grp_ref[0, 0] = best
    for c in range(D // 128):
        u2_ref[pl.ds(c, TM, stride=XROWS), :] = u2[:, 128 * c:128 * (c + 1)]
    u2_ref[pl.ds(D // 128, TM, stride=XROWS), :] = side
    for c in range(D // 128 + 1, XROWS):
        u2_ref[pl.ds(c, TM, stride=XROWS), :] = jnp.zeros((TM, 128), F32)


def _outproj(b_router, a, a_col, a_w, o_f, o_b, proj, g_col, h, modsel, gnorm, wa, wr, lng, lnb, wrt, *,
             dv, dvp, row_off):
    b, rc, _ = proj.shape
    nt = rc // TM - row_off
    wrec = R_HEADS * dvp
    full = lambda shape: pl.BlockSpec(shape, lambda i, j: (0,) * len(shape))
    h_specs, h_args = _row_specs(h, row_off)
    return pl.pallas_call(
        functools.partial(_outproj_kernel, dvp=dvp, dv=dv),
        grid=(b, nt),
        in_specs=[pl.BlockSpec(memory_space=pltpu.SMEM),
                  pl.BlockSpec((1, TM, a_w), lambda i, j: (i, j + (row_off if a.shape[1] == rc else 0), a_col)),
                  pl.BlockSpec((1, TM, wrec), lambda i, j: (i, j + row_off, 0)),
                  pl.BlockSpec((1, TM, wrec), lambda i, j: (i, j + row_off, 0)),
                  pl.BlockSpec((1, TM, wrec), lambda i, j: (i, j + row_off, g_col))] + h_specs + [
                  pl.BlockSpec((1, 1, 6, D), lambda i, j: (i, jnp.minimum(j + row_off, 1), 0, 0)),
                  full((1, wrec)), full((a_w, D)), full((wrec, D)), full((1, D)), full((1, D)), full((D, GW))],
        out_specs=[pl.BlockSpec((1, TM, D), lambda i, j: (i, j, 0)),
                   pl.BlockSpec((TM * XROWS, 128), lambda i, j: (i * nt + j, 0)),
                   pl.BlockSpec((1, 1, 1, TM), lambda i, j: (i, j, 0, 0))],
        out_shape=[jax.ShapeDtypeStruct((b, nt * TM, D), F32),
                   jax.ShapeDtypeStruct((b * nt * TM * XROWS, 128), F32),
                   jax.ShapeDtypeStruct((b, nt, 1, TM), F32)],
        compiler_params=_cp(("parallel", "arbitrary"), V7X_VMEM_LIMIT),
        name="outproj",
    )(b_router, a, o_f, o_b, proj, *h_args, modsel, gnorm, wa, wr, lng, lnb, wrt)


def _start_token_gather(idx_ref, base, src_hbm, dst_ref, sem, rows):
    def body(r, carry):
        t = idx_ref[base + r]
        pltpu.make_async_copy(src_hbm.at[pl.ds(pl.multiple_of(t * rows, rows), rows), :],
                              dst_ref.at[pl.ds(pl.multiple_of(r * rows, rows), rows), :], sem).start()
        return carry

    lax.fori_loop(0, dst_ref.shape[0] // rows, body, 0, unroll=8)


def _wait_token_gather(src_hbm, dst_ref, sem):
    pltpu.make_async_copy(src_hbm.at[pl.ds(0, dst_ref.shape[0]), :], dst_ref, sem).wait()


def _gather_ring(idx_ref, first_of, step, last, src_hbm, buf_ref, sem, rows):
    depth = buf_ref.shape[0]
    slot = step % depth

    @pl.when(step == 0)
    def _():
        for k in range(depth - 1):
            _start_token_gather(idx_ref, first_of(jnp.minimum(k, last)), src_hbm, buf_ref.at[k], sem.at[k], rows)

    _wait_token_gather(src_hbm, buf_ref.at[slot], sem.at[slot])
    ahead = (step + depth - 1) % depth
    nxt = first_of(jnp.minimum(step + depth - 1, last))

    def issue(r0, r1):
        for r in range(r0, r1):
            t = idx_ref[nxt + r]
            pltpu.make_async_copy(src_hbm.at[pl.ds(pl.multiple_of(t * rows, rows), rows), :],
                                  buf_ref.at[ahead, pl.ds(r * rows, rows), :], sem.at[ahead]).start()

    def finish():
        @pl.when(step == last)
        def _():
            for k in range(1, depth):
                other = (step + k) % depth
                _wait_token_gather(src_hbm, buf_ref.at[other], sem.at[other])

    return slot, issue, finish


def _moe_kernel(tg_ref, first_ref, src_ref, x_hbm, w1_ref, w3_ref, w2_ref, o_ref, xb_ref, sem,
                w1b_ref, w3b_ref, w2b_ref):
    i = pl.program_id(0)
    slot, issue, finish = _gather_ring(src_ref, lambda k: first_ref[k], i, pl.num_programs(0) - 1,
                                       x_hbm, xb_ref, sem, XROWS)

    @pl.when((i == 0) | (tg_ref[i] != tg_ref[jnp.maximum(i - 1, 0)]))
    def _():
        for e in range(E_PER):
            w1b_ref[e] = w1_ref[e].astype(BF16)
            w3b_ref[e] = w3_ref[e].astype(BF16)
            w2b_ref[e] = w2_ref[e].astype(BF16)

    xb = xb_ref.at[slot]
    x = jnp.concatenate([xb[pl.ds(c, TME, stride=XROWS), :] for c in range(D // 128)], axis=1).astype(BF16)
    side = xb[pl.ds(D // 128, TME, stride=XROWS), :]
    y = jnp.zeros((TME, D), F32)
    per = TME // E_PER
    for e in range(E_PER):
        issue(e * per, (e + 1) * per)
        h1 = jnp.dot(x, w1b_ref[e], preferred_element_type=F32)
        h3 = jnp.dot(x, w3b_ref[e], preferred_element_type=F32)
        hh = (_silu(h1) * h3 * side[:, e:e + 1]).astype(BF16)
        y = y + jnp.dot(hh, w2b_ref[e], preferred_element_type=F32)
    for c in range(D // 128):
        o_ref[pl.ds(c, TME, stride=YROWS), :] = y[:, 128 * c:128 * (c + 1)]
    finish()


def _moe(tile_grp, tile_first, order, u2x, w1, w3, w2, layer):
    mp = tile_grp.shape[0] * TME
    wspec = lambda k, n: pl.BlockSpec((None, E_PER, k, n), lambda i, tg, tf, od: (layer, tg[i], 0, 0),
                                      pipeline_mode=pl.Buffered(1))
    return pl.pallas_call(
        _moe_kernel,
        grid_spec=pltpu.PrefetchScalarGridSpec(
            num_scalar_prefetch=3, grid=(mp // TME,),
            in_specs=[pl.BlockSpec(memory_space=pl.ANY), wspec(D, D_FF), wspec(D, D_FF), wspec(D_FF, D)],
            out_specs=pl.BlockSpec((TME * YROWS, 128), lambda i, tg, tf, od: (i, 0)),
            scratch_shapes=[pltpu.VMEM((3, TME * XROWS, 128), F32), pltpu.SemaphoreType.DMA((3,)),
                            pltpu.VMEM((E_PER, D, D_FF), BF16), pltpu.VMEM((E_PER, D, D_FF), BF16),
                            pltpu.VMEM((E_PER, D_FF, D), BF16)]),
        out_shape=jax.ShapeDtypeStruct((mp * YROWS, 128), F32),
        compiler_params=_cp(("arbitrary",), V7X_VMEM_LIMIT),
        name="moe",
    )(tile_grp, tile_first, order, u2x, w1, w3, w2)


def _combine_kernel(pos_ref, y_hbm, h_ref, mod_ref, lng_ref, lnb_ref, o_ref, yb_ref, sem, *, nt):
    step = pl.program_id(0) * nt + pl.program_id(1)
    slot, issue, finish = _gather_ring(pos_ref, lambda k: k * TM, step, pl.num_programs(0) * nt - 1,
                                       y_hbm, yb_ref, sem, YROWS)
    yb = yb_ref.at[slot]
    g2 = mod_ref[0, 0, 5:6, :]
    strip = TM // 4
    for r0 in range(0, TM, strip):
        issue(r0, r0 + strip)
        y = jnp.concatenate([yb[pl.ds(r0 * YROWS + c, strip, stride=YROWS), :] for c in range(D // 128)], axis=1)
        o_ref[0, r0:r0 + strip, :] = (_ln(ALPHA * h_ref[0, r0:r0 + strip, :] + g2 * y) * lng_ref[0:1, :]
                                      + lnb_ref[0:1, :])
    finish()


def _combine(pos, y_sorted, h1, modsel, lng, lnb, row_off):
    b, r, _ = h1.shape
    nt = r // TM
    full = lambda shape: pl.BlockSpec(shape, lambda i, j, p: (0,) * len(shape))
    return pl.pallas_call(
        functools.partial(_combine_kernel, nt=nt),
        grid_spec=pltpu.PrefetchScalarGridSpec(
            num_scalar_prefetch=1, grid=(b, nt),
            in_specs=[pl.BlockSpec(memory_space=pl.ANY),
                      pl.BlockSpec((1, TM, D), lambda i, j, p: (i, j, 0)),
                      pl.BlockSpec((1, 1, 6, D), lambda i, j, p: (i, jnp.minimum(j + row_off, 1), 0, 0)),
                      full((1, D)), full((1, D))],
            out_specs=pl.BlockSpec((1, TM, D), lambda i, j, p: (i, j, 0)),
            scratch_shapes=[pltpu.VMEM((3, TM * YROWS, 128), F32), pltpu.SemaphoreType.DMA((3,))]),
        out_shape=jax.ShapeDtypeStruct((b, r, D), F32),
        compiler_params=_cp(("arbitrary", "arbitrary")),
        name="moe_combine",
    )(pos, y_sorted, h1, modsel, lng, lnb)


def _sort_by_group(grp):
    n = grp.shape[0]
    onehot = (grp[:, None] == jnp.arange(N_GRP, dtype=jnp.int32)[None, :]).astype(jnp.int32)
    csum = jnp.cumsum(onehot, axis=0)
    counts = csum[-1]
    padded = ((counts + TME - 1) // TME) * TME
    ends = jnp.cumsum(padded)
    starts = ends - padded
    rank = jnp.sum(csum * onehot, axis=1) - 1
    pos = (starts[grp] + rank).astype(jnp.int32)
    _, order = lax.sort((grp, jnp.arange(n, dtype=jnp.int32)), num_keys=1, is_stable=True)
    order = jnp.concatenate([order, jnp.zeros((TME,), jnp.int32)])
    ntile = n // TME + N_GRP
    tile_start = jnp.arange(ntile, dtype=jnp.int32) * TME
    tile_grp = jnp.minimum(jnp.sum((tile_start[:, None] >= ends[None, :]).astype(jnp.int32), axis=1), N_GRP - 1)
    first = (jnp.cumsum(counts) - counts)[tile_grp] + tile_start - starts[tile_grp]
    return pos, order, tile_grp.astype(jnp.int32), jnp.minimum(first, n).astype(jnp.int32)


def _moe_layer(u2x, grp, h1, modsel, experts, lng, lnb, row_off):
    pos, order, tile_grp, tile_first = _sort_by_group(grp.reshape(-1).astype(jnp.int32))
    y_sorted = _moe(tile_grp, tile_first, order, u2x, *experts)
    return _combine(pos, y_sorted, h1, modsel, lng, lnb, row_off)


def _rope_tables(rc, t):
    half = A_HD // 4
    inv = ROPE_BASE ** (-jnp.arange(half, dtype=F32) / half)
    tpos = jnp.arange(t)
    rows = (tpos // GRID_W).astype(F32)
    cols = (tpos % GRID_W).astype(F32)

    def cs(pos):
        ang = pos[:, None] * inv[None, :]
        c, s = jnp.cos(ang), jnp.sin(ang)
        return jnp.concatenate([c, c], -1), jnp.concatenate([-s, s], -1)

    cr, sr = cs(rows)
    cc_, sc_ = cs(cols)
    cos_h = jnp.concatenate([cr, cc_], -1)
    sin_h = jnp.concatenate([sr, sc_], -1)
    one = jnp.ones((t, A_HD), F32)
    zero = jnp.zeros((t, A_HD), F32)
    lat = jnp.stack([jnp.concatenate([cos_h, cos_h], -1), jnp.concatenate([sin_h, sin_h], -1),
                     jnp.concatenate([cos_h, one], -1), jnp.concatenate([sin_h, zero], -1)])
    ctx = jnp.stack([jnp.ones((CTX, 128), F32), jnp.zeros((CTX, 128), F32)] * 2)
    return jnp.concatenate([ctx, lat], axis=1)


def _dft_tables(t):
    def cos_sin(num, den):
        ang = (2.0 * math.pi / den) * (num % den).astype(F32)
        return jnp.cos(ang), jnp.sin(ang)

    k = jnp.arange(t, dtype=jnp.int32)[:, None]
    ca, sa = cos_sin(k * jnp.arange(t // GRID_W, dtype=jnp.int32)[None, :], t // GRID_W)
    cb, sb = cos_sin(k * jnp.arange(GRID_W, dtype=jnp.int32)[None, :], t)
    ct = (ca[:, :, None] * cb[:, None, :] - sa[:, :, None] * sb[:, None, :]).reshape(t, t)
    st = (sa[:, :, None] * cb[:, None, :] + ca[:, :, None] * sb[:, None, :]).reshape(t, t)
    c = jnp.arange(F_GD, dtype=jnp.int32)
    cc, sc = cos_sin(c[:, None] * c[None, :], F_GD)
    eye = jnp.eye(F_GROUPS, dtype=F32)
    cs = jnp.concatenate([jnp.kron(eye, cc), jnp.kron(eye, sc)], axis=1)
    return cs.astype(BF16), ct.astype(BF16), st.astype(BF16)


def _pad_heads(w, width, padded):
    lead = w.shape[:-1]
    w = w.reshape(lead + (R_HEADS, width))
    w = jnp.pad(w, [(0, 0)] * len(lead) + [(0, 0), (0, padded - width)])
    return w.reshape(lead + (R_HEADS * padded,))


EVEN_SEGS = tuple([(128 * g, 128 * (g + 1), 0) for g in range(4)] + [(c, c + 512, None) for c in range(512, 3072, 512)]
                  + [(3072 + 128 * g, 3072 + 128 * (g + 1), 1) for g in range(2)])
ODD_W = 512 + 512 + 1024 + 1024 + 256 + 128
ODD_SEGS = tuple([(c, c + 512, None) for c in range(0, 3072, 512)] + [(3072, ODD_W, None)])


def _layer_even(h, modsel, tbl, w_in, sink, lb_logits, g_norm, w_out, lng, lnb, w_router, b_router, experts):
    aq, ak, av, ff, fb, hq, hi, hg = jnp.split(w_in, np.cumsum([512, 128, 128, 512, 512, 512, 512]).tolist(), axis=1)
    kv = [jnp.concatenate([ak[:, A_HD * j:A_HD * (j + 1)], av[:, A_HD * j:A_HD * (j + 1)]], axis=1)
          for j in range(A_KV)]
    w = jnp.concatenate([aq, ff, fb, hq, hi, hg] + kv, axis=1).astype(BF16)
    proj = _inproj(h, modsel, w, tbl, EVEN_SEGS)
    o_att = _attention(proj, sink, 3072 // (2 * A_HD * A_KV))
    lb = jnp.cumsum(jax.nn.softmax(lb_logits.astype(F32), axis=1), axis=1)[:, 0]
    o_f, o_b = _hgrn_scan(proj, lb, 3, 4, 1, 2)
    wo = w_out.astype(BF16)
    h1, u2x, grp = _outproj(b_router, o_att, 0, 512, o_f, o_b, proj, 5, h, modsel, g_norm.reshape(1, -1),
                            wo[:512], wo[512:], lng[0:1], lnb[0:1], w_router, dv=128, dvp=128, row_off=0)
    return _moe_layer(u2x, grp, h1, modsel, experts, lng[1:2], lnb[1:2], 0)


def _layer_odd(h, modsel, tbl, dft, w_in, gate_w, gate_b, g_norm, w_out, lng, lnb, w_router, b_router, experts, t):
    q, k, v, rf, rb, g, z = jnp.split(w_in, np.cumsum([384, 384, 768, 16, 16, 768]).tolist(), axis=1)
    r = jnp.pad(jnp.concatenate([rf, rb], axis=1), ((0, 0), (0, 96)))
    w = jnp.concatenate([_pad_heads(q, GLA_DK, DKP), _pad_heads(k, GLA_DK, DKP), _pad_heads(v, GLA_DV, GLA_DVP),
                         _pad_heads(g, GLA_DV, GLA_DVP), z, r], axis=1).astype(BF16)
    proj = _inproj(h, modsel, w, tbl, ODD_SEGS)
    gw = _pad_heads(gate_w, GLA_DK, DKP)
    gwf = jnp.pad(gw[0], ((0, 112), (0, 0)))
    gwb = jnp.pad(gw[1], ((16, 96), (0, 0)))
    gb = _pad_heads(gate_b, GLA_DK, DKP)
    o_f, o_b = _gla_scan(proj, gwf, gwb, gb, 0, 1, 1, 3328 // 128)
    cs, ct, st = dft
    four = _fourier(proj, 3072 // (F_GROUPS * F_GD), cs, ct, st, t)
    wo = w_out
    wr = jnp.pad(wo[:768].reshape(R_HEADS, GLA_DV, D), ((0, 0), (0, GLA_DVP - GLA_DV), (0, 0))).reshape(-1, D)
    h1, u2x, grp = _outproj(b_router, four, 0, 256, o_f, o_b, proj, 2, h, modsel,
                            _pad_heads(g_norm, GLA_DV, GLA_DVP).reshape(1, -1),
                            wo[768:].astype(BF16), wr.astype(BF16), lng[0:1], lnb[0:1], w_router,
                            dv=GLA_DV, dvp=GLA_DVP, row_off=CTX // TM)
    return _moe_layer(u2x, grp, h1, modsel, experts, lng[1:2], lnb[1:2], CTX // TM)


def kernel(x, c, ctx, c_ctx, w_ada, b_ada, ln_g, ln_b, w_in_even, attn_sink, hgrn_lb_logits, hgrn_norm, w_out_even, w_in_odd, gla_gate_w, gla_gate_b, gla_norm, w_out_odd, w_router, b_router, w_expert_gate, w_expert_up, w_expert_down):
    b, t, _ = x.shape
    rc = CTX + t
    assert ctx.shape[1] == CTX
    cc = jnp.zeros((16, D), F32).at[:b].set(c).at[b].set(c_ctx)
    mods = _ada_mods(cc, w_ada, b_ada).reshape(2, 16, 6, D)

    def modsel(l):
        return jnp.stack([jnp.broadcast_to(mods[l, b], (b, 6, D)), mods[l, :b]], axis=1)

    tbl = _rope_tables(rc, t)
    wrt = jnp.pad(w_router, ((0, 0), (0, GW - N_EXP)))
    experts = (w_expert_gate, w_expert_up, w_expert_down)
    h = _layer_even((ctx, x), modsel(0), tbl, w_in_even[0], attn_sink[0], hgrn_lb_logits, hgrn_norm[0], w_out_even[0],
                    ln_g[0], ln_b[0], wrt, b_router, experts + (0,))
    return _layer_odd(h, modsel(1), tbl, _dft_tables(t), w_in_odd[0], gla_gate_w[0], gla_gate_b[0], gla_norm[0],
                      w_out_odd[0], ln_g[1], ln_b[1], wrt, b_router, experts + (1,), t)
```

```python
import functools
import math

import numpy as np
import jax
import jax.numpy as jnp
from jax import lax
from jax.experimental import pallas as pl
from jax.experimental.pallas import tpu as pltpu

F32 = jnp.float32
BF16 = jnp.bfloat16
HIGHEST = lax.Precision.HIGHEST

D = 1024
CTX = 256
GRID_W = 64
LN_EPS = 1e-5
NEG = -1e30
ALPHA = 4.0 ** 0.25
ROPE_BASE = 10000.0

A_HEADS, A_KV, A_HD, A_BLK = 8, 2, 64, 128
A_SUB = 2
R_HEADS = 4
DKP = 128
GLA_DK, GLA_DV, GLA_DVP = 96, 192, 256
GLA_TAU = 16.0
CHUNK = 64
N_EXP, N_GRP, E_PER = 16, 4, 4
D_FF = 512
F_GROUPS, F_GD = 4, 64

TM = 256
IN_SUB = 3
GW = 128
XROWS = 16
YROWS = 8
TME = 256
V7X_VMEM_LIMIT = 56 * 1024 * 1024


def _cp(sem, vmem=None):
    return pltpu.CompilerParams(dimension_semantics=sem, vmem_limit_bytes=vmem)


def _ln(x):
    mu = jnp.mean(x, axis=-1, keepdims=True)
    xc = x - mu
    var = jnp.mean(xc * xc, axis=-1, keepdims=True)
    return xc * lax.rsqrt(var + LN_EPS)


def _silu(x):
    return x * jax.nn.sigmoid(x)


def _split2(w, axis):
    hi = w.astype(BF16)
    return jnp.concatenate([hi, (w - hi.astype(F32)).astype(BF16)], axis=axis)


_NT = (((1,), (1,)), ((), ()))
_TN = (((0,), (0,)), ((), ()))


def _ada_kernel(c_ref, w_ref, b_ref, o_ref):
    s = _silu(c_ref[...])
    o_ref[0] = jnp.dot(s, w_ref[0], precision=HIGHEST, preferred_element_type=F32) + b_ref[0]


def _ada_mods(cc, w_ada, b_ada):
    depth, _, n = w_ada.shape
    tn = 1536
    return pl.pallas_call(
        _ada_kernel,
        grid=(depth, n // tn),
        in_specs=[pl.BlockSpec((16, D), lambda l, i: (0, 0)),
                  pl.BlockSpec((1, D, tn), lambda l, i: (l, 0, i)),
                  pl.BlockSpec((1, 1, tn), lambda l, i: (l, 0, i))],
        out_specs=pl.BlockSpec((1, 16, tn), lambda l, i: (l, 0, i)),
        out_shape=jax.ShapeDtypeStruct((depth, 16, n), F32),
        compiler_params=_cp(("arbitrary", "arbitrary"), 40 * 1024 * 1024),
        name="ada_mod",
    )(cc, w_ada, b_ada.reshape(depth, 1, n))


def _rows_of(h_refs):
    if len(h_refs) == 1:
        return h_refs[0][0]
    return jnp.where(pl.program_id(1) == 0, h_refs[0][0], h_refs[1][0])


def _row_specs(h, row_off=0):
    if isinstance(h, tuple):
        assert row_off == 0 and h[0].shape[1] == TM
        return [pl.BlockSpec((1, TM, D), lambda i, j: (i, 0, 0)),
                pl.BlockSpec((1, TM, D), lambda i, j: (i, jnp.maximum(j - 1, 0), 0))], list(h)
    return [pl.BlockSpec((1, TM, D), lambda i, j: (i, j + row_off, 0))], [h]


def _inproj_kernel(*refs, segs):
    *h_refs, mod0_ref, mod1_ref, w_ref, tbl_ref, o_ref = refs
    if len(h_refs) == 1:
        x = h_refs[0][0]
    else:
        ctx_ref, *x_refs = h_refs
        x = jnp.concatenate([jnp.where(pl.program_id(1) == 0, ctx_ref[0], x_refs[0][0])]
                            + [r[0] for r in x_refs[1:]], axis=0)
    xn = _ln(x)
    u = jnp.concatenate([xn[:TM] * (1.0 + mod0_ref[0, 0, 1:2, :]) + mod0_ref[0, 0, 0:1, :],
                         xn[TM:] * (1.0 + mod1_ref[0, 0, 1:2, :]) + mod1_ref[0, 0, 0:1, :]], axis=0).astype(BF16)
    lane = lax.broadcasted_iota(jnp.int32, (x.shape[0], 128), 1)
    low = (lane % 32) < 16
    for c0, c1, rope in segs:
        acc = jnp.dot(u, w_ref[:, c0:c1], preferred_element_type=F32)
        if rope is not None:
            partner = jnp.where(low, pltpu.roll(acc, 112, axis=1), pltpu.roll(acc, 16, axis=1))
            acc = acc * tbl_ref[2 * rope] + partner * tbl_ref[2 * rope + 1]
        o_ref[0, :, c0:c1] = acc.astype(BF16)


def _inproj(h, modsel, w, tbl, segs):
    b, rc = modsel.shape[0], tbl.shape[1]
    n = w.shape[1]
    tmi = IN_SUB * TM
    assert rc % tmi == 0
    if isinstance(h, tuple):
        assert h[0].shape[1] == TM
        h_specs = [pl.BlockSpec((1, TM, D), lambda i, j: (i, 0, 0))] + [
            pl.BlockSpec((1, TM, D), lambda i, j, k=k: (i, jnp.maximum(IN_SUB * j + k - 1, 0), 0))
            for k in range(IN_SUB)]
        h_args = [h[0]] + [h[1]] * IN_SUB
    else:
        h_specs, h_args = [pl.BlockSpec((1, tmi, D), lambda i, j: (i, j, 0))], [h]
    return pl.pallas_call(
        functools.partial(_inproj_kernel, segs=segs),
        grid=(b, rc // tmi),
        in_specs=h_specs + [pl.BlockSpec((1, 1, 6, D), lambda i, j: (i, jnp.minimum(j, 1), 0, 0)),
                            pl.BlockSpec((1, 1, 6, D), lambda i, j: (i, 1, 0, 0)),
                            pl.BlockSpec((D, n), lambda i, j: (0, 0)),
                            pl.BlockSpec((4, tmi, 128), lambda i, j: (0, j, 0))],
        out_specs=pl.BlockSpec((1, tmi, n), lambda i, j: (i, j, 0)),
        out_shape=jax.ShapeDtypeStruct((b, rc, n), BF16),
        compiler_params=_cp(("parallel", "arbitrary"), V7X_VMEM_LIMIT),
        name="inproj",
    )(*h_args, modsel, modsel, w, tbl)


def _attn_kernel(sink_ref, q_ref, kvc_ref, *refs):
    band_refs, bias_refs, o_ref = refs[:A_SUB + 2], refs[A_SUB + 2:2 * A_SUB + 2], refs[-1]
    step = pl.program_id(1)
    g_per = A_HEADS // A_KV
    nq = g_per * A_BLK
    rgrp = lax.broadcasted_iota(jnp.int32, (nq, 1), 0) // A_BLK
    sink_col = []
    for h in range(A_KV):
        col = jnp.zeros((nq, 1), F32)
        for g in range(g_per):
            col = jnp.where(rgrp == g, sink_ref[h * g_per + g], col)
        sink_col.append(col)

    def attend(sb, kv, bias):
        q = q_ref[0, A_BLK * sb:A_BLK * (sb + 1), :] * (A_HD ** -0.5)
        q4 = [jnp.concatenate([q[:, A_HD * (g_per * h + g):A_HD * (g_per * h + g + 1)] for g in range(g_per)], axis=0)
              for h in range(A_KV)]
        s = [lax.dot_general(q4[h], kv[:, 2 * A_HD * h:2 * A_HD * h + A_HD], _NT, preferred_element_type=F32)
             for h in range(A_KV)]
        if bias is not None:
            bias4 = jnp.concatenate([bias] * g_per, axis=0)
            s = [x + bias4 for x in s]
        m = [jnp.maximum(jnp.max(s[h], axis=-1, keepdims=True), sink_col[h]) for h in range(A_KV)]
        p = [jnp.exp(s[h] - m[h]) for h in range(A_KV)]
        den = [jnp.sum(p[h], axis=-1, keepdims=True) + jnp.exp(sink_col[h] - m[h]) for h in range(A_KV)]
        o = [jnp.dot(p[h].astype(BF16), kv[:, 2 * A_HD * h + A_HD:2 * A_HD * (h + 1)], preferred_element_type=F32)
             / den[h] for h in range(A_KV)]
        o_ref[0, A_BLK * sb:A_BLK * (sb + 1), :] = jnp.concatenate(
            [o[h][A_BLK * g:A_BLK * (g + 1), :] for h in range(A_KV) for g in range(g_per)], axis=1).astype(BF16)

    @pl.when(step < CTX // (A_SUB * A_BLK))
    def _():
        for sb in range(A_SUB):
            attend(sb, kvc_ref[0], None)

    @pl.when(step >= CTX // (A_SUB * A_BLK))
    def _():
        for sb in range(A_SUB):
            attend(sb, jnp.concatenate([kvc_ref[0]] + [band_refs[sb + k][0] for k in range(3)], axis=0),
                   bias_refs[sb][0])


def _attn_bias():
    nk = CTX + 3 * A_BLK
    qi = jnp.arange(A_BLK, dtype=jnp.int32)[:, None]
    col = jnp.arange(nk, dtype=jnp.int32)[None, :]
    rel = col - (CTX + A_BLK)
    dist = qi - rel

    def one(lo, hi):
        in_win = (dist <= A_BLK) & (dist >= -A_BLK) & (rel >= lo) & (rel < hi)
        return jnp.where((col < CTX) | in_win, 0.0, NEG).astype(F32)

    return jnp.stack([one(0, 2 * A_BLK), one(-A_BLK, 2 * A_BLK), one(-A_BLK, A_BLK)])


def _attention(proj, sink, kv_blk):
    b, rc, _ = proj.shape
    nb = rc // A_BLK
    c_b = CTX // A_BLK
    assert nb - c_b >= 2 and nb % A_SUB == 0 and c_b % A_SUB == 0
    kw = 2 * A_HD * A_KV

    def band(k):
        return pl.BlockSpec((1, A_BLK, kw), lambda i, j: (i, jnp.clip(A_SUB * j - 1 + k, c_b, nb - 1), kv_blk))

    def bias(sb):
        def index(i, j):
            n = A_SUB * j + sb
            return (jnp.where(n <= c_b, 0, jnp.where(n == nb - 1, 2, 1)), 0, 0)
        return pl.BlockSpec((1, A_BLK, CTX + 3 * A_BLK), index)

    qo = pl.BlockSpec((1, A_SUB * A_BLK, A_HEADS * A_HD), lambda i, j: (i, j, 0))
    bias_tbl = _attn_bias()
    return pl.pallas_call(
        _attn_kernel,
        grid=(b, nb // A_SUB),
        in_specs=[pl.BlockSpec(memory_space=pltpu.SMEM), qo, pl.BlockSpec((1, CTX, kw), lambda i, j: (i, 0, kv_blk))]
        + [band(k) for k in range(A_SUB + 2)] + [bias(sb) for sb in range(A_SUB)],
        out_specs=qo,
        out_shape=jax.ShapeDtypeStruct((b, rc, A_HEADS * A_HD), BF16),
        compiler_params=_cp(("parallel", "arbitrary")),
        name="win_attn",
    )(sink, proj, proj, *([proj] * (A_SUB + 2)), *([bias_tbl] * A_SUB))


def _scan_blocks(dirs, *, dvp):
    r_i = lax.broadcasted_iota(jnp.int32, (CHUNK, CHUNK), 0)
    c_i = lax.broadcasted_iota(jnp.int32, (CHUNK, CHUNK), 1)
    units = []
    for q, k, v, logf, s_ref, o_ref, reverse in dirs:
        nchunk = q.shape[0] // CHUNK
        tri = (r_i <= c_i) if reverse else (r_i >= c_i)
        tri3 = jnp.concatenate([jnp.where(tri, 1.0, 0.0).astype(BF16)] * 3, axis=1)
        for c in (range(nchunk - 1, -1, -1) if reverse else range(nchunk)):
            sl = slice(c * CHUNK, (c + 1) * CHUNK)
            units.append(dict(q=q[sl], k=k[sl], v=v[sl].astype(BF16), lf=logf[sl], sl=sl, tri=tri, tri3=tri3,
                              i_tot=0 if reverse else CHUNK - 1,
                              i_ref=CHUNK // 2 - 1 if reverse else CHUNK // 2, s_ref=s_ref, o_ref=o_ref))
    heads = [(slice(h * DKP, (h + 1) * DKP), slice(h * dvp, (h + 1) * dvp)) for h in range(R_HEADS)]
    for u in units:
        lf = u["lf"]
        hi = lf.astype(BF16)
        r1 = lf - hi.astype(F32)
        mid = r1.astype(BF16)
        lo = (r1 - mid.astype(F32)).astype(BF16)
        u["bsum"] = jnp.dot(u["tri3"], jnp.concatenate([hi, mid, lo], axis=0), preferred_element_type=F32)
    for u in units:
        bsum = u["bsum"]
        tot = bsum[u["i_tot"]:u["i_tot"] + 1, :]
        ref = bsum[u["i_ref"]:u["i_ref"] + 1, :]
        qg = u["q"] * jnp.exp(bsum - ref)
        kg = u["k"] * jnp.exp(ref - bsum)
        u["qb"] = (qg * jnp.exp(ref)).astype(BF16)
        u["kd"] = (kg * jnp.exp(tot - ref)).astype(BF16)
        u["qg"] = qg.astype(BF16)
        u["kg"] = kg.astype(BF16)
        u["dec"] = jnp.exp(tot)
    for u in units:
        u["att"] = [lax.dot_general(u["qg"][:, ks], u["kg"][:, ks], _NT, preferred_element_type=F32)
                    for ks, _ in heads]
        u["ds"] = [lax.dot_general(u["v"][:, vs], u["kd"][:, ks], _TN, preferred_element_type=F32)
                   for ks, vs in heads]
    for u in units:
        u["oi"] = [jnp.dot(jnp.where(u["tri"], a, 0.0).astype(BF16), u["v"][:, vs], preferred_element_type=F32)
                   for a, (_, vs) in zip(u["att"], heads)]
    state = {}
    for u in units:
        key = id(u["s_ref"])
        if key not in state:
            state[key] = [u["s_ref"][h] for h in range(R_HEADS)]
        st = state[key]
        outs = []
        for h, (ks, _) in enumerate(heads):
            outs.append(u["oi"][h] + lax.dot_general(u["qb"][:, ks], st[h].astype(BF16), _NT,
                                                     preferred_element_type=F32))
            st[h] = st[h] * u["dec"][:, ks] + u["ds"][h]
        u["o_ref"][0, u["sl"], :] = jnp.concatenate(outs, axis=1).astype(BF16)
    for _, _, _, _, s_ref, _, _ in dirs:
        for h in range(R_HEADS):
            s_ref[h] = state[id(s_ref)][h]


def _hgrn_scan_kernel(lb_ref, qf_ref, vf_ref, zf_ref, qb_ref, vb_ref, zb_ref, of_ref, ob_ref, sf_ref, sb_ref):
    @pl.when(pl.program_id(1) == 0)
    def _():
        sf_ref[...] = jnp.zeros_like(sf_ref)
        sb_ref[...] = jnp.zeros_like(sb_ref)

    def gates(z, lb):
        f = lb + (1.0 - lb) * jax.nn.sigmoid(z.astype(F32))
        return 1.0 - f, jnp.log(f)

    kf, lf = gates(zf_ref[0], lb_ref[0:1, :])
    kb, lbw = gates(zb_ref[0], lb_ref[1:2, :])
    _scan_blocks([(qf_ref[0], kf, vf_ref[0], lf, sf_ref, of_ref, False),
                  (qb_ref[0], kb, vb_ref[0], lbw, sb_ref, ob_ref, True)], dvp=128)


def _bwd_blk(j, nt):
    return jnp.where(j == 0, 0, nt - j)


def _hgrn_scan(proj, lb, cq, cv, cf, cb):
    b, rc, _ = proj.shape
    nt = rc // TM
    w = R_HEADS * DKP

    def fw(col):
        return pl.BlockSpec((1, TM, w), lambda i, j: (i, j, col))

    def bw(col):
        return pl.BlockSpec((1, TM, w), lambda i, j: (i, _bwd_blk(j, nt), col))

    o_sd = jax.ShapeDtypeStruct((b, rc, w), BF16)
    return pl.pallas_call(
        _hgrn_scan_kernel,
        grid=(b, nt),
        in_specs=[pl.BlockSpec((2, w), lambda i, j: (0, 0)),
                  fw(cq), fw(cv), fw(cf), bw(cq), bw(cv), bw(cb)],
        out_specs=[pl.BlockSpec((1, TM, w), lambda i, j: (i, j, 0)),
                   pl.BlockSpec((1, TM, w), lambda i, j: (i, _bwd_blk(j, nt), 0))],
        out_shape=[o_sd, o_sd],
        scratch_shapes=[pltpu.VMEM((R_HEADS, 128, DKP), F32), pltpu.VMEM((R_HEADS, 128, DKP), F32)],
        compiler_params=_cp(("parallel", "arbitrary"), 40 * 1024 * 1024),
        name="hgrn_scan",
    )(lb, proj, proj, proj, proj, proj, proj)


def _gla_scan_kernel(gwf_ref, gwb_ref, gb_ref, qf_ref, kf_ref, vf_ref, rf_ref, qb_ref, kb_ref, vb_ref, rb_ref,
                     of_ref, ob_ref, sf_ref, sb_ref):
    @pl.when(pl.program_id(1) == 0)
    def _():
        sf_ref[...] = jnp.zeros_like(sf_ref)
        sb_ref[...] = jnp.zeros_like(sb_ref)

    def decay(r, gw, gb):
        zz = jnp.dot(jnp.concatenate([r, r], axis=1), _split2(gw, 0), preferred_element_type=F32) + gb
        return (jnp.minimum(zz, 0.0) - jnp.log(1.0 + jnp.exp(-jnp.abs(zz)))) * (1.0 / GLA_TAU)

    qs = GLA_DK ** -0.5
    _scan_blocks([(qf_ref[0].astype(F32) * qs, kf_ref[0], vf_ref[0],
                   decay(rf_ref[0], gwf_ref[...], gb_ref[0:1, :]), sf_ref, of_ref, False),
                  (qb_ref[0].astype(F32) * qs, kb_ref[0], vb_ref[0],
                   decay(rb_ref[0], gwb_ref[...], gb_ref[1:2, :]), sb_ref, ob_ref, True)], dvp=GLA_DVP)


def _gla_scan(proj, gwf, gwb, gb, cq, ck, cv, cr):
    b, rc, _ = proj.shape
    nt = rc // TM
    wk = R_HEADS * DKP
    wv = R_HEADS * GLA_DVP

    def spec(width, col, back):
        if back:
            return pl.BlockSpec((1, TM, width), lambda i, j: (i, _bwd_blk(j, nt), col))
        return pl.BlockSpec((1, TM, width), lambda i, j: (i, j, col))

    o_sd = jax.ShapeDtypeStruct((b, rc, wv), BF16)
    full = lambda shape: pl.BlockSpec(shape, lambda i, j: (0,) * len(shape))
    return pl.pallas_call(
        _gla_scan_kernel,
        grid=(b, nt),
        in_specs=[full((128, wk)), full((128, wk)), full((2, wk)),
                  spec(wk, cq, False), spec(wk, ck, False), spec(wv, cv, False), spec(128, cr, False),
                  spec(wk, cq, True), spec(wk, ck, True), spec(wv, cv, True), spec(128, cr, True)],
        out_specs=[spec(wv, 0, False), spec(wv, 0, True)],
        out_shape=[o_sd, o_sd],
        scratch_shapes=[pltpu.VMEM((R_HEADS, GLA_DVP, DKP), F32), pltpu.VMEM((R_HEADS, GLA_DVP, DKP), F32)],
        compiler_params=_cp(("parallel", "arbitrary"), 48 * 1024 * 1024),
        name="gla_scan",
    )(gwf, gwb, gb, proj, proj, proj, proj, proj, proj, proj, proj)


def _fourier_kernel(z_ref, cs_ref, ct_ref, st_ref, o_ref, *, scale):
    w = F_GROUPS * F_GD
    zz = jnp.dot(z_ref[0, CTX:, :], cs_ref[...], preferred_element_type=F32).astype(BF16)
    y = jnp.dot(ct_ref[...], zz[:, :w], preferred_element_type=F32)
    y = y - jnp.dot(st_ref[...], zz[:, w:], preferred_element_type=F32)
    o_ref[0] = (y * scale).astype(BF16)


def _fourier(proj, zcol, cs, ct, st, t):
    b, rc, _ = proj.shape
    w = F_GROUPS * F_GD
    const = lambda shape: pl.BlockSpec(shape, lambda i: (0, 0), pipeline_mode=pl.Buffered(1))
    return pl.pallas_call(
        functools.partial(_fourier_kernel, scale=1.0 / math.sqrt(t * F_GD)),
        grid=(b,),
        in_specs=[pl.BlockSpec((1, rc, w), lambda i: (i, 0, zcol)), const((w, 2 * w)), const((t, t)), const((t, t))],
        out_specs=pl.BlockSpec((1, t, w), lambda i: (i, 0, 0)),
        out_shape=jax.ShapeDtypeStruct((b, t, w), BF16),
        compiler_params=_cp(("arbitrary",), 48 * 1024 * 1024),
        name="fourier",
    )(proj, cs, ct, st)


def _route(logits, rb_ref):
    lt = logits.T
    l = [lt[e:e + 1, :] for e in range(N_EXP)]
    m = functools.reduce(jnp.maximum, l)
    ex = [jnp.exp(v - m) for v in l]
    den = functools.reduce(lambda a, c: a + c, ex)
    p = [v / den for v in ex]
    sel = [p[e] + rb_ref[e] for e in range(N_EXP)]
    g_score, g_gate = [], []
    for g in range(N_GRP):
        s = sel[E_PER * g:E_PER * (g + 1)]
        chosen = []
        for i in range(E_PER):
            rank = jnp.zeros_like(s[i])
            for j in range(E_PER):
                if j == i:
                    continue
                ahead = (s[j] >= s[i]) if j < i else (s[j] > s[i])
                rank = rank + jnp.where(ahead, 1.0, 0.0)
            chosen.append(rank < 2.0)
        g_score.append(functools.reduce(lambda a, c: a + c,
                                        [jnp.where(chosen[i], s[i], 0.0) for i in range(E_PER)]))
        g_gate.append([jnp.where(chosen[i], p[E_PER * g + i], 0.0) for i in range(E_PER)])
    best = jnp.zeros_like(g_score[0])
    best_s = g_score[0]
    for g in range(1, N_GRP):
        upd = g_score[g] > best_s
        best = jnp.where(upd, float(g), best)
        best_s = jnp.where(upd, g_score[g], best_s)
    gate = []
    for i in range(E_PER):
        gi = g_gate[0][i]
        for g in range(1, N_GRP):
            gi = jnp.where(best == float(g), g_gate[g][i], gi)
        gate.append(gi)
    wsum = functools.reduce(lambda a, c: a + c, gate)
    row = lax.broadcasted_iota(jnp.int32, (GW, lt.shape[1]), 0)
    side = jnp.zeros((GW, lt.shape[1]), F32)
    for i in range(E_PER):
        side = jnp.where(row == i, gate[i] / wsum, side)
    return side.T, best


def _outproj_kernel(rb_ref, a_ref, of_ref, ob_ref, gt_ref, *refs, dvp, dv):
    *h_refs, mod_ref, gn_ref, wa_ref, wr_ref, lng_ref, lnb_ref, wrt_ref, h1_ref, u2_ref, grp_ref = refs
    o = of_ref[0].astype(F32) + ob_ref[0].astype(F32)
    parts = []
    for hh in range(R_HEADS):
        sl = slice(hh * dvp, (hh + 1) * dvp)
        oh = o[:, sl]
        ms = jnp.sum(oh * oh, axis=-1, keepdims=True) * (1.0 / dv)
        parts.append(oh * lax.rsqrt(ms + LN_EPS))
    rec = jnp.concatenate(parts, axis=1) * gn_ref[...] * _silu(gt_ref[0].astype(F32))
    y = jnp.dot(a_ref[0], wa_ref[...], preferred_element_type=F32)
    y = y + jnp.dot(rec.astype(BF16), wr_ref[...], preferred_element_type=F32)
    g1 = mod_ref[0, 0, 2:3, :]
    h1 = _ln(ALPHA * _rows_of(h_refs) + g1 * y) * lng_ref[0:1, :] + lnb_ref[0:1, :]
    h1_ref[0] = h1
    u2 = _ln(h1) * (1.0 + mod_ref[0, 0, 4:5, :]) + mod_ref[0, 0, 3:4, :]
    lg = jnp.dot(_split2(u2, 0), _split2(wrt_ref[...], 1), preferred_element_type=F32)
    lg = lg[:TM] + lg[TM:]
    logits = lg[:, :GW] + lg[:, GW:]
    side, best = _route(logits, rb_ref)
    grp_ref[0, 0] = best
    for c in range(D // 128):
        u2_ref[pl.ds(c, TM, stride=XROWS), :] = u2[:, 128 * c:128 * (c + 1)]
    u2_ref[pl.ds(D // 128, TM, stride=XROWS), :] = side
    for c in range(D // 128 + 1, XROWS):
        u2_ref[pl.ds(c, TM, stride=XROWS), :] = jnp.zeros((TM, 128), F32)


def _outproj(b_router, a, a_col, a_w, o_f, o_b, proj, g_col, h, modsel, gnorm, wa, wr, lng, lnb, wrt, *,
             dv, dvp, row_off):
    b, rc, _ = proj.shape
    nt = rc // TM - row_off
    wrec = R_HEADS * dvp
    full = lambda shape: pl.BlockSpec(shape, lambda i, j: (0,) * len(shape))
    h_specs, h_args = _row_specs(h, row_off)
    return pl.pallas_call(
        functools.partial(_outproj_kernel, dvp=dvp, dv=dv),
        grid=(b, nt),
        in_specs=[pl.BlockSpec(memory_space=pltpu.SMEM),
                  pl.BlockSpec((1, TM, a_w), lambda i, j: (i, j + (row_off if a.shape[1] == rc else 0), a_col)),
                  pl.BlockSpec((1, TM, wrec), lambda i, j: (i, j + row_off, 0)),
                  pl.BlockSpec((1, TM, wrec), lambda i, j: (i, j + row_off, 0)),
                  pl.BlockSpec((1, TM, wrec), lambda i, j: (i, j + row_off, g_col))] + h_specs + [
                  pl.BlockSpec((1, 1, 6, D), lambda i, j: (i, jnp.minimum(j + row_off, 1), 0, 0)),
                  full((1, wrec)), full((a_w, D)), full((wrec, D)), full((1, D)), full((1, D)), full((D, GW))],
        out_specs=[pl.BlockSpec((1, TM, D), lambda i, j: (i, j, 0)),
                   pl.BlockSpec((TM * XROWS, 128), lambda i, j: (i * nt + j, 0)),
                   pl.BlockSpec((1, 1, 1, TM), lambda i, j: (i, j, 0, 0))],
        out_shape=[jax.ShapeDtypeStruct((b, nt * TM, D), F32),
                   jax.ShapeDtypeStruct((b * nt * TM * XROWS, 128), F32),
                   jax.ShapeDtypeStruct((b, nt, 1, TM), F32)],
        compiler_params=_cp(("parallel", "arbitrary"), V7X_VMEM_LIMIT),
        name="outproj",
    )(b_router, a, o_f, o_b, proj, *h_args, modsel, gnorm, wa, wr, lng, lnb, wrt)


def _start_token_gather(idx_ref, base, src_hbm, dst_ref, sem, rows):
    def body(r, carry):
        t = idx_ref[base + r]
        pltpu.make_async_copy(src_hbm.at[pl.ds(pl.multiple_of(t * rows, rows), rows), :],
                              dst_ref.at[pl.ds(pl.multiple_of(r * rows, rows), rows), :], sem).start()
        return carry

    lax.fori_loop(0, dst_ref.shape[0] // rows, body, 0, unroll=8)


def _wait_token_gather(src_hbm, dst_ref, sem):
    pltpu.make_async_copy(src_hbm.at[pl.ds(0, dst_ref.shape[0]), :], dst_ref, sem).wait()


def _gather_ring(idx_ref, first_of, step, last, src_hbm, buf_ref, sem, rows):
    depth = buf_ref.shape[0]
    slot = step % depth

    @pl.when(step == 0)
    def _():
        for k in range(depth - 1):
            _start_token_gather(idx_ref, first_of(jnp.minimum(k, last)), src_hbm, buf_ref.at[k], sem.at[k], rows)

    _wait_token_gather(src_hbm, buf_ref.at[slot], sem.at[slot])
    ahead = (step + depth - 1) % depth
    nxt = first_of(jnp.minimum(step + depth - 1, last))

    def issue(r0, r1):
        for r in range(r0, r1):
            t = idx_ref[nxt + r]
            pltpu.make_async_copy(src_hbm.at[pl.ds(pl.multiple_of(t * rows, rows), rows), :],
                                  buf_ref.at[ahead, pl.ds(r * rows, rows), :], sem.at[ahead]).start()

    def finish():
        @pl.when(step == last)
        def _():
            for k in range(1, depth):
                other = (step + k) % depth
                _wait_token_gather(src_hbm, buf_ref.at[other], sem.at[other])

    return slot, issue, finish


def _moe_kernel(tg_ref, first_ref, src_ref, x_hbm, w1_ref, w3_ref, w2_ref, o_ref, xb_ref, sem,
                w1b_ref, w3b_ref, w2b_ref):
    i = pl.program_id(0)
    slot, issue, finish = _gather_ring(src_ref, lambda k: first_ref[k], i, pl.num_programs(0) - 1,
                                       x_hbm, xb_ref, sem, XROWS)

    @pl.when((i == 0) | (tg_ref[i] != tg_ref[jnp.maximum(i - 1, 0)]))
    def _():
        for e in range(E_PER):
            w1b_ref[e] = w1_ref[e].astype(BF16)
            w3b_ref[e] = w3_ref[e].astype(BF16)
            w2b_ref[e] = w2_ref[e].astype(BF16)

    xb = xb_ref.at[slot]
    x = jnp.concatenate([xb[pl.ds(c, TME, stride=XROWS), :] for c in range(D // 128)], axis=1).astype(BF16)
    side = xb[pl.ds(D // 128, TME, stride=XROWS), :]
    y = jnp.zeros((TME, D), F32)
    per = TME // E_PER
    for e in range(E_PER):
        issue(e * per, (e + 1) * per)
        h1 = jnp.dot(x, w1b_ref[e], preferred_element_type=F32)
        h3 = jnp.dot(x, w3b_ref[e], preferred_element_type=F32)
        hh = (_silu(h1) * h3 * side[:, e:e + 1]).astype(BF16)
        y = y + jnp.dot(hh, w2b_ref[e], preferred_element_type=F32)
    for c in range(D // 128):
        o_ref[pl.ds(c, TME, stride=YROWS), :] = y[:, 128 * c:128 * (c + 1)]
    finish()


def _moe(tile_grp, tile_first, order, u2x, w1, w3, w2, layer):
    mp = tile_grp.shape[0] * TME
    wspec = lambda k, n: pl.BlockSpec((None, E_PER, k, n), lambda i, tg, tf, od: (layer, tg[i], 0, 0),
                                      pipeline_mode=pl.Buffered(1))
    return pl.pallas_call(
        _moe_kernel,
        grid_spec=pltpu.PrefetchScalarGridSpec(
            num_scalar_prefetch=3, grid=(mp // TME,),
            in_specs=[pl.BlockSpec(memory_space=pl.ANY), wspec(D, D_FF), wspec(D, D_FF), wspec(D_FF, D)],
            out_specs=pl.BlockSpec((TME * YROWS, 128), lambda i, tg, tf, od: (i, 0)),
            scratch_shapes=[pltpu.VMEM((2, TME * XROWS, 128), F32), pltpu.SemaphoreType.DMA((2,)),
                            pltpu.VMEM((E_PER, D, D_FF), BF16), pltpu.VMEM((E_PER, D, D_FF), BF16),
                            pltpu.VMEM((E_PER, D_FF, D), BF16)]),
        out_shape=jax.ShapeDtypeStruct((mp * YROWS, 128), F32),
        compiler_params=_cp(("arbitrary",), V7X_VMEM_LIMIT),
        name="moe",
    )(tile_grp, tile_first, order, u2x, w1, w3, w2)


def _combine_kernel(pos_ref, y_hbm, h_ref, mod_ref, lng_ref, lnb_ref, o_ref, yb_ref, sem, *, nt):
    step = pl.program_id(0) * nt + pl.program_id(1)
    slot, issue, finish = _gather_ring(pos_ref, lambda k: k * TM, step, pl.num_programs(0) * nt - 1,
                                       y_hbm, yb_ref, sem, YROWS)
    yb = yb_ref.at[slot]
    g2 = mod_ref[0, 0, 5:6, :]
    strip = TM // 2
    for r0 in range(0, TM, strip):
        issue(r0, r0 + strip)
        y = jnp.concatenate([yb[pl.ds(r0 * YROWS + c, strip, stride=YROWS), :] for c in range(D // 128)], axis=1)
        o_ref[0, r0:r0 + strip, :] = (_ln(ALPHA * h_ref[0, r0:r0 + strip, :] + g2 * y) * lng_ref[0:1, :]
                                      + lnb_ref[0:1, :])
    finish()


def _combine(pos, y_sorted, h1, modsel, lng, lnb, row_off):
    b, r, _ = h1.shape
    nt = r // TM
    full = lambda shape: pl.BlockSpec(shape, lambda i, j, p: (0,) * len(shape))
    return pl.pallas_call(
        functools.partial(_combine_kernel, nt=nt),
        grid_spec=pltpu.PrefetchScalarGridSpec(
            num_scalar_prefetch=1, grid=(b, nt),
            in_specs=[pl.BlockSpec(memory_space=pl.ANY),
                      pl.BlockSpec((1, TM, D), lambda i, j, p: (i, j, 0)),
                      pl.BlockSpec((1, 1, 6, D), lambda i, j, p: (i, jnp.minimum(j + row_off, 1), 0, 0)),
                      full((1, D)), full((1, D))],
            out_specs=pl.BlockSpec((1, TM, D), lambda i, j, p: (i, j, 0)),
            scratch_shapes=[pltpu.VMEM((3, TM * YROWS, 128), F32), pltpu.SemaphoreType.DMA((3,))]),
        out_shape=jax.ShapeDtypeStruct((b, r, D), F32),
        compiler_params=_cp(("arbitrary", "arbitrary")),
        name="moe_combine",
    )(pos, y_sorted, h1, modsel, lng, lnb)


def _sort_by_group(grp):
    n = grp.shape[0]
    onehot = (grp[:, None] == jnp.arange(N_GRP, dtype=jnp.int32)[None, :]).astype(jnp.int32)
    csum = jnp.cumsum(onehot, axis=0)
    counts = csum[-1]
    padded = ((counts + TME - 1) // TME) * TME
    ends = jnp.cumsum(padded)
    starts = ends - padded
    rank = jnp.sum(csum * onehot, axis=1) - 1
    pos = (starts[grp] + rank).astype(jnp.int32)
    _, order = lax.sort((grp, jnp.arange(n, dtype=jnp.int32)), num_keys=1, is_stable=True)
    order = jnp.concatenate([order, jnp.zeros((TME,), jnp.int32)])
    ntile = n // TME + N_GRP
    tile_start = jnp.arange(ntile, dtype=jnp.int32) * TME
    tile_grp = jnp.minimum(jnp.sum((tile_start[:, None] >= ends[None, :]).astype(jnp.int32), axis=1), N_GRP - 1)
    first = (jnp.cumsum(counts) - counts)[tile_grp] + tile_start - starts[tile_grp]
    return pos, order, tile_grp.astype(jnp.int32), jnp.minimum(first, n).astype(jnp.int32)


def _moe_layer(u2x, grp, h1, modsel, experts, lng, lnb, row_off):
    pos, order, tile_grp, tile_first = _sort_by_group(grp.reshape(-1).astype(jnp.int32))
    y_sorted = _moe(tile_grp, tile_first, order, u2x, *experts)
    return _combine(pos, y_sorted, h1, modsel, lng, lnb, row_off)


def _rope_tables(rc, t):
    half = A_HD // 4
    inv = ROPE_BASE ** (-jnp.arange(half, dtype=F32) / half)
    tpos = jnp.arange(t)
    rows = (tpos // GRID_W).astype(F32)
    cols = (tpos % GRID_W).astype(F32)

    def cs(pos):
        ang = pos[:, None] * inv[None, :]
        c, s = jnp.cos(ang), jnp.sin(ang)
        return jnp.concatenate([c, c], -1), jnp.concatenate([-s, s], -1)

    cr, sr = cs(rows)
    cc_, sc_ = cs(cols)
    cos_h = jnp.concatenate([cr, cc_], -1)
    sin_h = jnp.concatenate([sr, sc_], -1)
    one = jnp.ones((t, A_HD), F32)
    zero = jnp.zeros((t, A_HD), F32)
    lat = jnp.stack([jnp.concatenate([cos_h, cos_h], -1), jnp.concatenate([sin_h, sin_h], -1),
                     jnp.concatenate([cos_h, one], -1), jnp.concatenate([sin_h, zero], -1)])
    ctx = jnp.stack([jnp.ones((CTX, 128), F32), jnp.zeros((CTX, 128), F32)] * 2)
    return jnp.concatenate([ctx, lat], axis=1)


def _dft_tables(t):
    def cos_sin(num, den):
        ang = (2.0 * math.pi / den) * (num % den).astype(F32)
        return jnp.cos(ang), jnp.sin(ang)

    k = jnp.arange(t, dtype=jnp.int32)[:, None]
    ca, sa = cos_sin(k * jnp.arange(t // GRID_W, dtype=jnp.int32)[None, :], t // GRID_W)
    cb, sb = cos_sin(k * jnp.arange(GRID_W, dtype=jnp.int32)[None, :], t)
    ct = (ca[:, :, None] * cb[:, None, :] - sa[:, :, None] * sb[:, None, :]).reshape(t, t)
    st = (sa[:, :, None] * cb[:, None, :] + ca[:, :, None] * sb[:, None, :]).reshape(t, t)
    c = jnp.arange(F_GD, dtype=jnp.int32)
    cc, sc = cos_sin(c[:, None] * c[None, :], F_GD)
    eye = jnp.eye(F_GROUPS, dtype=F32)
    cs = jnp.concatenate([jnp.kron(eye, cc), jnp.kron(eye, sc)], axis=1)
    return cs.astype(BF16), ct.astype(BF16), st.astype(BF16)


def _pad_heads(w, width, padded):
    lead = w.shape[:-1]
    w = w.reshape(lead + (R_HEADS, width))
    w = jnp.pad(w, [(0, 0)] * len(lead) + [(0, 0), (0, padded - width)])
    return w.reshape(lead + (R_HEADS * padded,))


EVEN_SEGS = tuple([(128 * g, 128 * (g + 1), 0) for g in range(4)] + [(c, c + 512, None) for c in range(512, 3072, 512)]
                  + [(3072 + 128 * g, 3072 + 128 * (g + 1), 1) for g in range(2)])
ODD_W = 512 + 512 + 1024 + 1024 + 256 + 128
ODD_SEGS = tuple([(c, c + 512, None) for c in range(0, 3072, 512)] + [(3072, ODD_W, None)])


def _layer_even(h, modsel, tbl, w_in, sink, lb_logits, g_norm, w_out, lng, lnb, w_router, b_router, experts):
    aq, ak, av, ff, fb, hq, hi, hg = jnp.split(w_in, np.cumsum([512, 128, 128, 512, 512, 512, 512]).tolist(), axis=1)
    kv = [jnp.concatenate([ak[:, A_HD * j:A_HD * (j + 1)], av[:, A_HD * j:A_HD * (j + 1)]], axis=1)
          for j in range(A_KV)]
    w = jnp.concatenate([aq, ff, fb, hq, hi, hg] + kv, axis=1).astype(BF16)
    proj = _inproj(h, modsel, w, tbl, EVEN_SEGS)
    o_att = _attention(proj, sink, 3072 // (2 * A_HD * A_KV))
    lb = jnp.cumsum(jax.nn.softmax(lb_logits.astype(F32), axis=1), axis=1)[:, 0]
    o_f, o_b = _hgrn_scan(proj, lb, 3, 4, 1, 2)
    wo = w_out.astype(BF16)
    h1, u2x, grp = _outproj(b_router, o_att, 0, 512, o_f, o_b, proj, 5, h, modsel, g_norm.reshape(1, -1),
                            wo[:512], wo[512:], lng[0:1], lnb[0:1], w_router, dv=128, dvp=128, row_off=0)
    return _moe_layer(u2x, grp, h1, modsel, experts, lng[1:2], lnb[1:2], 0)


def _layer_odd(h, modsel, tbl, dft, w_in, gate_w, gate_b, g_norm, w_out, lng, lnb, w_router, b_router, experts, t):
    q, k, v, rf, rb, g, z = jnp.split(w_in, np.cumsum([384, 384, 768, 16, 16, 768]).tolist(), axis=1)
    r = jnp.pad(jnp.concatenate([rf, rb], axis=1), ((0, 0), (0, 96)))
    w = jnp.concatenate([_pad_heads(q, GLA_DK, DKP), _pad_heads(k, GLA_DK, DKP), _pad_heads(v, GLA_DV, GLA_DVP),
                         _pad_heads(g, GLA_DV, GLA_DVP), z, r], axis=1).astype(BF16)
    proj = _inproj(h, modsel, w, tbl, ODD_SEGS)
    gw = _pad_heads(gate_w, GLA_DK, DKP)
    gwf = jnp.pad(gw[0], ((0, 112), (0, 0)))
    gwb = jnp.pad(gw[1], ((16, 96), (0, 0)))
    gb = _pad_heads(gate_b, GLA_DK, DKP)
    o_f, o_b = _gla_scan(proj, gwf, gwb, gb, 0, 1, 1, 3328 // 128)
    cs, ct, st = dft
    four = _fourier(proj, 3072 // (F_GROUPS * F_GD), cs, ct, st, t)
    wo = w_out
    wr = jnp.pad(wo[:768].reshape(R_HEADS, GLA_DV, D), ((0, 0), (0, GLA_DVP - GLA_DV), (0, 0))).reshape(-1, D)
    h1, u2x, grp = _outproj(b_router, four, 0, 256, o_f, o_b, proj, 2, h, modsel,
                            _pad_heads(g_norm, GLA_DV, GLA_DVP).reshape(1, -1),
                            wo[768:].astype(BF16), wr.astype(BF16), lng[0:1], lnb[0:1], w_router,
                            dv=GLA_DV, dvp=GLA_DVP, row_off=CTX // TM)
    return _moe_layer(u2x, grp, h1, modsel, experts, lng[1:2], lnb[1:2], CTX // TM)


def kernel(x, c, ctx, c_ctx, w_ada, b_ada, ln_g, ln_b, w_in_even, attn_sink, hgrn_lb_logits, hgrn_norm, w_out_even, w_in_odd, gla_gate_w, gla_gate_b, gla_norm, w_out_odd, w_router, b_router, w_expert_gate, w_expert_up, w_expert_down):
    b, t, _ = x.shape
    rc = CTX + t
    assert ctx.shape[1] == CTX
    cc = jnp.zeros((16, D), F32).at[:b].set(c).at[b].set(c_ctx)
    mods = _ada_mods(cc, w_ada, b_ada).reshape(2, 16, 6, D)

    def modsel(l):
        return jnp.stack([jnp.broadcast_to(mods[l, b], (b, 6, D)), mods[l, :b]], axis=1)

    tbl = _rope_tables(rc, t)
    wrt = jnp.pad(w_router, ((0, 0), (0, GW - N_EXP)))
    experts = (w_expert_gate, w_expert_up, w_expert_down)
    h = _layer_even((ctx, x), modsel(0), tbl, w_in_even[0], attn_sink[0], hgrn_lb_logits, hgrn_norm[0], w_out_even[0],
                    ln_g[0], ln_b[0], wrt, b_router, experts + (0,))
    return _layer_odd(h, modsel(1), tbl, _dft_tables(t), w_in_odd[0], gla_gate_w[0], gla_gate_b[0], gla_norm[0],
                      w_out_odd[0], ln_g[1], ln_b[1], wrt, b_router, experts + (1,), t)
```
